```python
import math
import jax, jax.numpy as jnp
from jax import lax
import numpy as np

D_MODEL = 1024
BATCH = 8
SEQ = 2048
DEPTH = 2
DEC_BATCH = 128
DEC_SEQ = 8
PAST_LEN = 16384
PAGE_SIZE = 128

R_HEADS = 6
R_DK = 64
R_DV = 64
G_HEADS = 6
G_DK = 32
G_DV = 64
G_RANK = 16
G_NORMALIZER = 16.0
M_HEADS = 4
M_DK = 64
M_DV = 64
CONV_W = 4
R_W = R_HEADS * R_DV
G_W = G_HEADS * G_DV
M_W = M_HEADS * M_DV
D_MIX = R_W + G_W + M_W
X_HEADS = 4
X_HD = D_MODEL // X_HEADS
N_MEM = 256
D_FF = int(math.ceil(8 * D_MODEL / 3 / 256)) * 256
CHUNK = 128
EPS = 1e-6
ROPE_BASE = 10000.0
IN_SIZES = (R_HEADS * R_DK, R_HEADS * R_DK, R_W, R_W,
            G_HEADS * G_DK, G_HEADS * G_DK, G_W, G_W, G_RANK,
            2 * M_HEADS * M_DK, M_W, M_W, M_HEADS, M_HEADS)
D_IN = sum(IN_SIZES)
SPLIT_POINTS = tuple(int(v) for v in np.cumsum(IN_SIZES)[:-1])

kernel_name = 'hybrid_ret_gla_mlstm_step'


def rmsnorm(x, g):
    x32 = x.astype(jnp.float32)
    y = x32 * lax.rsqrt(jnp.mean(x32 * x32, axis=-1, keepdims=True) + EPS)
    return (y * g).astype(x.dtype)


def head_rmsnorm(o, g):
    B, T = o.shape[:2]
    y = o * lax.rsqrt(jnp.mean(o * o, axis=-1, keepdims=True) + EPS)
    return y.reshape(B, T, -1) * g


def rope(x, pos):
    d = x.shape[-1]
    half = d // 2
    inv = ROPE_BASE ** (-jnp.arange(half, dtype=jnp.float32) * 2.0 / d)
    ang = pos[:, None] * inv[None, :]
    cos = jnp.cos(ang)[None, :, None, :]
    sin = jnp.sin(ang)[None, :, None, :]
    x1, x2 = x[..., :half], x[..., half:]
    return jnp.concatenate([x1 * cos - x2 * sin, x1 * sin + x2 * cos], axis=-1)


def _chunk_len(T):
    return CHUNK if T % CHUNK == 0 else T


def _to_chunks(a, L):
    B, T, H = a.shape[:3]
    a = a.reshape((B, T // L, L, H) + a.shape[3:])
    return jnp.moveaxis(jnp.swapaxes(a, 2, 3), 1, 0)


def _from_chunks(o):
    N, B, H, L, d = o.shape
    return jnp.transpose(o, (1, 0, 3, 2, 4)).reshape(B, N * L, H, d)


def retention(q, k, v, S0):
    T = q.shape[1]
    H = q.shape[2]
    L = _chunk_len(T)
    log_gamma = jnp.log(1.0 - 2.0 ** (-5.0 - jnp.arange(H, dtype=jnp.float32)))
    idx = jnp.arange(L, dtype=jnp.float32)
    rel = idx[:, None] - idx[None, :]
    causal = rel >= 0
    decay = jnp.where(causal[None], jnp.exp(log_gamma[:, None, None] * jnp.where(causal, rel, 0.0)[None]), 0.0)
    q_decay = jnp.exp(log_gamma[:, None] * (idx + 1.0))[..., None]
    k_decay = jnp.exp(log_gamma[:, None] * (L - 1.0 - idx))[..., None]
    chunk_decay = jnp.exp(log_gamma * L)[:, None, None]

    def step(S, inp):
        qc, kc, vc = inp
        s = jnp.einsum('bhid,bhjd->bhij', qc, kc) * decay
        o = jnp.einsum('bhij,bhjv->bhiv', s, vc) + jnp.einsum('bhid,bhdv->bhiv', qc, S) * q_decay
        S = S * chunk_decay + jnp.einsum('bhjd,bhjv->bhdv', kc * k_decay, vc)
        return S, o

    S, o = lax.scan(step, S0, (_to_chunks(q, L), _to_chunks(k, L), _to_chunks(v, L)))
    return _from_chunks(o), S


def gla(q, k, v, log_a, S0):
    T = q.shape[1]
    L = _chunk_len(T)
    mask = jnp.tril(jnp.ones((L, L), dtype=bool))[None, None, :, :, None]

    def step(S, inp):
        qc, kc, vc, ac = inp
        b = jnp.cumsum(ac, axis=2)
        rel = jnp.where(mask, b[:, :, :, None, :] - b[:, :, None, :, :], -jnp.inf)
        s = jnp.einsum('bhic,bhjc,bhijc->bhij', qc, kc, jnp.exp(rel))
        o = jnp.einsum('bhij,bhjv->bhiv', s, vc) + jnp.einsum('bhic,bhcv->bhiv', qc * jnp.exp(b), S)
        bL = b[:, :, -1:, :]
        S = S * jnp.exp(bL[:, :, 0, :, None]) + jnp.einsum('bhjc,bhjv->bhcv', kc * jnp.exp(bL - b), vc)
        return S, o

    S, o = lax.scan(step, S0, (_to_chunks(q, L), _to_chunks(k, L), _to_chunks(v, L), _to_chunks(log_a, L)))
    return _from_chunks(o), S


def mlstm(q, k, v, log_i, log_f, C0, n0, m0):
    T = q.shape[1]
    L = _chunk_len(T)
    mask = jnp.tril(jnp.ones((L, L), dtype=bool))

    def step(carry, inp):
        C, n, m = carry
        qc, kc, vc, ic, fc = inp
        F = jnp.cumsum(fc, axis=-1)
        a = F + m[..., None]
        D = jnp.where(mask, F[..., :, None] - F[..., None, :] + ic[..., None, :], -jnp.inf)
        m_tok = jnp.maximum(a, jnp.max(D, axis=-1))
        w_inter = jnp.exp(a - m_tok)
        s = jnp.einsum('bhid,bhjd->bhij', qc, kc) * jnp.exp(D - m_tok[..., None])
        num = jnp.einsum('bhij,bhjv->bhiv', s, vc) + jnp.einsum('bhid,bhdv->bhiv', qc, C) * w_inter[..., None]
        den = jnp.sum(s, axis=-1) + jnp.einsum('bhid,bhd->bhi', qc, n) * w_inter
        h = num / jnp.maximum(jnp.abs(den), jnp.exp(-m_tok))[..., None]
        m_new = m_tok[..., -1]
        wk = jnp.exp(F[..., -1:] - F + ic - m_new[..., None])[..., None]
        scale = jnp.exp(a[..., -1] - m_new)
        C = C * scale[..., None, None] + jnp.einsum('bhjd,bhjv->bhdv', kc * wk, vc)
        n = n * scale[..., None] + jnp.sum(kc * wk, axis=2)
        return (C, n, m_new), h

    (C, n, m), h = lax.scan(step, (C0, n0, m0),
                            (_to_chunks(q, L), _to_chunks(k, L), _to_chunks(v, L),
                             _to_chunks(log_i, L), _to_chunks(log_f, L)))
    return _from_chunks(h), C, n, m


def causal_conv(u, buf, w, b):
    T = u.shape[1]
    up = jnp.concatenate([buf, u], axis=1)
    y = b
    for j in range(CONV_W):
        y = y + up[:, j:j + T, :] * w[j]
    return jax.nn.silu(y), up[:, -(CONV_W - 1):, :]


def mixing(xn, pos, state, w_in, w_ga2, b_ga, conv_w, conv_b, b_i, b_f, g_ret, g_gla, g_mlstm, w_out):
    S_r0, S_g0, C0, n0, m0, buf0 = state
    B, T, _ = xn.shape
    f32 = jnp.float32
    proj = (xn @ w_in).astype(f32)
    (qr, kr, vr, gr, qg, kg, vg, rg, ag, qkm, vm, om, im, fm) = jnp.split(proj, SPLIT_POINTS, axis=-1)
    qr = rope(qr.reshape(B, T, R_HEADS, R_DK), pos) * (R_DK ** -0.5)
    kr = rope(kr.reshape(B, T, R_HEADS, R_DK), pos)
    o_r, S_r = retention(qr, kr, vr.reshape(B, T, R_HEADS, R_DV), S_r0.astype(f32))
    o_r = head_rmsnorm(o_r, g_ret) * jax.nn.silu(gr)
    log_a = jax.nn.log_sigmoid(ag @ w_ga2 + b_ga) / G_NORMALIZER
    o_g, S_g = gla(qg.reshape(B, T, G_HEADS, G_DK) * (G_DK ** -0.5), kg.reshape(B, T, G_HEADS, G_DK),
                   vg.reshape(B, T, G_HEADS, G_DV), log_a.reshape(B, T, G_HEADS, G_DK), S_g0.astype(f32))
    o_g = head_rmsnorm(o_g, g_gla) * jax.nn.silu(rg)
    qk, buf = causal_conv(qkm, buf0.astype(f32), conv_w, conv_b)
    qm, km = jnp.split(qk, 2, axis=-1)
    log_i = im + b_i
    log_f = jax.nn.log_sigmoid(fm + b_f)
    h_m, C, n, m = mlstm(qm.reshape(B, T, M_HEADS, M_DK), km.reshape(B, T, M_HEADS, M_DK) * (M_DK ** -0.5),
                         vm.reshape(B, T, M_HEADS, M_DV), log_i, log_f,
                         C0.astype(f32), n0.astype(f32), m0.astype(f32))
    o_m = head_rmsnorm(h_m, g_mlstm) * jax.nn.sigmoid(om)
    out = jnp.concatenate([o_r, o_g, o_m], axis=-1).astype(xn.dtype) @ w_out
    new_state = (S_r.astype(S_r0.dtype), S_g.astype(S_g0.dtype), C.astype(C0.dtype),
                 n.astype(n0.dtype), m.astype(m0.dtype), buf.astype(buf0.dtype))
    return out, new_state


def memory_kv(mem, g_mem, w_xk, w_xv):
    B = mem.shape[0]
    mn = rmsnorm(mem, g_mem)
    k = (mn @ w_xk).reshape(B, N_MEM, X_HEADS, X_HD)
    v = (mn @ w_xv).reshape(B, N_MEM, X_HEADS, X_HD)
    return k, v


def cross_attn(xn, mk, mv, w_xq, w_xo):
    B, T, _ = xn.shape
    q = (xn @ w_xq).reshape(B, T, X_HEADS, X_HD)
    s = jnp.einsum('bthd,bmhd->bhtm', q, mk).astype(jnp.float32) * (X_HD ** -0.5)
    p = jax.nn.softmax(s, axis=-1).astype(xn.dtype)
    o = jnp.einsum('bhtm,bmhd->bthd', p, mv).reshape(B, T, X_HEADS * X_HD)
    return o @ w_xo


def swiglu(xn, w_gate, w_up, w_down):
    return (jax.nn.silu(xn @ w_gate) * (xn @ w_up)) @ w_down


def layer(x, pos, mem_k, mem_v, state, lw):
    (g_mix, w_in, w_ga2, b_ga, conv_w, conv_b, b_i, b_f, g_ret, g_gla, g_mlstm, w_out,
     g_xattn, w_xq, w_xo, g_ffn, w_gate, w_up, w_down) = lw
    mix, new_state = mixing(rmsnorm(x, g_mix), pos, state, w_in, w_ga2, b_ga, conv_w, conv_b,
                            b_i, b_f, g_ret, g_gla, g_mlstm, w_out)
    x = x + mix
    x = x + cross_attn(rmsnorm(x, g_xattn), mem_k, mem_v, w_xq, w_xo)
    x = x + swiglu(rmsnorm(x, g_ffn), w_gate, w_up, w_down)
    return x, new_state


def setup_inputs(seed: int = 0) -> dict:
    key = jax.random.key(seed)
    ks = iter(jax.random.split(key, 48))
    f32 = jnp.float32

    def nrm(shape, scale):
        return scale * jax.random.normal(next(ks), shape, f32)

    def gain(shape):
        return 1.0 + 0.01 * jax.random.normal(next(ks), shape, f32)

    QK_M = 2 * M_HEADS * M_DK
    return {
        'x_prompt': nrm((BATCH, SEQ, D_MODEL), 1.0),
        'x_sample': nrm((DEC_BATCH, DEC_SEQ, D_MODEL), 1.0),
        'state_ret': nrm((DEPTH, DEC_BATCH, R_HEADS, R_DK, R_DV), 1.0),
        'state_gla': nrm((DEPTH, DEC_BATCH, G_HEADS, G_DK, G_DV), 1.0),
        'state_mlstm_C': nrm((DEPTH, DEC_BATCH, M_HEADS, M_DK, M_DV), 0.5),
        'state_mlstm_n': nrm((DEPTH, DEC_BATCH, M_HEADS, M_DK), 1.0),
        'state_mlstm_m': nrm((DEPTH, DEC_BATCH, M_HEADS), 0.5),
        'state_mlstm_conv': nrm((DEPTH, DEC_BATCH, CONV_W - 1, QK_M), 1.0),
        'cache_mem_k': nrm((DEPTH, DEC_BATCH, N_MEM, X_HEADS, X_HD), 1.0),
        'cache_mem_v': nrm((DEPTH, DEC_BATCH, N_MEM, X_HEADS, X_HD), 1.0),
        'mem_prompt': nrm((BATCH, N_MEM, D_MODEL), 1.0),
        'g_mix': gain((DEPTH, D_MODEL)),
        'w_in': nrm((DEPTH, D_MODEL, D_IN), D_MODEL ** -0.5),
        'w_ga2': nrm((DEPTH, G_RANK, G_HEADS * G_DK), G_RANK ** -0.5),
        'b_ga': nrm((DEPTH, G_HEADS * G_DK), 0.1),
        'conv_w': nrm((DEPTH, CONV_W, QK_M), CONV_W ** -0.5),
        'conv_b': nrm((DEPTH, QK_M), 0.01),
        'b_i': nrm((DEPTH, M_HEADS), 0.1),
        'b_f': jnp.linspace(3.0, 6.0, M_HEADS, dtype=f32)[None, :] + nrm((DEPTH, M_HEADS), 0.1),
        'g_ret': gain((DEPTH, R_W)),
        'g_gla': gain((DEPTH, G_W)),
        'g_mlstm': gain((DEPTH, M_W)),
        'w_out': nrm((DEPTH, D_MIX, D_MODEL), D_MIX ** -0.5),
        'g_xattn': gain((DEPTH, D_MODEL)),
        'g_mem': gain((DEPTH, D_MODEL)),
        'w_xq': nrm((DEPTH, D_MODEL, X_HEADS * X_HD), D_MODEL ** -0.5),
        'w_xk': nrm((DEPTH, D_MODEL, X_HEADS * X_HD), D_MODEL ** -0.5),
        'w_xv': nrm((DEPTH, D_MODEL, X_HEADS * X_HD), D_MODEL ** -0.5),
        'w_xo': nrm((DEPTH, X_HEADS * X_HD, D_MODEL), (X_HEADS * X_HD) ** -0.5),
        'g_ffn': gain((DEPTH, D_MODEL)),
        'w_gate': nrm((DEPTH, D_MODEL, D_FF), D_MODEL ** -0.5),
        'w_up': nrm((DEPTH, D_MODEL, D_FF), D_MODEL ** -0.5),
        'w_down': nrm((DEPTH, D_FF, D_MODEL), D_FF ** -0.5),
        'g_final': gain((D_MODEL,)),
    }


def reference(x_prompt, x_sample, state_ret, state_gla, state_mlstm_C, state_mlstm_n, state_mlstm_m,
              state_mlstm_conv, cache_mem_k, cache_mem_v, mem_prompt, g_mix, w_in, w_ga2, b_ga, conv_w,
              conv_b, b_i, b_f, g_ret, g_gla, g_mlstm, w_out, g_xattn, g_mem, w_xq, w_xk, w_xv, w_xo,
              g_ffn, w_gate, w_up, w_down, g_final):
    B, T, _ = x_prompt.shape
    Ts = x_sample.shape[1]
    dt = x_prompt.dtype
    pos_p = jnp.arange(T, dtype=jnp.float32)
    pos_s = PAST_LEN + jnp.arange(Ts, dtype=jnp.float32)
    hp, hs = x_prompt, x_sample
    p_st = [[] for _ in range(6)]
    s_st = [[] for _ in range(6)]
    p_mk, p_mv = [], []
    for l in range(DEPTH):
        lw = (g_mix[l], w_in[l], w_ga2[l], b_ga[l], conv_w[l], conv_b[l], b_i[l], b_f[l], g_ret[l],
              g_gla[l], g_mlstm[l], w_out[l], g_xattn[l], w_xq[l], w_xo[l], g_ffn[l], w_gate[l],
              w_up[l], w_down[l])
        mk, mv = memory_kv(mem_prompt, g_mem[l], w_xk[l], w_xv[l])
        init_p = (jnp.zeros((B, R_HEADS, R_DK, R_DV), dt), jnp.zeros((B, G_HEADS, G_DK, G_DV), dt),
                  jnp.zeros((B, M_HEADS, M_DK, M_DV), dt), jnp.zeros((B, M_HEADS, M_DK), dt),
                  jnp.zeros((B, M_HEADS), dt), jnp.zeros((B, CONV_W - 1, 2 * M_HEADS * M_DK), dt))
        hp, new_p = layer(hp, pos_p, mk, mv, init_p, lw)
        p_mk.append(mk)
        p_mv.append(mv)
        init_s = (state_ret[l], state_gla[l], state_mlstm_C[l], state_mlstm_n[l], state_mlstm_m[l],
                  state_mlstm_conv[l])
        hs, new_s = layer(hs, pos_s, cache_mem_k[l], cache_mem_v[l], init_s, lw)
        for i in range(6):
            p_st[i].append(new_p[i])
            s_st[i].append(new_s[i])
    y_prompt = rmsnorm(hp, g_final)
    y_sample = rmsnorm(hs, g_final)
    p_ret, p_gla, p_C, p_n, p_m, p_conv = [jnp.stack(a, axis=0) for a in p_st]
    s_ret, s_gla, s_C, s_n, s_m, s_conv = [jnp.stack(a, axis=0) for a in s_st]
    p_mem_k = jnp.stack(p_mk, axis=0)
    p_mem_v = jnp.stack(p_mv, axis=0)
    return (y_prompt, y_sample, p_ret, p_gla, p_C, p_n, p_m, p_conv, p_mem_k, p_mem_v,
            s_ret, s_gla, s_C, s_n, s_m, s_conv)
```

```python
import functools
import math

import numpy as np
import jax
import jax.numpy as jnp
from jax import lax
from jax.experimental import pallas as pl
from jax.experimental.pallas import tpu as pltpu

F32 = jnp.float32
BF16 = jnp.bfloat16

D_MODEL = 1024
PAST_LEN = 16384
R_HEADS, R_DK, R_DV = 6, 64, 64
G_HEADS, G_DK, G_DV, G_RANK = 6, 32, 64, 16
G_NORMALIZER = 16.0
M_HEADS, M_DK, M_DV = 4, 64, 64
CONV_W = 4
X_HEADS = 4
X_HD = D_MODEL // X_HEADS
N_MEM = 256
D_FF = int(math.ceil(8 * D_MODEL / 3 / 256)) * 256
CHUNK = 128
EPS = 1e-6
ROPE_BASE = 10000.0

R_W = R_HEADS * R_DV
G_QK = G_HEADS * G_DK
G_W = G_HEADS * G_DV
M_QK = 2 * M_HEADS * M_DK
M_W = M_HEADS * M_DV
D_MIX = R_W + G_W + M_W

C_QR, C_KR, C_VR, C_GR = 0, 384, 768, 1152
C_QG, C_KG, C_VG, C_RG = 1536, 1728, 1920, 2304
C_QKM, C_VM, C_OM, C_SM = 2688, 3200, 3456, 3712
D_IN_PAD = 3840
SM_AG, SM_I, SM_F = 0, 16, 20

V7X_LANES = 128
VMEM_LIMIT = 56 * 1024 * 1024


def _cparams(sem):
    return pltpu.CompilerParams(dimension_semantics=sem, vmem_limit_bytes=VMEM_LIMIT)


def _sigmoid(x):
    return 1.0 / (1.0 + jnp.exp(-x))


def _silu(x):
    return x * _sigmoid(x)


def _log_sigmoid(x):
    return -(jnp.maximum(-x, 0.0) + jnp.log1p(jnp.exp(-jnp.abs(x))))


def _dot(a, b):
    return jnp.dot(a, b, preferred_element_type=F32)


def _dot_nt(a, b):
    return lax.dot_general(a, b, (((1,), (1,)), ((), ())), preferred_element_type=F32)


def _dot_tn(a, b):
    return lax.dot_general(a, b, (((0,), (0,)), ((), ())), preferred_element_type=F32)


def _split3(x):
    hi = x.astype(BF16)
    r1 = x - hi.astype(F32)
    mid = r1.astype(BF16)
    lo = (r1 - mid.astype(F32)).astype(BF16)
    return hi, mid, lo


def _dot3(a, x):
    hi, mid, lo = _split3(x)
    return _dot(a, hi) + _dot(a, mid) + _dot(a, lo)


def _dot3_nt(a, x):
    hi, mid, lo = _split3(x)
    return _dot_nt(a, hi) + _dot_nt(a, mid) + _dot_nt(a, lo)


def _rms(x, g):
    return x * lax.rsqrt(jnp.mean(x * x, axis=-1, keepdims=True) + EPS) * g


def _rms_matmul_kernel(x_ref, g_ref, w_ref, o_ref, xn_ref):
    @pl.when(pl.program_id(1) == 0)
    def _():
        xn_ref[...] = _rms(x_ref[...], g_ref[...]).astype(BF16)

    o_ref[...] = _dot(xn_ref[...], w_ref[...]).astype(o_ref.dtype)


def rms_matmul(x, g, w, *, tm, tn, out_dtype=F32):
    M, D = x.shape
    N = w.shape[1]
    assert M % tm == 0 and N % tn == 0
    return pl.pallas_call(
        _rms_matmul_kernel,
        out_shape=jax.ShapeDtypeStruct((M, N), out_dtype),
        grid=(M // tm, N // tn),
        in_specs=[pl.BlockSpec((tm, D), lambda i, j: (i, 0)),
                  pl.BlockSpec((1, D), lambda i, j: (0, 0)),
                  pl.BlockSpec((D, tn), lambda i, j: (0, j))],
        out_specs=pl.BlockSpec((tm, tn), lambda i, j: (i, j)),
        scratch_shapes=[pltpu.VMEM((tm, D), BF16)],
        compiler_params=_cparams(("parallel", "arbitrary")),
        name="rms_matmul",
    )(x, g.reshape(1, D), w)


def _matmul_res_kernel(a_ref, w_ref, x_ref, o_ref):
    o_ref[...] = x_ref[...] + _dot(a_ref[...].astype(BF16), w_ref[...])


def matmul_res(a, w, x, *, tm):
    M, K = a.shape
    N = w.shape[1]
    assert M % tm == 0
    return pl.pallas_call(
        _matmul_res_kernel,
        out_shape=jax.ShapeDtypeStruct((M, N), F32),
        grid=(M // tm,),
        in_specs=[pl.BlockSpec((tm, K), lambda i: (i, 0)),
                  pl.BlockSpec((K, N), lambda i: (0, 0)),
                  pl.BlockSpec((tm, N), lambda i: (i, 0))],
        out_specs=pl.BlockSpec((tm, N), lambda i: (i, 0)),
        compiler_params=_cparams(("parallel",)),
        name="matmul_res",
    )(a, w, x)


def _swiglu_kernel(x_ref, g_ref, wg_ref, wu_ref, wd_ref, gf_ref, o_ref, xn_ref, acc_ref, *, final_norm):
    j = pl.program_id(1)

    @pl.when(j == 0)
    def _():
        xn_ref[...] = _rms(x_ref[...], g_ref[...]).astype(BF16)
        acc_ref[...] = x_ref[...]

    xn = xn_ref[...]
    h = _silu(_dot(xn, wg_ref[...])) * _dot(xn, wu_ref[...])
    acc_ref[...] += _dot(h.astype(BF16), wd_ref[...])

    @pl.when(j == pl.num_programs(1) - 1)
    def _():
        y = acc_ref[...]
        if final_norm:
            y = _rms(y, gf_ref[...])
        o_ref[...] = y


def swiglu_res(x, g, wg, wu, wd, g_final, *, tm, tf, final_norm):
    M, D = x.shape
    FF = wg.shape[1]
    assert M % tm == 0 and FF % tf == 0
    return pl.pallas_call(
        functools.partial(_swiglu_kernel, final_norm=final_norm),
        out_shape=jax.ShapeDtypeStruct((M, D), F32),
        grid=(M // tm, FF // tf),
        in_specs=[pl.BlockSpec((tm, D), lambda i, j: (i, 0)),
                  pl.BlockSpec((1, D), lambda i, j: (0, 0)),
                  pl.BlockSpec((D, tf), lambda i, j: (0, j)),
                  pl.BlockSpec((D, tf), lambda i, j: (0, j)),
                  pl.BlockSpec((tf, D), lambda i, j: (j, 0)),
                  pl.BlockSpec((1, D), lambda i, j: (0, 0))],
        out_specs=pl.BlockSpec((tm, D), lambda i, j: (i, 0)),
        scratch_shapes=[pltpu.VMEM((tm, D), BF16), pltpu.VMEM((tm, D), F32)],
        compiler_params=_cparams(("parallel", "arbitrary")),
        name="swiglu_res",
    )(x, g.reshape(1, D), wg, wu, wd, g_final.reshape(1, D))


def _xattn_kernel(q_ref, k_ref, v_ref, o_ref):
    scale = X_HD ** -0.5
    for h in range(X_HEADS):
        sl = slice(h * X_HD, (h + 1) * X_HD)
        qh = q_ref[:, sl].astype(BF16)
        kh = k_ref[:, sl].astype(BF16)
        vh = v_ref[:, sl].astype(BF16)
        s = _dot_nt(qh, kh) * scale
        p = jnp.exp(s - jnp.max(s, axis=-1, keepdims=True))
        l = jnp.sum(p, axis=-1, keepdims=True)
        o_ref[:, sl] = _dot(p.astype(BF16), vh) / l


def xattn(q, mk, mv, *, tq):
    B, T, D = q.shape
    assert T % tq == 0
    return pl.pallas_call(
        _xattn_kernel,
        out_shape=jax.ShapeDtypeStruct((B, T, D), F32),
        grid=(B, T // tq),
        in_specs=[pl.BlockSpec((None, tq, D), lambda b, i: (b, i, 0)),
                  pl.BlockSpec((None, N_MEM, D), lambda b, i: (b, 0, 0)),
                  pl.BlockSpec((None, N_MEM, D), lambda b, i: (b, 0, 0))],
        out_specs=pl.BlockSpec((None, tq, D), lambda b, i: (b, i, 0)),
        compiler_params=_cparams(("parallel", "arbitrary")),
        name="xattn",
    )(q, mk, mv)


def _head_norm_store(mix_ref, col, o, g_row, gate):
    d = o.shape[1]
    y = o * lax.rsqrt(jnp.mean(o * o, axis=-1, keepdims=True) + EPS)
    mix_ref[:, col:col + d] = y * g_row * gate


def _mixer_kernel(proj_ref, cos_ref, sin_ref, rdecay_ref, qdec_ref, kdec_ref,
                  tri_ref, btri_ref, bones_ref, selr_ref, emat_ref, gmask_ref,
                  wga_ref, bga_ref, convw_ref, convb_ref, bsm_ref, gcat_ref,
                  sr0_ref, sg0_ref, c0_ref, n0_ref, m0_ref, conv0_ref,
                  mix_ref, sr_out, sg_out, c_out, n_out, m_out, conv_out,
                  sr_scr, sgt_scr, c_scr, n_scr, m_scr, conv_scr,
                  *, L, CS, chunk_decay):
    c = pl.program_id(1)
    nc = pl.num_programs(1)

    @pl.when(c == 0)
    def _():
        sr_scr[...] = sr0_ref[...]
        sgt_scr[...] = jnp.zeros_like(sgt_scr)
        for h in range(G_HEADS):
            sgt_scr[h * G_DV:(h + 1) * G_DV, h * G_DK:(h + 1) * G_DK] = sg0_ref[h]
        c_scr[...] = c0_ref[...]
        n_scr[0:M_HEADS, :] = n0_ref[...]
        m_scr[...] = m0_ref[...]
        conv_scr[5:8, :] = conv0_ref[...]

    row = lax.broadcasted_iota(jnp.int32, (L, L), 0)
    col = lax.broadcasted_iota(jnp.int32, (L, L), 1)
    causal = col <= row

    cos = jnp.concatenate([cos_ref[...]] * 3, axis=1)
    sin = jnp.concatenate([sin_ref[...]] * 3, axis=1)
    lane = lax.broadcasted_iota(jnp.int32, (L, V7X_LANES), 1)
    first_half = (lane % R_DK) < (R_DK // 2)

    def rope(x):
        parts = []
        for s in range(R_W // V7X_LANES):
            xs = x[:, s * V7X_LANES:(s + 1) * V7X_LANES]
            parts.append(jnp.where(first_half,
                                   pltpu.roll(xs, V7X_LANES - R_DK // 2, 1),
                                   pltpu.roll(xs, R_DK // 2, 1)))
        return x * cos + jnp.concatenate(parts, axis=1) * sin

    q_r = rope(proj_ref[:, C_QR:C_QR + R_W]) * (R_DK ** -0.5)
    k_r = rope(proj_ref[:, C_KR:C_KR + R_W])
    v_r = proj_ref[:, C_VR:C_VR + R_W].astype(BF16)
    gate_r = _silu(proj_ref[:, C_GR:C_GR + R_W])
    kd_r = (k_r * kdec_ref[...]).astype(BF16)
    q_rb = q_r.astype(BF16)
    k_rb = k_r.astype(BF16)
    qdec = qdec_ref[...]
    for h in range(R_HEADS):
        sl = slice(h * R_DK, (h + 1) * R_DK)
        qh, kh, vh = q_rb[:, sl], k_rb[:, sl], v_r[:, sl]
        s = _dot_nt(qh, kh) * rdecay_ref[h]
        o = _dot(s.astype(BF16), vh) + _dot(qh, sr_scr[h].astype(BF16)) * qdec[:, sl]
        sr_scr[h] = sr_scr[h] * chunk_decay[h] + _dot_tn(kd_r[:, sl], vh)
        _head_norm_store(mix_ref, h * R_DV, o, gcat_ref[:, h * R_DV:(h + 1) * R_DV], gate_r[:, sl])

    small = proj_ref[:, C_SM:C_SM + V7X_LANES]
    z = _dot(small.astype(BF16), wga_ref[...]) + bga_ref[...]
    log_a = _log_sigmoid(z) / G_NORMALIZER
    b_loc = _dot3(btri_ref[...], log_a)
    b_tot = _dot3(bones_ref[...], log_a)
    q_g = proj_ref[:, C_QG:C_QG + G_QK] * (G_DK ** -0.5)
    k_g = proj_ref[:, C_KG:C_KG + G_QK]
    v_g = proj_ref[:, C_VG:C_VG + G_W]
    v_gb = v_g.astype(BF16)
    gate_g = _silu(proj_ref[:, C_RG:C_RG + G_W])
    qt_g = (q_g * jnp.exp(b_loc)).astype(BF16)
    kt_g = (k_g * jnp.exp(b_tot - b_loc)).astype(BF16)
    d_tot = jnp.exp(b_tot)
    emat = emat_ref[...]
    gmask = gmask_ref[...]
    sub_row = lax.broadcasted_iota(jnp.int32, (CS, G_QK), 0)
    o_g_parts = []
    for blk in range(L // CS):
        r0 = blk * CS
        rs = slice(r0, r0 + CS)
        st = sgt_scr[...]
        o_inter = _dot_nt(qt_g[rs], st.astype(BF16))
        bI, qI, kI = b_loc[rs], q_g[rs], k_g[rs]
        terms = []
        for i in range(CS):
            e = jnp.exp(jnp.where(sub_row <= i, bI[i:i + 1] - bI, -jnp.inf))
            terms.append(e * kI * qI[i:i + 1])
        t = jnp.concatenate(terms, axis=0).astype(BF16)
        w = _dot(t, emat)
        w = w * jnp.concatenate([v_g[rs]] * CS, axis=0)
        o_diag = jnp.sum(w.reshape(CS, CS, G_W), axis=1)
        o_g_parts.append(o_inter + o_diag)
        kv = _dot_tn(v_gb[rs], kt_g[rs])
        sgt_scr[...] = st * d_tot[r0:r0 + 1] + kv * gmask
    o_g = jnp.concatenate(o_g_parts, axis=0) if len(o_g_parts) > 1 else o_g_parts[0]
    for h in range(G_HEADS):
        sl = slice(h * G_DV, (h + 1) * G_DV)
        _head_norm_store(mix_ref, R_W + h * G_DV, o_g[:, sl],
                         gcat_ref[:, R_W + h * G_DV:R_W + (h + 1) * G_DV], gate_g[:, sl])

    u = proj_ref[:, C_QKM:C_QKM + M_QK]
    conv_scr[8:8 + L, :] = u
    y = convb_ref[...]
    for j in range(CONV_W - 1):
        y = y + conv_scr[5 + j:5 + j + L, :] * convw_ref[j:j + 1, :]
    y = y + u * convw_ref[CONV_W - 1:CONV_W, :]
    tail = conv_scr[5 + L:8 + L, :]
    conv_scr[5:8, :] = tail
    qk = _silu(y)
    q_m = qk[:, :M_HEADS * M_DK]
    k_m = qk[:, M_HEADS * M_DK:] * (M_DK ** -0.5)
    q_mb = q_m.astype(BF16)
    k_mb = k_m.astype(BF16)
    v_m = proj_ref[:, C_VM:C_VM + M_W].astype(BF16)
    gate_m = _sigmoid(proj_ref[:, C_OM:C_OM + M_W])
    gates = small + bsm_ref[...]
    f_all = _dot3(tri_ref[...], _log_sigmoid(gates))
    i_rows = _dot3_nt(selr_ref[...], gates)
    f_rows = _dot3_nt(selr_ref[...], f_all)
    for h in range(M_HEADS):
        sl = slice(h * M_DK, (h + 1) * M_DK)
        fc = f_all[:, SM_F + h:SM_F + h + 1]
        ic = gates[:, SM_I + h:SM_I + h + 1]
        fr = f_rows[M_HEADS + h:M_HEADS + h + 1, :]
        ir = i_rows[h:h + 1, :]
        m_prev = m_scr[h][0:1, 0:1]
        a = fc + m_prev
        dm = jnp.where(causal, (fc - fr) + ir, -jnp.inf)
        m_tok = jnp.maximum(a, jnp.max(dm, axis=-1, keepdims=True))
        w_inter = jnp.exp(a - m_tok)
        qh, kh, vh = q_mb[:, sl], k_mb[:, sl], v_m[:, sl]
        s = _dot_nt(qh, kh) * jnp.exp(dm - m_tok)
        num = _dot(s.astype(BF16), vh) + _dot(qh, c_scr[h].astype(BF16)) * w_inter
        n_row = n_scr[h:h + 1, :]
        den = (jnp.sum(s, axis=-1, keepdims=True)
               + jnp.sum(q_m[:, sl] * n_row, axis=-1, keepdims=True) * w_inter)
        hh = num / jnp.maximum(jnp.abs(den), jnp.exp(-m_tok))
        m_new = m_tok[L - 1:L, :]
        wk = jnp.exp(fc[L - 1:L, :] - fc + ic - m_new)
        scale = jnp.exp(a[L - 1:L, :] - m_new)
        kw = k_m[:, sl] * wk
        c_scr[h] = c_scr[h] * scale + _dot_tn(kw.astype(BF16), vh)
        n_scr[h:h + 1, :] = n_row * scale + jnp.sum(kw, axis=0, keepdims=True)
        m_scr[h] = jnp.broadcast_to(m_new, m_scr.shape[1:])
        _head_norm_store(mix_ref, R_W + G_W + h * M_DV, hh,
                         gcat_ref[:, R_W + G_W + h * M_DV:R_W + G_W + (h + 1) * M_DV], gate_m[:, sl])

    @pl.when(c == nc - 1)
    def _():
        sr_out[...] = sr_scr[...]
        for h in range(G_HEADS):
            sg_out[h] = sgt_scr[h * G_DV:(h + 1) * G_DV, h * G_DK:(h + 1) * G_DK]
        c_out[...] = c_scr[...]
        n_out[...] = n_scr[0:M_HEADS, :]
        m_out[...] = m_scr[...]
        conv_out[...] = conv_scr[5:8, :]


def _mixer_tables(T, L, CS, pos0):
    half = R_DK // 2
    inv = ROPE_BASE ** (-jnp.arange(half, dtype=F32) * 2.0 / R_DK)
    pos = pos0 + jnp.arange(T, dtype=F32)
    ang = pos[:, None] * inv[None, :]
    cos = jnp.tile(jnp.cos(ang), (1, V7X_LANES // half))
    sin_h = jnp.sin(ang)
    sin = jnp.tile(jnp.concatenate([-sin_h, sin_h], axis=1), (1, V7X_LANES // R_DK))

    log_gamma = jnp.log(1.0 - 2.0 ** (-5.0 - jnp.arange(R_HEADS, dtype=F32)))
    idx = jnp.arange(L, dtype=F32)
    rel = idx[:, None] - idx[None, :]
    causal = rel >= 0
    rdecay = jnp.where(causal[None], jnp.exp(log_gamma[:, None, None] * jnp.where(causal, rel, 0.0)[None]), 0.0)
    qdec = jnp.repeat(jnp.exp(log_gamma[:, None] * (idx + 1.0)).T, R_DK, axis=1)
    kdec = jnp.repeat(jnp.exp(log_gamma[:, None] * (L - 1.0 - idx)).T, R_DK, axis=1)
    lg64 = np.log(1.0 - 2.0 ** (-5.0 - np.arange(R_HEADS, dtype=np.float64)))
    chunk_decay = tuple(float(np.float32(np.exp(np.float32(v) * np.float32(L)))) for v in lg64.astype(np.float32))

    r = np.arange(L)
    tri = (r[None, :] <= r[:, None])
    same = (r[None, :] // CS) == (r[:, None] // CS)
    selr = np.zeros((16, V7X_LANES), np.float32)
    for h in range(M_HEADS):
        selr[h, SM_I + h] = 1.0
        selr[M_HEADS + h, SM_F + h] = 1.0
    hv = np.arange(G_W) // G_DV
    hc = np.arange(G_QK) // G_DK
    gmask = (hv[:, None] == hc[None, :]).astype(np.float32)
    return dict(
        cos=cos, sin=sin, rdecay=rdecay, qdec=qdec, kdec=kdec, chunk_decay=chunk_decay,
        tri=jnp.asarray(tri, BF16), btri=jnp.asarray(tri & same, BF16), bones=jnp.asarray(same, BF16),
        selr=jnp.asarray(selr, BF16), emat=jnp.asarray(gmask.T, BF16), gmask=jnp.asarray(gmask, F32))


def mixer(proj, tabs, lw, state, *, L, CS):
    B, T, _ = proj.shape
    NC = T // L
    sr0, sg0, c0, n0, m0, conv0 = state
    const2 = lambda b, c: (0, 0)
    const3 = lambda b, c: (0, 0, 0)
    per_b3 = lambda b, c: (b, 0, 0)
    per_b4 = lambda b, c: (b, 0, 0, 0)
    in_specs = [
        pl.BlockSpec((None, L, D_IN_PAD), lambda b, c: (b, c, 0)),
        pl.BlockSpec((L, V7X_LANES), lambda b, c: (c, 0)),
        pl.BlockSpec((L, V7X_LANES), lambda b, c: (c, 0)),
        pl.BlockSpec((R_HEADS, L, L), const3),
        pl.BlockSpec((L, R_W), const2),
        pl.BlockSpec((L, R_W), const2),
        pl.BlockSpec((L, L), const2),
        pl.BlockSpec((L, L), const2),
        pl.BlockSpec((L, L), const2),
        pl.BlockSpec((16, V7X_LANES), const2),
        pl.BlockSpec((G_QK, G_W), const2),
        pl.BlockSpec((G_W, G_QK), const2),
        pl.BlockSpec((V7X_LANES, G_QK), const2),
        pl.BlockSpec((1, G_QK), const2),
        pl.BlockSpec((CONV_W, M_QK), const2),
        pl.BlockSpec((1, M_QK), const2),
        pl.BlockSpec((1, V7X_LANES), const2),
        pl.BlockSpec((1, D_MIX), const2),
        pl.BlockSpec((None, R_HEADS, R_DK, R_DV), per_b4),
        pl.BlockSpec((None, G_HEADS, G_DV, G_DK), per_b4),
        pl.BlockSpec((None, M_HEADS, M_DK, M_DV), per_b4),
        pl.BlockSpec((None, M_HEADS, M_DK), per_b3),
        pl.BlockSpec((None, M_HEADS, 8, V7X_LANES), per_b4),
        pl.BlockSpec((None, CONV_W - 1, M_QK), per_b3),
    ]
    out_shape = (
        jax.ShapeDtypeStruct((B, T, D_MIX), F32),
        jax.ShapeDtypeStruct((B, R_HEADS, R_DK, R_DV), F32),
        jax.ShapeDtypeStruct((B, G_HEADS, G_DV, G_DK), F32),
        jax.ShapeDtypeStruct((B, M_HEADS, M_DK, M_DV), F32),
        jax.ShapeDtypeStruct((B, M_HEADS, M_DK), F32),
        jax.ShapeDtypeStruct((B, M_HEADS, 8, V7X_LANES), F32),
        jax.ShapeDtypeStruct((B, CONV_W - 1, M_QK), F32),
    )
    out_specs = (
        pl.BlockSpec((None, L, D_MIX), lambda b, c: (b, c, 0)),
        pl.BlockSpec((None, R_HEADS, R_DK, R_DV), per_b4),
        pl.BlockSpec((None, G_HEADS, G_DV, G_DK), per_b4),
        pl.BlockSpec((None, M_HEADS, M_DK, M_DV), per_b4),
        pl.BlockSpec((None, M_HEADS, M_DK), per_b3),
        pl.BlockSpec((None, M_HEADS, 8, V7X_LANES), per_b4),
        pl.BlockSpec((None, CONV_W - 1, M_QK), per_b3),
    )
    scratch = [
        pltpu.VMEM((R_HEADS, R_DK, R_DV), F32),
        pltpu.VMEM((G_W, G_QK), F32),
        pltpu.VMEM((M_HEADS, M_DK, M_DV), F32),
        pltpu.VMEM((8, M_DK), F32),
        pltpu.VMEM((M_HEADS, 8, V7X_LANES), F32),
        pltpu.VMEM((8 + L, M_QK), F32),
    ]
    kern = functools.partial(_mixer_kernel, L=L, CS=CS, chunk_decay=tabs["chunk_decay"])
    return pl.pallas_call(
        kern,
        out_shape=out_shape,
        grid=(B, NC),
        in_specs=in_specs,
        out_specs=out_specs,
        scratch_shapes=scratch,
        compiler_params=_cparams(("parallel", "arbitrary")),
        name="mixer",
    )(proj, tabs["cos"], tabs["sin"], tabs["rdecay"], tabs["qdec"], tabs["kdec"],
      tabs["tri"], tabs["btri"], tabs["bones"], tabs["selr"], tabs["emat"], tabs["gmask"],
      lw["wga"], lw["bga"], lw["convw"], lw["convb"], lw["bsm"], lw["gcat"],
      sr0, sg0, c0, n0, m0, conv0)


def _prep_layer(l, g_mix, w_in, w_ga2, b_ga, conv_w, conv_b, b_i, b_f, g_ret, g_gla, g_mlstm, w_out,
                g_xattn, g_mem, w_xq, w_xk, w_xv, w_xo, g_ffn, w_gate, w_up, w_down):
    w = w_in[l]
    a0 = 2 * R_HEADS * R_DK + 2 * R_W + 2 * G_QK + 2 * G_W
    m0 = a0 + G_RANK
    g0 = m0 + M_QK + 2 * M_W
    w_pad = jnp.concatenate(
        [w[:, :a0], w[:, m0:g0], w[:, a0:m0], w[:, g0:g0 + 2 * M_HEADS],
         jnp.zeros((D_MODEL, D_IN_PAD - C_SM - G_RANK - 2 * M_HEADS), F32)], axis=1).astype(BF16)
    wga = jnp.zeros((V7X_LANES, G_QK), F32).at[SM_AG:SM_AG + G_RANK].set(w_ga2[l]).astype(BF16)
    bsm = (jnp.zeros((1, V7X_LANES), F32).at[0, SM_I:SM_I + M_HEADS].set(b_i[l])
           .at[0, SM_F:SM_F + M_HEADS].set(b_f[l]))
    return dict(
        g_mix=g_mix[l], w_in=w_pad, wga=wga, bga=b_ga[l].reshape(1, G_QK), convw=conv_w[l],
        convb=conv_b[l].reshape(1, M_QK), bsm=bsm,
        gcat=jnp.concatenate([g_ret[l], g_gla[l], g_mlstm[l]]).reshape(1, D_MIX),
        w_out=w_out[l].astype(BF16), g_xattn=g_xattn[l], g_mem=g_mem[l],
        w_xq=w_xq[l].astype(BF16), w_xk=w_xk[l].astype(BF16), w_xv=w_xv[l].astype(BF16),
        w_xo=w_xo[l].astype(BF16), g_ffn=g_ffn[l], w_gate=w_gate[l].astype(BF16),
        w_up=w_up[l].astype(BF16), w_down=w_down[l].astype(BF16))


def _layer(x, mk, mv, state, tabs, lw, g_final, *, L, CS, tm, tq, final_norm):
    B, T, D = x.shape
    M = B * T
    x2 = x.reshape(M, D)
    proj = rms_matmul(x2, lw["g_mix"], lw["w_in"], tm=tm, tn=D_IN_PAD // 3)
    outs = mixer(proj.reshape(B, T, D_IN_PAD), tabs, lw, state, L=L, CS=CS)
    mix, new_state = outs[0], outs[1:]
    x2 = matmul_res(mix.reshape(M, D_MIX), lw["w_out"], x2, tm=tm)
    q = rms_matmul(x2, lw["g_xattn"], lw["w_xq"], tm=tm, tn=D)
    o = xattn(q.reshape(B, T, D), mk, mv, tq=tq)
    x2 = matmul_res(o.reshape(M, D), lw["w_xo"], x2, tm=tm)
    x2 = swiglu_res(x2, lw["g_ffn"], lw["w_gate"], lw["w_up"], lw["w_down"], g_final,
                    tm=tm, tf=D_FF // 2, final_norm=final_norm)
    return x2.reshape(B, T, D), new_state


def _state_in(sr, sg, c, n, m, conv):
    B = sr.shape[0]
    return (sr, jnp.swapaxes(sg, -1, -2), c, n,
            jnp.broadcast_to(m[:, :, None, None], (B, M_HEADS, 8, V7X_LANES)), conv)


def _state_out(st):
    sr, sgt, c, n, m, conv = st
    return sr, jnp.swapaxes(sgt, -1, -2), c, n, m[:, :, 0, 0], conv


def kernel(x_prompt, x_sample, state_ret, state_gla, state_mlstm_C, state_mlstm_n, state_mlstm_m, state_mlstm_conv, cache_mem_k, cache_mem_v, mem_prompt, g_mix, w_in, w_ga2, b_ga, conv_w, conv_b, b_i, b_f, g_ret, g_gla, g_mlstm, w_out, g_xattn, g_mem, w_xq, w_xk, w_xv, w_xo, g_ffn, w_gate, w_up, w_down, g_final):
    B, T, D = x_prompt.shape
    Bs, Ts, _ = x_sample.shape
    depth = w_in.shape[0]
    Lp = CHUNK if T % CHUNK == 0 else T
    Ls = CHUNK if Ts % CHUNK == 0 else Ts
    CSp = 16 if Lp % 16 == 0 else Lp
    CSs = 16 if Ls % 16 == 0 else Ls
    tabs_p = _mixer_tables(T, Lp, CSp, 0.0)
    tabs_s = _mixer_tables(Ts, Ls, CSs, float(PAST_LEN))
    tm_p = 512 if (B * T) % 512 == 0 else B * T
    tm_s = 512 if (Bs * Ts) % 512 == 0 else Bs * Ts
    tq_p = 512 if T % 512 == 0 else T

    hp, hs = x_prompt, x_sample
    p_states, s_states, p_mk, p_mv = [], [], [], []
    mem2 = mem_prompt.reshape(B * N_MEM, D)
    for l in range(depth):
        lw = _prep_layer(l, g_mix, w_in, w_ga2, b_ga, conv_w, conv_b, b_i, b_f, g_ret, g_gla, g_mlstm,
                         w_out, g_xattn, g_mem, w_xq, w_xk, w_xv, w_xo, g_ffn, w_gate, w_up, w_down)
        last = l == depth - 1
        tmm = 512 if (B * N_MEM) % 512 == 0 else B * N_MEM
        mk = rms_matmul(mem2, lw["g_mem"], lw["w_xk"], tm=tmm, tn=D).reshape(B, N_MEM, D)
        mv = rms_matmul(mem2, lw["g_mem"], lw["w_xv"], tm=tmm, tn=D).reshape(B, N_MEM, D)
        zero_state = _state_in(
            jnp.zeros((B, R_HEADS, R_DK, R_DV), F32), jnp.zeros((B, G_HEADS, G_DK, G_DV), F32),
            jnp.zeros((B, M_HEADS, M_DK, M_DV), F32), jnp.zeros((B, M_HEADS, M_DK), F32),
            jnp.zeros((B, M_HEADS), F32), jnp.zeros((B, CONV_W - 1, M_QK), F32))
        hp, new_p = _layer(hp, mk, mv, zero_state, tabs_p, lw, g_final,
                           L=Lp, CS=CSp, tm=tm_p, tq=tq_p, final_norm=last)
        p_mk.append(mk.reshape(B, N_MEM, X_HEADS, X_HD))
        p_mv.append(mv.reshape(B, N_MEM, X_HEADS, X_HD))
        init_s = _state_in(state_ret[l], state_gla[l], state_mlstm_C[l], state_mlstm_n[l],
                           state_mlstm_m[l], state_mlstm_conv[l])
        hs, new_s = _layer(hs, cache_mem_k[l].reshape(Bs, N_MEM, D), cache_mem_v[l].reshape(Bs, N_MEM, D),
                           init_s, tabs_s, lw, g_final, L=Ls, CS=CSs, tm=tm_s, tq=Ts, final_norm=last)
        p_states.append(_state_out(new_p))
        s_states.append(_state_out(new_s))
    p_st = [jnp.stack([st[i] for st in p_states], axis=0) for i in range(6)]
    s_st = [jnp.stack([st[i] for st in s_states], axis=0) for i in range(6)]
    return (hp, hs, *p_st, jnp.stack(p_mk, axis=0), jnp.stack(p_mv, axis=0), *s_st)
```

```python
import functools
import math

import numpy as np
import jax
import jax.numpy as jnp
from jax import lax
from jax.experimental import pallas as pl
from jax.experimental.pallas import tpu as pltpu

F32 = jnp.float32
BF16 = jnp.bfloat16

D_MODEL = 1024
PAST_LEN = 16384
R_HEADS, R_DK, R_DV = 6, 64, 64
G_HEADS, G_DK, G_DV, G_RANK = 6, 32, 64, 16
G_NORMALIZER = 16.0
M_HEADS, M_DK, M_DV = 4, 64, 64
CONV_W = 4
X_HEADS = 4
X_HD = D_MODEL // X_HEADS
N_MEM = 256
D_FF = int(math.ceil(8 * D_MODEL / 3 / 256)) * 256
CHUNK = 128
EPS = 1e-6
ROPE_BASE = 10000.0

R_W = R_HEADS * R_DV
G_QK = G_HEADS * G_DK
G_W = G_HEADS * G_DV
M_QK = 2 * M_HEADS * M_DK
M_W = M_HEADS * M_DV
D_MIX = R_W + G_W + M_W

C_QR, C_KR, C_VR, C_GR = 0, 384, 768, 1152
C_QG, C_KG, C_VG, C_RG = 1536, 1728, 1920, 2304
C_QKM, C_VM, C_OM, C_SM = 2688, 3200, 3456, 3712
D_IN_PAD = 3840
SM_AG, SM_I, SM_F = 0, 16, 20

V7X_LANES = 128
VMEM_LIMIT = 56 * 1024 * 1024


def _cparams(sem):
    return pltpu.CompilerParams(dimension_semantics=sem, vmem_limit_bytes=VMEM_LIMIT)


def _sigmoid(x):
    return 1.0 / (1.0 + jnp.exp(-x))


def _silu(x):
    return x * _sigmoid(x)


def _log_sigmoid(x):
    return -(jnp.maximum(-x, 0.0) + jnp.log1p(jnp.exp(-jnp.abs(x))))


def _dot(a, b):
    return jnp.dot(a, b, preferred_element_type=F32)


def _dot_nt(a, b):
    return lax.dot_general(a, b, (((1,), (1,)), ((), ())), preferred_element_type=F32)


def _dot_tn(a, b):
    return lax.dot_general(a, b, (((0,), (0,)), ((), ())), preferred_element_type=F32)


def _split3(x):
    hi = x.astype(BF16)
    r1 = x - hi.astype(F32)
    mid = r1.astype(BF16)
    lo = (r1 - mid.astype(F32)).astype(BF16)
    return hi, mid, lo


def _dot3(a, x):
    hi, mid, lo = _split3(x)
    return _dot(a, hi) + _dot(a, mid) + _dot(a, lo)


def _dot3_nt(a, x):
    hi, mid, lo = _split3(x)
    return _dot_nt(a, hi) + _dot_nt(a, mid) + _dot_nt(a, lo)


def _rms(x, g):
    return x * lax.rsqrt(jnp.mean(x * x, axis=-1, keepdims=True) + EPS) * g


def _rms_matmul_kernel(x_ref, g_ref, w_ref, o_ref, xn_ref):
    @pl.when(pl.program_id(1) == 0)
    def _():
        xn_ref[...] = _rms(x_ref[...], g_ref[...]).astype(BF16)

    o_ref[...] = _dot(xn_ref[...], w_ref[...]).astype(o_ref.dtype)


def rms_matmul(x, g, w, *, tm, tn, out_dtype=F32):
    M, D = x.shape
    N = w.shape[1]
    assert M % tm == 0 and N % tn == 0
    return pl.pallas_call(
        _rms_matmul_kernel,
        out_shape=jax.ShapeDtypeStruct((M, N), out_dtype),
        grid=(M // tm, N // tn),
        in_specs=[pl.BlockSpec((tm, D), lambda i, j: (i, 0)),
                  pl.BlockSpec((1, D), lambda i, j: (0, 0)),
                  pl.BlockSpec((D, tn), lambda i, j: (0, j))],
        out_specs=pl.BlockSpec((tm, tn), lambda i, j: (i, j)),
        scratch_shapes=[pltpu.VMEM((tm, D), BF16)],
        compiler_params=_cparams(("parallel", "arbitrary")),
        name="rms_matmul",
    )(x, g.reshape(1, D), w)


def _matmul_res_kernel(a_ref, w_ref, x_ref, o_ref):
    o_ref[...] = x_ref[...] + _dot(a_ref[...].astype(BF16), w_ref[...])


def matmul_res(a, w, x, *, tm):
    M, K = a.shape
    N = w.shape[1]
    assert M % tm == 0
    return pl.pallas_call(
        _matmul_res_kernel,
        out_shape=jax.ShapeDtypeStruct((M, N), F32),
        grid=(M // tm,),
        in_specs=[pl.BlockSpec((tm, K), lambda i: (i, 0)),
                  pl.BlockSpec((K, N), lambda i: (0, 0)),
                  pl.BlockSpec((tm, N), lambda i: (i, 0))],
        out_specs=pl.BlockSpec((tm, N), lambda i: (i, 0)),
        compiler_params=_cparams(("parallel",)),
        name="matmul_res",
    )(a, w, x)


def _swiglu_kernel(x_ref, g_ref, wg_ref, wu_ref, wd_ref, gf_ref, o_ref, xn_ref, acc_ref, *, final_norm):
    j = pl.program_id(1)

    @pl.when(j == 0)
    def _():
        xn_ref[...] = _rms(x_ref[...], g_ref[...]).astype(BF16)
        acc_ref[...] = x_ref[...]

    xn = xn_ref[...]
    h = _silu(_dot(xn, wg_ref[...])) * _dot(xn, wu_ref[...])
    acc_ref[...] += _dot(h.astype(BF16), wd_ref[...])

    @pl.when(j == pl.num_programs(1) - 1)
    def _():
        y = acc_ref[...]
        if final_norm:
            y = _rms(y, gf_ref[...])
        o_ref[...] = y


def swiglu_res(x, g, wg, wu, wd, g_final, *, tm, tf, final_norm):
    M, D = x.shape
    FF = wg.shape[1]
    assert M % tm == 0 and FF % tf == 0
    return pl.pallas_call(
        functools.partial(_swiglu_kernel, final_norm=final_norm),
        out_shape=jax.ShapeDtypeStruct((M, D), F32),
        grid=(M // tm, FF // tf),
        in_specs=[pl.BlockSpec((tm, D), lambda i, j: (i, 0)),
                  pl.BlockSpec((1, D), lambda i, j: (0, 0)),
                  pl.BlockSpec((D, tf), lambda i, j: (0, j)),
                  pl.BlockSpec((D, tf), lambda i, j: (0, j)),
                  pl.BlockSpec((tf, D), lambda i, j: (j, 0)),
                  pl.BlockSpec((1, D), lambda i, j: (0, 0))],
        out_specs=pl.BlockSpec((tm, D), lambda i, j: (i, 0)),
        scratch_shapes=[pltpu.VMEM((tm, D), BF16), pltpu.VMEM((tm, D), F32)],
        compiler_params=_cparams(("parallel", "arbitrary")),
        name="swiglu_res",
    )(x, g.reshape(1, D), wg, wu, wd, g_final.reshape(1, D))


def _memkv_kernel(x_ref, g_ref, wk_ref, wv_ref, *refs):
    k5_ref, v5_ref, kb_ref, vb_ref = refs[-4:]
    xn = _rms(x_ref[...], g_ref[...]).astype(BF16)
    for w_ref, o5_ref, ob_ref in ((wk_ref, k5_ref, kb_ref), (wv_ref, v5_ref, vb_ref)):
        y = _dot(xn, w_ref[...])
        ob_ref[...] = y.astype(BF16)
        for h in range(X_HEADS):
            o5_ref[:, h, :] = y[:, h * X_HD:(h + 1) * X_HD]


def memory_kv(mem, g, wk, wv, l, depth, prev):
    B, _, D = mem.shape
    o5 = jax.ShapeDtypeStruct((depth, B, N_MEM, X_HEADS, X_HD), F32)
    ob = jax.ShapeDtypeStruct((B, N_MEM, D), BF16)
    in_specs = [pl.BlockSpec((None, N_MEM, D), lambda b: (b, 0, 0)),
                pl.BlockSpec((1, D), lambda b: (0, 0)),
                pl.BlockSpec((D, D), lambda b: (0, 0)),
                pl.BlockSpec((D, D), lambda b: (0, 0))]
    args = [mem, g.reshape(1, D), wk, wv]
    aliases = {}
    if prev is not None:
        in_specs += [pl.BlockSpec(memory_space=pl.ANY)] * 2
        args += list(prev)
        aliases = {4: 0, 5: 1}
    spec5 = pl.BlockSpec((None, None, N_MEM, X_HEADS, X_HD), lambda b: (l, b, 0, 0, 0))
    specb = pl.BlockSpec((None, N_MEM, D), lambda b: (b, 0, 0))
    return pl.pallas_call(
        _memkv_kernel,
        out_shape=(o5, o5, ob, ob),
        grid=(B,),
        in_specs=in_specs,
        out_specs=(spec5, spec5, specb, specb),
        input_output_aliases=aliases,
        compiler_params=_cparams(("parallel",)),
        name="memory_kv",
    )(*args)


def _xattn_kernel(q_ref, k_ref, v_ref, o_ref):
    scale = X_HD ** -0.5
    for h in range(X_HEADS):
        sl = slice(h * X_HD, (h + 1) * X_HD)
        qh = q_ref[:, sl].astype(BF16)
        s = _dot_nt(qh, k_ref[:, sl]) * scale
        p = jnp.exp(s - jnp.max(s, axis=-1, keepdims=True))
        l = jnp.sum(p, axis=-1, keepdims=True)
        o_ref[:, sl] = _dot(p.astype(BF16), v_ref[:, sl]) / l


def xattn(q, mk, mv, *, tq):
    B, T, D = q.shape
    assert T % tq == 0
    return pl.pallas_call(
        _xattn_kernel,
        out_shape=jax.ShapeDtypeStruct((B, T, D), F32),
        grid=(B, T // tq),
        in_specs=[pl.BlockSpec((None, tq, D), lambda b, i: (b, i, 0)),
                  pl.BlockSpec((None, N_MEM, D), lambda b, i: (b, 0, 0)),
                  pl.BlockSpec((None, N_MEM, D), lambda b, i: (b, 0, 0))],
        out_specs=pl.BlockSpec((None, tq, D), lambda b, i: (b, i, 0)),
        compiler_params=_cparams(("parallel", "arbitrary")),
        name="xattn",
    )(q, mk, mv)


def _xattn_cache_kernel(q_ref, k_ref, v_ref, o_ref, *, nb):
    T = q_ref.shape[1]
    R = X_HEADS * T
    rowh = lax.broadcasted_iota(jnp.int32, (R, N_MEM * X_HEADS), 0) // T
    colh = lax.broadcasted_iota(jnp.int32, (R, N_MEM * X_HEADS), 1) % X_HEADS
    own = rowh == colh
    for s in range(nb):
        q = q_ref[s]
        qf = jnp.concatenate([q[:, h * X_HD:(h + 1) * X_HD] for h in range(X_HEADS)], axis=0).astype(BF16)
        kf = k_ref[s].reshape(N_MEM * X_HEADS, X_HD).astype(BF16)
        vf = v_ref[s].reshape(N_MEM * X_HEADS, X_HD).astype(BF16)
        sc = jnp.where(own, _dot_nt(qf, kf) * (X_HD ** -0.5), -jnp.inf)
        p = jnp.exp(sc - jnp.max(sc, axis=-1, keepdims=True))
        l = jnp.sum(p, axis=-1, keepdims=True)
        o = _dot(p.astype(BF16), vf) / l
        for h in range(X_HEADS):
            o_ref[s, :, h * X_HD:(h + 1) * X_HD] = o[h * T:(h + 1) * T]


def xattn_cache(q, ck, cv, l, *, nb):
    B, T, D = q.shape
    assert B % nb == 0
    cspec = pl.BlockSpec((None, nb, N_MEM, X_HEADS, X_HD), lambda b: (l, b, 0, 0, 0))
    return pl.pallas_call(
        functools.partial(_xattn_cache_kernel, nb=nb),
        out_shape=jax.ShapeDtypeStruct((B, T, D), F32),
        grid=(B // nb,),
        in_specs=[pl.BlockSpec((nb, T, D), lambda b: (b, 0, 0)), cspec, cspec],
        out_specs=pl.BlockSpec((nb, T, D), lambda b: (b, 0, 0)),
        compiler_params=_cparams(("parallel",)),
        name="xattn_cache",
    )(q, ck, cv)


def _head_norm_store(mix_ref, s, col, o, g_row, gate):
    d = o.shape[1]
    y = o * lax.rsqrt(jnp.mean(o * o, axis=-1, keepdims=True) + EPS)
    mix_ref[s, :, col:col + d] = y * g_row * gate


def _eye(n):
    r = lax.broadcasted_iota(jnp.int32, (n, n), 0)
    c = lax.broadcasted_iota(jnp.int32, (n, n), 1)
    return jnp.where(r == c, 1.0, 0.0).astype(BF16)


def _mixer_kernel(proj_ref, cos_ref, sin_ref, rdecay_ref, qdec_ref, kdec_ref,
                  tri_ref, btri_ref, bones_ref, selr_ref, emat_ref, gmask_ref,
                  wga_ref, bga_ref, convw_ref, convb_ref, bsm_ref, gcat_ref,
                  sr0_ref, sg0_ref, c0_ref, n0_ref, m0_ref, conv0_ref, *rest,
                  L, CS, NB, NC, chunk_decay):
    (mix_ref, sr_out, sg_out, c_out, n_out, m_out, conv_out,
     sr_scr, sgt_scr, c_scr, n_scr, m_scr, conv_scr) = rest[-13:]
    c = pl.program_id(1)

    def load_state():
        eye_v = _eye(G_DV)
        for s in range(NB):
            sr_scr[s] = sr0_ref[s]
            sgt_scr[s] = jnp.zeros(sgt_scr.shape[1:], F32)
            for h in range(G_HEADS):
                sgt_scr[s, h * G_DV:(h + 1) * G_DV, h * G_DK:(h + 1) * G_DK] = _dot3_nt(eye_v, sg0_ref[s, h])
            c_scr[s] = c0_ref[s]
            n_scr[s, 0:M_HEADS, :] = n0_ref[s]
            m0 = m0_ref[s]
            for h in range(M_HEADS):
                m_scr[s, h] = jnp.broadcast_to(m0[0:1, h:h + 1], m_scr.shape[2:])
            conv_scr[s, 5:8, :] = conv0_ref[s]

    def store_state():
        eye_k = _eye(G_DK)
        for s in range(NB):
            sr_out[s] = sr_scr[s]
            for h in range(G_HEADS):
                sg_out[s, h] = _dot3_nt(eye_k, sgt_scr[s, h * G_DV:(h + 1) * G_DV, h * G_DK:(h + 1) * G_DK])
            c_out[s] = c_scr[s]
            n_out[s] = n_scr[s, 0:M_HEADS, :]
            for h in range(M_HEADS):
                m_out[s, 0:1, h:h + 1] = m_scr[s, h][0:1, 0:1]
            conv_out[s] = conv_scr[s, 5:8, :]

    if NC == 1:
        load_state()
    else:
        pl.when(c == 0)(load_state)

    row = lax.broadcasted_iota(jnp.int32, (L, L), 0)
    col = lax.broadcasted_iota(jnp.int32, (L, L), 1)
    causal = col <= row
    cos = jnp.concatenate([cos_ref[...]] * 3, axis=1)
    sin = jnp.concatenate([sin_ref[...]] * 3, axis=1)
    lane = lax.broadcasted_iota(jnp.int32, (L, V7X_LANES), 1)
    first_half = (lane % R_DK) < (R_DK // 2)
    sub_row = lax.broadcasted_iota(jnp.int32, (CS, G_QK), 0)

    def rope(x):
        parts = []
        for t in range(R_W // V7X_LANES):
            xs = x[:, t * V7X_LANES:(t + 1) * V7X_LANES]
            parts.append(jnp.where(first_half,
                                   pltpu.roll(xs, V7X_LANES - R_DK // 2, 1),
                                   pltpu.roll(xs, R_DK // 2, 1)))
        return x * cos + jnp.concatenate(parts, axis=1) * sin

    for s in range(NB):
        q_r = rope(proj_ref[s, :, C_QR:C_QR + R_W]) * (R_DK ** -0.5)
        k_r = rope(proj_ref[s, :, C_KR:C_KR + R_W])
        v_r = proj_ref[s, :, C_VR:C_VR + R_W].astype(BF16)
        gate_r = _silu(proj_ref[s, :, C_GR:C_GR + R_W])
        kd_r = (k_r * kdec_ref[...]).astype(BF16)
        q_rb = q_r.astype(BF16)
        k_rb = k_r.astype(BF16)
        qdec = qdec_ref[...]
        for h in range(R_HEADS):
            sl = slice(h * R_DK, (h + 1) * R_DK)
            qh, kh, vh = q_rb[:, sl], k_rb[:, sl], v_r[:, sl]
            sc = _dot_nt(qh, kh) * rdecay_ref[h]
            o = _dot(sc.astype(BF16), vh) + _dot(qh, sr_scr[s, h].astype(BF16)) * qdec[:, sl]
            sr_scr[s, h] = sr_scr[s, h] * chunk_decay[h] + _dot_tn(kd_r[:, sl], vh)
            _head_norm_store(mix_ref, s, h * R_DV, o, gcat_ref[:, h * R_DV:(h + 1) * R_DV], gate_r[:, sl])

        small = proj_ref[s, :, C_SM:C_SM + V7X_LANES]
        z = _dot(small.astype(BF16), wga_ref[...]) + bga_ref[...]
        log_a = _log_sigmoid(z) / G_NORMALIZER
        b_loc = _dot3(btri_ref[...], log_a)
        b_tot = _dot3(bones_ref[...], log_a)
        q_g = proj_ref[s, :, C_QG:C_QG + G_QK] * (G_DK ** -0.5)
        k_g = proj_ref[s, :, C_KG:C_KG + G_QK]
        v_g = proj_ref[s, :, C_VG:C_VG + G_W]
        v_gb = v_g.astype(BF16)
        gate_g = _silu(proj_ref[s, :, C_RG:C_RG + G_W])
        qt_g = (q_g * jnp.exp(b_loc)).astype(BF16)
        kt_g = (k_g * jnp.exp(b_tot - b_loc)).astype(BF16)
        d_tot = jnp.exp(b_tot)
        emat = emat_ref[...]
        gmask = gmask_ref[...]
        o_g_parts = []
        for blk in range(L // CS):
            r0 = blk * CS
            rs = slice(r0, r0 + CS)
            st = sgt_scr[s]
            o_inter = _dot_nt(qt_g[rs], st.astype(BF16))
            bI, qI, kI = b_loc[rs], q_g[rs], k_g[rs]
            terms = []
            for j in range(CS):
                e = jnp.exp(jnp.where(sub_row >= j, bI - bI[j:j + 1], -jnp.inf))
                terms.append(e * qI * kI[j:j + 1])
            t = jnp.concatenate(terms, axis=0).astype(BF16)
            w = _dot(t, emat)
            o_blk = o_inter
            for j in range(CS):
                o_blk = o_blk + w[j * CS:(j + 1) * CS] * v_g[r0 + j:r0 + j + 1]
            o_g_parts.append(o_blk)
            kv = _dot_tn(v_gb[rs], kt_g[rs])
            sgt_scr[s] = st * d_tot[r0:r0 + 1] + kv * gmask
        o_g = jnp.concatenate(o_g_parts, axis=0) if len(o_g_parts) > 1 else o_g_parts[0]
        for h in range(G_HEADS):
            sl = slice(h * G_DV, (h + 1) * G_DV)
            _head_norm_store(mix_ref, s, R_W + h * G_DV, o_g[:, sl],
                             gcat_ref[:, R_W + h * G_DV:R_W + (h + 1) * G_DV], gate_g[:, sl])

        u = proj_ref[s, :, C_QKM:C_QKM + M_QK]
        conv_scr[s, 8:8 + L, :] = u
        y = convb_ref[...]
        for j in range(CONV_W - 1):
            y = y + conv_scr[s, 5 + j:5 + j + L, :] * convw_ref[j:j + 1, :]
        y = y + u * convw_ref[CONV_W - 1:CONV_W, :]
        tail = conv_scr[s, 5 + L:8 + L, :]
        conv_scr[s, 5:8, :] = tail
        qk = _silu(y)
        q_m = qk[:, :M_HEADS * M_DK]
        k_m = qk[:, M_HEADS * M_DK:] * (M_DK ** -0.5)
        q_mb = q_m.astype(BF16)
        k_mb = k_m.astype(BF16)
        v_m = proj_ref[s, :, C_VM:C_VM + M_W].astype(BF16)
        gate_m = _sigmoid(proj_ref[s, :, C_OM:C_OM + M_W])
        gates = small + bsm_ref[...]
        f_all = _dot3(tri_ref[...], _log_sigmoid(gates))
        i_rows = _dot3_nt(selr_ref[...], gates)
        f_rows = _dot3_nt(selr_ref[...], f_all)
        for h in range(M_HEADS):
            sl = slice(h * M_DK, (h + 1) * M_DK)
            fc = f_all[:, SM_F + h:SM_F + h + 1]
            ic = gates[:, SM_I + h:SM_I + h + 1]
            fr = f_rows[M_HEADS + h:M_HEADS + h + 1, :]
            ir = i_rows[h:h + 1, :]
            m_prev = m_scr[s, h][0:1, 0:1]
            a = fc + m_prev
            dm = jnp.where(causal, (fc - fr) + ir, -jnp.inf)
            m_tok = jnp.maximum(a, jnp.max(dm, axis=-1, keepdims=True))
            w_inter = jnp.exp(a - m_tok)
            qh, kh, vh = q_mb[:, sl], k_mb[:, sl], v_m[:, sl]
            sc = _dot_nt(qh, kh) * jnp.exp(dm - m_tok)
            num = _dot(sc.astype(BF16), vh) + _dot(qh, c_scr[s, h].astype(BF16)) * w_inter
            n_row = n_scr[s, h:h + 1, :]
            den = (jnp.sum(sc, axis=-1, keepdims=True)
                   + jnp.sum(q_m[:, sl] * n_row, axis=-1, keepdims=True) * w_inter)
            hh = num / jnp.maximum(jnp.abs(den), jnp.exp(-m_tok))
            m_new = m_tok[L - 1:L, :]
            wk = jnp.exp(fc[L - 1:L, :] - fc + ic - m_new)
            scale = jnp.exp(a[L - 1:L, :] - m_new)
            kw = k_m[:, sl] * wk
            c_scr[s, h] = c_scr[s, h] * scale + _dot_tn(kw.astype(BF16), vh)
            n_scr[s, h:h + 1, :] = n_row * scale + jnp.sum(kw, axis=0, keepdims=True)
            m_scr[s, h] = jnp.broadcast_to(m_new, m_scr.shape[2:])
            _head_norm_store(mix_ref, s, R_W + G_W + h * M_DV, hh,
                             gcat_ref[:, R_W + G_W + h * M_DV:R_W + G_W + (h + 1) * M_DV], gate_m[:, sl])

    if NC == 1:
        store_state()
    else:
        pl.when(c == NC - 1)(store_state)


def _mixer_tables(T, L, CS, pos0):
    half = R_DK // 2
    inv = ROPE_BASE ** (-jnp.arange(half, dtype=F32) * 2.0 / R_DK)
    pos = pos0 + jnp.arange(T, dtype=F32)
    ang = pos[:, None] * inv[None, :]
    cos = jnp.tile(jnp.cos(ang), (1, V7X_LANES // half))
    sin_h = jnp.sin(ang)
    sin = jnp.tile(jnp.concatenate([-sin_h, sin_h], axis=1), (1, V7X_LANES // R_DK))

    log_gamma = jnp.log(1.0 - 2.0 ** (-5.0 - jnp.arange(R_HEADS, dtype=F32)))
    idx = jnp.arange(L, dtype=F32)
    rel = idx[:, None] - idx[None, :]
    causal = rel >= 0
    rdecay = jnp.where(causal[None], jnp.exp(log_gamma[:, None, None] * jnp.where(causal, rel, 0.0)[None]), 0.0)
    qdec = jnp.repeat(jnp.exp(log_gamma[:, None] * (idx + 1.0)).T, R_DK, axis=1)
    kdec = jnp.repeat(jnp.exp(log_gamma[:, None] * (L - 1.0 - idx)).T, R_DK, axis=1)
    lg32 = np.log(1.0 - 2.0 ** (-5.0 - np.arange(R_HEADS, dtype=np.float64))).astype(np.float32)
    chunk_decay = tuple(float(np.exp(v * np.float32(L))) for v in lg32)

    r = np.arange(L)
    tri = (r[None, :] <= r[:, None])
    same = (r[None, :] // CS) == (r[:, None] // CS)
    selr = np.zeros((16, V7X_LANES), np.float32)
    for h in range(M_HEADS):
        selr[h, SM_I + h] = 1.0
        selr[M_HEADS + h, SM_F + h] = 1.0
    hv = np.arange(G_W) // G_DV
    hc = np.arange(G_QK) // G_DK
    gmask = (hv[:, None] == hc[None, :]).astype(np.float32)
    return dict(
        cos=cos, sin=sin, rdecay=rdecay, qdec=qdec, kdec=kdec, chunk_decay=chunk_decay,
        tri=jnp.asarray(tri, BF16), btri=jnp.asarray(tri & same, BF16), bones=jnp.asarray(same, BF16),
        selr=jnp.asarray(selr, BF16), emat=jnp.asarray(gmask.T, BF16), gmask=jnp.asarray(gmask, F32))


_STATE_DIMS = ((R_HEADS, R_DK, R_DV), (G_HEADS, G_DK, G_DV), (M_HEADS, M_DK, M_DV), (M_HEADS, M_DK),
               (1, M_HEADS), (CONV_W - 1, M_QK))


def mixer(proj, tabs, lw, state, l_in, l_out, depth, prev, *, L, CS, NB):
    B, T, _ = proj.shape
    NC = T // L
    assert B % NB == 0
    const2 = lambda b, c: (0, 0)
    const3 = lambda b, c: (0, 0, 0)

    def st_spec(dims, l):
        return pl.BlockSpec((None, NB) + dims, lambda b, c: (l, b) + (0,) * len(dims))

    in_specs = [
        pl.BlockSpec((NB, L, D_IN_PAD), lambda b, c: (b, c, 0)),
        pl.BlockSpec((L, V7X_LANES), lambda b, c: (c, 0)),
        pl.BlockSpec((L, V7X_LANES), lambda b, c: (c, 0)),
        pl.BlockSpec((R_HEADS, L, L), const3),
        pl.BlockSpec((L, R_W), const2),
        pl.BlockSpec((L, R_W), const2),
        pl.BlockSpec((L, L), const2),
        pl.BlockSpec((L, L), const2),
        pl.BlockSpec((L, L), const2),
        pl.BlockSpec((16, V7X_LANES), const2),
        pl.BlockSpec((G_QK, G_W), const2),
        pl.BlockSpec((G_W, G_QK), const2),
        pl.BlockSpec((V7X_LANES, G_QK), const2),
        pl.BlockSpec((1, G_QK), const2),
        pl.BlockSpec((CONV_W, M_QK), const2),
        pl.BlockSpec((1, M_QK), const2),
        pl.BlockSpec((1, V7X_LANES), const2),
        pl.BlockSpec((1, D_MIX), const2),
    ] + [st_spec(d, l_in) for d in _STATE_DIMS]
    args = [proj, tabs["cos"], tabs["sin"], tabs["rdecay"], tabs["qdec"], tabs["kdec"],
            tabs["tri"], tabs["btri"], tabs["bones"], tabs["selr"], tabs["emat"], tabs["gmask"],
            lw["wga"], lw["bga"], lw["convw"], lw["convb"], lw["bsm"], lw["gcat"]] + list(state)
    aliases = {}
    if prev is not None:
        n_in = len(args)
        in_specs += [pl.BlockSpec(memory_space=pl.ANY)] * 6
        args += list(prev)
        aliases = {n_in + i: 1 + i for i in range(6)}
    out_shape = (jax.ShapeDtypeStruct((B, T, D_MIX), F32),) + tuple(
        jax.ShapeDtypeStruct((depth, B) + d, F32) for d in _STATE_DIMS)
    out_specs = (pl.BlockSpec((NB, L, D_MIX), lambda b, c: (b, c, 0)),) + tuple(
        st_spec(d, l_out) for d in _STATE_DIMS)
    scratch = [
        pltpu.VMEM((NB, R_HEADS, R_DK, R_DV), F32),
        pltpu.VMEM((NB, G_W, G_QK), F32),
        pltpu.VMEM((NB, M_HEADS, M_DK, M_DV), F32),
        pltpu.VMEM((NB, 8, M_DK), F32),
        pltpu.VMEM((NB, M_HEADS, 8, V7X_LANES), F32),
        pltpu.VMEM((NB, 8 + L, M_QK), F32),
    ]
    kern = functools.partial(_mixer_kernel, L=L, CS=CS, NB=NB, NC=NC, chunk_decay=tabs["chunk_decay"])
    outs = pl.pallas_call(
        kern,
        out_shape=out_shape,
        grid=(B // NB, NC),
        in_specs=in_specs,
        out_specs=out_specs,
        scratch_shapes=scratch,
        input_output_aliases=aliases,
        compiler_params=_cparams(("parallel", "arbitrary")),
        name="mixer",
    )(*args)
    return outs[0], tuple(outs[1:])


def _prep_layer(l, g_mix, w_in, w_ga2, b_ga, conv_w, conv_b, b_i, b_f, g_ret, g_gla, g_mlstm, w_out,
                g_xattn, g_mem, w_xq, w_xk, w_xv, w_xo, g_ffn, w_gate, w_up, w_down):
    w = w_in[l]
    a0 = 2 * R_HEADS * R_DK + 2 * R_W + 2 * G_QK + 2 * G_W
    m0 = a0 + G_RANK
    g0 = m0 + M_QK + 2 * M_W
    w_pad = jnp.concatenate(
        [w[:, :a0], w[:, m0:g0], w[:, a0:m0], w[:, g0:g0 + 2 * M_HEADS],
         jnp.zeros((D_MODEL, D_IN_PAD - C_SM - G_RANK - 2 * M_HEADS), F32)], axis=1).astype(BF16)
    wga = jnp.zeros((V7X_LANES, G_QK), F32).at[SM_AG:SM_AG + G_RANK].set(w_ga2[l]).astype(BF16)
    bsm = (jnp.zeros((1, V7X_LANES), F32).at[0, SM_I:SM_I + M_HEADS].set(b_i[l])
           .at[0, SM_F:SM_F + M_HEADS].set(b_f[l]))
    return dict(
        g_mix=g_mix[l], w_in=w_pad, wga=wga, bga=b_ga[l].reshape(1, G_QK), convw=conv_w[l],
        convb=conv_b[l].reshape(1, M_QK), bsm=bsm,
        gcat=jnp.concatenate([g_ret[l], g_gla[l], g_mlstm[l]]).reshape(1, D_MIX),
        w_out=w_out[l].astype(BF16), g_xattn=g_xattn[l], g_mem=g_mem[l],
        w_xq=w_xq[l].astype(BF16), w_xk=w_xk[l].astype(BF16), w_xv=w_xv[l].astype(BF16),
        w_xo=w_xo[l].astype(BF16), g_ffn=g_ffn[l], w_gate=w_gate[l].astype(BF16),
        w_up=w_up[l].astype(BF16), w_down=w_down[l].astype(BF16))


def _layer(x, attend, state, l_in, l_out, depth, prev, tabs, lw, g_final, *, L, CS, NB, tm, final_norm):
    B, T, D = x.shape
    M = B * T
    x2 = x.reshape(M, D)
    proj = rms_matmul(x2, lw["g_mix"], lw["w_in"], tm=tm, tn=D_IN_PAD // 3)
    mix, new_state = mixer(proj.reshape(B, T, D_IN_PAD), tabs, lw, state, l_in, l_out, depth, prev,
                           L=L, CS=CS, NB=NB)
    x2 = matmul_res(mix.reshape(M, D_MIX), lw["w_out"], x2, tm=tm)
    q = rms_matmul(x2, lw["g_xattn"], lw["w_xq"], tm=tm, tn=D)
    o = attend(q.reshape(B, T, D))
    x2 = matmul_res(o.reshape(M, D), lw["w_xo"], x2, tm=tm)
    x2 = swiglu_res(x2, lw["g_ffn"], lw["w_gate"], lw["w_up"], lw["w_down"], g_final,
                    tm=tm, tf=D_FF // 2, final_norm=final_norm)
    return x2.reshape(B, T, D), new_state


def kernel(x_prompt, x_sample, state_ret, state_gla, state_mlstm_C, state_mlstm_n, state_mlstm_m, state_mlstm_conv, cache_mem_k, cache_mem_v, mem_prompt, g_mix, w_in, w_ga2, b_ga, conv_w, conv_b, b_i, b_f, g_ret, g_gla, g_mlstm, w_out, g_xattn, g_mem, w_xq, w_xk, w_xv, w_xo, g_ffn, w_gate, w_up, w_down, g_final):
    B, T, D = x_prompt.shape
    Bs, Ts, _ = x_sample.shape
    depth = w_in.shape[0]
    Lp = CHUNK if T % CHUNK == 0 else T
    Ls = CHUNK if Ts % CHUNK == 0 else Ts
    CSp = 16 if Lp % 16 == 0 else Lp
    CSs = 16 if Ls % 16 == 0 else Ls
    tabs_p = _mixer_tables(T, Lp, CSp, 0.0)
    tabs_s = _mixer_tables(Ts, Ls, CSs, float(PAST_LEN))
    tm_p = 512 if (B * T) % 512 == 0 else B * T
    tm_s = 512 if (Bs * Ts) % 512 == 0 else Bs * Ts
    tq_p = 512 if T % 512 == 0 else T
    nb_s = 4 if Bs % 4 == 0 else 1
    nb_x = 8 if Bs % 8 == 0 else 1

    zero_state = tuple(jnp.zeros((1, B) + d, F32) for d in _STATE_DIMS)
    sample_state = (state_ret, state_gla, state_mlstm_C, state_mlstm_n,
                    state_mlstm_m.reshape(depth, Bs, 1, M_HEADS), state_mlstm_conv)
    hp, hs = x_prompt, x_sample
    p_st = s_st = p_mem = None
    for l in range(depth):
        lw = _prep_layer(l, g_mix, w_in, w_ga2, b_ga, conv_w, conv_b, b_i, b_f, g_ret, g_gla, g_mlstm,
                         w_out, g_xattn, g_mem, w_xq, w_xk, w_xv, w_xo, g_ffn, w_gate, w_up, w_down)
        last = l == depth - 1
        k5, v5, kb, vb = memory_kv(mem_prompt, lw["g_mem"], lw["w_xk"], lw["w_xv"], l, depth, p_mem)
        p_mem = (k5, v5)
        hp, p_st = _layer(hp, lambda q: xattn(q, kb, vb, tq=tq_p), zero_state, 0, l, depth, p_st,
                          tabs_p, lw, g_final, L=Lp, CS=CSp, NB=1, tm=tm_p, final_norm=last)
        hs, s_st = _layer(hs, lambda q: xattn_cache(q, cache_mem_k, cache_mem_v, l, nb=nb_x),
                          sample_state, l, l, depth, s_st,
                          tabs_s, lw, g_final, L=Ls, CS=CSs, NB=nb_s, tm=tm_s, final_norm=last)

    def fin(st, b):
        return st[:4] + (st[4].reshape(depth, b, M_HEADS), st[5])

    return (hp, hs, *fin(p_st, B), *p_mem, *fin(s_st, Bs))
```

```python
import functools
import math

import numpy as np
import jax
import jax.numpy as jnp
from jax import lax
from jax.experimental import pallas as pl
from jax.experimental.pallas import tpu as pltpu

F32 = jnp.float32
BF16 = jnp.bfloat16

D_MODEL = 1024
PAST_LEN = 16384
R_HEADS, R_DK, R_DV = 6, 64, 64
G_HEADS, G_DK, G_DV, G_RANK = 6, 32, 64, 16
G_NORMALIZER = 16.0
M_HEADS, M_DK, M_DV = 4, 64, 64
CONV_W = 4
X_HEADS = 4
X_HD = D_MODEL // X_HEADS
N_MEM = 256
D_FF = int(math.ceil(8 * D_MODEL / 3 / 256)) * 256
CHUNK = 128
EPS = 1e-6
ROPE_BASE = 10000.0

R_W = R_HEADS * R_DV
G_QK = G_HEADS * G_DK
G_W = G_HEADS * G_DV
M_QK = 2 * M_HEADS * M_DK
M_W = M_HEADS * M_DV
D_MIX = R_W + G_W + M_W

C_QR, C_KR, C_VR, C_GR = 0, 384, 768, 1152
C_QG, C_KG, C_VG, C_RG = 1536, 1728, 1920, 2304
C_QKM, C_VM, C_OM, C_SM = 2688, 3200, 3456, 3712
D_IN_PAD = 3840
SM_I, SM_F, SM_AG = 0, 4, 8
HEAD_W = 64
GLA_SAFE_LOG_RANGE = 60.0

V7X_LANES = 128
VMEM_LIMIT = 56 * 1024 * 1024


def _cparams(sem):
    return pltpu.CompilerParams(dimension_semantics=sem, vmem_limit_bytes=VMEM_LIMIT)


def _sigmoid(x):
    return 1.0 / (1.0 + jnp.exp(-x))


def _silu(x):
    return x * _sigmoid(x)


def _log_sigmoid(x):
    return jnp.minimum(x, 0.0) - jnp.log(1.0 + jnp.exp(-jnp.abs(x)))


def _dot(a, b):
    return jnp.dot(a, b, preferred_element_type=F32)


def _dot_nt(a, b):
    return lax.dot_general(a, b, (((1,), (1,)), ((), ())), preferred_element_type=F32)


def _dot_tn(a, b):
    return lax.dot_general(a, b, (((0,), (0,)), ((), ())), preferred_element_type=F32)


def _split3(x):
    hi = x.astype(BF16)
    r1 = x - hi.astype(F32)
    mid = r1.astype(BF16)
    lo = (r1 - mid.astype(F32)).astype(BF16)
    return hi, mid, lo


def _dot3(a, x):
    hi, mid, lo = _split3(x)
    return _dot(a, hi) + _dot(a, mid) + _dot(a, lo)


def _dot3_nt(a, x):
    hi, mid, lo = _split3(x)
    return _dot_nt(a, hi) + _dot_nt(a, mid) + _dot_nt(a, lo)


def _rms(x, g):
    return x * lax.rsqrt(jnp.mean(x * x, axis=-1, keepdims=True) + EPS) * g


def _rms_matmul_kernel(x_ref, g_ref, w_ref, o_ref, xn_ref):
    @pl.when(pl.program_id(1) == 0)
    def _():
        xn_ref[...] = _rms(x_ref[...], g_ref[...]).astype(BF16)

    o_ref[...] = _dot(xn_ref[...], w_ref[...]).astype(o_ref.dtype)


def rms_matmul(x, g, w, *, tm, tn, out_dtype=F32):
    M, D = x.shape
    N = w.shape[1]
    assert M % tm == 0 and N % tn == 0
    return pl.pallas_call(
        _rms_matmul_kernel,
        out_shape=jax.ShapeDtypeStruct((M, N), out_dtype),
        grid=(M // tm, N // tn),
        in_specs=[pl.BlockSpec((tm, D), lambda i, j: (i, 0)),
                  pl.BlockSpec((1, D), lambda i, j: (0, 0)),
                  pl.BlockSpec((D, tn), lambda i, j: (0, j))],
        out_specs=pl.BlockSpec((tm, tn), lambda i, j: (i, j)),
        scratch_shapes=[pltpu.VMEM((tm, D), BF16)],
        compiler_params=_cparams(("parallel", "arbitrary")),
        name="rms_matmul",
    )(x, g.reshape(1, D), w)


def _matmul_res_kernel(a_ref, w_ref, x_ref, o_ref):
    o_ref[...] = x_ref[...] + _dot(a_ref[...].astype(BF16), w_ref[...])


def matmul_res(a, w, x, *, tm):
    M, K = a.shape
    N = w.shape[1]
    assert M % tm == 0
    return pl.pallas_call(
        _matmul_res_kernel,
        out_shape=jax.ShapeDtypeStruct((M, N), F32),
        grid=(M // tm,),
        in_specs=[pl.BlockSpec((tm, K), lambda i: (i, 0)),
                  pl.BlockSpec((K, N), lambda i: (0, 0)),
                  pl.BlockSpec((tm, N), lambda i: (i, 0))],
        out_specs=pl.BlockSpec((tm, N), lambda i: (i, 0)),
        compiler_params=_cparams(("parallel",)),
        name="matmul_res",
    )(a, w, x)


def _swiglu_kernel(x_ref, g_ref, wg_ref, wu_ref, wd_ref, gf_ref, o_ref, xn_ref, acc_ref, *, final_norm):
    j = pl.program_id(1)

    @pl.when(j == 0)
    def _():
        xn_ref[...] = _rms(x_ref[...], g_ref[...]).astype(BF16)
        acc_ref[...] = x_ref[...]

    xn = xn_ref[...]
    h = _silu(_dot(xn, wg_ref[...])) * _dot(xn, wu_ref[...])
    acc_ref[...] += _dot(h.astype(BF16), wd_ref[...])

    @pl.when(j == pl.num_programs(1) - 1)
    def _():
        y = acc_ref[...]
        if final_norm:
            y = _rms(y, gf_ref[...])
        o_ref[...] = y


def swiglu_res(x, g, wg, wu, wd, g_final, *, tm, tf, final_norm):
    M, D = x.shape
    FF = wg.shape[1]
    assert M % tm == 0 and FF % tf == 0
    return pl.pallas_call(
        functools.partial(_swiglu_kernel, final_norm=final_norm),
        out_shape=jax.ShapeDtypeStruct((M, D), F32),
        grid=(M // tm, FF // tf),
        in_specs=[pl.BlockSpec((tm, D), lambda i, j: (i, 0)),
                  pl.BlockSpec((1, D), lambda i, j: (0, 0)),
                  pl.BlockSpec((D, tf), lambda i, j: (0, j)),
                  pl.BlockSpec((D, tf), lambda i, j: (0, j)),
                  pl.BlockSpec((tf, D), lambda i, j: (j, 0)),
                  pl.BlockSpec((1, D), lambda i, j: (0, 0))],
        out_specs=pl.BlockSpec((tm, D), lambda i, j: (i, 0)),
        scratch_shapes=[pltpu.VMEM((tm, D), BF16), pltpu.VMEM((tm, D), F32)],
        compiler_params=_cparams(("parallel", "arbitrary")),
        name="swiglu_res",
    )(x, g.reshape(1, D), wg, wu, wd, g_final.reshape(1, D))


def _memkv_kernel(x_ref, g_ref, wk_ref, wv_ref, *refs):
    k5_ref, v5_ref, kb_ref, vb_ref = refs[-4:]
    xn = _rms(x_ref[...], g_ref[...]).astype(BF16)
    for w_ref, o5_ref, ob_ref in ((wk_ref, k5_ref, kb_ref), (wv_ref, v5_ref, vb_ref)):
        y = _dot(xn, w_ref[...])
        ob_ref[...] = y.astype(BF16)
        for h in range(X_HEADS):
            o5_ref[:, h, :] = y[:, h * X_HD:(h + 1) * X_HD]


def memory_kv(mem, g, wk, wv, l, depth, prev):
    B, _, D = mem.shape
    o5 = jax.ShapeDtypeStruct((depth, B, N_MEM, X_HEADS, X_HD), F32)
    ob = jax.ShapeDtypeStruct((B, N_MEM, D), BF16)
    in_specs = [pl.BlockSpec((None, N_MEM, D), lambda b: (b, 0, 0)),
                pl.BlockSpec((1, D), lambda b: (0, 0)),
                pl.BlockSpec((D, D), lambda b: (0, 0)),
                pl.BlockSpec((D, D), lambda b: (0, 0))]
    args = [mem, g.reshape(1, D), wk, wv]
    aliases = {}
    if prev is not None:
        in_specs += [pl.BlockSpec(memory_space=pl.ANY)] * 2
        args += list(prev)
        aliases = {4: 0, 5: 1}
    spec5 = pl.BlockSpec((None, None, N_MEM, X_HEADS, X_HD), lambda b: (l, b, 0, 0, 0))
    specb = pl.BlockSpec((None, N_MEM, D), lambda b: (b, 0, 0))
    return pl.pallas_call(
        _memkv_kernel,
        out_shape=(o5, o5, ob, ob),
        grid=(B,),
        in_specs=in_specs,
        out_specs=(spec5, spec5, specb, specb),
        input_output_aliases=aliases,
        compiler_params=_cparams(("parallel",)),
        name="memory_kv",
    )(*args)


def _xattn_kernel(q_ref, k_ref, v_ref, o_ref):
    scale = X_HD ** -0.5
    for h in range(X_HEADS):
        sl = slice(h * X_HD, (h + 1) * X_HD)
        qh = q_ref[:, sl].astype(BF16)
        s = _dot_nt(qh, k_ref[:, sl]) * scale
        p = jnp.exp(s - jnp.max(s, axis=-1, keepdims=True))
        l = jnp.sum(p, axis=-1, keepdims=True)
        o_ref[:, sl] = _dot(p.astype(BF16), v_ref[:, sl]) / l


def xattn(q, mk, mv, *, tq):
    B, T, D = q.shape
    assert T % tq == 0
    return pl.pallas_call(
        _xattn_kernel,
        out_shape=jax.ShapeDtypeStruct((B, T, D), F32),
        grid=(B, T // tq),
        in_specs=[pl.BlockSpec((None, tq, D), lambda b, i: (b, i, 0)),
                  pl.BlockSpec((None, N_MEM, D), lambda b, i: (b, 0, 0)),
                  pl.BlockSpec((None, N_MEM, D), lambda b, i: (b, 0, 0))],
        out_specs=pl.BlockSpec((None, tq, D), lambda b, i: (b, i, 0)),
        compiler_params=_cparams(("parallel", "arbitrary")),
        name="xattn",
    )(q, mk, mv)


def _xattn_cache_kernel(q_ref, k_ref, v_ref, o_ref, *, nb):
    T = q_ref.shape[1]
    R = X_HEADS * T
    rowh = lax.broadcasted_iota(jnp.int32, (R, N_MEM * X_HEADS), 0) // T
    colh = lax.broadcasted_iota(jnp.int32, (R, N_MEM * X_HEADS), 1) % X_HEADS
    own = rowh == colh
    for s in range(nb):
        q = q_ref[s]
        qf = jnp.concatenate([q[:, h * X_HD:(h + 1) * X_HD] for h in range(X_HEADS)], axis=0).astype(BF16)
        kf = k_ref[s].reshape(N_MEM * X_HEADS, X_HD).astype(BF16)
        vf = v_ref[s].reshape(N_MEM * X_HEADS, X_HD).astype(BF16)
        sc = jnp.where(own, _dot_nt(qf, kf) * (X_HD ** -0.5), -jnp.inf)
        p = jnp.exp(sc - jnp.max(sc, axis=-1, keepdims=True))
        l = jnp.sum(p, axis=-1, keepdims=True)
        o = _dot(p.astype(BF16), vf) / l
        for h in range(X_HEADS):
            o_ref[s, :, h * X_HD:(h + 1) * X_HD] = o[h * T:(h + 1) * T]


def xattn_cache(q, ck, cv, l, *, nb):
    B, T, D = q.shape
    assert B % nb == 0
    cspec = pl.BlockSpec((None, nb, N_MEM, X_HEADS, X_HD), lambda b: (l, b, 0, 0, 0))
    return pl.pallas_call(
        functools.partial(_xattn_cache_kernel, nb=nb),
        out_shape=jax.ShapeDtypeStruct((B, T, D), F32),
        grid=(B // nb,),
        in_specs=[pl.BlockSpec((nb, T, D), lambda b: (b, 0, 0)), cspec, cspec],
        out_specs=pl.BlockSpec((nb, T, D), lambda b: (b, 0, 0)),
        compiler_params=_cparams(("parallel",)),
        name="xattn_cache",
    )(q, ck, cv)


def _head_norm_store(mix_ref, s, col, o, g_row, gate):
    d = o.shape[1]
    y = o * lax.rsqrt(jnp.mean(o * o, axis=-1, keepdims=True) + EPS)
    mix_ref[s, :, col:col + d] = y * g_row * gate


def _eye(n):
    r = lax.broadcasted_iota(jnp.int32, (n, n), 0)
    c = lax.broadcasted_iota(jnp.int32, (n, n), 1)
    return jnp.where(r == c, 1.0, 0.0).astype(BF16)


def _mixer_kernel(proj_ref, cos_ref, sin_ref, rdecay_ref, qdec_ref, kdec_ref,
                  tri_ref, btri_ref, bones_ref, selr_ref, emat_ref, gmask_ref,
                  wga_ref, bga_ref, convw_ref, convb_ref, bsm_ref, gcat_ref,
                  sr0_ref, sg0_ref, c0_ref, n0_ref, m0_ref, conv0_ref, *rest,
                  L, CS, NB, NC, chunk_decay):
    (mix_ref, sr_out, sg_out, c_out, n_out, m_out, conv_out,
     sr_scr, sgt_scr, c_scr, n_scr, m_scr, conv_scr) = rest[-13:]
    c = pl.program_id(1)

    def load_state():
        eye_v = _eye(G_DV)
        for s in range(NB):
            sr_scr[s] = sr0_ref[s]
            sgt_scr[s] = jnp.zeros(sgt_scr.shape[1:], F32)
            for h in range(G_HEADS):
                sgt_scr[s, h * G_DV:(h + 1) * G_DV, h * G_DK:(h + 1) * G_DK] = _dot3_nt(eye_v, sg0_ref[s, h])
            c_scr[s] = c0_ref[s]
            n_scr[s, 0:M_HEADS, :] = n0_ref[s]
            m0 = m0_ref[s]
            for h in range(M_HEADS):
                m_scr[s, h] = jnp.broadcast_to(m0[0:1, h:h + 1], m_scr.shape[2:])
            conv_scr[s, 5:8, :] = conv0_ref[s]

    def store_state():
        eye_k = _eye(G_DK)
        for s in range(NB):
            sr_out[s] = sr_scr[s]
            for h in range(G_HEADS):
                sg_out[s, h] = _dot3_nt(eye_k, sgt_scr[s, h * G_DV:(h + 1) * G_DV, h * G_DK:(h + 1) * G_DK])
            c_out[s] = c_scr[s]
            n_out[s] = n_scr[s, 0:M_HEADS, :]
            for h in range(M_HEADS):
                m_out[s, 0:1, h:h + 1] = m_scr[s, h][0:1, 0:1]
            conv_out[s] = conv_scr[s, 5:8, :]

    if NC == 1:
        load_state()
    else:
        pl.when(c == 0)(load_state)

    row = lax.broadcasted_iota(jnp.int32, (L, L), 0)
    col = lax.broadcasted_iota(jnp.int32, (L, L), 1)
    causal = col <= row
    cos = jnp.concatenate([cos_ref[...]] * 3, axis=1)
    sin = jnp.concatenate([sin_ref[...]] * 3, axis=1)
    lane = lax.broadcasted_iota(jnp.int32, (L, V7X_LANES), 1)
    first_half = (lane % R_DK) < (R_DK // 2)
    sub_row = lax.broadcasted_iota(jnp.int32, (CS, G_QK), 0)

    def rope(x):
        parts = []
        for t in range(R_W // V7X_LANES):
            xs = x[:, t * V7X_LANES:(t + 1) * V7X_LANES]
            parts.append(jnp.where(first_half,
                                   pltpu.roll(xs, V7X_LANES - R_DK // 2, 1),
                                   pltpu.roll(xs, R_DK // 2, 1)))
        return x * cos + jnp.concatenate(parts, axis=1) * sin

    for s in range(NB):
        q_r = rope(proj_ref[s, :, C_QR:C_QR + R_W]) * (R_DK ** -0.5)
        k_r = rope(proj_ref[s, :, C_KR:C_KR + R_W])
        v_r = proj_ref[s, :, C_VR:C_VR + R_W].astype(BF16)
        gate_r = _silu(proj_ref[s, :, C_GR:C_GR + R_W])
        kd_r = (k_r * kdec_ref[...]).astype(BF16)
        q_rb = q_r.astype(BF16)
        k_rb = k_r.astype(BF16)
        qdec = qdec_ref[...]
        for h in range(R_HEADS):
            sl = slice(h * R_DK, (h + 1) * R_DK)
            qh, kh, vh = q_rb[:, sl], k_rb[:, sl], v_r[:, sl]
            sc = _dot_nt(qh, kh) * rdecay_ref[h]
            o = _dot(sc.astype(BF16), vh) + _dot(qh, sr_scr[s, h].astype(BF16)) * qdec[:, sl]
            sr_scr[s, h] = sr_scr[s, h] * chunk_decay[h] + _dot_tn(kd_r[:, sl], vh)
            _head_norm_store(mix_ref, s, h * R_DV, o, gcat_ref[:, h * R_DV:(h + 1) * R_DV], gate_r[:, sl])

        small = proj_ref[s, :, C_SM:C_SM + V7X_LANES]
        z = _dot(small.astype(BF16), wga_ref[...]) + bga_ref[...]
        log_a = _log_sigmoid(z) / G_NORMALIZER
        b_loc = _dot3(btri_ref[...], log_a)
        b_tot = _dot3(bones_ref[...], log_a)
        q_g = proj_ref[s, :, C_QG:C_QG + G_QK] * (G_DK ** -0.5)
        k_g = proj_ref[s, :, C_KG:C_KG + G_QK]
        v_g = proj_ref[s, :, C_VG:C_VG + G_W]
        v_gb = v_g.astype(BF16)
        gate_g = _silu(proj_ref[s, :, C_RG:C_RG + G_W])
        qt_g = (q_g * jnp.exp(b_loc)).astype(BF16)
        kt_g = (k_g * jnp.exp(b_tot - b_loc)).astype(BF16)
        d_tot = jnp.exp(b_tot)
        emat = emat_ref[...]
        gmask = gmask_ref[...]
        o_g_parts = []
        for blk in range(L // CS):
            r0 = blk * CS
            rs = slice(r0, r0 + CS)
            st = sgt_scr[s]
            o_inter = _dot_nt(qt_g[rs], st.astype(BF16))
            bI, qI, kI = b_loc[rs], q_g[rs], k_g[rs]
            terms = []
            for j in range(CS):
                e = jnp.exp(jnp.where(sub_row >= j, bI - bI[j:j + 1], -jnp.inf))
                terms.append(e * qI * kI[j:j + 1])
            t = jnp.concatenate(terms, axis=0).astype(BF16)
            w = _dot(t, emat)
            o_blk = o_inter
            for j in range(CS):
                o_blk = o_blk + w[j * CS:(j + 1) * CS] * v_g[r0 + j:r0 + j + 1]
            o_g_parts.append(o_blk)
            kv = _dot_tn(v_gb[rs], kt_g[rs])
            sgt_scr[s] = st * d_tot[r0:r0 + 1] + kv * gmask
        o_g = jnp.concatenate(o_g_parts, axis=0) if len(o_g_parts) > 1 else o_g_parts[0]
        for h in range(G_HEADS):
            sl = slice(h * G_DV, (h + 1) * G_DV)
            _head_norm_store(mix_ref, s, R_W + h * G_DV, o_g[:, sl],
                             gcat_ref[:, R_W + h * G_DV:R_W + (h + 1) * G_DV], gate_g[:, sl])

        u = proj_ref[s, :, C_QKM:C_QKM + M_QK]
        conv_scr[s, 8:8 + L, :] = u
        y = convb_ref[...]
        for j in range(CONV_W - 1):
            y = y + conv_scr[s, 5 + j:5 + j + L, :] * convw_ref[j:j + 1, :]
        y = y + u * convw_ref[CONV_W - 1:CONV_W, :]
        tail = conv_scr[s, 5 + L:8 + L, :]
        conv_scr[s, 5:8, :] = tail
        qk = _silu(y)
        q_m = qk[:, :M_HEADS * M_DK]
        k_m = qk[:, M_HEADS * M_DK:] * (M_DK ** -0.5)
        q_mb = q_m.astype(BF16)
        k_mb = k_m.astype(BF16)
        v_m = proj_ref[s, :, C_VM:C_VM + M_W].astype(BF16)
        gate_m = _sigmoid(proj_ref[s, :, C_OM:C_OM + M_W])
        gates = small + bsm_ref[...]
        f_all = _dot3(tri_ref[...], _log_sigmoid(gates))
        i_rows = _dot3_nt(selr_ref[...], gates)
        f_rows = _dot3_nt(selr_ref[...], f_all)
        for h in range(M_HEADS):
            sl = slice(h * M_DK, (h + 1) * M_DK)
            fc = f_all[:, SM_F + h:SM_F + h + 1]
            ic = gates[:, SM_I + h:SM_I + h + 1]
            fr = f_rows[M_HEADS + h:M_HEADS + h + 1, :]
            ir = i_rows[h:h + 1, :]
            m_prev = m_scr[s, h][0:1, 0:1]
            a = fc + m_prev
            dm = jnp.where(causal, (fc - fr) + ir, -jnp.inf)
            m_tok = jnp.maximum(a, jnp.max(dm, axis=-1, keepdims=True))
            w_inter = jnp.exp(a - m_tok)
            qh, kh, vh = q_mb[:, sl], k_mb[:, sl], v_m[:, sl]
            sc = _dot_nt(qh, kh) * jnp.exp(dm - m_tok)
            num = _dot(sc.astype(BF16), vh) + _dot(qh, c_scr[s, h].astype(BF16)) * w_inter
            n_row = n_scr[s, h:h + 1, :]
            den = (jnp.sum(sc, axis=-1, keepdims=True)
                   + jnp.sum(q_m[:, sl] * n_row, axis=-1, keepdims=True) * w_inter)
            hh = num / jnp.maximum(jnp.abs(den), jnp.exp(-m_tok))
            m_new = m_tok[L - 1:L, :]
            wk = jnp.exp(fc[L - 1:L, :] - fc + ic - m_new)
            scale = jnp.exp(a[L - 1:L, :] - m_new)
            kw = k_m[:, sl] * wk
            c_scr[s, h] = c_scr[s, h] * scale + _dot_tn(kw.astype(BF16), vh)
            n_scr[s, h:h + 1, :] = n_row * scale + jnp.sum(kw, axis=0, keepdims=True)
            m_scr[s, h] = jnp.broadcast_to(m_new, m_scr.shape[2:])
            _head_norm_store(mix_ref, s, R_W + G_W + h * M_DV, hh,
                             gcat_ref[:, R_W + G_W + h * M_DV:R_W + G_W + (h + 1) * M_DV], gate_m[:, sl])

    if NC == 1:
        store_state()
    else:
        pl.when(c == NC - 1)(store_state)


def _mixer_tables(T, L, CS, pos0):
    half = R_DK // 2
    inv = ROPE_BASE ** (-jnp.arange(half, dtype=F32) * 2.0 / R_DK)
    pos = pos0 + jnp.arange(T, dtype=F32)
    ang = pos[:, None] * inv[None, :]
    cos = jnp.tile(jnp.cos(ang), (1, V7X_LANES // half))
    sin_h = jnp.sin(ang)
    sin = jnp.tile(jnp.concatenate([-sin_h, sin_h], axis=1), (1, V7X_LANES // R_DK))

    log_gamma = jnp.log(1.0 - 2.0 ** (-5.0 - jnp.arange(R_HEADS, dtype=F32)))
    idx = jnp.arange(L, dtype=F32)
    rel = idx[:, None] - idx[None, :]
    causal = rel >= 0
    rdecay = jnp.where(causal[None], jnp.exp(log_gamma[:, None, None] * jnp.where(causal, rel, 0.0)[None]), 0.0)
    qdec = jnp.repeat(jnp.exp(log_gamma[:, None] * (idx + 1.0)).T, R_DK, axis=1)
    kdec = jnp.repeat(jnp.exp(log_gamma[:, None] * (L - 1.0 - idx)).T, R_DK, axis=1)
    lg32 = np.log(1.0 - 2.0 ** (-5.0 - np.arange(R_HEADS, dtype=np.float64))).astype(np.float32)
    chunk_decay = tuple(float(np.exp(v * np.float32(L))) for v in lg32)

    r = np.arange(L)
    tri = (r[None, :] <= r[:, None])
    same = (r[None, :] // CS) == (r[:, None] // CS)
    selr = np.zeros((16, V7X_LANES), np.float32)
    for h in range(M_HEADS):
        selr[h, SM_I + h] = 1.0
        selr[M_HEADS + h, SM_F + h] = 1.0
    hv = np.arange(G_W) // G_DV
    hc = np.arange(G_QK) // G_DK
    gmask = (hv[:, None] == hc[None, :]).astype(np.float32)
    return dict(
        cos=cos, sin=sin, rdecay=rdecay, qdec=qdec, kdec=kdec, chunk_decay=chunk_decay,
        tri=jnp.asarray(tri, BF16), btri=jnp.asarray(tri & same, BF16), bones=jnp.asarray(same, BF16),
        selr=jnp.asarray(selr, BF16), emat=jnp.asarray(gmask.T, BF16), gmask=jnp.asarray(gmask, F32))


_STATE_DIMS = ((R_HEADS, R_DK, R_DV), (G_HEADS, G_DK, G_DV), (M_HEADS, M_DK, M_DV), (M_HEADS, M_DK),
               (1, M_HEADS), (CONV_W - 1, M_QK))


def mixer(proj, tabs, lw, state, l_in, l_out, depth, prev, *, L, CS, NB):
    B, T, _ = proj.shape
    NC = T // L
    assert B % NB == 0
    const2 = lambda b, c: (0, 0)
    const3 = lambda b, c: (0, 0, 0)

    def st_spec(dims, l):
        return pl.BlockSpec((None, NB) + dims, lambda b, c: (l, b) + (0,) * len(dims))

    in_specs = [
        pl.BlockSpec((NB, L, D_IN_PAD), lambda b, c: (b, c, 0)),
        pl.BlockSpec((L, V7X_LANES), lambda b, c: (c, 0)),
        pl.BlockSpec((L, V7X_LANES), lambda b, c: (c, 0)),
        pl.BlockSpec((R_HEADS, L, L), const3),
        pl.BlockSpec((L, R_W), const2),
        pl.BlockSpec((L, R_W), const2),
        pl.BlockSpec((L, L), const2),
        pl.BlockSpec((L, L), const2),
        pl.BlockSpec((L, L), const2),
        pl.BlockSpec((16, V7X_LANES), const2),
        pl.BlockSpec((G_QK, G_W), const2),
        pl.BlockSpec((G_W, G_QK), const2),
        pl.BlockSpec((V7X_LANES, G_QK), const2),
        pl.BlockSpec((1, G_QK), const2),
        pl.BlockSpec((CONV_W, M_QK), const2),
        pl.BlockSpec((1, M_QK), const2),
        pl.BlockSpec((1, V7X_LANES), const2),
        pl.BlockSpec((1, D_MIX), const2),
    ] + [st_spec(d, l_in) for d in _STATE_DIMS]
    args = [proj, tabs["cos"], tabs["sin"], tabs["rdecay"], tabs["qdec"], tabs["kdec"],
            tabs["tri"], tabs["btri"], tabs["bones"], tabs["selr"], tabs["emat"], tabs["gmask"],
            lw["wga"], lw["bga"], lw["convw"], lw["convb"], lw["bsm"], lw["gcat"]] + list(state)
    aliases = {}
    if prev is not None:
        n_in = len(args)
        in_specs += [pl.BlockSpec(memory_space=pl.ANY)] * 6
        args += list(prev)
        aliases = {n_in + i: 1 + i for i in range(6)}
    out_shape = (jax.ShapeDtypeStruct((B, T, D_MIX), F32),) + tuple(
        jax.ShapeDtypeStruct((depth, B) + d, F32) for d in _STATE_DIMS)
    out_specs = (pl.BlockSpec((NB, L, D_MIX), lambda b, c: (b, c, 0)),) + tuple(
        st_spec(d, l_out) for d in _STATE_DIMS)
    scratch = [
        pltpu.VMEM((NB, R_HEADS, R_DK, R_DV), F32),
        pltpu.VMEM((NB, G_W, G_QK), F32),
        pltpu.VMEM((NB, M_HEADS, M_DK, M_DV), F32),
        pltpu.VMEM((NB, 8, M_DK), F32),
        pltpu.VMEM((NB, M_HEADS, 8, V7X_LANES), F32),
        pltpu.VMEM((NB, 8 + L, M_QK), F32),
    ]
    kern = functools.partial(_mixer_kernel, L=L, CS=CS, NB=NB, NC=NC, chunk_decay=tabs["chunk_decay"])
    outs = pl.pallas_call(
        kern,
        out_shape=out_shape,
        grid=(B // NB, NC),
        in_specs=in_specs,
        out_specs=out_specs,
        scratch_shapes=scratch,
        input_output_aliases=aliases,
        compiler_params=_cparams(("parallel", "arbitrary")),
        name="mixer",
    )(*args)
    return outs[0], tuple(outs[1:])


def _split2(x):
    hi = x.astype(BF16)
    lo = (x - hi.astype(F32)).astype(BF16)
    return hi, lo


def _head_mean_sq(o, hmat):
    hi, lo = _split2(o * o)
    return (_dot(hi, hmat) + _dot(lo, hmat)) * (1.0 / HEAD_W)


def _norm_gate(o, hmat, g_row, gate):
    return o * lax.rsqrt(_head_mean_sq(o, hmat) + EPS) * g_row * gate


def _lane_pick(cols, idx_lo, idx_hi, lo_mask):
    L = cols.shape[0]
    a = jnp.broadcast_to(cols[:, idx_lo:idx_lo + 1], (L, V7X_LANES))
    b = jnp.broadcast_to(cols[:, idx_hi:idx_hi + 1], (L, V7X_LANES))
    return jnp.where(lo_mask, a, b)


def _stack_masked(x, m_a, m_b):
    z = jnp.zeros_like(x)
    return jnp.concatenate([jnp.where(m_a, x, z), jnp.where(m_b, x, z)], axis=0)


def _mixer_prompt_kernel(proj_ref, cosq_ref, sinq_ref, cosk_ref, sink_ref, rdec_ref, qdec_ref, kdec_ref,
                         tri_ref, btri_ref, bones_ref, selr_ref, hmat_ref, emat_ref, gmask_ref, pmask_ref,
                         wga_ref, bga_ref, convw_ref, convb_ref, bsm_ref, gcat_ref, *rest,
                         L, CS, NC, chunk_decay):
    (mix_ref, sr_out, sg_out, c_out, n_out, m_out, conv_out,
     srp_scr, sgt_scr, cp_scr, n_scr, m_scr, conv_scr, og_scr) = rest[-14:]
    c = pl.program_id(1)

    @pl.when(c == 0)
    def _():
        srp_scr[...] = jnp.zeros_like(srp_scr)
        sgt_scr[...] = jnp.zeros_like(sgt_scr)
        cp_scr[...] = jnp.zeros_like(cp_scr)
        n_scr[...] = jnp.zeros_like(n_scr)
        m_scr[...] = jnp.zeros_like(m_scr)
        conv_scr[0:8, :] = jnp.zeros((8, M_QK), F32)

    lane = lax.broadcasted_iota(jnp.int32, (L, V7X_LANES), 1)
    lo = lane < HEAD_W
    hi_m = lane >= HEAD_W
    first_half = (lane % R_DK) < (R_DK // 2)
    row2 = lax.broadcasted_iota(jnp.int32, (L, 2 * L), 0)
    col2 = lax.broadcasted_iota(jnp.int32, (L, 2 * L), 1) % L
    causal2 = col2 <= row2
    lo_row = lax.broadcasted_iota(jnp.int32, (1, V7X_LANES), 1) < HEAD_W
    lo8 = lax.broadcasted_iota(jnp.int32, (8, V7X_LANES), 1) < HEAD_W
    hmat = hmat_ref[...]
    pmask = pmask_ref[...]

    def rope(x, cos_ref, sin_ref):
        cos = cos_ref[...]
        sin = sin_ref[...]
        parts = []
        for t in range(R_W // V7X_LANES):
            xs = x[:, t * V7X_LANES:(t + 1) * V7X_LANES]
            rot = jnp.where(first_half, pltpu.roll(xs, V7X_LANES - R_DK // 2, 1), pltpu.roll(xs, R_DK // 2, 1))
            parts.append(xs * cos + rot * sin)
        return jnp.concatenate(parts, axis=1)

    q_r = rope(proj_ref[:, C_QR:C_QR + R_W], cosq_ref, sinq_ref)
    k_r = rope(proj_ref[:, C_KR:C_KR + R_W], cosk_ref, sink_ref)
    q_rb = q_r.astype(BF16)
    k_rb = k_r.astype(BF16)
    kd_rb = (k_r * kdec_ref[...]).astype(BF16)
    v_rb = proj_ref[:, C_VR:C_VR + R_W].astype(BF16)
    o_parts = []
    for p in range(R_HEADS // 2):
        sl = slice(p * V7X_LANES, (p + 1) * V7X_LANES)
        qs, ks, vs = q_rb[:, sl], k_rb[:, sl], v_rb[:, sl]
        sc = _dot_nt(qs, _stack_masked(ks, lo, hi_m)) * rdec_ref[p]
        sp = srp_scr[p]
        o = _dot(sc.astype(BF16), _stack_masked(vs, lo, hi_m)) + _dot(qs, sp.astype(BF16)) * qdec_ref[:, sl]
        cd = jnp.where(lo_row, chunk_decay[2 * p], chunk_decay[2 * p + 1])
        srp_scr[p] = sp * cd + _dot_tn(kd_rb[:, sl], vs) * pmask
        o_parts.append(o)
    o_r = jnp.concatenate(o_parts, axis=1)
    gate_r = _silu(proj_ref[:, C_GR:C_GR + R_W])
    mix_ref[:, 0:R_W] = _norm_gate(o_r, hmat, gcat_ref[:, 0:R_W], gate_r)

    small = proj_ref[:, C_SM:C_SM + V7X_LANES]
    z = _dot(small.astype(BF16), wga_ref[...]) + bga_ref[...]
    log_a = _log_sigmoid(z) / G_NORMALIZER
    b = _dot3(tri_ref[...], log_a)
    b_last = b[L - 1:L, :]
    safe = jnp.max(-b_last) <= GLA_SAFE_LOG_RANGE
    q_g = proj_ref[:, C_QG:C_QG + G_QK] * (G_DK ** -0.5)
    k_g = proj_ref[:, C_KG:C_KG + G_QK]
    v_gb = proj_ref[:, C_VG:C_VG + G_W].astype(BF16)

    @pl.when(safe)
    def _():
        qt = (q_g * jnp.exp(b)).astype(BF16)
        kt = (k_g * jnp.exp(-b)).astype(BF16)
        kl = (k_g * jnp.exp(b_last - b)).astype(BF16)
        st = sgt_scr[...]
        o_inter = _dot_nt(qt, st.astype(BF16))
        lane_q = lax.broadcasted_iota(jnp.int32, (L, V7X_LANES), 1) // G_DK
        parts = []
        pad = jnp.zeros((L, 2 * V7X_LANES - G_QK), BF16)
        qt_p = jnp.concatenate([qt, pad], axis=1)
        kt_p = jnp.concatenate([kt, pad], axis=1)
        for g in range(G_HEADS // 2):
            s0 = (2 * g * G_DK) // V7X_LANES * V7X_LANES
            hA = (2 * g * G_DK - s0) // G_DK
            qs, ks = qt_p[:, s0:s0 + V7X_LANES], kt_p[:, s0:s0 + V7X_LANES]
            sc = _dot_nt(qs, _stack_masked(ks, lane_q == hA, lane_q == hA + 1))
            sc = jnp.where(causal2, sc, 0.0).astype(BF16)
            vs = v_gb[:, g * V7X_LANES:(g + 1) * V7X_LANES]
            parts.append(_dot(sc, _stack_masked(vs, lo, hi_m)))
        og_scr[...] = o_inter + jnp.concatenate(parts, axis=1)
        sgt_scr[...] = st * jnp.exp(b_last) + _dot_tn(v_gb, kl) * gmask_ref[...]

    @pl.when(jnp.logical_not(safe))
    def _():
        v_g = proj_ref[:, C_VG:C_VG + G_W]
        b_loc = _dot3(btri_ref[...], log_a)
        b_tot = _dot3(bones_ref[...], log_a)
        qt_g = (q_g * jnp.exp(b_loc)).astype(BF16)
        kt_g = (k_g * jnp.exp(b_tot - b_loc)).astype(BF16)
        d_tot = jnp.exp(b_tot)
        emat = emat_ref[...]
        gmask = gmask_ref[...]
        sub_row = lax.broadcasted_iota(jnp.int32, (CS, G_QK), 0)
        for blk in range(L // CS):
            r0 = blk * CS
            rs = slice(r0, r0 + CS)
            st = sgt_scr[...]
            o_blk = _dot_nt(qt_g[rs], st.astype(BF16))
            bI, qI, kI = b_loc[rs], q_g[rs], k_g[rs]
            terms = []
            for j in range(CS):
                e = jnp.exp(jnp.where(sub_row >= j, bI - bI[j:j + 1], -jnp.inf))
                terms.append(e * qI * kI[j:j + 1])
            t = jnp.concatenate(terms, axis=0).astype(BF16)
            w = _dot(t, emat)
            for j in range(CS):
                o_blk = o_blk + w[j * CS:(j + 1) * CS] * v_g[r0 + j:r0 + j + 1]
            og_scr[rs, :] = o_blk
            sgt_scr[...] = st * d_tot[r0:r0 + 1] + _dot_tn(v_gb[rs], kt_g[rs]) * gmask

    gate_g = _silu(proj_ref[:, C_RG:C_RG + G_W])
    mix_ref[:, R_W:R_W + G_W] = _norm_gate(og_scr[...], hmat, gcat_ref[:, R_W:R_W + G_W], gate_g)

    u = proj_ref[:, C_QKM:C_QKM + M_QK]
    conv_scr[8:8 + L, :] = u
    y = convb_ref[...]
    for j in range(CONV_W - 1):
        y = y + conv_scr[5 + j:5 + j + L, :] * convw_ref[j:j + 1, :]
    y = y + u * convw_ref[CONV_W - 1:CONV_W, :]
    tail = conv_scr[5 + L:8 + L, :]
    conv_scr[5:8, :] = tail
    qk = _silu(y)
    q_m = qk[:, :M_W]
    k_m = qk[:, M_W:] * (M_DK ** -0.5)
    q_mb = q_m.astype(BF16)
    k_mb = k_m.astype(BF16)
    v_mb = proj_ref[:, C_VM:C_VM + M_W].astype(BF16)
    gates = small + bsm_ref[...]
    f_cum = _dot3(tri_ref[...], _log_sigmoid(gates))
    i_rows = _dot3_nt(selr_ref[...], gates)
    f_rows = _dot3_nt(selr_ref[...], f_cum)
    head_lane = (lane >= SM_I) & (lane < SM_I + M_HEADS)
    fc = jnp.where(head_lane, pltpu.roll(f_cum, V7X_LANES - (SM_F - SM_I), 1), 0.0)
    m_prev = m_scr[0:1, :]
    a = fc + m_prev
    mx = jnp.full((L, V7X_LANES), -jnp.inf, F32)
    row = lax.broadcasted_iota(jnp.int32, (L, L), 0)
    col = lax.broadcasted_iota(jnp.int32, (L, L), 1)
    causal = col <= row
    dms = []
    for h in range(M_HEADS):
        dm = jnp.where(causal, (fc[:, SM_I + h:SM_I + h + 1] - f_rows[M_HEADS + h:M_HEADS + h + 1, :])
                       + i_rows[h:h + 1, :], -jnp.inf)
        dms.append(dm)
        mx = jnp.where(lane == SM_I + h, jnp.max(dm, axis=-1, keepdims=True), mx)
    m_tok = jnp.maximum(a, mx)
    w_inter = jnp.exp(a - m_tok)
    n_full = n_scr[0:1, :]
    hs, lows = _split2(q_m * n_full)
    qn_b = _dot(hs, hmat_ref[0:M_W, 0:M_W]) + _dot(lows, hmat_ref[0:M_W, 0:M_W])
    rs_all = jnp.zeros((L, V7X_LANES), F32)
    qn_all = jnp.zeros((L, V7X_LANES), F32)
    scs = []
    for p in range(M_HEADS // 2):
        sl = slice(p * V7X_LANES, (p + 1) * V7X_LANES)
        sc = _dot_nt(q_mb[:, sl], _stack_masked(k_mb[:, sl], lo, hi_m))
        mt_a = jnp.broadcast_to(m_tok[:, SM_I + 2 * p:SM_I + 2 * p + 1], (L, L))
        mt_b = jnp.broadcast_to(m_tok[:, SM_I + 2 * p + 1:SM_I + 2 * p + 2], (L, L))
        sc_a = sc[:, :L] * jnp.exp(dms[2 * p] - mt_a)
        sc_b = sc[:, L:] * jnp.exp(dms[2 * p + 1] - mt_b)
        rs_all = jnp.where(lane == SM_I + 2 * p, jnp.sum(sc_a, axis=-1, keepdims=True), rs_all)
        rs_all = jnp.where(lane == SM_I + 2 * p + 1, jnp.sum(sc_b, axis=-1, keepdims=True), rs_all)
        qn_all = jnp.where(lane == SM_I + 2 * p, qn_b[:, p * V7X_LANES:p * V7X_LANES + 1], qn_all)
        qn_all = jnp.where(lane == SM_I + 2 * p + 1, qn_b[:, p * V7X_LANES + HEAD_W:p * V7X_LANES + HEAD_W + 1], qn_all)
        scs.append(jnp.concatenate([sc_a, sc_b], axis=1).astype(BF16))
    den = rs_all + qn_all * w_inter
    inv = 1.0 / jnp.maximum(jnp.abs(den), jnp.exp(-m_tok))
    m_new = jnp.where(head_lane, m_tok, 0.0)[L - 1:L, :]
    wk = jnp.exp(fc[L - 1:L, :] - fc + gates - m_new)
    scale = jnp.broadcast_to(jnp.exp(a[L - 1:L, :] - m_new), (8, V7X_LANES))
    h_parts = []
    kw_parts = []
    scale_parts = []
    for p in range(M_HEADS // 2):
        sl = slice(p * V7X_LANES, (p + 1) * V7X_LANES)
        ia, ib = SM_I + 2 * p, SM_I + 2 * p + 1
        cpair = cp_scr[p]
        num = (_dot(scs[p], _stack_masked(v_mb[:, sl], lo, hi_m))
               + _dot(q_mb[:, sl], cpair.astype(BF16)) * _lane_pick(w_inter, ia, ib, lo))
        h_parts.append(num * _lane_pick(inv, ia, ib, lo))
        kw = k_m[:, sl] * _lane_pick(wk, ia, ib, lo)
        kw_parts.append(kw)
        sc_row = _lane_pick(scale, ia, ib, lo8)[0:1]
        scale_parts.append(sc_row)
        cp_scr[p] = cpair * sc_row + _dot_tn(kw.astype(BF16), v_mb[:, sl]) * pmask
    kw_all = jnp.concatenate(kw_parts, axis=1)
    n_new = n_full * jnp.concatenate(scale_parts, axis=1) + jnp.sum(kw_all, axis=0, keepdims=True)
    n_scr[...] = jnp.broadcast_to(n_new, n_scr.shape)
    m_scr[...] = jnp.broadcast_to(m_new, m_scr.shape)
    gate_m = _sigmoid(proj_ref[:, C_OM:C_OM + M_W])
    mix_ref[:, R_W + G_W:D_MIX] = _norm_gate(jnp.concatenate(h_parts, axis=1), hmat_ref[0:M_W, 0:M_W],
                                             gcat_ref[:, R_W + G_W:D_MIX], gate_m)

    @pl.when(c == NC - 1)
    def _():
        eye_k = _eye(G_DK)
        for p in range(R_HEADS // 2):
            sr_out[2 * p] = srp_scr[p, 0:HEAD_W, 0:HEAD_W]
            sr_out[2 * p + 1] = srp_scr[p, HEAD_W:, HEAD_W:]
        for h in range(G_HEADS):
            sg_out[h] = _dot3_nt(eye_k, sgt_scr[h * G_DV:(h + 1) * G_DV, h * G_DK:(h + 1) * G_DK])
        for p in range(M_HEADS // 2):
            c_out[2 * p] = cp_scr[p, 0:HEAD_W, 0:HEAD_W]
            c_out[2 * p + 1] = cp_scr[p, HEAD_W:, HEAD_W:]
        for h in range(M_HEADS):
            n_out[h:h + 1, :] = n_scr[0:1, h * M_DK:(h + 1) * M_DK]
        m_out[...] = m_scr[0:1, SM_I:SM_I + M_HEADS]
        conv_out[...] = conv_scr[5:8, :]


def _prompt_tables(T, L, CS):
    half = R_DK // 2
    inv = ROPE_BASE ** (-jnp.arange(half, dtype=F32) * 2.0 / R_DK)
    pos = jnp.arange(T, dtype=F32)
    ang = pos[:, None] * inv[None, :]
    cos = jnp.tile(jnp.cos(ang), (1, V7X_LANES // half))
    sin_h = jnp.sin(ang)
    sin = jnp.tile(jnp.concatenate([-sin_h, sin_h], axis=1), (1, V7X_LANES // R_DK))
    qs = R_DK ** -0.5

    log_gamma = jnp.log(1.0 - 2.0 ** (-5.0 - jnp.arange(R_HEADS, dtype=F32)))
    idx = jnp.arange(L, dtype=F32)
    rel = idx[:, None] - idx[None, :]
    causal = rel >= 0
    rdecay = jnp.where(causal[None], jnp.exp(log_gamma[:, None, None] * jnp.where(causal, rel, 0.0)[None]), 0.0)
    rdec = jnp.concatenate([rdecay[0::2], rdecay[1::2]], axis=2)
    qdec = jnp.repeat(jnp.exp(log_gamma[:, None] * (idx + 1.0)).T, R_DK, axis=1)
    kdec = jnp.repeat(jnp.exp(log_gamma[:, None] * (L - 1.0 - idx)).T, R_DK, axis=1)
    lg32 = np.log(1.0 - 2.0 ** (-5.0 - np.arange(R_HEADS, dtype=np.float64))).astype(np.float32)
    chunk_decay = tuple(float(np.exp(v * np.float32(L))) for v in lg32)

    r = np.arange(L)
    tri = (r[None, :] <= r[:, None])
    same = (r[None, :] // CS) == (r[:, None] // CS)
    selr = np.zeros((16, V7X_LANES), np.float32)
    for h in range(M_HEADS):
        selr[h, SM_I + h] = 1.0
        selr[M_HEADS + h, SM_F + h] = 1.0
    hv = np.arange(G_W) // G_DV
    hc = np.arange(G_QK) // G_DK
    gmask = (hv[:, None] == hc[None, :]).astype(np.float32)
    hmat = (hv[:, None] == hv[None, :]).astype(np.float32)
    pm = np.arange(V7X_LANES) // HEAD_W
    pmask = (pm[:, None] == pm[None, :]).astype(np.float32)
    return dict(
        cosq=cos * qs, sinq=sin * qs, cosk=cos, sink=sin, rdec=rdec, qdec=qdec, kdec=kdec, chunk_decay=chunk_decay,
        tri=jnp.asarray(tri, BF16), btri=jnp.asarray(tri & same, BF16), bones=jnp.asarray(same, BF16),
        selr=jnp.asarray(selr, BF16), hmat=jnp.asarray(hmat, BF16), emat=jnp.asarray(gmask.T, BF16),
        gmask=jnp.asarray(gmask, F32), pmask=jnp.asarray(pmask, F32))


def mixer_prompt(proj, tabs, lw, l_out, depth, prev, *, L, CS):
    B, T, _ = proj.shape
    NC = T // L
    const2 = lambda b, c: (0, 0)
    const3 = lambda b, c: (0, 0, 0)
    tspec = pl.BlockSpec((L, V7X_LANES), lambda b, c: (c, 0))
    in_specs = [
        pl.BlockSpec((None, L, D_IN_PAD), lambda b, c: (b, c, 0)),
        tspec, tspec, tspec, tspec,
        pl.BlockSpec((R_HEADS // 2, L, 2 * L), const3),
        pl.BlockSpec((L, R_W), const2),
        pl.BlockSpec((L, R_W), const2),
        pl.BlockSpec((L, L), const2),
        pl.BlockSpec((L, L), const2),
        pl.BlockSpec((L, L), const2),
        pl.BlockSpec((16, V7X_LANES), const2),
        pl.BlockSpec((G_W, G_W), const2),
        pl.BlockSpec((G_QK, G_W), const2),
        pl.BlockSpec((G_W, G_QK), const2),
        pl.BlockSpec((V7X_LANES, V7X_LANES), const2),
        pl.BlockSpec((V7X_LANES, G_QK), const2),
        pl.BlockSpec((1, G_QK), const2),
        pl.BlockSpec((CONV_W, M_QK), const2),
        pl.BlockSpec((1, M_QK), const2),
        pl.BlockSpec((1, V7X_LANES), const2),
        pl.BlockSpec((1, D_MIX), const2),
    ]
    args = [proj, tabs["cosq"], tabs["sinq"], tabs["cosk"], tabs["sink"], tabs["rdec"], tabs["qdec"], tabs["kdec"],
            tabs["tri"], tabs["btri"], tabs["bones"], tabs["selr"], tabs["hmat"], tabs["emat"], tabs["gmask"],
            tabs["pmask"], lw["wga"], lw["bga"], lw["convw"], lw["convb"], lw["bsm"], lw["gcat"]]
    aliases = {}
    if prev is not None:
        n_in = len(args)
        in_specs += [pl.BlockSpec(memory_space=pl.ANY)] * 6
        args += list(prev)
        aliases = {n_in + i: 1 + i for i in range(6)}

    def st_spec(dims):
        return pl.BlockSpec((None, None) + dims, lambda b, c: (l_out, b) + (0,) * len(dims))

    out_shape = (jax.ShapeDtypeStruct((B, T, D_MIX), F32),) + tuple(
        jax.ShapeDtypeStruct((depth, B) + d, F32) for d in _STATE_DIMS)
    out_specs = (pl.BlockSpec((None, L, D_MIX), lambda b, c: (b, c, 0)),) + tuple(st_spec(d) for d in _STATE_DIMS)
    scratch = [
        pltpu.VMEM((R_HEADS // 2, V7X_LANES, V7X_LANES), F32),
        pltpu.VMEM((G_W, G_QK), F32),
        pltpu.VMEM((M_HEADS // 2, V7X_LANES, V7X_LANES), F32),
        pltpu.VMEM((8, M_W), F32),
        pltpu.VMEM((8, V7X_LANES), F32),
        pltpu.VMEM((8 + L, M_QK), F32),
        pltpu.VMEM((L, G_W), F32),
    ]
    kern = functools.partial(_mixer_prompt_kernel, L=L, CS=CS, NC=NC, chunk_decay=tabs["chunk_decay"])
    outs = pl.pallas_call(
        kern, out_shape=out_shape, grid=(B, NC), in_specs=in_specs, out_specs=out_specs,
        scratch_shapes=scratch, input_output_aliases=aliases,
        compiler_params=_cparams(("parallel", "arbitrary")), name="mixer_prompt",
    )(*args)
    return outs[0], tuple(outs[1:])


def _prep_layer(l, g_mix, w_in, w_ga2, b_ga, conv_w, conv_b, b_i, b_f, g_ret, g_gla, g_mlstm, w_out,
                g_xattn, g_mem, w_xq, w_xk, w_xv, w_xo, g_ffn, w_gate, w_up, w_down):
    w = w_in[l]
    a0 = 2 * R_HEADS * R_DK + 2 * R_W + 2 * G_QK + 2 * G_W
    m0 = a0 + G_RANK
    g0 = m0 + M_QK + 2 * M_W
    w_pad = jnp.concatenate(
        [w[:, :a0], w[:, m0:g0], w[:, g0:g0 + 2 * M_HEADS], w[:, a0:m0],
         jnp.zeros((D_MODEL, D_IN_PAD - C_SM - G_RANK - 2 * M_HEADS), F32)], axis=1).astype(BF16)
    wga = jnp.zeros((V7X_LANES, G_QK), F32).at[SM_AG:SM_AG + G_RANK].set(w_ga2[l]).astype(BF16)
    bsm = (jnp.zeros((1, V7X_LANES), F32).at[0, SM_I:SM_I + M_HEADS].set(b_i[l])
           .at[0, SM_F:SM_F + M_HEADS].set(b_f[l]))
    return dict(
        g_mix=g_mix[l], w_in=w_pad, wga=wga, bga=b_ga[l].reshape(1, G_QK), convw=conv_w[l],
        convb=conv_b[l].reshape(1, M_QK), bsm=bsm,
        gcat=jnp.concatenate([g_ret[l], g_gla[l], g_mlstm[l]]).reshape(1, D_MIX),
        w_out=w_out[l].astype(BF16), g_xattn=g_xattn[l], g_mem=g_mem[l],
        w_xq=w_xq[l].astype(BF16), w_xk=w_xk[l].astype(BF16), w_xv=w_xv[l].astype(BF16),
        w_xo=w_xo[l].astype(BF16), g_ffn=g_ffn[l], w_gate=w_gate[l].astype(BF16),
        w_up=w_up[l].astype(BF16), w_down=w_down[l].astype(BF16))


def _layer(x, mix_fn, attend, lw, g_final, *, tm, final_norm):
    B, T, D = x.shape
    M = B * T
    x2 = x.reshape(M, D)
    proj = rms_matmul(x2, lw["g_mix"], lw["w_in"], tm=tm, tn=D_IN_PAD // 3)
    mix, new_state = mix_fn(proj.reshape(B, T, D_IN_PAD))
    x2 = matmul_res(mix.reshape(M, D_MIX), lw["w_out"], x2, tm=tm)
    q = rms_matmul(x2, lw["g_xattn"], lw["w_xq"], tm=tm, tn=D)
    o = attend(q.reshape(B, T, D))
    x2 = matmul_res(o.reshape(M, D), lw["w_xo"], x2, tm=tm)
    x2 = swiglu_res(x2, lw["g_ffn"], lw["w_gate"], lw["w_up"], lw["w_down"], g_final,
                    tm=tm, tf=D_FF // 2, final_norm=final_norm)
    return x2.reshape(B, T, D), new_state


def kernel(x_prompt, x_sample, state_ret, state_gla, state_mlstm_C, state_mlstm_n, state_mlstm_m, state_mlstm_conv, cache_mem_k, cache_mem_v, mem_prompt, g_mix, w_in, w_ga2, b_ga, conv_w, conv_b, b_i, b_f, g_ret, g_gla, g_mlstm, w_out, g_xattn, g_mem, w_xq, w_xk, w_xv, w_xo, g_ffn, w_gate, w_up, w_down, g_final):
    B, T, D = x_prompt.shape
    Bs, Ts, _ = x_sample.shape
    depth = w_in.shape[0]
    Lp = CHUNK if T % CHUNK == 0 else T
    Ls = CHUNK if Ts % CHUNK == 0 else Ts
    CSp = 16 if Lp % 16 == 0 else Lp
    CSs = 16 if Ls % 16 == 0 else Ls
    tabs_p = _prompt_tables(T, Lp, CSp)
    tabs_s = _mixer_tables(Ts, Ls, CSs, float(PAST_LEN))
    tm_p = 512 if (B * T) % 512 == 0 else B * T
    tm_s = 512 if (Bs * Ts) % 512 == 0 else Bs * Ts
    tq_p = 512 if T % 512 == 0 else T
    nb_x = 8 if Bs % 8 == 0 else 1

    sample_state = (state_ret, state_gla, state_mlstm_C, state_mlstm_n,
                    state_mlstm_m.reshape(depth, Bs, 1, M_HEADS), state_mlstm_conv)
    hp, hs = x_prompt, x_sample
    p_st = s_st = p_mem = None
    for l in range(depth):
        lw = _prep_layer(l, g_mix, w_in, w_ga2, b_ga, conv_w, conv_b, b_i, b_f, g_ret, g_gla, g_mlstm,
                         w_out, g_xattn, g_mem, w_xq, w_xk, w_xv, w_xo, g_ffn, w_gate, w_up, w_down)
        last = l == depth - 1
        k5, v5, kb, vb = memory_kv(mem_prompt, lw["g_mem"], lw["w_xk"], lw["w_xv"], l, depth, p_mem)
        p_mem = (k5, v5)
        hp, p_st = _layer(hp, lambda pr: mixer_prompt(pr, tabs_p, lw, l, depth, p_st, L=Lp, CS=CSp),
                          lambda q: xattn(q, kb, vb, tq=tq_p), lw, g_final, tm=tm_p, final_norm=last)
        hs, s_st = _layer(hs, lambda pr: mixer(pr, tabs_s, lw, sample_state, l, l, depth, s_st, L=Ls, CS=CSs, NB=1),
                          lambda q: xattn_cache(q, cache_mem_k, cache_mem_v, l, nb=nb_x),
                          lw, g_final, tm=tm_s, final_norm=last)

    def fin(st, b):
        return st[:4] + (st[4].reshape(depth, b, M_HEADS), st[5])

    return (hp, hs, *fin(p_st, B), *p_mem, *fin(s_st, Bs))
```

```python
import functools
import math

import numpy as np
import jax
import jax.numpy as jnp
from jax import lax
from jax.experimental import pallas as pl
from jax.experimental.pallas import tpu as pltpu

F32 = jnp.float32
BF16 = jnp.bfloat16

D_MODEL = 1024
PAST_LEN = 16384
R_HEADS, R_DK, R_DV = 6, 64, 64
G_HEADS, G_DK, G_DV, G_RANK = 6, 32, 64, 16
G_NORMALIZER = 16.0
M_HEADS, M_DK, M_DV = 4, 64, 64
CONV_W = 4
X_HEADS = 4
X_HD = D_MODEL // X_HEADS
N_MEM = 256
D_FF = int(math.ceil(8 * D_MODEL / 3 / 256)) * 256
CHUNK = 128
EPS = 1e-6
ROPE_BASE = 10000.0

R_W = R_HEADS * R_DV
G_QK = G_HEADS * G_DK
G_W = G_HEADS * G_DV
M_QK = 2 * M_HEADS * M_DK
M_W = M_HEADS * M_DV
D_MIX = R_W + G_W + M_W

C_QR, C_KR, C_VR, C_GR = 0, 384, 768, 1152
C_QG, C_KG, C_VG, C_RG = 1536, 1728, 1920, 2304
C_QKM, C_VM, C_OM, C_SM = 2688, 3200, 3456, 3712
D_IN_PAD = 3840
SM_I, SM_F, SM_AG = 0, 4, 8
HEAD_W = 64
GLA_SAFE_LOG_RANGE = 60.0

V7X_LANES = 128
VMEM_LIMIT = 56 * 1024 * 1024


def _cparams(sem):
    return pltpu.CompilerParams(dimension_semantics=sem, vmem_limit_bytes=VMEM_LIMIT)


def _sigmoid(x):
    return 1.0 / (1.0 + jnp.exp(-x))


def _silu(x):
    return x * _sigmoid(x)


def _log_sigmoid(x):
    return jnp.minimum(x, 0.0) - jnp.log(1.0 + jnp.exp(-jnp.abs(x)))


def _dot(a, b):
    return jnp.dot(a, b, preferred_element_type=F32)


def _dot_nt(a, b):
    return lax.dot_general(a, b, (((1,), (1,)), ((), ())), preferred_element_type=F32)


def _dot_tn(a, b):
    return lax.dot_general(a, b, (((0,), (0,)), ((), ())), preferred_element_type=F32)


def _split3(x):
    hi = x.astype(BF16)
    r1 = x - hi.astype(F32)
    mid = r1.astype(BF16)
    lo = (r1 - mid.astype(F32)).astype(BF16)
    return hi, mid, lo


def _dot3(a, x):
    hi, mid, lo = _split3(x)
    return _dot(a, hi) + _dot(a, mid) + _dot(a, lo)


def _dot3_nt(a, x):
    hi, mid, lo = _split3(x)
    return _dot_nt(a, hi) + _dot_nt(a, mid) + _dot_nt(a, lo)


def _rms(x, g):
    return x * lax.rsqrt(jnp.mean(x * x, axis=-1, keepdims=True) + EPS) * g


def _rms_matmul_kernel(x_ref, g_ref, w_ref, o_ref, xn_ref):
    @pl.when(pl.program_id(1) == 0)
    def _():
        xn_ref[...] = _rms(x_ref[...], g_ref[...]).astype(BF16)

    o_ref[...] = _dot(xn_ref[...], w_ref[...]).astype(o_ref.dtype)


def rms_matmul(x, g, w, *, tm, tn, out_dtype=F32):
    M, D = x.shape
    N = w.shape[1]
    assert M % tm == 0 and N % tn == 0
    return pl.pallas_call(
        _rms_matmul_kernel,
        out_shape=jax.ShapeDtypeStruct((M, N), out_dtype),
        grid=(M // tm, N // tn),
        in_specs=[pl.BlockSpec((tm, D), lambda i, j: (i, 0)),
                  pl.BlockSpec((1, D), lambda i, j: (0, 0)),
                  pl.BlockSpec((D, tn), lambda i, j: (0, j))],
        out_specs=pl.BlockSpec((tm, tn), lambda i, j: (i, j)),
        scratch_shapes=[pltpu.VMEM((tm, D), BF16)],
        compiler_params=_cparams(("parallel", "arbitrary")),
        name="rms_matmul",
    )(x, g.reshape(1, D), w)


def _out_q_kernel(a_ref, wo_ref, x_ref, g_ref, wq_ref, x1_ref, q_ref):
    x1 = x_ref[...] + _dot(a_ref[...], wo_ref[...])
    x1_ref[...] = x1
    q_ref[...] = _dot(_rms(x1, g_ref[...]).astype(BF16), wq_ref[...]).astype(q_ref.dtype)


def out_q(a, wo, x, g, wq, *, tm):
    M, K = a.shape
    D = wo.shape[1]
    assert M % tm == 0
    row = lambda i: (i, 0)
    const = lambda i: (0, 0)
    return pl.pallas_call(
        _out_q_kernel,
        out_shape=(jax.ShapeDtypeStruct((M, D), F32), jax.ShapeDtypeStruct((M, D), BF16)),
        grid=(M // tm,),
        in_specs=[pl.BlockSpec((tm, K), row), pl.BlockSpec((K, D), const), pl.BlockSpec((tm, D), row),
                  pl.BlockSpec((1, D), const), pl.BlockSpec((D, D), const)],
        out_specs=(pl.BlockSpec((tm, D), row), pl.BlockSpec((tm, D), row)),
        compiler_params=_cparams(("parallel",)),
        name="out_q",
    )(a, wo, x, g.reshape(1, D), wq)


def _swiglu_kernel(x_ref, g_ref, wg_ref, wu_ref, wd_ref, gf_ref, o_ref, xn_ref, acc_ref, *, final_norm):
    j = pl.program_id(1)

    @pl.when(j == 0)
    def _():
        xn_ref[...] = _rms(x_ref[...], g_ref[...]).astype(BF16)
        acc_ref[...] = x_ref[...]

    xn = xn_ref[...]
    h = _silu(_dot(xn, wg_ref[...])) * _dot(xn, wu_ref[...])
    acc_ref[...] += _dot(h.astype(BF16), wd_ref[...])

    @pl.when(j == pl.num_programs(1) - 1)
    def _():
        y = acc_ref[...]
        if final_norm:
            y = _rms(y, gf_ref[...])
        o_ref[...] = y


def swiglu_res(x, g, wg, wu, wd, g_final, *, tm, tf, final_norm):
    M, D = x.shape
    FF = wg.shape[1]
    assert M % tm == 0 and FF % tf == 0
    return pl.pallas_call(
        functools.partial(_swiglu_kernel, final_norm=final_norm),
        out_shape=jax.ShapeDtypeStruct((M, D), F32),
        grid=(M // tm, FF // tf),
        in_specs=[pl.BlockSpec((tm, D), lambda i, j: (i, 0)),
                  pl.BlockSpec((1, D), lambda i, j: (0, 0)),
                  pl.BlockSpec((D, tf), lambda i, j: (0, j)),
                  pl.BlockSpec((D, tf), lambda i, j: (0, j)),
                  pl.BlockSpec((tf, D), lambda i, j: (j, 0)),
                  pl.BlockSpec((1, D), lambda i, j: (0, 0))],
        out_specs=pl.BlockSpec((tm, D), lambda i, j: (i, 0)),
        scratch_shapes=[pltpu.VMEM((tm, D), BF16), pltpu.VMEM((tm, D), F32)],
        compiler_params=_cparams(("parallel", "arbitrary")),
        name="swiglu_res",
    )(x, g.reshape(1, D), wg, wu, wd, g_final.reshape(1, D))


def _memkv_kernel(x_ref, g_ref, wk_ref, wv_ref, *refs):
    k5_ref, v5_ref, kb_ref, vb_ref = refs[-4:]
    xn = _rms(x_ref[...], g_ref[...]).astype(BF16)
    for w_ref, o5_ref, ob_ref in ((wk_ref, k5_ref, kb_ref), (wv_ref, v5_ref, vb_ref)):
        y = _dot(xn, w_ref[...])
        ob_ref[...] = y.astype(BF16)
        for h in range(X_HEADS):
            o5_ref[:, h, :] = y[:, h * X_HD:(h + 1) * X_HD]


def memory_kv(mem, g, wk, wv, l, depth, prev):
    B, _, D = mem.shape
    o5 = jax.ShapeDtypeStruct((depth, B, N_MEM, X_HEADS, X_HD), F32)
    ob = jax.ShapeDtypeStruct((B, N_MEM, D), BF16)
    in_specs = [pl.BlockSpec((None, N_MEM, D), lambda b: (b, 0, 0)),
                pl.BlockSpec((1, D), lambda b: (0, 0)),
                pl.BlockSpec((D, D), lambda b: (0, 0)),
                pl.BlockSpec((D, D), lambda b: (0, 0))]
    in_specs += [pl.BlockSpec(memory_space=pl.ANY)] * 2
    args = [mem, g.reshape(1, D), wk, wv] + list(prev)
    aliases = {4: 0, 5: 1}
    spec5 = pl.BlockSpec((None, None, N_MEM, X_HEADS, X_HD), lambda b: (l, b, 0, 0, 0))
    specb = pl.BlockSpec((None, N_MEM, D), lambda b: (b, 0, 0))
    return pl.pallas_call(
        _memkv_kernel,
        out_shape=(o5, o5, ob, ob),
        grid=(B,),
        in_specs=in_specs,
        out_specs=(spec5, spec5, specb, specb),
        input_output_aliases=aliases,
        compiler_params=_cparams(("parallel",)),
        name="memory_kv",
    )(*args)


def _xattn_kernel(q_ref, k_ref, v_ref, x_ref, wo_ref, o_ref):
    scale = X_HD ** -0.5
    parts = []
    for h in range(X_HEADS):
        sl = slice(h * X_HD, (h + 1) * X_HD)
        s = _dot_nt(q_ref[:, sl], k_ref[:, sl]) * scale
        p = jnp.exp(s - jnp.max(s, axis=-1, keepdims=True))
        l = jnp.sum(p, axis=-1, keepdims=True)
        parts.append((_dot(p.astype(BF16), v_ref[:, sl]) / l).astype(BF16))
    o_ref[...] = x_ref[...] + _dot(jnp.concatenate(parts, axis=1), wo_ref[...])


def xattn(q, mk, mv, x, wo, *, tq):
    B, T, D = q.shape
    assert T % tq == 0
    tok = pl.BlockSpec((None, tq, D), lambda b, i: (b, i, 0))
    mem = pl.BlockSpec((None, N_MEM, D), lambda b, i: (b, 0, 0))
    return pl.pallas_call(
        _xattn_kernel,
        out_shape=jax.ShapeDtypeStruct((B, T, D), F32),
        grid=(B, T // tq),
        in_specs=[tok, mem, mem, tok, pl.BlockSpec((D, D), lambda b, i: (0, 0))],
        out_specs=tok,
        compiler_params=_cparams(("parallel", "arbitrary")),
        name="xattn",
    )(q, mk, mv, x, wo)


def _xattn_cache_kernel(q_ref, k_ref, v_ref, x_ref, wo_ref, o_ref, *, nb):
    T = q_ref.shape[1]
    R = X_HEADS * T
    rowh = lax.broadcasted_iota(jnp.int32, (R, N_MEM * X_HEADS), 0) // T
    colh = lax.broadcasted_iota(jnp.int32, (R, N_MEM * X_HEADS), 1) % X_HEADS
    own = rowh == colh
    outs = []
    for s in range(nb):
        q = q_ref[s]
        qf = jnp.concatenate([q[:, h * X_HD:(h + 1) * X_HD] for h in range(X_HEADS)], axis=0)
        kf = k_ref[s].reshape(N_MEM * X_HEADS, X_HD).astype(BF16)
        vf = v_ref[s].reshape(N_MEM * X_HEADS, X_HD).astype(BF16)
        sc = jnp.where(own, _dot_nt(qf, kf) * (X_HD ** -0.5), -jnp.inf)
        p = jnp.exp(sc - jnp.max(sc, axis=-1, keepdims=True))
        l = jnp.sum(p, axis=-1, keepdims=True)
        o = (_dot(p.astype(BF16), vf) / l).astype(BF16)
        outs.append(jnp.concatenate([o[h * T:(h + 1) * T] for h in range(X_HEADS)], axis=1))
    y = _dot(jnp.concatenate(outs, axis=0), wo_ref[...])
    for s in range(nb):
        o_ref[s] = x_ref[s] + y[s * T:(s + 1) * T]


def xattn_cache(q, ck, cv, l, x, wo, *, nb):
    B, T, D = q.shape
    assert B % nb == 0
    cspec = pl.BlockSpec((None, nb, N_MEM, X_HEADS, X_HD), lambda b: (l, b, 0, 0, 0))
    tok = pl.BlockSpec((nb, T, D), lambda b: (b, 0, 0))
    return pl.pallas_call(
        functools.partial(_xattn_cache_kernel, nb=nb),
        out_shape=jax.ShapeDtypeStruct((B, T, D), F32),
        grid=(B // nb,),
        in_specs=[tok, cspec, cspec, tok, pl.BlockSpec((D, D), lambda b: (0, 0))],
        out_specs=tok,
        compiler_params=_cparams(("parallel",)),
        name="xattn_cache",
    )(q, ck, cv, x, wo)


def _eye(n):
    r = lax.broadcasted_iota(jnp.int32, (n, n), 0)
    c = lax.broadcasted_iota(jnp.int32, (n, n), 1)
    return jnp.where(r == c, 1.0, 0.0).astype(BF16)


_STATE_DIMS = ((R_HEADS, R_DK, R_DV), (G_HEADS, G_DK, G_DV), (M_HEADS, M_DK, M_DV), (M_HEADS, M_DK),
               (1, M_HEADS), (CONV_W - 1, M_QK))


def _split2(x):
    hi = x.astype(BF16)
    lo = (x - hi.astype(F32)).astype(BF16)
    return hi, lo


def _head_mean_sq(o, hmat):
    hi, lo = _split2(o * o)
    return (_dot(hi, hmat) + _dot(lo, hmat)) * (1.0 / HEAD_W)


def _norm_gate(o, hmat, g_row, gate):
    return (o * lax.rsqrt(_head_mean_sq(o, hmat) + EPS) * g_row * gate).astype(BF16)


def _lane_pick(cols, idx_lo, idx_hi, lo_mask):
    L = cols.shape[0]
    a = jnp.broadcast_to(cols[:, idx_lo:idx_lo + 1], (L, V7X_LANES))
    b = jnp.broadcast_to(cols[:, idx_hi:idx_hi + 1], (L, V7X_LANES))
    return jnp.where(lo_mask, a, b)


def _stack_masked(x, m_a, m_b):
    z = jnp.zeros_like(x)
    return jnp.concatenate([jnp.where(m_a, x, z), jnp.where(m_b, x, z)], axis=0)


def _mixer_prompt_kernel(proj_ref, cosq_ref, sinq_ref, cosk_ref, sink_ref, rdec_ref, qdec_ref, kdec_ref,
                         tri_ref, btri_ref, bones_ref, selr_ref, hmat_ref, emat_ref, gmask_ref, pmask_ref,
                         wga_ref, bga_ref, convw_ref, convb_ref, bsm_ref, gcat_ref, *rest,
                         L, CS, NC, chunk_decay):
    (mix_ref, sr_out, sg_out, c_out, n_out, m_out, conv_out,
     srp_scr, sgt_scr, cp_scr, n_scr, m_scr, conv_scr, og_scr) = rest[-14:]
    c = pl.program_id(1)

    @pl.when(c == 0)
    def _():
        srp_scr[...] = jnp.zeros_like(srp_scr)
        sgt_scr[...] = jnp.zeros_like(sgt_scr)
        cp_scr[...] = jnp.zeros_like(cp_scr)
        n_scr[...] = jnp.zeros_like(n_scr)
        m_scr[...] = jnp.zeros_like(m_scr)
        conv_scr[0:8, :] = jnp.zeros((8, M_QK), F32)

    lane = lax.broadcasted_iota(jnp.int32, (L, V7X_LANES), 1)
    lo = lane < HEAD_W
    hi_m = lane >= HEAD_W
    first_half = (lane % R_DK) < (R_DK // 2)
    row2 = lax.broadcasted_iota(jnp.int32, (L, 2 * L), 0)
    col2 = lax.broadcasted_iota(jnp.int32, (L, 2 * L), 1) % L
    causal2 = col2 <= row2
    lo_row = lax.broadcasted_iota(jnp.int32, (1, V7X_LANES), 1) < HEAD_W
    lo8 = lax.broadcasted_iota(jnp.int32, (8, V7X_LANES), 1) < HEAD_W
    hmat = hmat_ref[...]
    pmask = pmask_ref[...]

    def rope(x, cos_ref, sin_ref):
        cos = cos_ref[...]
        sin = sin_ref[...]
        parts = []
        for t in range(R_W // V7X_LANES):
            xs = x[:, t * V7X_LANES:(t + 1) * V7X_LANES]
            rot = jnp.where(first_half, pltpu.roll(xs, V7X_LANES - R_DK // 2, 1), pltpu.roll(xs, R_DK // 2, 1))
            parts.append(xs * cos + rot * sin)
        return jnp.concatenate(parts, axis=1)

    q_r = rope(proj_ref[:, C_QR:C_QR + R_W], cosq_ref, sinq_ref)
    k_r = rope(proj_ref[:, C_KR:C_KR + R_W], cosk_ref, sink_ref)
    q_rb = q_r.astype(BF16)
    k_rb = k_r.astype(BF16)
    kd_rb = (k_r * kdec_ref[...]).astype(BF16)
    v_rb = proj_ref[:, C_VR:C_VR + R_W].astype(BF16)
    o_parts = []
    for p in range(R_HEADS // 2):
        sl = slice(p * V7X_LANES, (p + 1) * V7X_LANES)
        qs, ks, vs = q_rb[:, sl], k_rb[:, sl], v_rb[:, sl]
        sc = _dot_nt(qs, _stack_masked(ks, lo, hi_m)) * rdec_ref[p]
        sp = srp_scr[p]
        o = _dot(sc.astype(BF16), _stack_masked(vs, lo, hi_m)) + _dot(qs, sp.astype(BF16)) * qdec_ref[:, sl]
        cd = jnp.where(lo_row, chunk_decay[2 * p], chunk_decay[2 * p + 1])
        srp_scr[p] = sp * cd + _dot_tn(kd_rb[:, sl], vs) * pmask
        o_parts.append(o)
    o_r = jnp.concatenate(o_parts, axis=1)
    gate_r = _silu(proj_ref[:, C_GR:C_GR + R_W])
    mix_ref[:, 0:R_W] = _norm_gate(o_r, hmat, gcat_ref[:, 0:R_W], gate_r)

    small = proj_ref[:, C_SM:C_SM + V7X_LANES]
    z = _dot(small.astype(BF16), wga_ref[...]) + bga_ref[...]
    log_a = _log_sigmoid(z) / G_NORMALIZER
    b = _dot3(tri_ref[...], log_a)
    b_last = b[L - 1:L, :]
    safe = jnp.max(-b_last) <= GLA_SAFE_LOG_RANGE
    q_g = proj_ref[:, C_QG:C_QG + G_QK] * (G_DK ** -0.5)
    k_g = proj_ref[:, C_KG:C_KG + G_QK]
    v_gb = proj_ref[:, C_VG:C_VG + G_W].astype(BF16)

    @pl.when(safe)
    def _():
        qt = (q_g * jnp.exp(b)).astype(BF16)
        kt = (k_g * jnp.exp(-b)).astype(BF16)
        kl = (k_g * jnp.exp(b_last - b)).astype(BF16)
        st = sgt_scr[...]
        o_inter = _dot_nt(qt, st.astype(BF16))
        lane_q = lax.broadcasted_iota(jnp.int32, (L, V7X_LANES), 1) // G_DK
        parts = []
        pad = jnp.zeros((L, 2 * V7X_LANES - G_QK), BF16)
        qt_p = jnp.concatenate([qt, pad], axis=1)
        kt_p = jnp.concatenate([kt, pad], axis=1)
        for g in range(G_HEADS // 2):
            s0 = (2 * g * G_DK) // V7X_LANES * V7X_LANES
            hA = (2 * g * G_DK - s0) // G_DK
            qs, ks = qt_p[:, s0:s0 + V7X_LANES], kt_p[:, s0:s0 + V7X_LANES]
            sc = _dot_nt(qs, _stack_masked(ks, lane_q == hA, lane_q == hA + 1))
            sc = jnp.where(causal2, sc, 0.0).astype(BF16)
            vs = v_gb[:, g * V7X_LANES:(g + 1) * V7X_LANES]
            parts.append(_dot(sc, _stack_masked(vs, lo, hi_m)))
        og_scr[...] = o_inter + jnp.concatenate(parts, axis=1)
        sgt_scr[...] = st * jnp.exp(b_last) + _dot_tn(v_gb, kl) * gmask_ref[...]

    @pl.when(jnp.logical_not(safe))
    def _():
        v_g = proj_ref[:, C_VG:C_VG + G_W]
        b_loc = _dot3(btri_ref[...], log_a)
        b_tot = _dot3(bones_ref[...], log_a)
        qt_g = (q_g * jnp.exp(b_loc)).astype(BF16)
        kt_g = (k_g * jnp.exp(b_tot - b_loc)).astype(BF16)
        d_tot = jnp.exp(b_tot)
        emat = emat_ref[...]
        gmask = gmask_ref[...]
        sub_row = lax.broadcasted_iota(jnp.int32, (CS, G_QK), 0)
        for blk in range(L // CS):
            r0 = blk * CS
            rs = slice(r0, r0 + CS)
            st = sgt_scr[...]
            o_blk = _dot_nt(qt_g[rs], st.astype(BF16))
            bI, qI, kI = b_loc[rs], q_g[rs], k_g[rs]
            terms = []
            for j in range(CS):
                e = jnp.exp(jnp.where(sub_row >= j, bI - bI[j:j + 1], -jnp.inf))
                terms.append(e * qI * kI[j:j + 1])
            t = jnp.concatenate(terms, axis=0).astype(BF16)
            w = _dot(t, emat)
            for j in range(CS):
                o_blk = o_blk + w[j * CS:(j + 1) * CS] * v_g[r0 + j:r0 + j + 1]
            og_scr[rs, :] = o_blk
            sgt_scr[...] = st * d_tot[r0:r0 + 1] + _dot_tn(v_gb[rs], kt_g[rs]) * gmask

    gate_g = _silu(proj_ref[:, C_RG:C_RG + G_W])
    mix_ref[:, R_W:R_W + G_W] = _norm_gate(og_scr[...], hmat, gcat_ref[:, R_W:R_W + G_W], gate_g)

    u = proj_ref[:, C_QKM:C_QKM + M_QK]
    conv_scr[8:8 + L, :] = u
    y = convb_ref[...]
    for j in range(CONV_W - 1):
        y = y + conv_scr[5 + j:5 + j + L, :] * convw_ref[j:j + 1, :]
    y = y + u * convw_ref[CONV_W - 1:CONV_W, :]
    tail = conv_scr[5 + L:8 + L, :]
    conv_scr[5:8, :] = tail
    qk = _silu(y)
    q_m = qk[:, :M_W]
    k_m = qk[:, M_W:] * (M_DK ** -0.5)
    q_mb = q_m.astype(BF16)
    k_mb = k_m.astype(BF16)
    v_mb = proj_ref[:, C_VM:C_VM + M_W].astype(BF16)
    gates = small + bsm_ref[...]
    f_cum = _dot3(tri_ref[...], _log_sigmoid(gates))
    i_rows = _dot3_nt(selr_ref[...], gates)
    f_rows = _dot3_nt(selr_ref[...], f_cum)
    head_lane = (lane >= SM_I) & (lane < SM_I + M_HEADS)
    fc = jnp.where(head_lane, pltpu.roll(f_cum, V7X_LANES - (SM_F - SM_I), 1), 0.0)
    m_prev = m_scr[0:1, :]
    a = fc + m_prev
    mx = jnp.full((L, V7X_LANES), -jnp.inf, F32)
    row = lax.broadcasted_iota(jnp.int32, (L, L), 0)
    col = lax.broadcasted_iota(jnp.int32, (L, L), 1)
    causal = col <= row
    dms = []
    for h in range(M_HEADS):
        dm = jnp.where(causal, (fc[:, SM_I + h:SM_I + h + 1] - f_rows[M_HEADS + h:M_HEADS + h + 1, :])
                       + i_rows[h:h + 1, :], -jnp.inf)
        dms.append(dm)
        mx = jnp.where(lane == SM_I + h, jnp.max(dm, axis=-1, keepdims=True), mx)
    m_tok = jnp.maximum(a, mx)
    w_inter = jnp.exp(a - m_tok)
    n_full = n_scr[0:1, :]
    hs, lows = _split2(q_m * n_full)
    qn_b = _dot(hs, hmat_ref[0:M_W, 0:M_W]) + _dot(lows, hmat_ref[0:M_W, 0:M_W])
    rs_all = jnp.zeros((L, V7X_LANES), F32)
    qn_all = jnp.zeros((L, V7X_LANES), F32)
    scs = []
    for p in range(M_HEADS // 2):
        sl = slice(p * V7X_LANES, (p + 1) * V7X_LANES)
        sc = _dot_nt(q_mb[:, sl], _stack_masked(k_mb[:, sl], lo, hi_m))
        mt_a = jnp.broadcast_to(m_tok[:, SM_I + 2 * p:SM_I + 2 * p + 1], (L, L))
        mt_b = jnp.broadcast_to(m_tok[:, SM_I + 2 * p + 1:SM_I + 2 * p + 2], (L, L))
        sc_a = sc[:, :L] * jnp.exp(dms[2 * p] - mt_a)
        sc_b = sc[:, L:] * jnp.exp(dms[2 * p + 1] - mt_b)
        rs_all = jnp.where(lane == SM_I + 2 * p, jnp.sum(sc_a, axis=-1, keepdims=True), rs_all)
        rs_all = jnp.where(lane == SM_I + 2 * p + 1, jnp.sum(sc_b, axis=-1, keepdims=True), rs_all)
        qn_all = jnp.where(lane == SM_I + 2 * p, qn_b[:, p * V7X_LANES:p * V7X_LANES + 1], qn_all)
        qn_all = jnp.where(lane == SM_I + 2 * p + 1, qn_b[:, p * V7X_LANES + HEAD_W:p * V7X_LANES + HEAD_W + 1], qn_all)
        scs.append(jnp.concatenate([sc_a, sc_b], axis=1).astype(BF16))
    den = rs_all + qn_all * w_inter
    inv = 1.0 / jnp.maximum(jnp.abs(den), jnp.exp(-m_tok))
    m_new = jnp.where(head_lane, m_tok, 0.0)[L - 1:L, :]
    wk = jnp.exp(fc[L - 1:L, :] - fc + gates - m_new)
    scale = jnp.broadcast_to(jnp.exp(a[L - 1:L, :] - m_new), (8, V7X_LANES))
    h_parts = []
    kw_parts = []
    scale_parts = []
    for p in range(M_HEADS // 2):
        sl = slice(p * V7X_LANES, (p + 1) * V7X_LANES)
        ia, ib = SM_I + 2 * p, SM_I + 2 * p + 1
        cpair = cp_scr[p]
        num = (_dot(scs[p], _stack_masked(v_mb[:, sl], lo, hi_m))
               + _dot(q_mb[:, sl], cpair.astype(BF16)) * _lane_pick(w_inter, ia, ib, lo))
        h_parts.append(num * _lane_pick(inv, ia, ib, lo))
        kw = k_m[:, sl] * _lane_pick(wk, ia, ib, lo)
        kw_parts.append(kw)
        sc_row = _lane_pick(scale, ia, ib, lo8)[0:1]
        scale_parts.append(sc_row)
        cp_scr[p] = cpair * sc_row + _dot_tn(kw.astype(BF16), v_mb[:, sl]) * pmask
    kw_all = jnp.concatenate(kw_parts, axis=1)
    n_new = n_full * jnp.concatenate(scale_parts, axis=1) + jnp.sum(kw_all, axis=0, keepdims=True)
    n_scr[...] = jnp.broadcast_to(n_new, n_scr.shape)
    m_scr[...] = jnp.broadcast_to(m_new, m_scr.shape)
    gate_m = _sigmoid(proj_ref[:, C_OM:C_OM + M_W])
    mix_ref[:, R_W + G_W:D_MIX] = _norm_gate(jnp.concatenate(h_parts, axis=1), hmat_ref[0:M_W, 0:M_W],
                                             gcat_ref[:, R_W + G_W:D_MIX], gate_m)

    @pl.when(c == NC - 1)
    def _():
        eye_k = _eye(G_DK)
        for p in range(R_HEADS // 2):
            sr_out[2 * p] = srp_scr[p, 0:HEAD_W, 0:HEAD_W]
            sr_out[2 * p + 1] = srp_scr[p, HEAD_W:, HEAD_W:]
        for h in range(G_HEADS):
            sg_out[h] = _dot3_nt(eye_k, sgt_scr[h * G_DV:(h + 1) * G_DV, h * G_DK:(h + 1) * G_DK])
        for p in range(M_HEADS // 2):
            c_out[2 * p] = cp_scr[p, 0:HEAD_W, 0:HEAD_W]
            c_out[2 * p + 1] = cp_scr[p, HEAD_W:, HEAD_W:]
        for h in range(M_HEADS):
            n_out[h:h + 1, :] = n_scr[0:1, h * M_DK:(h + 1) * M_DK]
        m_out[...] = m_scr[0:1, SM_I:SM_I + M_HEADS]
        conv_out[...] = conv_scr[5:8, :]


def _prompt_tables(T, L, CS):
    half = R_DK // 2
    inv = ROPE_BASE ** (-jnp.arange(half, dtype=F32) * 2.0 / R_DK)
    pos = jnp.arange(T, dtype=F32)
    ang = pos[:, None] * inv[None, :]
    cos = jnp.tile(jnp.cos(ang), (1, V7X_LANES // half))
    sin_h = jnp.sin(ang)
    sin = jnp.tile(jnp.concatenate([-sin_h, sin_h], axis=1), (1, V7X_LANES // R_DK))
    qs = R_DK ** -0.5

    log_gamma = jnp.log(1.0 - 2.0 ** (-5.0 - jnp.arange(R_HEADS, dtype=F32)))
    idx = jnp.arange(L, dtype=F32)
    rel = idx[:, None] - idx[None, :]
    causal = rel >= 0
    rdecay = jnp.where(causal[None], jnp.exp(log_gamma[:, None, None] * jnp.where(causal, rel, 0.0)[None]), 0.0)
    rdec = jnp.concatenate([rdecay[0::2], rdecay[1::2]], axis=2)
    qdec = jnp.repeat(jnp.exp(log_gamma[:, None] * (idx + 1.0)).T, R_DK, axis=1)
    kdec = jnp.repeat(jnp.exp(log_gamma[:, None] * (L - 1.0 - idx)).T, R_DK, axis=1)
    lg32 = np.log(1.0 - 2.0 ** (-5.0 - np.arange(R_HEADS, dtype=np.float64))).astype(np.float32)
    chunk_decay = tuple(float(np.exp(v * np.float32(L))) for v in lg32)

    r = np.arange(L)
    tri = (r[None, :] <= r[:, None])
    same = (r[None, :] // CS) == (r[:, None] // CS)
    selr = np.zeros((16, V7X_LANES), np.float32)
    for h in range(M_HEADS):
        selr[h, SM_I + h] = 1.0
        selr[M_HEADS + h, SM_F + h] = 1.0
    hv = np.arange(G_W) // G_DV
    hc = np.arange(G_QK) // G_DK
    gmask = (hv[:, None] == hc[None, :]).astype(np.float32)
    hmat = (hv[:, None] == hv[None, :]).astype(np.float32)
    pm = np.arange(V7X_LANES) // HEAD_W
    pmask = (pm[:, None] == pm[None, :]).astype(np.float32)
    return dict(
        cosq=cos * qs, sinq=sin * qs, cosk=cos, sink=sin, rdec=rdec, qdec=qdec, kdec=kdec, chunk_decay=chunk_decay,
        tri=jnp.asarray(tri, BF16), btri=jnp.asarray(tri & same, BF16), bones=jnp.asarray(same, BF16),
        selr=jnp.asarray(selr, BF16), hmat=jnp.asarray(hmat, BF16), emat=jnp.asarray(gmask.T, BF16),
        gmask=jnp.asarray(gmask, F32), pmask=jnp.asarray(pmask, F32))


def mixer_prompt(proj, tabs, lw, l_out, depth, prev, *, L, CS):
    B, T, _ = proj.shape
    NC = T // L
    const2 = lambda b, c: (0, 0)
    const3 = lambda b, c: (0, 0, 0)
    tspec = pl.BlockSpec((L, V7X_LANES), lambda b, c: (c, 0))
    in_specs = [
        pl.BlockSpec((None, L, D_IN_PAD), lambda b, c: (b, c, 0)),
        tspec, tspec, tspec, tspec,
        pl.BlockSpec((R_HEADS // 2, L, 2 * L), const3),
        pl.BlockSpec((L, R_W), const2),
        pl.BlockSpec((L, R_W), const2),
        pl.BlockSpec((L, L), const2),
        pl.BlockSpec((L, L), const2),
        pl.BlockSpec((L, L), const2),
        pl.BlockSpec((16, V7X_LANES), const2),
        pl.BlockSpec((G_W, G_W), const2),
        pl.BlockSpec((G_QK, G_W), const2),
        pl.BlockSpec((G_W, G_QK), const2),
        pl.BlockSpec((V7X_LANES, V7X_LANES), const2),
        pl.BlockSpec((V7X_LANES, G_QK), const2),
        pl.BlockSpec((1, G_QK), const2),
        pl.BlockSpec((CONV_W, M_QK), const2),
        pl.BlockSpec((1, M_QK), const2),
        pl.BlockSpec((1, V7X_LANES), const2),
        pl.BlockSpec((1, D_MIX), const2),
    ]
    args = [proj, tabs["cosq"], tabs["sinq"], tabs["cosk"], tabs["sink"], tabs["rdec"], tabs["qdec"], tabs["kdec"],
            tabs["tri"], tabs["btri"], tabs["bones"], tabs["selr"], tabs["hmat"], tabs["emat"], tabs["gmask"],
            tabs["pmask"], lw["wga"], lw["bga"], lw["convw"], lw["convb"], lw["bsm"], lw["gcat"]]
    n_in = len(args)
    in_specs += [pl.BlockSpec(memory_space=pl.ANY)] * 6
    args += list(prev)
    aliases = {n_in + i: 1 + i for i in range(6)}

    def st_spec(dims):
        return pl.BlockSpec((None, None) + dims, lambda b, c: (l_out, b) + (0,) * len(dims))

    out_shape = (jax.ShapeDtypeStruct((B, T, D_MIX), BF16),) + tuple(
        jax.ShapeDtypeStruct((depth, B) + d, F32) for d in _STATE_DIMS)
    out_specs = (pl.BlockSpec((None, L, D_MIX), lambda b, c: (b, c, 0)),) + tuple(st_spec(d) for d in _STATE_DIMS)
    scratch = [
        pltpu.VMEM((R_HEADS // 2, V7X_LANES, V7X_LANES), F32),
        pltpu.VMEM((G_W, G_QK), F32),
        pltpu.VMEM((M_HEADS // 2, V7X_LANES, V7X_LANES), F32),
        pltpu.VMEM((8, M_W), F32),
        pltpu.VMEM((8, V7X_LANES), F32),
        pltpu.VMEM((8 + L, M_QK), F32),
        pltpu.VMEM((L, G_W), F32),
    ]
    kern = functools.partial(_mixer_prompt_kernel, L=L, CS=CS, NC=NC, chunk_decay=tabs["chunk_decay"])
    outs = pl.pallas_call(
        kern, out_shape=out_shape, grid=(B, NC), in_specs=in_specs, out_specs=out_specs,
        scratch_shapes=scratch, input_output_aliases=aliases,
        compiler_params=_cparams(("parallel", "arbitrary")), name="mixer_prompt",
    )(*args)
    return outs[0], tuple(outs[1:])


NS = 16
ROWS = 128


def _seq_bcast(x, t, T):
    n, w = x.shape
    x3 = x.reshape(n // T, T, w)
    return jnp.broadcast_to(x3[:, t:t + 1, :], (n // T, T, w)).reshape(n, w)


def _mixer_sample_kernel(proj_ref, cosq_ref, sinq_ref, cosk_ref, sink_ref, rdec_ref, qdec_ref, kdec_ref,
                         tri_ref, segones_ref, selr_ref, hmat_ref, emat_ref, mseg_ref, msegt_ref,
                         wga_ref, bga_ref, convw_ref, convb_ref, bsm_ref, gcat_ref,
                         sr_ref, sg_ref, c_ref, n_ref, m_ref, conv_ref, *rest, T, chunk_decay):
    (mix_ref, sr_out, sg_out, c_out, n_out, m_out, conv_out, conv_scr) = rest[-8:]
    L = ROWS
    ns = L // T
    lane = lax.broadcasted_iota(jnp.int32, (L, V7X_LANES), 1)
    lo = lane < HEAD_W
    hi_m = lane >= HEAD_W
    first_half = (lane % R_DK) < (R_DK // 2)
    row = lax.broadcasted_iota(jnp.int32, (L, L), 0)
    col = lax.broadcasted_iota(jnp.int32, (L, L), 1)
    segcausal = (row // T == col // T) & (col <= row)
    hmat = hmat_ref[...]
    mseg = mseg_ref[...]
    msegt = msegt_ref[...]

    def rope(x, cos_ref, sin_ref):
        cos = cos_ref[...]
        sin = sin_ref[...]
        parts = []
        for t in range(R_W // V7X_LANES):
            xs = x[:, t * V7X_LANES:(t + 1) * V7X_LANES]
            rot = jnp.where(first_half, pltpu.roll(xs, V7X_LANES - R_DK // 2, 1), pltpu.roll(xs, R_DK // 2, 1))
            parts.append(xs * cos + rot * sin)
        return jnp.concatenate(parts, axis=1)

    def tile_lanes(x, n):
        return jnp.concatenate([x] * n, axis=1)

    def tile_rows(x, n):
        return jnp.concatenate([x] * n, axis=0)

    def pair_state_terms(qs, ks_f32, vs, st_ref, p, hd):
        r = st_ref[:, 2 * p:2 * p + 2].reshape(ns * 2 * hd, hd).astype(BF16)
        kt = tile_rows(ks_f32.T.astype(BF16), ns) * msegt
        inter, kv = [], []
        for hh, m in ((0, lo), (1, hi_m)):
            qh = jnp.where(m, qs, jnp.zeros_like(qs))
            inter.append(_dot(tile_lanes(qh, ns) * mseg, r))
            kv.append(_dot(kt, vs[:, hh * hd:(hh + 1) * hd]))
        return inter, kv

    q_r = rope(proj_ref[:, C_QR:C_QR + R_W], cosq_ref, sinq_ref)
    k_r = rope(proj_ref[:, C_KR:C_KR + R_W], cosk_ref, sink_ref)
    q_rb = q_r.astype(BF16)
    k_rb = k_r.astype(BF16)
    kd_r = k_r * kdec_ref[...]
    v_rb = proj_ref[:, C_VR:C_VR + R_W].astype(BF16)
    o_parts = []
    for p in range(R_HEADS // 2):
        sl = slice(p * V7X_LANES, (p + 1) * V7X_LANES)
        qs, ks, vs = q_rb[:, sl], k_rb[:, sl], v_rb[:, sl]
        sc = _dot_nt(qs, _stack_masked(ks, lo, hi_m)) * rdec_ref[p]
        inter, kv = pair_state_terms(qs, kd_r[:, sl], vs, sr_ref, p, R_DK)
        o = (_dot(sc.astype(BF16), _stack_masked(vs, lo, hi_m))
             + jnp.concatenate(inter, axis=1) * qdec_ref[:, sl])
        o_parts.append(o)
        for hh in range(2):
            h = 2 * p + hh
            sr_out[:, h] = sr_ref[:, h] * chunk_decay[h] + kv[hh].reshape(ns, 2, R_DK, R_DV)[:, hh]
    gate_r = _silu(proj_ref[:, C_GR:C_GR + R_W])
    mix_ref[:, 0:R_W] = _norm_gate(jnp.concatenate(o_parts, axis=1), hmat, gcat_ref[:, 0:R_W], gate_r)

    small = proj_ref[:, C_SM:C_SM + V7X_LANES]
    z = _dot(small.astype(BF16), wga_ref[...]) + bga_ref[...]
    log_a = _log_sigmoid(z) / G_NORMALIZER
    b = _dot3(tri_ref[...], log_a)
    b_tot = _seq_bcast(b, T - 1, T)
    q_g = proj_ref[:, C_QG:C_QG + G_QK] * (G_DK ** -0.5)
    k_g = proj_ref[:, C_KG:C_KG + G_QK]
    v_g = proj_ref[:, C_VG:C_VG + G_W]
    v_gb = v_g.astype(BF16)
    tok = lax.broadcasted_iota(jnp.int32, (L, G_QK), 0) % T
    emat = emat_ref[...]
    o_g = jnp.zeros((L, G_W), F32)
    for j in range(T):
        e = jnp.exp(jnp.where(tok >= j, b - _seq_bcast(b, j, T), -jnp.inf))
        tj = (e * q_g * _seq_bcast(k_g, j, T)).astype(BF16)
        o_g = o_g + _dot(tj, emat) * _seq_bcast(v_g, j, T)
    qt = (q_g * jnp.exp(b)).astype(BF16)
    kl = k_g * jnp.exp(b_tot - b)
    dtot = jnp.exp(b_tot)
    lane_h = lane // G_DK
    inter_parts = [None] * G_HEADS
    for h0, heads in ((0, (0, 1, 2, 3)), (2, (4, 5))):
        c0 = h0 * G_DK
        q_s = qt[:, c0:c0 + V7X_LANES]
        r = sg_ref[:, h0:h0 + 4].reshape(ns * V7X_LANES, G_DV)
        rb = r.astype(BF16)
        kt = tile_rows(kl[:, c0:c0 + V7X_LANES].T.astype(BF16), ns) * msegt
        dt = dtot[:, c0:c0 + V7X_LANES].T
        dcols = []
        for s in range(ns):
            dcols.append(jnp.broadcast_to(dt[:, s * T:s * T + 1], (V7X_LANES, G_DV)))
        dfull = jnp.concatenate(dcols, axis=0).reshape(ns, 4, G_DK, G_DV)
        for h in heads:
            qh = jnp.where(lane_h == h - h0, q_s, jnp.zeros_like(q_s))
            inter_parts[h] = _dot(tile_lanes(qh, ns) * mseg, rb)
            kv = _dot(kt, v_gb[:, h * G_DV:(h + 1) * G_DV])
            sg_out[:, h] = sg_ref[:, h] * dfull[:, h - h0] + kv.reshape(ns, 4, G_DK, G_DV)[:, h - h0]
    o_g = o_g + jnp.concatenate(inter_parts, axis=1)
    gate_g = _silu(proj_ref[:, C_RG:C_RG + G_W])
    mix_ref[:, R_W:R_W + G_W] = _norm_gate(o_g, hmat, gcat_ref[:, R_W:R_W + G_W], gate_g)

    u = proj_ref[:, C_QKM:C_QKM + M_QK]
    conv_scr[:, 5:8, :] = conv_ref[...]
    conv_scr[:, 8:8 + T, :] = u.reshape(ns, T, M_QK)
    y = convb_ref[...]
    for j in range(CONV_W - 1):
        y = y + conv_scr[:, 5 + j:5 + j + T, :].reshape(L, M_QK) * convw_ref[j:j + 1, :]
    y = y + u * convw_ref[CONV_W - 1:CONV_W, :]
    conv_out[...] = conv_scr[:, 5 + T:8 + T, :]
    qk = _silu(y)
    q_m = qk[:, :M_W]
    k_m = qk[:, M_W:] * (M_DK ** -0.5)
    q_mb = q_m.astype(BF16)
    k_mb = k_m.astype(BF16)
    v_mb = proj_ref[:, C_VM:C_VM + M_W].astype(BF16)
    gates = small + bsm_ref[...]
    f_cum = _dot3(tri_ref[...], _log_sigmoid(gates))
    i_rows = _dot3_nt(selr_ref[...], gates)
    f_rows = _dot3_nt(selr_ref[...], f_cum)
    head_lane = (lane >= SM_I) & (lane < SM_I + M_HEADS)
    fc = jnp.where(head_lane, pltpu.roll(f_cum, V7X_LANES - (SM_F - SM_I), 1), 0.0)
    m_prev = m_ref[...]
    a = fc + m_prev
    mx = jnp.full((L, V7X_LANES), -jnp.inf, F32)
    dms = []
    for h in range(M_HEADS):
        dm = jnp.where(segcausal, (fc[:, SM_I + h:SM_I + h + 1] - f_rows[M_HEADS + h:M_HEADS + h + 1, :])
                       + i_rows[h:h + 1, :], -jnp.inf)
        dms.append(dm)
        mx = jnp.where(lane == SM_I + h, jnp.max(dm, axis=-1, keepdims=True), mx)
    m_tok = jnp.maximum(a, mx)
    w_inter = jnp.exp(a - m_tok)
    n_rows = n_ref[...]
    hs, lows = _split2(q_m * n_rows)
    qn_b = _dot(hs, hmat_ref[0:M_W, 0:M_W]) + _dot(lows, hmat_ref[0:M_W, 0:M_W])
    rs_all = jnp.zeros((L, V7X_LANES), F32)
    qn_all = jnp.zeros((L, V7X_LANES), F32)
    scs = []
    for p in range(M_HEADS // 2):
        sl = slice(p * V7X_LANES, (p + 1) * V7X_LANES)
        sc = _dot_nt(q_mb[:, sl], _stack_masked(k_mb[:, sl], lo, hi_m))
        mt_a = jnp.broadcast_to(m_tok[:, SM_I + 2 * p:SM_I + 2 * p + 1], (L, L))
        mt_b = jnp.broadcast_to(m_tok[:, SM_I + 2 * p + 1:SM_I + 2 * p + 2], (L, L))
        sc_a = sc[:, :L] * jnp.exp(dms[2 * p] - mt_a)
        sc_b = sc[:, L:] * jnp.exp(dms[2 * p + 1] - mt_b)
        rs_all = jnp.where(lane == SM_I + 2 * p, jnp.sum(sc_a, axis=-1, keepdims=True), rs_all)
        rs_all = jnp.where(lane == SM_I + 2 * p + 1, jnp.sum(sc_b, axis=-1, keepdims=True), rs_all)
        qn_all = jnp.where(lane == SM_I + 2 * p, qn_b[:, p * V7X_LANES:p * V7X_LANES + 1], qn_all)
        qn_all = jnp.where(lane == SM_I + 2 * p + 1, qn_b[:, p * V7X_LANES + HEAD_W:p * V7X_LANES + HEAD_W + 1], qn_all)
        scs.append(jnp.concatenate([sc_a, sc_b], axis=1).astype(BF16))
    den = rs_all + qn_all * w_inter
    inv = 1.0 / jnp.maximum(jnp.abs(den), jnp.exp(-m_tok))
    m_new = _seq_bcast(jnp.where(head_lane, m_tok, 0.0), T - 1, T)
    wk = jnp.exp(_seq_bcast(fc, T - 1, T) - fc + gates - m_new)
    scale = jnp.exp(_seq_bcast(a, T - 1, T) - m_new)
    h_parts = []
    kw_parts = []
    scale_parts = []
    for p in range(M_HEADS // 2):
        sl = slice(p * V7X_LANES, (p + 1) * V7X_LANES)
        ia, ib = SM_I + 2 * p, SM_I + 2 * p + 1
        kw = k_m[:, sl] * _lane_pick(wk, ia, ib, lo)
        kw_parts.append(kw)
        inter, kv = pair_state_terms(q_mb[:, sl], kw, v_mb[:, sl], c_ref, p, M_DK)
        num = (_dot(scs[p], _stack_masked(v_mb[:, sl], lo, hi_m))
               + jnp.concatenate(inter, axis=1) * _lane_pick(w_inter, ia, ib, lo))
        h_parts.append(num * _lane_pick(inv, ia, ib, lo))
        scale_parts.append(_lane_pick(scale, ia, ib, lo))
        for hh in range(2):
            h = 2 * p + hh
            sc_rows = jnp.broadcast_to(scale[:, SM_I + h:SM_I + h + 1], (L, M_DV)).reshape(ns, T, M_DV)
            sc_h = jnp.broadcast_to(sc_rows[:, 0:1, :], (ns, M_DK, M_DV))
            c_out[:, h] = c_ref[:, h] * sc_h + kv[hh].reshape(ns, 2, M_DK, M_DV)[:, hh]
    kw_all = jnp.concatenate(kw_parts, axis=1)
    n_out[...] = n_rows * jnp.concatenate(scale_parts, axis=1) + _dot3(segones_ref[...], kw_all)
    m_out[...] = m_new
    gate_m = _sigmoid(proj_ref[:, C_OM:C_OM + M_W])
    mix_ref[:, R_W + G_W:D_MIX] = _norm_gate(jnp.concatenate(h_parts, axis=1), hmat_ref[0:M_W, 0:M_W],
                                             gcat_ref[:, R_W + G_W:D_MIX], gate_m)


def _sample_tables(T, pos0):
    L = ROWS
    half = R_DK // 2
    inv = ROPE_BASE ** (-jnp.arange(half, dtype=F32) * 2.0 / R_DK)
    tok = np.arange(L) % T
    seq = np.arange(L) // T
    pos = pos0 + jnp.asarray(tok, F32)
    ang = pos[:, None] * inv[None, :]
    cos = jnp.tile(jnp.cos(ang), (1, V7X_LANES // half))
    sin_h = jnp.sin(ang)
    sin = jnp.tile(jnp.concatenate([-sin_h, sin_h], axis=1), (1, V7X_LANES // R_DK))
    qs = R_DK ** -0.5

    log_gamma = jnp.log(1.0 - 2.0 ** (-5.0 - jnp.arange(R_HEADS, dtype=F32)))
    tf = jnp.asarray(tok, F32)
    rel = tf[:, None] - tf[None, :]
    ok_np = (seq[:, None] == seq[None, :]) & (tok[None, :] <= tok[:, None])
    ok = jnp.asarray(ok_np)
    rdecay = jnp.where(ok[None], jnp.exp(log_gamma[:, None, None] * jnp.where(ok, rel, 0.0)[None]), 0.0)
    rdec = jnp.concatenate([rdecay[0::2], rdecay[1::2]], axis=2)
    qdec = jnp.repeat(jnp.exp(log_gamma[:, None] * (tf + 1.0)).T, R_DK, axis=1)
    kdec = jnp.repeat(jnp.exp(log_gamma[:, None] * (T - 1.0 - tf)).T, R_DK, axis=1)
    lg32 = np.log(1.0 - 2.0 ** (-5.0 - np.arange(R_HEADS, dtype=np.float64))).astype(np.float32)
    chunk_decay = tuple(float(np.exp(v * np.float32(T))) for v in lg32)

    selr = np.zeros((16, V7X_LANES), np.float32)
    for h in range(M_HEADS):
        selr[h, SM_I + h] = 1.0
        selr[M_HEADS + h, SM_F + h] = 1.0
    hv = np.arange(G_W) // G_DV
    hc = np.arange(G_QK) // G_DK
    gmask = (hv[:, None] == hc[None, :]).astype(np.float32)
    hmat = (hv[:, None] == hv[None, :]).astype(np.float32)
    ns = L // T
    mseg = (seq[:, None] == (np.arange(ns * V7X_LANES) // V7X_LANES)[None, :]).astype(np.float32)
    return dict(
        cosq=cos * qs, sinq=sin * qs, cosk=cos, sink=sin, rdec=rdec, qdec=qdec, kdec=kdec, chunk_decay=chunk_decay,
        tri=jnp.asarray(ok_np, BF16), segones=jnp.asarray(seq[:, None] == seq[None, :], BF16),
        selr=jnp.asarray(selr, BF16), hmat=jnp.asarray(hmat, BF16),
        emat=jnp.asarray(gmask.T, BF16), mseg=jnp.asarray(mseg, BF16), msegt=jnp.asarray(mseg.T, BF16))


_SAMPLE_STATE_DIMS = ((R_HEADS, R_DK, R_DV), (G_HEADS, G_DK, G_DV), (M_HEADS, M_DK, M_DV), (M_W,),
                      (V7X_LANES,), (CONV_W - 1, M_QK))


def mixer_sample(proj, tabs, lw, state, l, depth, prev, *, T):
    M = proj.shape[0]
    B = M // T
    assert B % NS == 0 and NS * T == ROWS
    L = ROWS
    const2 = lambda b: (0, 0)
    const3 = lambda b: (0, 0, 0)

    def st_spec(dims):
        return pl.BlockSpec((None, NS) + dims, lambda b: (l, b) + (0,) * len(dims))

    def row_spec(w):
        return pl.BlockSpec((None, L, w), lambda b: (l, b, 0))

    st_specs = ([st_spec(d) for d in _SAMPLE_STATE_DIMS[:3]]
                + [row_spec(M_W), row_spec(V7X_LANES), st_spec(_SAMPLE_STATE_DIMS[5])])
    in_specs = [
        pl.BlockSpec((L, D_IN_PAD), lambda b: (b, 0)),
        pl.BlockSpec((L, V7X_LANES), const2), pl.BlockSpec((L, V7X_LANES), const2),
        pl.BlockSpec((L, V7X_LANES), const2), pl.BlockSpec((L, V7X_LANES), const2),
        pl.BlockSpec((R_HEADS // 2, L, 2 * L), const3),
        pl.BlockSpec((L, R_W), const2),
        pl.BlockSpec((L, R_W), const2),
        pl.BlockSpec((L, L), const2),
        pl.BlockSpec((L, L), const2),
        pl.BlockSpec((16, V7X_LANES), const2),
        pl.BlockSpec((G_W, G_W), const2),
        pl.BlockSpec((G_QK, G_W), const2),
        pl.BlockSpec((L, NS * V7X_LANES), const2),
        pl.BlockSpec((NS * V7X_LANES, L), const2),
        pl.BlockSpec((V7X_LANES, G_QK), const2),
        pl.BlockSpec((1, G_QK), const2),
        pl.BlockSpec((CONV_W, M_QK), const2),
        pl.BlockSpec((1, M_QK), const2),
        pl.BlockSpec((1, V7X_LANES), const2),
        pl.BlockSpec((1, D_MIX), const2),
    ] + st_specs
    args = [proj, tabs["cosq"], tabs["sinq"], tabs["cosk"], tabs["sink"], tabs["rdec"], tabs["qdec"], tabs["kdec"],
            tabs["tri"], tabs["segones"], tabs["selr"], tabs["hmat"], tabs["emat"], tabs["mseg"], tabs["msegt"],
            lw["wga"], lw["bga"], lw["convw"], lw["convb"], lw["bsm"], lw["gcat"]] + list(state)
    n_in = len(args)
    in_specs += [pl.BlockSpec(memory_space=pl.ANY)] * 6
    args += list(prev)
    aliases = {n_in + i: 1 + i for i in range(6)}
    shapes = [(depth, B) + d for d in _SAMPLE_STATE_DIMS]
    shapes[3], shapes[4] = (depth, M, M_W), (depth, M, V7X_LANES)
    out_shape = (jax.ShapeDtypeStruct((M, D_MIX), BF16),) + tuple(jax.ShapeDtypeStruct(sh, F32) for sh in shapes)
    out_specs = (pl.BlockSpec((L, D_MIX), lambda b: (b, 0)),) + tuple(st_specs)
    kern = functools.partial(_mixer_sample_kernel, T=T, chunk_decay=tabs["chunk_decay"])
    outs = pl.pallas_call(
        kern, out_shape=out_shape, grid=(B // NS,), in_specs=in_specs, out_specs=out_specs,
        scratch_shapes=[pltpu.VMEM((NS, 8 + T, M_QK), F32)],
        input_output_aliases=aliases,
        compiler_params=_cparams(("parallel",)), name="mixer_sample",
    )(*args)
    return outs[0], tuple(outs[1:])


def _prep_layer(l, g_mix, w_in, w_ga2, b_ga, conv_w, conv_b, b_i, b_f, g_ret, g_gla, g_mlstm, w_out,
                g_xattn, g_mem, w_xq, w_xk, w_xv, w_xo, g_ffn, w_gate, w_up, w_down):
    w = w_in[l]
    a0 = 2 * R_HEADS * R_DK + 2 * R_W + 2 * G_QK + 2 * G_W
    m0 = a0 + G_RANK
    g0 = m0 + M_QK + 2 * M_W
    w_pad = jnp.concatenate(
        [w[:, :a0], w[:, m0:g0], w[:, g0:g0 + 2 * M_HEADS], w[:, a0:m0],
         jnp.zeros((D_MODEL, D_IN_PAD - C_SM - G_RANK - 2 * M_HEADS), F32)], axis=1).astype(BF16)
    wga = jnp.zeros((V7X_LANES, G_QK), F32).at[SM_AG:SM_AG + G_RANK].set(w_ga2[l]).astype(BF16)
    bsm = (jnp.zeros((1, V7X_LANES), F32).at[0, SM_I:SM_I + M_HEADS].set(b_i[l])
           .at[0, SM_F:SM_F + M_HEADS].set(b_f[l]))
    return dict(
        g_mix=g_mix[l], w_in=w_pad, wga=wga, bga=b_ga[l].reshape(1, G_QK), convw=conv_w[l],
        convb=conv_b[l].reshape(1, M_QK), bsm=bsm,
        gcat=jnp.concatenate([g_ret[l], g_gla[l], g_mlstm[l]]).reshape(1, D_MIX),
        w_out=w_out[l].astype(BF16), g_xattn=g_xattn[l], g_mem=g_mem[l],
        w_xq=w_xq[l].astype(BF16), w_xk=w_xk[l].astype(BF16), w_xv=w_xv[l].astype(BF16),
        w_xo=w_xo[l].astype(BF16), g_ffn=g_ffn[l], w_gate=w_gate[l].astype(BF16),
        w_up=w_up[l].astype(BF16), w_down=w_down[l].astype(BF16))


def _layer(x, mix_fn, attend, lw, g_final, *, tm, final_norm):
    B, T, D = x.shape
    M = B * T
    x2 = x.reshape(M, D)
    proj = rms_matmul(x2, lw["g_mix"], lw["w_in"], tm=tm, tn=D_IN_PAD)
    mix, new_state = mix_fn(proj)
    x2, q = out_q(mix, lw["w_out"], x2, lw["g_xattn"], lw["w_xq"], tm=tm)
    x2 = attend(q.reshape(B, T, D), x2.reshape(B, T, D)).reshape(M, D)
    x2 = swiglu_res(x2, lw["g_ffn"], lw["w_gate"], lw["w_up"], lw["w_down"], g_final,
                    tm=tm, tf=D_FF // 2, final_norm=final_norm)
    return x2.reshape(B, T, D), new_state


def kernel(x_prompt, x_sample, state_ret, state_gla, state_mlstm_C, state_mlstm_n, state_mlstm_m, state_mlstm_conv, cache_mem_k, cache_mem_v, mem_prompt, g_mix, w_in, w_ga2, b_ga, conv_w, conv_b, b_i, b_f, g_ret, g_gla, g_mlstm, w_out, g_xattn, g_mem, w_xq, w_xk, w_xv, w_xo, g_ffn, w_gate, w_up, w_down, g_final):
    B, T, D = x_prompt.shape
    Bs, Ts, _ = x_sample.shape
    depth = w_in.shape[0]
    assert T % CHUNK == 0 and Ts * NS == ROWS and Bs % NS == 0
    tabs_p = _prompt_tables(T, CHUNK, 16)
    tabs_s = _sample_tables(Ts, float(PAST_LEN))
    tm_p = 512 if (B * T) % 512 == 0 else B * T
    tm_s = 512 if (Bs * Ts) % 512 == 0 else Bs * Ts
    tq_p = 512 if T % 512 == 0 else T
    nb_x = 8 if Bs % 8 == 0 else 1

    sample_state = (state_ret, state_gla, state_mlstm_C,
                    jnp.repeat(state_mlstm_n.reshape(depth, Bs, M_W), Ts, axis=1),
                    jnp.repeat(jnp.pad(state_mlstm_m, ((0, 0), (0, 0), (0, V7X_LANES - M_HEADS))), Ts, axis=1),
                    state_mlstm_conv)
    s_shapes = [(depth, Bs) + d for d in _SAMPLE_STATE_DIMS]
    s_shapes[3], s_shapes[4] = (depth, Bs * Ts, M_W), (depth, Bs * Ts, V7X_LANES)
    s_st = tuple(jnp.zeros(sh, F32) for sh in s_shapes)
    p_st = tuple(jnp.zeros((depth, B) + d, F32) for d in _STATE_DIMS)
    p_mem = tuple(jnp.zeros((depth, B, N_MEM, X_HEADS, X_HD), F32) for _ in range(2))
    hp, hs = x_prompt, x_sample
    for l in range(depth):
        lw = _prep_layer(l, g_mix, w_in, w_ga2, b_ga, conv_w, conv_b, b_i, b_f, g_ret, g_gla, g_mlstm,
                         w_out, g_xattn, g_mem, w_xq, w_xk, w_xv, w_xo, g_ffn, w_gate, w_up, w_down)
        last = l == depth - 1
        k5, v5, kb, vb = memory_kv(mem_prompt, lw["g_mem"], lw["w_xk"], lw["w_xv"], l, depth, p_mem)
        p_mem = (k5, v5)

        def mix_p(proj):
            mix, st = mixer_prompt(proj.reshape(B, T, D_IN_PAD), tabs_p, lw, l, depth, p_st, L=CHUNK, CS=16)
            return mix.reshape(B * T, D_MIX), st

        hp, p_st = _layer(hp, mix_p, lambda q, x: xattn(q, kb, vb, x, lw["w_xo"], tq=tq_p),
                          lw, g_final, tm=tm_p, final_norm=last)
        hs, s_st = _layer(hs, lambda proj: mixer_sample(proj, tabs_s, lw, sample_state, l, depth, s_st, T=Ts),
                          lambda q, x: xattn_cache(q, cache_mem_k, cache_mem_v, l, x, lw["w_xo"], nb=nb_x),
                          lw, g_final, tm=tm_s, final_norm=last)

    p_out = p_st[:4] + (p_st[4].reshape(depth, B, M_HEADS), p_st[5])
    s_out = s_st[:3] + (s_st[3][:, ::Ts].reshape(depth, Bs, M_HEADS, M_DK), s_st[4][:, ::Ts, :M_HEADS], s_st[5])
    return (hp, hs, *p_out, *p_mem, *s_out)
```

```python
import functools
import math

import numpy as np
import jax
import jax.numpy as jnp
from jax import lax
from jax.experimental import pallas as pl
from jax.experimental.pallas import tpu as pltpu

F32 = jnp.float32
BF16 = jnp.bfloat16

D_MODEL = 1024
PAST_LEN = 16384
R_HEADS, R_DK, R_DV = 6, 64, 64
G_HEADS, G_DK, G_DV, G_RANK = 6, 32, 64, 16
G_NORMALIZER = 16.0
M_HEADS, M_DK, M_DV = 4, 64, 64
CONV_W = 4
X_HEADS = 4
X_HD = D_MODEL // X_HEADS
N_MEM = 256
D_FF = int(math.ceil(8 * D_MODEL / 3 / 256)) * 256
CHUNK = 128
EPS = 1e-6
ROPE_BASE = 10000.0

R_W = R_HEADS * R_DV
G_QK = G_HEADS * G_DK
G_W = G_HEADS * G_DV
M_QK = 2 * M_HEADS * M_DK
M_W = M_HEADS * M_DV
D_MIX = R_W + G_W + M_W

C_QR, C_KR, C_VR, C_GR = 0, 384, 768, 1152
C_QG, C_KG, C_VG, C_RG = 1536, 1728, 1920, 2304
C_QKM, C_VM, C_OM, C_SM = 2688, 3200, 3456, 3712
D_IN_PAD = 3840
SM_I, SM_F, SM_AG = 0, 4, 8
HEAD_W = 64
GLA_SAFE_LOG_RANGE = 60.0

V7X_LANES = 128
VMEM_LIMIT = 56 * 1024 * 1024


def _cparams(sem):
    return pltpu.CompilerParams(dimension_semantics=sem, vmem_limit_bytes=VMEM_LIMIT)


def _sigmoid(x):
    return 1.0 / (1.0 + jnp.exp(-x))


def _silu(x):
    return x * _sigmoid(x)


def _log_sigmoid(x):
    return jnp.minimum(x, 0.0) - jnp.log(1.0 + jnp.exp(-jnp.abs(x)))


def _dot(a, b):
    return jnp.dot(a, b, preferred_element_type=F32)


def _dot_nt(a, b):
    return lax.dot_general(a, b, (((1,), (1,)), ((), ())), preferred_element_type=F32)


def _dot_tn(a, b):
    return lax.dot_general(a, b, (((0,), (0,)), ((), ())), preferred_element_type=F32)


def _split3(x):
    hi = x.astype(BF16)
    r1 = x - hi.astype(F32)
    mid = r1.astype(BF16)
    lo = (r1 - mid.astype(F32)).astype(BF16)
    return hi, mid, lo


def _dot3(a, x):
    hi, mid, lo = _split3(x)
    return _dot(a, hi) + _dot(a, mid) + _dot(a, lo)


def _dot3_nt(a, x):
    hi, mid, lo = _split3(x)
    return _dot_nt(a, hi) + _dot_nt(a, mid) + _dot_nt(a, lo)


def _rms(x, g):
    return x * lax.rsqrt(jnp.mean(x * x, axis=-1, keepdims=True) + EPS) * g


def _rms_matmul_kernel(x_ref, g_ref, w_ref, o_ref, xn_ref):
    @pl.when(pl.program_id(1) == 0)
    def _():
        xn_ref[...] = _rms(x_ref[...], g_ref[...]).astype(BF16)

    o_ref[...] = _dot(xn_ref[...], w_ref[...]).astype(o_ref.dtype)


def rms_matmul(x, g, w, *, tm, tn, out_dtype=F32):
    M, D = x.shape
    N = w.shape[1]
    assert M % tm == 0 and N % tn == 0
    return pl.pallas_call(
        _rms_matmul_kernel,
        out_shape=jax.ShapeDtypeStruct((M, N), out_dtype),
        grid=(M // tm, N // tn),
        in_specs=[pl.BlockSpec((tm, D), lambda i, j: (i, 0)),
                  pl.BlockSpec((1, D), lambda i, j: (0, 0)),
                  pl.BlockSpec((D, tn), lambda i, j: (0, j))],
        out_specs=pl.BlockSpec((tm, tn), lambda i, j: (i, j)),
        scratch_shapes=[pltpu.VMEM((tm, D), BF16)],
        compiler_params=_cparams(("parallel", "arbitrary")),
        name="rms_matmul",
    )(x, g.reshape(1, D), w)


def _out_q_kernel(a_ref, wo_ref, x_ref, g_ref, wq_ref, x1_ref, q_ref):
    x1 = x_ref[...] + _dot(a_ref[...], wo_ref[...])
    x1_ref[...] = x1
    q_ref[...] = _dot(_rms(x1, g_ref[...]).astype(BF16), wq_ref[...]).astype(q_ref.dtype)


def out_q(a, wo, x, g, wq, *, tm):
    M, K = a.shape
    D = wo.shape[1]
    assert M % tm == 0
    row = lambda i: (i, 0)
    const = lambda i: (0, 0)
    return pl.pallas_call(
        _out_q_kernel,
        out_shape=(jax.ShapeDtypeStruct((M, D), F32), jax.ShapeDtypeStruct((M, D), BF16)),
        grid=(M // tm,),
        in_specs=[pl.BlockSpec((tm, K), row), pl.BlockSpec((K, D), const), pl.BlockSpec((tm, D), row),
                  pl.BlockSpec((1, D), const), pl.BlockSpec((D, D), const)],
        out_specs=(pl.BlockSpec((tm, D), row), pl.BlockSpec((tm, D), row)),
        compiler_params=_cparams(("parallel",)),
        name="out_q",
    )(a, wo, x, g.reshape(1, D), wq)


def _swiglu_kernel(x_ref, g_ref, wg_ref, wu_ref, wd_ref, gf_ref, o_ref, xn_ref, acc_ref, *, final_norm):
    j = pl.program_id(1)

    @pl.when(j == 0)
    def _():
        xn_ref[...] = _rms(x_ref[...], g_ref[...]).astype(BF16)
        acc_ref[...] = x_ref[...]

    xn = xn_ref[...]
    h = _silu(_dot(xn, wg_ref[...])) * _dot(xn, wu_ref[...])
    acc_ref[...] += _dot(h.astype(BF16), wd_ref[...])

    @pl.when(j == pl.num_programs(1) - 1)
    def _():
        y = acc_ref[...]
        if final_norm:
            y = _rms(y, gf_ref[...])
        o_ref[...] = y


def swiglu_res(x, g, wg, wu, wd, g_final, *, tm, tf, final_norm):
    M, D = x.shape
    FF = wg.shape[1]
    assert M % tm == 0 and FF % tf == 0
    return pl.pallas_call(
        functools.partial(_swiglu_kernel, final_norm=final_norm),
        out_shape=jax.ShapeDtypeStruct((M, D), F32),
        grid=(M // tm, FF // tf),
        in_specs=[pl.BlockSpec((tm, D), lambda i, j: (i, 0)),
                  pl.BlockSpec((1, D), lambda i, j: (0, 0)),
                  pl.BlockSpec((D, tf), lambda i, j: (0, j)),
                  pl.BlockSpec((D, tf), lambda i, j: (0, j)),
                  pl.BlockSpec((tf, D), lambda i, j: (j, 0)),
                  pl.BlockSpec((1, D), lambda i, j: (0, 0))],
        out_specs=pl.BlockSpec((tm, D), lambda i, j: (i, 0)),
        scratch_shapes=[pltpu.VMEM((tm, D), BF16), pltpu.VMEM((tm, D), F32)],
        compiler_params=_cparams(("parallel", "arbitrary")),
        name="swiglu_res",
    )(x, g.reshape(1, D), wg, wu, wd, g_final.reshape(1, D))


def _memkv_kernel(x_ref, g_ref, wk_ref, wv_ref, *refs):
    k5_ref, v5_ref, kb_ref, vb_ref = refs[-4:]
    xn = _rms(x_ref[...], g_ref[...]).astype(BF16)
    for w_ref, o5_ref, ob_ref in ((wk_ref, k5_ref, kb_ref), (wv_ref, v5_ref, vb_ref)):
        y = _dot(xn, w_ref[...])
        ob_ref[...] = y.astype(BF16)
        for h in range(X_HEADS):
            o5_ref[:, h, :] = y[:, h * X_HD:(h + 1) * X_HD]


def memory_kv(mem, g, wk, wv, l, depth, prev):
    B, _, D = mem.shape
    o5 = jax.ShapeDtypeStruct((depth, B, N_MEM, X_HEADS, X_HD), F32)
    ob = jax.ShapeDtypeStruct((B, N_MEM, D), BF16)
    in_specs = [pl.BlockSpec((None, N_MEM, D), lambda b: (b, 0, 0)),
                pl.BlockSpec((1, D), lambda b: (0, 0)),
                pl.BlockSpec((D, D), lambda b: (0, 0)),
                pl.BlockSpec((D, D), lambda b: (0, 0))]
    in_specs += [pl.BlockSpec(memory_space=pl.ANY)] * 2
    args = [mem, g.reshape(1, D), wk, wv] + list(prev)
    aliases = {4: 0, 5: 1}
    spec5 = pl.BlockSpec((None, None, N_MEM, X_HEADS, X_HD), lambda b: (l, b, 0, 0, 0))
    specb = pl.BlockSpec((None, N_MEM, D), lambda b: (b, 0, 0))
    return pl.pallas_call(
        _memkv_kernel,
        out_shape=(o5, o5, ob, ob),
        grid=(B,),
        in_specs=in_specs,
        out_specs=(spec5, spec5, specb, specb),
        input_output_aliases=aliases,
        compiler_params=_cparams(("parallel",)),
        name="memory_kv",
    )(*args)


def _xattn_kernel(q_ref, k_ref, v_ref, x_ref, wo_ref, o_ref):
    scale = X_HD ** -0.5
    parts = []
    for h in range(X_HEADS):
        sl = slice(h * X_HD, (h + 1) * X_HD)
        s = _dot_nt(q_ref[:, sl], k_ref[:, sl]) * scale
        p = jnp.exp(s - jnp.max(s, axis=-1, keepdims=True))
        l = jnp.sum(p, axis=-1, keepdims=True)
        parts.append((_dot(p.astype(BF16), v_ref[:, sl]) / l).astype(BF16))
    o_ref[...] = x_ref[...] + _dot(jnp.concatenate(parts, axis=1), wo_ref[...])


def xattn(q, mk, mv, x, wo, *, tq):
    B, T, D = q.shape
    assert T % tq == 0
    tok = pl.BlockSpec((None, tq, D), lambda b, i: (b, i, 0))
    mem = pl.BlockSpec((None, N_MEM, D), lambda b, i: (b, 0, 0))
    return pl.pallas_call(
        _xattn_kernel,
        out_shape=jax.ShapeDtypeStruct((B, T, D), F32),
        grid=(B, T // tq),
        in_specs=[tok, mem, mem, tok, pl.BlockSpec((D, D), lambda b, i: (0, 0))],
        out_specs=tok,
        compiler_params=_cparams(("parallel", "arbitrary")),
        name="xattn",
    )(q, mk, mv, x, wo)


def _xattn_cache_kernel(q_ref, k_ref, v_ref, x_ref, wo_ref, o_ref, *, nb):
    T = q_ref.shape[1]
    R = X_HEADS * T
    rowh = lax.broadcasted_iota(jnp.int32, (R, N_MEM * X_HEADS), 0) // T
    colh = lax.broadcasted_iota(jnp.int32, (R, N_MEM * X_HEADS), 1) % X_HEADS
    own = rowh == colh
    outs = []
    for s in range(nb):
        q = q_ref[s]
        qf = jnp.concatenate([q[:, h * X_HD:(h + 1) * X_HD] for h in range(X_HEADS)], axis=0)
        kf = k_ref[s].reshape(N_MEM * X_HEADS, X_HD).astype(BF16)
        vf = v_ref[s].reshape(N_MEM * X_HEADS, X_HD).astype(BF16)
        sc = jnp.where(own, _dot_nt(qf, kf) * (X_HD ** -0.5), -jnp.inf)
        p = jnp.exp(sc - jnp.max(sc, axis=-1, keepdims=True))
        l = jnp.sum(p, axis=-1, keepdims=True)
        o = (_dot(p.astype(BF16), vf) / l).astype(BF16)
        outs.append(jnp.concatenate([o[h * T:(h + 1) * T] for h in range(X_HEADS)], axis=1))
    y = _dot(jnp.concatenate(outs, axis=0), wo_ref[...])
    for s in range(nb):
        o_ref[s] = x_ref[s] + y[s * T:(s + 1) * T]


def xattn_cache(q, ck, cv, l, x, wo, *, nb):
    B, T, D = q.shape
    assert B % nb == 0
    cspec = pl.BlockSpec((None, nb, N_MEM, X_HEADS, X_HD), lambda b: (l, b, 0, 0, 0))
    tok = pl.BlockSpec((nb, T, D), lambda b: (b, 0, 0))
    return pl.pallas_call(
        functools.partial(_xattn_cache_kernel, nb=nb),
        out_shape=jax.ShapeDtypeStruct((B, T, D), F32),
        grid=(B // nb,),
        in_specs=[tok, cspec, cspec, tok, pl.BlockSpec((D, D), lambda b: (0, 0))],
        out_specs=tok,
        compiler_params=_cparams(("parallel",)),
        name="xattn_cache",
    )(q, ck, cv, x, wo)


def _eye(n):
    r = lax.broadcasted_iota(jnp.int32, (n, n), 0)
    c = lax.broadcasted_iota(jnp.int32, (n, n), 1)
    return jnp.where(r == c, 1.0, 0.0).astype(BF16)


_STATE_DIMS = ((R_HEADS, R_DK, R_DV), (G_HEADS, G_DK, G_DV), (M_HEADS, M_DK, M_DV), (M_HEADS, M_DK),
               (1, M_HEADS), (CONV_W - 1, M_QK))


def _split2(x):
    hi = x.astype(BF16)
    lo = (x - hi.astype(F32)).astype(BF16)
    return hi, lo


def _head_mean_sq(o, hmat):
    hi, lo = _split2(o * o)
    return (_dot(hi, hmat) + _dot(lo, hmat)) * (1.0 / HEAD_W)


def _norm_gate(o, hmat, g_row, gate):
    return (o * lax.rsqrt(_head_mean_sq(o, hmat) + EPS) * g_row * gate).astype(BF16)


def _lane_pick(cols, idx_lo, idx_hi, lo_mask):
    L = cols.shape[0]
    a = jnp.broadcast_to(cols[:, idx_lo:idx_lo + 1], (L, V7X_LANES))
    b = jnp.broadcast_to(cols[:, idx_hi:idx_hi + 1], (L, V7X_LANES))
    return jnp.where(lo_mask, a, b)


def _stack_masked(x, m_a, m_b):
    z = jnp.zeros_like(x)
    return jnp.concatenate([jnp.where(m_a, x, z), jnp.where(m_b, x, z)], axis=0)


def _mixer_prompt_kernel(proj_ref, cosq_ref, sinq_ref, cosk_ref, sink_ref, rdec_ref, qdec_ref, kdec_ref,
                         tri_ref, btri_ref, bones_ref, selr_ref, hmat_ref, emat_ref, pmask_ref,
                         wga_ref, bga_ref, convw_ref, convb_ref, bsm_ref, gcat_ref, *rest,
                         L, CS, NC, chunk_decay):
    (mix_ref, sr_out, sg_out, c_out, n_out, m_out, conv_out,
     srp_scr, sgt_scr, cp_scr, n_scr, m_scr, conv_scr, og_scr) = rest[-14:]
    c = pl.program_id(1)

    @pl.when(c == 0)
    def _():
        srp_scr[...] = jnp.zeros_like(srp_scr)
        sgt_scr[...] = jnp.zeros_like(sgt_scr)
        cp_scr[...] = jnp.zeros_like(cp_scr)
        n_scr[...] = jnp.zeros_like(n_scr)
        m_scr[...] = jnp.zeros_like(m_scr)
        conv_scr[0:8, :] = jnp.zeros((8, M_QK), F32)

    lane = lax.broadcasted_iota(jnp.int32, (L, V7X_LANES), 1)
    lo = lane < HEAD_W
    hi_m = lane >= HEAD_W
    first_half = (lane % R_DK) < (R_DK // 2)
    row2 = lax.broadcasted_iota(jnp.int32, (L, 2 * L), 0)
    col2 = lax.broadcasted_iota(jnp.int32, (L, 2 * L), 1) % L
    causal2 = col2 <= row2
    lo_row = lax.broadcasted_iota(jnp.int32, (1, V7X_LANES), 1) < HEAD_W
    lo8 = lax.broadcasted_iota(jnp.int32, (8, V7X_LANES), 1) < HEAD_W
    hmat = hmat_ref[...]
    pmask = pmask_ref[...]

    def rope(x, cos_ref, sin_ref):
        cos = cos_ref[...]
        sin = sin_ref[...]
        parts = []
        for t in range(R_W // V7X_LANES):
            xs = x[:, t * V7X_LANES:(t + 1) * V7X_LANES]
            rot = jnp.where(first_half, pltpu.roll(xs, V7X_LANES - R_DK // 2, 1), pltpu.roll(xs, R_DK // 2, 1))
            parts.append(xs * cos + rot * sin)
        return jnp.concatenate(parts, axis=1)

    q_r = rope(proj_ref[:, C_QR:C_QR + R_W], cosq_ref, sinq_ref)
    k_r = rope(proj_ref[:, C_KR:C_KR + R_W], cosk_ref, sink_ref)
    q_rb = q_r.astype(BF16)
    k_rb = k_r.astype(BF16)
    kd_rb = (k_r * kdec_ref[...]).astype(BF16)
    v_rb = proj_ref[:, C_VR:C_VR + R_W].astype(BF16)
    o_parts = []
    for p in range(R_HEADS // 2):
        sl = slice(p * V7X_LANES, (p + 1) * V7X_LANES)
        qs, ks, vs = q_rb[:, sl], k_rb[:, sl], v_rb[:, sl]
        sc = _dot_nt(qs, _stack_masked(ks, lo, hi_m)) * rdec_ref[p]
        sp = srp_scr[p]
        o = _dot(sc.astype(BF16), _stack_masked(vs, lo, hi_m)) + _dot(qs, sp.astype(BF16)) * qdec_ref[:, sl]
        cd = jnp.where(lo_row, chunk_decay[2 * p], chunk_decay[2 * p + 1])
        srp_scr[p] = sp * cd + _dot_tn(kd_rb[:, sl], vs) * pmask
        o_parts.append(o)
    o_r = jnp.concatenate(o_parts, axis=1)
    gate_r = _silu(proj_ref[:, C_GR:C_GR + R_W])
    mix_ref[:, 0:R_W] = _norm_gate(o_r, hmat, gcat_ref[:, 0:R_W], gate_r)

    small = proj_ref[:, C_SM:C_SM + V7X_LANES]

    u = proj_ref[:, C_QKM:C_QKM + M_QK]
    conv_scr[8:8 + L, :] = u
    y = convb_ref[...]
    for j in range(CONV_W - 1):
        y = y + conv_scr[5 + j:5 + j + L, :] * convw_ref[j:j + 1, :]
    y = y + u * convw_ref[CONV_W - 1:CONV_W, :]
    tail = conv_scr[5 + L:8 + L, :]
    conv_scr[5:8, :] = tail
    qk = _silu(y)
    q_m = qk[:, :M_W]
    k_m = qk[:, M_W:] * (M_DK ** -0.5)
    q_mb = q_m.astype(BF16)
    k_mb = k_m.astype(BF16)
    v_mb = proj_ref[:, C_VM:C_VM + M_W].astype(BF16)
    gates = small + bsm_ref[...]
    f_cum = _dot3(tri_ref[...], _log_sigmoid(gates))
    i_rows = _dot3_nt(selr_ref[...], gates)
    f_rows = _dot3_nt(selr_ref[...], f_cum)
    head_lane = (lane >= SM_I) & (lane < SM_I + M_HEADS)
    fc = jnp.where(head_lane, pltpu.roll(f_cum, V7X_LANES - (SM_F - SM_I), 1), 0.0)
    m_prev = m_scr[0:1, :]
    a = fc + m_prev
    mx = jnp.full((L, V7X_LANES), -jnp.inf, F32)
    row = lax.broadcasted_iota(jnp.int32, (L, L), 0)
    col = lax.broadcasted_iota(jnp.int32, (L, L), 1)
    causal = col <= row
    dms = []
    for h in range(M_HEADS):
        dm = jnp.where(causal, (fc[:, SM_I + h:SM_I + h + 1] - f_rows[M_HEADS + h:M_HEADS + h + 1, :])
                       + i_rows[h:h + 1, :], -jnp.inf)
        dms.append(dm)
        mx = jnp.where(lane == SM_I + h, jnp.max(dm, axis=-1, keepdims=True), mx)
    m_tok = jnp.maximum(a, mx)
    w_inter = jnp.exp(a - m_tok)
    n_full = n_scr[0:1, :]
    hs, lows = _split2(q_m * n_full)
    qn_b = _dot(hs, hmat_ref[0:M_W, 0:M_W]) + _dot(lows, hmat_ref[0:M_W, 0:M_W])
    rs_all = jnp.zeros((L, V7X_LANES), F32)
    qn_all = jnp.zeros((L, V7X_LANES), F32)
    scs = []
    for p in range(M_HEADS // 2):
        sl = slice(p * V7X_LANES, (p + 1) * V7X_LANES)
        sc = _dot_nt(q_mb[:, sl], _stack_masked(k_mb[:, sl], lo, hi_m))
        mt_a = jnp.broadcast_to(m_tok[:, SM_I + 2 * p:SM_I + 2 * p + 1], (L, L))
        mt_b = jnp.broadcast_to(m_tok[:, SM_I + 2 * p + 1:SM_I + 2 * p + 2], (L, L))
        sc_a = sc[:, :L] * jnp.exp(dms[2 * p] - mt_a)
        sc_b = sc[:, L:] * jnp.exp(dms[2 * p + 1] - mt_b)
        rs_all = jnp.where(lane == SM_I + 2 * p, jnp.sum(sc_a, axis=-1, keepdims=True), rs_all)
        rs_all = jnp.where(lane == SM_I + 2 * p + 1, jnp.sum(sc_b, axis=-1, keepdims=True), rs_all)
        qn_all = jnp.where(lane == SM_I + 2 * p, qn_b[:, p * V7X_LANES:p * V7X_LANES + 1], qn_all)
        qn_all = jnp.where(lane == SM_I + 2 * p + 1, qn_b[:, p * V7X_LANES + HEAD_W:p * V7X_LANES + HEAD_W + 1], qn_all)
        scs.append(jnp.concatenate([sc_a, sc_b], axis=1).astype(BF16))
    den = rs_all + qn_all * w_inter
    inv = 1.0 / jnp.maximum(jnp.abs(den), jnp.exp(-m_tok))
    m_new = jnp.where(head_lane, m_tok, 0.0)[L - 1:L, :]
    wk = jnp.exp(fc[L - 1:L, :] - fc + gates - m_new)
    scale = jnp.broadcast_to(jnp.exp(a[L - 1:L, :] - m_new), (8, V7X_LANES))
    h_parts = []
    kw_parts = []
    scale_parts = []
    for p in range(M_HEADS // 2):
        sl = slice(p * V7X_LANES, (p + 1) * V7X_LANES)
        ia, ib = SM_I + 2 * p, SM_I + 2 * p + 1
        cpair = cp_scr[p]
        num = (_dot(scs[p], _stack_masked(v_mb[:, sl], lo, hi_m))
               + _dot(q_mb[:, sl], cpair.astype(BF16)) * _lane_pick(w_inter, ia, ib, lo))
        h_parts.append(num * _lane_pick(inv, ia, ib, lo))
        kw = k_m[:, sl] * _lane_pick(wk, ia, ib, lo)
        kw_parts.append(kw)
        sc_row = _lane_pick(scale, ia, ib, lo8)[0:1]
        scale_parts.append(sc_row)
        cp_scr[p] = cpair * sc_row + _dot_tn(kw.astype(BF16), v_mb[:, sl]) * pmask
    kw_all = jnp.concatenate(kw_parts, axis=1)
    n_new = n_full * jnp.concatenate(scale_parts, axis=1) + jnp.sum(kw_all, axis=0, keepdims=True)
    n_scr[...] = jnp.broadcast_to(n_new, n_scr.shape)
    m_scr[...] = jnp.broadcast_to(m_new, m_scr.shape)
    gate_m = _sigmoid(proj_ref[:, C_OM:C_OM + M_W])
    mix_ref[:, R_W + G_W:D_MIX] = _norm_gate(jnp.concatenate(h_parts, axis=1), hmat_ref[0:M_W, 0:M_W],
                                             gcat_ref[:, R_W + G_W:D_MIX], gate_m)

    z = _dot(small.astype(BF16), wga_ref[...]) + bga_ref[...]
    log_a = _log_sigmoid(z) / G_NORMALIZER
    b = _dot3(tri_ref[...], log_a)
    b_last = b[L - 1:L, :]
    safe = jnp.max(-b_last) <= GLA_SAFE_LOG_RANGE
    q_g = proj_ref[:, C_QG:C_QG + G_QK] * (G_DK ** -0.5)
    k_g = proj_ref[:, C_KG:C_KG + G_QK]
    v_gb = proj_ref[:, C_VG:C_VG + G_W].astype(BF16)

    slot = lax.broadcasted_iota(jnp.int32, (V7X_LANES, V7X_LANES), 1) // G_DK
    row_head = lax.broadcasted_iota(jnp.int32, (V7X_LANES, V7X_LANES), 0) // HEAD_W
    lane_q = lane // G_DK

    def pad2(x):
        return jnp.concatenate([x, jnp.zeros((x.shape[0], 2 * V7X_LANES - G_QK), x.dtype)], axis=1)

    def slab(x_p, g):
        s0 = (2 * g * G_DK) // V7X_LANES * V7X_LANES
        return x_p[:, s0:s0 + V7X_LANES], (2 * g * G_DK - s0) // G_DK

    def gla_inter(q_p):
        return jnp.concatenate([_dot_nt(slab(q_p, g)[0], sgt_scr[g].astype(BF16)) for g in range(G_HEADS // 2)],
                               axis=1)

    def gla_update(decay_p, k_p, v_b):
        for g in range(G_HEADS // 2):
            ks, h_a = slab(k_p, g)
            kv = _dot_tn(v_b[:, g * V7X_LANES:(g + 1) * V7X_LANES], ks)
            sgt_scr[g] = sgt_scr[g] * slab(decay_p, g)[0] + jnp.where(slot == h_a + row_head, kv, 0.0)

    @pl.when(safe)
    def _():
        qt_p = pad2((q_g * jnp.exp(b)).astype(BF16))
        kt_p = pad2((k_g * jnp.exp(-b)).astype(BF16))
        kl_p = pad2((k_g * jnp.exp(b_last - b)).astype(BF16))
        o_inter = gla_inter(qt_p)
        parts = []
        for g in range(G_HEADS // 2):
            qs, h_a = slab(qt_p, g)
            ks, _ = slab(kt_p, g)
            sc = _dot_nt(qs, _stack_masked(ks, lane_q == h_a, lane_q == h_a + 1))
            sc = jnp.where(causal2, sc, 0.0).astype(BF16)
            vs = v_gb[:, g * V7X_LANES:(g + 1) * V7X_LANES]
            parts.append(_dot(sc, _stack_masked(vs, lo, hi_m)))
        og_scr[...] = o_inter + jnp.concatenate(parts, axis=1)
        gla_update(pad2(jnp.exp(b_last)), kl_p, v_gb)

    @pl.when(jnp.logical_not(safe))
    def _():
        v_g = proj_ref[:, C_VG:C_VG + G_W]
        b_loc = _dot3(btri_ref[...], log_a)
        b_tot = _dot3(bones_ref[...], log_a)
        qt_p = pad2((q_g * jnp.exp(b_loc)).astype(BF16))
        kt_p = pad2((k_g * jnp.exp(b_tot - b_loc)).astype(BF16))
        d_tot_p = pad2(jnp.exp(b_tot))
        emat = emat_ref[...]
        sub_row = lax.broadcasted_iota(jnp.int32, (CS, G_QK), 0)
        for blk in range(L // CS):
            r0 = blk * CS
            rs = slice(r0, r0 + CS)
            o_blk = gla_inter(qt_p[rs])
            bI, qI, kI = b_loc[rs], q_g[rs], k_g[rs]
            terms = []
            for j in range(CS):
                e = jnp.exp(jnp.where(sub_row >= j, bI - bI[j:j + 1], -jnp.inf))
                terms.append(e * qI * kI[j:j + 1])
            t = jnp.concatenate(terms, axis=0).astype(BF16)
            w = _dot(t, emat)
            for j in range(CS):
                o_blk = o_blk + w[j * CS:(j + 1) * CS] * v_g[r0 + j:r0 + j + 1]
            og_scr[rs, :] = o_blk
            gla_update(d_tot_p[r0:r0 + 1], kt_p[rs], v_gb[rs])

    gate_g = _silu(proj_ref[:, C_RG:C_RG + G_W])
    mix_ref[:, R_W:R_W + G_W] = _norm_gate(og_scr[...], hmat, gcat_ref[:, R_W:R_W + G_W], gate_g)

    @pl.when(c == NC - 1)
    def _():
        eye_k = _eye(G_DK)
        for p in range(R_HEADS // 2):
            sr_out[2 * p] = srp_scr[p, 0:HEAD_W, 0:HEAD_W]
            sr_out[2 * p + 1] = srp_scr[p, HEAD_W:, HEAD_W:]
        for h in range(G_HEADS):
            g, hh = h // 2, h % 2
            c0 = (h * G_DK) % V7X_LANES
            sg_out[h] = _dot3_nt(eye_k, sgt_scr[g, hh * G_DV:(hh + 1) * G_DV, c0:c0 + G_DK])
        for p in range(M_HEADS // 2):
            c_out[2 * p] = cp_scr[p, 0:HEAD_W, 0:HEAD_W]
            c_out[2 * p + 1] = cp_scr[p, HEAD_W:, HEAD_W:]
        for h in range(M_HEADS):
            n_out[h:h + 1, :] = n_scr[0:1, h * M_DK:(h + 1) * M_DK]
        m_out[...] = m_scr[0:1, SM_I:SM_I + M_HEADS]
        conv_out[...] = conv_scr[5:8, :]


def _prompt_tables(T, L, CS):
    half = R_DK // 2
    inv = ROPE_BASE ** (-jnp.arange(half, dtype=F32) * 2.0 / R_DK)
    pos = jnp.arange(T, dtype=F32)
    ang = pos[:, None] * inv[None, :]
    cos = jnp.tile(jnp.cos(ang), (1, V7X_LANES // half))
    sin_h = jnp.sin(ang)
    sin = jnp.tile(jnp.concatenate([-sin_h, sin_h], axis=1), (1, V7X_LANES // R_DK))
    qs = R_DK ** -0.5

    log_gamma = jnp.log(1.0 - 2.0 ** (-5.0 - jnp.arange(R_HEADS, dtype=F32)))
    idx = jnp.arange(L, dtype=F32)
    rel = idx[:, None] - idx[None, :]
    causal = rel >= 0
    rdecay = jnp.where(causal[None], jnp.exp(log_gamma[:, None, None] * jnp.where(causal, rel, 0.0)[None]), 0.0)
    rdec = jnp.concatenate([rdecay[0::2], rdecay[1::2]], axis=2)
    qdec = jnp.repeat(jnp.exp(log_gamma[:, None] * (idx + 1.0)).T, R_DK, axis=1)
    kdec = jnp.repeat(jnp.exp(log_gamma[:, None] * (L - 1.0 - idx)).T, R_DK, axis=1)
    lg32 = np.log(1.0 - 2.0 ** (-5.0 - np.arange(R_HEADS, dtype=np.float64))).astype(np.float32)
    chunk_decay = tuple(float(np.exp(v * np.float32(L))) for v in lg32)

    r = np.arange(L)
    tri = (r[None, :] <= r[:, None])
    same = (r[None, :] // CS) == (r[:, None] // CS)
    selr = np.zeros((16, V7X_LANES), np.float32)
    for h in range(M_HEADS):
        selr[h, SM_I + h] = 1.0
        selr[M_HEADS + h, SM_F + h] = 1.0
    hv = np.arange(G_W) // G_DV
    hc = np.arange(G_QK) // G_DK
    gmask = (hv[:, None] == hc[None, :]).astype(np.float32)
    hmat = (hv[:, None] == hv[None, :]).astype(np.float32)
    pm = np.arange(V7X_LANES) // HEAD_W
    pmask = (pm[:, None] == pm[None, :]).astype(np.float32)
    return dict(
        cosq=cos * qs, sinq=sin * qs, cosk=cos, sink=sin, rdec=rdec, qdec=qdec, kdec=kdec, chunk_decay=chunk_decay,
        tri=jnp.asarray(tri, BF16), btri=jnp.asarray(tri & same, BF16), bones=jnp.asarray(same, BF16),
        selr=jnp.asarray(selr, BF16), hmat=jnp.asarray(hmat, BF16), emat=jnp.asarray(gmask.T, BF16),
        pmask=jnp.asarray(pmask, F32))


def mixer_prompt(proj, tabs, lw, l_out, depth, prev, *, L, CS):
    B, T, _ = proj.shape
    NC = T // L
    const2 = lambda b, c: (0, 0)
    const3 = lambda b, c: (0, 0, 0)
    tspec = pl.BlockSpec((L, V7X_LANES), lambda b, c: (c, 0))
    in_specs = [
        pl.BlockSpec((None, L, D_IN_PAD), lambda b, c: (b, c, 0)),
        tspec, tspec, tspec, tspec,
        pl.BlockSpec((R_HEADS // 2, L, 2 * L), const3),
        pl.BlockSpec((L, R_W), const2),
        pl.BlockSpec((L, R_W), const2),
        pl.BlockSpec((L, L), const2),
        pl.BlockSpec((L, L), const2),
        pl.BlockSpec((L, L), const2),
        pl.BlockSpec((16, V7X_LANES), const2),
        pl.BlockSpec((G_W, G_W), const2),
        pl.BlockSpec((G_QK, G_W), const2),
        pl.BlockSpec((V7X_LANES, V7X_LANES), const2),
        pl.BlockSpec((V7X_LANES, G_QK), const2),
        pl.BlockSpec((1, G_QK), const2),
        pl.BlockSpec((CONV_W, M_QK), const2),
        pl.BlockSpec((1, M_QK), const2),
        pl.BlockSpec((1, V7X_LANES), const2),
        pl.BlockSpec((1, D_MIX), const2),
    ]
    args = [proj, tabs["cosq"], tabs["sinq"], tabs["cosk"], tabs["sink"], tabs["rdec"], tabs["qdec"], tabs["kdec"],
            tabs["tri"], tabs["btri"], tabs["bones"], tabs["selr"], tabs["hmat"], tabs["emat"],
            tabs["pmask"], lw["wga"], lw["bga"], lw["convw"], lw["convb"], lw["bsm"], lw["gcat"]]
    n_in = len(args)
    in_specs += [pl.BlockSpec(memory_space=pl.ANY)] * 6
    args += list(prev)
    aliases = {n_in + i: 1 + i for i in range(6)}

    def st_spec(dims):
        return pl.BlockSpec((None, None) + dims, lambda b, c: (l_out, b) + (0,) * len(dims))

    out_shape = (jax.ShapeDtypeStruct((B, T, D_MIX), BF16),) + tuple(
        jax.ShapeDtypeStruct((depth, B) + d, F32) for d in _STATE_DIMS)
    out_specs = (pl.BlockSpec((None, L, D_MIX), lambda b, c: (b, c, 0)),) + tuple(st_spec(d) for d in _STATE_DIMS)
    scratch = [
        pltpu.VMEM((R_HEADS // 2, V7X_LANES, V7X_LANES), F32),
        pltpu.VMEM((G_HEADS // 2, V7X_LANES, V7X_LANES), F32),
        pltpu.VMEM((M_HEADS // 2, V7X_LANES, V7X_LANES), F32),
        pltpu.VMEM((8, M_W), F32),
        pltpu.VMEM((8, V7X_LANES), F32),
        pltpu.VMEM((8 + L, M_QK), F32),
        pltpu.VMEM((L, G_W), F32),
    ]
    kern = functools.partial(_mixer_prompt_kernel, L=L, CS=CS, NC=NC, chunk_decay=tabs["chunk_decay"])
    outs = pl.pallas_call(
        kern, out_shape=out_shape, grid=(B, NC), in_specs=in_specs, out_specs=out_specs,
        scratch_shapes=scratch, input_output_aliases=aliases,
        compiler_params=_cparams(("parallel", "arbitrary")), name="mixer_prompt",
    )(*args)
    return outs[0], tuple(outs[1:])


NS = 16
ROWS = 128


def _seq_bcast(x, t, T):
    n, w = x.shape
    x3 = x.reshape(n // T, T, w)
    return jnp.broadcast_to(x3[:, t:t + 1, :], (n // T, T, w)).reshape(n, w)


def _mixer_sample_kernel(proj_ref, cosq_ref, sinq_ref, cosk_ref, sink_ref, rdec_ref, qdec_ref, kdec_ref,
                         tri_ref, segones_ref, selr_ref, hmat_ref, emat_ref, mseg_ref, msegt_ref,
                         wga_ref, bga_ref, convw_ref, convb_ref, bsm_ref, gcat_ref,
                         sr_ref, sg_ref, c_ref, n_ref, m_ref, conv_ref, *rest, T, chunk_decay):
    (mix_ref, sr_out, sg_out, c_out, n_out, m_out, conv_out, conv_scr) = rest[-8:]
    L = ROWS
    ns = L // T
    lane = lax.broadcasted_iota(jnp.int32, (L, V7X_LANES), 1)
    lo = lane < HEAD_W
    hi_m = lane >= HEAD_W
    first_half = (lane % R_DK) < (R_DK // 2)
    row = lax.broadcasted_iota(jnp.int32, (L, L), 0)
    col = lax.broadcasted_iota(jnp.int32, (L, L), 1)
    segcausal = (row // T == col // T) & (col <= row)
    hmat = hmat_ref[...]
    mseg = mseg_ref[...]
    msegt = msegt_ref[...]

    def rope(x, cos_ref, sin_ref):
        cos = cos_ref[...]
        sin = sin_ref[...]
        parts = []
        for t in range(R_W // V7X_LANES):
            xs = x[:, t * V7X_LANES:(t + 1) * V7X_LANES]
            rot = jnp.where(first_half, pltpu.roll(xs, V7X_LANES - R_DK // 2, 1), pltpu.roll(xs, R_DK // 2, 1))
            parts.append(xs * cos + rot * sin)
        return jnp.concatenate(parts, axis=1)

    def tile_lanes(x, n):
        return jnp.concatenate([x] * n, axis=1)

    def tile_rows(x, n):
        return jnp.concatenate([x] * n, axis=0)

    def pair_state_terms(qs, ks_f32, vs, st_ref, p, hd):
        r = st_ref[:, 2 * p:2 * p + 2].reshape(ns * 2 * hd, hd).astype(BF16)
        kt = tile_rows(ks_f32.T.astype(BF16), ns) * msegt
        inter, kv = [], []
        for hh, m in ((0, lo), (1, hi_m)):
            qh = jnp.where(m, qs, jnp.zeros_like(qs))
            inter.append(_dot(tile_lanes(qh, ns) * mseg, r))
            kv.append(_dot(kt, vs[:, hh * hd:(hh + 1) * hd]))
        return inter, kv

    q_r = rope(proj_ref[:, C_QR:C_QR + R_W], cosq_ref, sinq_ref)
    k_r = rope(proj_ref[:, C_KR:C_KR + R_W], cosk_ref, sink_ref)
    q_rb = q_r.astype(BF16)
    k_rb = k_r.astype(BF16)
    kd_r = k_r * kdec_ref[...]
    v_rb = proj_ref[:, C_VR:C_VR + R_W].astype(BF16)
    o_parts = []
    for p in range(R_HEADS // 2):
        sl = slice(p * V7X_LANES, (p + 1) * V7X_LANES)
        qs, ks, vs = q_rb[:, sl], k_rb[:, sl], v_rb[:, sl]
        sc = _dot_nt(qs, _stack_masked(ks, lo, hi_m)) * rdec_ref[p]
        inter, kv = pair_state_terms(qs, kd_r[:, sl], vs, sr_ref, p, R_DK)
        o = (_dot(sc.astype(BF16), _stack_masked(vs, lo, hi_m))
             + jnp.concatenate(inter, axis=1) * qdec_ref[:, sl])
        o_parts.append(o)
        for hh in range(2):
            h = 2 * p + hh
            sr_out[:, h] = sr_ref[:, h] * chunk_decay[h] + kv[hh].reshape(ns, 2, R_DK, R_DV)[:, hh]
    gate_r = _silu(proj_ref[:, C_GR:C_GR + R_W])
    mix_ref[:, 0:R_W] = _norm_gate(jnp.concatenate(o_parts, axis=1), hmat, gcat_ref[:, 0:R_W], gate_r)

    small = proj_ref[:, C_SM:C_SM + V7X_LANES]
    z = _dot(small.astype(BF16), wga_ref[...]) + bga_ref[...]
    log_a = _log_sigmoid(z) / G_NORMALIZER
    b = _dot3(tri_ref[...], log_a)
    b_tot = _seq_bcast(b, T - 1, T)
    q_g = proj_ref[:, C_QG:C_QG + G_QK] * (G_DK ** -0.5)
    k_g = proj_ref[:, C_KG:C_KG + G_QK]
    v_g = proj_ref[:, C_VG:C_VG + G_W]
    v_gb = v_g.astype(BF16)
    tok = lax.broadcasted_iota(jnp.int32, (L, G_QK), 0) % T
    emat = emat_ref[...]
    o_g = jnp.zeros((L, G_W), F32)
    for j in range(T):
        e = jnp.exp(jnp.where(tok >= j, b - _seq_bcast(b, j, T), -jnp.inf))
        tj = (e * q_g * _seq_bcast(k_g, j, T)).astype(BF16)
        o_g = o_g + _dot(tj, emat) * _seq_bcast(v_g, j, T)
    qt = (q_g * jnp.exp(b)).astype(BF16)
    kl = k_g * jnp.exp(b_tot - b)
    dtot = jnp.exp(b_tot)
    lane_h = lane // G_DK
    inter_parts = [None] * G_HEADS
    for h0, heads in ((0, (0, 1, 2, 3)), (2, (4, 5))):
        c0 = h0 * G_DK
        q_s = qt[:, c0:c0 + V7X_LANES]
        r = sg_ref[:, h0:h0 + 4].reshape(ns * V7X_LANES, G_DV)
        rb = r.astype(BF16)
        kt = tile_rows(kl[:, c0:c0 + V7X_LANES].T.astype(BF16), ns) * msegt
        dt = dtot[:, c0:c0 + V7X_LANES].T
        dcols = []
        for s in range(ns):
            dcols.append(jnp.broadcast_to(dt[:, s * T:s * T + 1], (V7X_LANES, G_DV)))
        dfull = jnp.concatenate(dcols, axis=0).reshape(ns, 4, G_DK, G_DV)
        for h in heads:
            qh = jnp.where(lane_h == h - h0, q_s, jnp.zeros_like(q_s))
            inter_parts[h] = _dot(tile_lanes(qh, ns) * mseg, rb)
            kv = _dot(kt, v_gb[:, h * G_DV:(h + 1) * G_DV])
            sg_out[:, h] = sg_ref[:, h] * dfull[:, h - h0] + kv.reshape(ns, 4, G_DK, G_DV)[:, h - h0]
    o_g = o_g + jnp.concatenate(inter_parts, axis=1)
    gate_g = _silu(proj_ref[:, C_RG:C_RG + G_W])
    mix_ref[:, R_W:R_W + G_W] = _norm_gate(o_g, hmat, gcat_ref[:, R_W:R_W + G_W], gate_g)

    u = proj_ref[:, C_QKM:C_QKM + M_QK]
    conv_scr[:, 5:8, :] = conv_ref[...]
    conv_scr[:, 8:8 + T, :] = u.reshape(ns, T, M_QK)
    y = convb_ref[...]
    for j in range(CONV_W - 1):
        y = y + conv_scr[:, 5 + j:5 + j + T, :].reshape(L, M_QK) * convw_ref[j:j + 1, :]
    y = y + u * convw_ref[CONV_W - 1:CONV_W, :]
    conv_out[...] = conv_scr[:, 5 + T:8 + T, :]
    qk = _silu(y)
    q_m = qk[:, :M_W]
    k_m = qk[:, M_W:] * (M_DK ** -0.5)
    q_mb = q_m.astype(BF16)
    k_mb = k_m.astype(BF16)
    v_mb = proj_ref[:, C_VM:C_VM + M_W].astype(BF16)
    gates = small + bsm_ref[...]
    f_cum = _dot3(tri_ref[...], _log_sigmoid(gates))
    i_rows = _dot3_nt(selr_ref[...], gates)
    f_rows = _dot3_nt(selr_ref[...], f_cum)
    head_lane = (lane >= SM_I) & (lane < SM_I + M_HEADS)
    fc = jnp.where(head_lane, pltpu.roll(f_cum, V7X_LANES - (SM_F - SM_I), 1), 0.0)
    m_prev = m_ref[...]
    a = fc + m_prev
    mx = jnp.full((L, V7X_LANES), -jnp.inf, F32)
    dms = []
    for h in range(M_HEADS):
        dm = jnp.where(segcausal, (fc[:, SM_I + h:SM_I + h + 1] - f_rows[M_HEADS + h:M_HEADS + h + 1, :])
                       + i_rows[h:h + 1, :], -jnp.inf)
        dms.append(dm)
        mx = jnp.where(lane == SM_I + h, jnp.max(dm, axis=-1, keepdims=True), mx)
    m_tok = jnp.maximum(a, mx)
    w_inter = jnp.exp(a - m_tok)
    n_rows = n_ref[...]
    hs, lows = _split2(q_m * n_rows)
    qn_b = _dot(hs, hmat_ref[0:M_W, 0:M_W]) + _dot(lows, hmat_ref[0:M_W, 0:M_W])
    rs_all = jnp.zeros((L, V7X_LANES), F32)
    qn_all = jnp.zeros((L, V7X_LANES), F32)
    scs = []
    for p in range(M_HEADS // 2):
        sl = slice(p * V7X_LANES, (p + 1) * V7X_LANES)
        sc = _dot_nt(q_mb[:, sl], _stack_masked(k_mb[:, sl], lo, hi_m))
        mt_a = jnp.broadcast_to(m_tok[:, SM_I + 2 * p:SM_I + 2 * p + 1], (L, L))
        mt_b = jnp.broadcast_to(m_tok[:, SM_I + 2 * p + 1:SM_I + 2 * p + 2], (L, L))
        sc_a = sc[:, :L] * jnp.exp(dms[2 * p] - mt_a)
        sc_b = sc[:, L:] * jnp.exp(dms[2 * p + 1] - mt_b)
        rs_all = jnp.where(lane == SM_I + 2 * p, jnp.sum(sc_a, axis=-1, keepdims=True), rs_all)
        rs_all = jnp.where(lane == SM_I + 2 * p + 1, jnp.sum(sc_b, axis=-1, keepdims=True), rs_all)
        qn_all = jnp.where(lane == SM_I + 2 * p, qn_b[:, p * V7X_LANES:p * V7X_LANES + 1], qn_all)
        qn_all = jnp.where(lane == SM_I + 2 * p + 1, qn_b[:, p * V7X_LANES + HEAD_W:p * V7X_LANES + HEAD_W + 1], qn_all)
        scs.append(jnp.concatenate([sc_a, sc_b], axis=1).astype(BF16))
    den = rs_all + qn_all * w_inter
    inv = 1.0 / jnp.maximum(jnp.abs(den), jnp.exp(-m_tok))
    m_new = _seq_bcast(jnp.where(head_lane, m_tok, 0.0), T - 1, T)
    wk = jnp.exp(_seq_bcast(fc, T - 1, T) - fc + gates - m_new)
    scale = jnp.exp(_seq_bcast(a, T - 1, T) - m_new)
    h_parts = []
    kw_parts = []
    scale_parts = []
    for p in range(M_HEADS // 2):
        sl = slice(p * V7X_LANES, (p + 1) * V7X_LANES)
        ia, ib = SM_I + 2 * p, SM_I + 2 * p + 1
        kw = k_m[:, sl] * _lane_pick(wk, ia, ib, lo)
        kw_parts.append(kw)
        inter, kv = pair_state_terms(q_mb[:, sl], kw, v_mb[:, sl], c_ref, p, M_DK)
        num = (_dot(scs[p], _stack_masked(v_mb[:, sl], lo, hi_m))
               + jnp.concatenate(inter, axis=1) * _lane_pick(w_inter, ia, ib, lo))
        h_parts.append(num * _lane_pick(inv, ia, ib, lo))
        scale_parts.append(_lane_pick(scale, ia, ib, lo))
        for hh in range(2):
            h = 2 * p + hh
            sc_rows = jnp.broadcast_to(scale[:, SM_I + h:SM_I + h + 1], (L, M_DV)).reshape(ns, T, M_DV)
            sc_h = jnp.broadcast_to(sc_rows[:, 0:1, :], (ns, M_DK, M_DV))
            c_out[:, h] = c_ref[:, h] * sc_h + kv[hh].reshape(ns, 2, M_DK, M_DV)[:, hh]
    kw_all = jnp.concatenate(kw_parts, axis=1)
    n_out[...] = n_rows * jnp.concatenate(scale_parts, axis=1) + _dot3(segones_ref[...], kw_all)
    m_out[...] = m_new
    gate_m = _sigmoid(proj_ref[:, C_OM:C_OM + M_W])
    mix_ref[:, R_W + G_W:D_MIX] = _norm_gate(jnp.concatenate(h_parts, axis=1), hmat_ref[0:M_W, 0:M_W],
                                             gcat_ref[:, R_W + G_W:D_MIX], gate_m)


def _sample_tables(T, pos0):
    L = ROWS
    half = R_DK // 2
    inv = ROPE_BASE ** (-jnp.arange(half, dtype=F32) * 2.0 / R_DK)
    tok = np.arange(L) % T
    seq = np.arange(L) // T
    pos = pos0 + jnp.asarray(tok, F32)
    ang = pos[:, None] * inv[None, :]
    cos = jnp.tile(jnp.cos(ang), (1, V7X_LANES // half))
    sin_h = jnp.sin(ang)
    sin = jnp.tile(jnp.concatenate([-sin_h, sin_h], axis=1), (1, V7X_LANES // R_DK))
    qs = R_DK ** -0.5

    log_gamma = jnp.log(1.0 - 2.0 ** (-5.0 - jnp.arange(R_HEADS, dtype=F32)))
    tf = jnp.asarray(tok, F32)
    rel = tf[:, None] - tf[None, :]
    ok_np = (seq[:, None] == seq[None, :]) & (tok[None, :] <= tok[:, None])
    ok = jnp.asarray(ok_np)
    rdecay = jnp.where(ok[None], jnp.exp(log_gamma[:, None, None] * jnp.where(ok, rel, 0.0)[None]), 0.0)
    rdec = jnp.concatenate([rdecay[0::2], rdecay[1::2]], axis=2)
    qdec = jnp.repeat(jnp.exp(log_gamma[:, None] * (tf + 1.0)).T, R_DK, axis=1)
    kdec = jnp.repeat(jnp.exp(log_gamma[:, None] * (T - 1.0 - tf)).T, R_DK, axis=1)
    lg32 = np.log(1.0 - 2.0 ** (-5.0 - np.arange(R_HEADS, dtype=np.float64))).astype(np.float32)
    chunk_decay = tuple(float(np.exp(v * np.float32(T))) for v in lg32)

    selr = np.zeros((16, V7X_LANES), np.float32)
    for h in range(M_HEADS):
        selr[h, SM_I + h] = 1.0
        selr[M_HEADS + h, SM_F + h] = 1.0
    hv = np.arange(G_W) // G_DV
    hc = np.arange(G_QK) // G_DK
    gmask = (hv[:, None] == hc[None, :]).astype(np.float32)
    hmat = (hv[:, None] == hv[None, :]).astype(np.float32)
    ns = L // T
    mseg = (seq[:, None] == (np.arange(ns * V7X_LANES) // V7X_LANES)[None, :]).astype(np.float32)
    return dict(
        cosq=cos * qs, sinq=sin * qs, cosk=cos, sink=sin, rdec=rdec, qdec=qdec, kdec=kdec, chunk_decay=chunk_decay,
        tri=jnp.asarray(ok_np, BF16), segones=jnp.asarray(seq[:, None] == seq[None, :], BF16),
        selr=jnp.asarray(selr, BF16), hmat=jnp.asarray(hmat, BF16),
        emat=jnp.asarray(gmask.T, BF16), mseg=jnp.asarray(mseg, BF16), msegt=jnp.asarray(mseg.T, BF16))


_SAMPLE_STATE_DIMS = ((R_HEADS, R_DK, R_DV), (G_HEADS, G_DK, G_DV), (M_HEADS, M_DK, M_DV), (M_W,),
                      (V7X_LANES,), (CONV_W - 1, M_QK))


def mixer_sample(proj, tabs, lw, state, l, depth, prev, *, T):
    M = proj.shape[0]
    B = M // T
    assert B % NS == 0 and NS * T == ROWS
    L = ROWS
    const2 = lambda b: (0, 0)
    const3 = lambda b: (0, 0, 0)

    def st_spec(dims):
        return pl.BlockSpec((None, NS) + dims, lambda b: (l, b) + (0,) * len(dims))

    def row_spec(w):
        return pl.BlockSpec((None, L, w), lambda b: (l, b, 0))

    st_specs = ([st_spec(d) for d in _SAMPLE_STATE_DIMS[:3]]
                + [row_spec(M_W), row_spec(V7X_LANES), st_spec(_SAMPLE_STATE_DIMS[5])])
    in_specs = [
        pl.BlockSpec((L, D_IN_PAD), lambda b: (b, 0)),
        pl.BlockSpec((L, V7X_LANES), const2), pl.BlockSpec((L, V7X_LANES), const2),
        pl.BlockSpec((L, V7X_LANES), const2), pl.BlockSpec((L, V7X_LANES), const2),
        pl.BlockSpec((R_HEADS // 2, L, 2 * L), const3),
        pl.BlockSpec((L, R_W), const2),
        pl.BlockSpec((L, R_W), const2),
        pl.BlockSpec((L, L), const2),
        pl.BlockSpec((L, L), const2),
        pl.BlockSpec((16, V7X_LANES), const2),
        pl.BlockSpec((G_W, G_W), const2),
        pl.BlockSpec((G_QK, G_W), const2),
        pl.BlockSpec((L, NS * V7X_LANES), const2),
        pl.BlockSpec((NS * V7X_LANES, L), const2),
        pl.BlockSpec((V7X_LANES, G_QK), const2),
        pl.BlockSpec((1, G_QK), const2),
        pl.BlockSpec((CONV_W, M_QK), const2),
        pl.BlockSpec((1, M_QK), const2),
        pl.BlockSpec((1, V7X_LANES), const2),
        pl.BlockSpec((1, D_MIX), const2),
    ] + st_specs
    args = [proj, tabs["cosq"], tabs["sinq"], tabs["cosk"], tabs["sink"], tabs["rdec"], tabs["qdec"], tabs["kdec"],
            tabs["tri"], tabs["segones"], tabs["selr"], tabs["hmat"], tabs["emat"], tabs["mseg"], tabs["msegt"],
            lw["wga"], lw["bga"], lw["convw"], lw["convb"], lw["bsm"], lw["gcat"]] + list(state)
    n_in = len(args)
    in_specs += [pl.BlockSpec(memory_space=pl.ANY)] * 6
    args += list(prev)
    aliases = {n_in + i: 1 + i for i in range(6)}
    shapes = [(depth, B) + d for d in _SAMPLE_STATE_DIMS]
    shapes[3], shapes[4] = (depth, M, M_W), (depth, M, V7X_LANES)
    out_shape = (jax.ShapeDtypeStruct((M, D_MIX), BF16),) + tuple(jax.ShapeDtypeStruct(sh, F32) for sh in shapes)
    out_specs = (pl.BlockSpec((L, D_MIX), lambda b: (b, 0)),) + tuple(st_specs)
    kern = functools.partial(_mixer_sample_kernel, T=T, chunk_decay=tabs["chunk_decay"])
    outs = pl.pallas_call(
        kern, out_shape=out_shape, grid=(B // NS,), in_specs=in_specs, out_specs=out_specs,
        scratch_shapes=[pltpu.VMEM((NS, 8 + T, M_QK), F32)],
        input_output_aliases=aliases,
        compiler_params=_cparams(("parallel",)), name="mixer_sample",
    )(*args)
    return outs[0], tuple(outs[1:])


def _prep_layer(l, g_mix, w_in, w_ga2, b_ga, conv_w, conv_b, b_i, b_f, g_ret, g_gla, g_mlstm, w_out,
                g_xattn, g_mem, w_xq, w_xk, w_xv, w_xo, g_ffn, w_gate, w_up, w_down):
    w = w_in[l]
    a0 = 2 * R_HEADS * R_DK + 2 * R_W + 2 * G_QK + 2 * G_W
    m0 = a0 + G_RANK
    g0 = m0 + M_QK + 2 * M_W
    w_pad = jnp.concatenate(
        [w[:, :a0], w[:, m0:g0], w[:, g0:g0 + 2 * M_HEADS], w[:, a0:m0],
         jnp.zeros((D_MODEL, D_IN_PAD - C_SM - G_RANK - 2 * M_HEADS), F32)], axis=1).astype(BF16)
    wga = jnp.zeros((V7X_LANES, G_QK), F32).at[SM_AG:SM_AG + G_RANK].set(w_ga2[l]).astype(BF16)
    bsm = (jnp.zeros((1, V7X_LANES), F32).at[0, SM_I:SM_I + M_HEADS].set(b_i[l])
           .at[0, SM_F:SM_F + M_HEADS].set(b_f[l]))
    return dict(
        g_mix=g_mix[l], w_in=w_pad, wga=wga, bga=b_ga[l].reshape(1, G_QK), convw=conv_w[l],
        convb=conv_b[l].reshape(1, M_QK), bsm=bsm,
        gcat=jnp.concatenate([g_ret[l], g_gla[l], g_mlstm[l]]).reshape(1, D_MIX),
        w_out=w_out[l].astype(BF16), g_xattn=g_xattn[l], g_mem=g_mem[l],
        w_xq=w_xq[l].astype(BF16), w_xk=w_xk[l].astype(BF16), w_xv=w_xv[l].astype(BF16),
        w_xo=w_xo[l].astype(BF16), g_ffn=g_ffn[l], w_gate=w_gate[l].astype(BF16),
        w_up=w_up[l].astype(BF16), w_down=w_down[l].astype(BF16))


def _layer(x, mix_fn, attend, lw, g_final, *, tm, final_norm):
    B, T, D = x.shape
    M = B * T
    x2 = x.reshape(M, D)
    proj = rms_matmul(x2, lw["g_mix"], lw["w_in"], tm=tm, tn=D_IN_PAD)
    mix, new_state = mix_fn(proj)
    x2, q = out_q(mix, lw["w_out"], x2, lw["g_xattn"], lw["w_xq"], tm=tm)
    x2 = attend(q.reshape(B, T, D), x2.reshape(B, T, D)).reshape(M, D)
    x2 = swiglu_res(x2, lw["g_ffn"], lw["w_gate"], lw["w_up"], lw["w_down"], g_final,
                    tm=tm, tf=D_FF, final_norm=final_norm)
    return x2.reshape(B, T, D), new_state


def kernel(x_prompt, x_sample, state_ret, state_gla, state_mlstm_C, state_mlstm_n, state_mlstm_m, state_mlstm_conv, cache_mem_k, cache_mem_v, mem_prompt, g_mix, w_in, w_ga2, b_ga, conv_w, conv_b, b_i, b_f, g_ret, g_gla, g_mlstm, w_out, g_xattn, g_mem, w_xq, w_xk, w_xv, w_xo, g_ffn, w_gate, w_up, w_down, g_final):
    B, T, D = x_prompt.shape
    Bs, Ts, _ = x_sample.shape
    depth = w_in.shape[0]
    assert T % CHUNK == 0 and Ts * NS == ROWS and Bs % NS == 0
    tabs_p = _prompt_tables(T, CHUNK, 16)
    tabs_s = _sample_tables(Ts, float(PAST_LEN))
    tm_p = 512 if (B * T) % 512 == 0 else B * T
    tm_s = 512 if (Bs * Ts) % 512 == 0 else Bs * Ts
    tq_p = 512 if T % 512 == 0 else T
    nb_x = 8 if Bs % 8 == 0 else 1

    sample_state = (state_ret, state_gla, state_mlstm_C,
                    jnp.repeat(state_mlstm_n.reshape(depth, Bs, M_W), Ts, axis=1),
                    jnp.repeat(jnp.pad(state_mlstm_m, ((0, 0), (0, 0), (0, V7X_LANES - M_HEADS))), Ts, axis=1),
                    state_mlstm_conv)
    s_shapes = [(depth, Bs) + d for d in _SAMPLE_STATE_DIMS]
    s_shapes[3], s_shapes[4] = (depth, Bs * Ts, M_W), (depth, Bs * Ts, V7X_LANES)
    s_st = tuple(jnp.zeros(sh, F32) for sh in s_shapes)
    p_st = tuple(jnp.zeros((depth, B) + d, F32) for d in _STATE_DIMS)
    p_mem = tuple(jnp.zeros((depth, B, N_MEM, X_HEADS, X_HD), F32) for _ in range(2))
    hp, hs = x_prompt, x_sample
    for l in range(depth):
        lw = _prep_layer(l, g_mix, w_in, w_ga2, b_ga, conv_w, conv_b, b_i, b_f, g_ret, g_gla, g_mlstm,
                         w_out, g_xattn, g_mem, w_xq, w_xk, w_xv, w_xo, g_ffn, w_gate, w_up, w_down)
        last = l == depth - 1
        k5, v5, kb, vb = memory_kv(mem_prompt, lw["g_mem"], lw["w_xk"], lw["w_xv"], l, depth, p_mem)
        p_mem = (k5, v5)

        def mix_p(proj):
            mix, st = mixer_prompt(proj.reshape(B, T, D_IN_PAD), tabs_p, lw, l, depth, p_st, L=CHUNK, CS=16)
            return mix.reshape(B * T, D_MIX), st

        hp, p_st = _layer(hp, mix_p, lambda q, x: xattn(q, kb, vb, x, lw["w_xo"], tq=tq_p),
                          lw, g_final, tm=tm_p, final_norm=last)
        hs, s_st = _layer(hs, lambda proj: mixer_sample(proj, tabs_s, lw, sample_state, l, depth, s_st, T=Ts),
                          lambda q, x: xattn_cache(q, cache_mem_k, cache_mem_v, l, x, lw["w_xo"], nb=nb_x),
                          lw, g_final, tm=tm_s, final_norm=last)

    p_out = p_st[:4] + (p_st[4].reshape(depth, B, M_HEADS), p_st[5])
    s_out = s_st[:3] + (s_st[3][:, ::Ts].reshape(depth, Bs, M_HEADS, M_DK), s_st[4][:, ::Ts, :M_HEADS], s_st[5])
    return (hp, hs, *p_out, *p_mem, *s_out)
```

```python
import functools
import math

import numpy as np
import jax
import jax.numpy as jnp
from jax import lax
from jax.experimental import pallas as pl
from jax.experimental.pallas import tpu as pltpu

F32 = jnp.float32
BF16 = jnp.bfloat16

D_MODEL = 1024
PAST_LEN = 16384
R_HEADS, R_DK, R_DV = 6, 64, 64
G_HEADS, G_DK, G_DV, G_RANK = 6, 32, 64, 16
G_NORMALIZER = 16.0
M_HEADS, M_DK, M_DV = 4, 64, 64
CONV_W = 4
X_HEADS = 4
X_HD = D_MODEL // X_HEADS
N_MEM = 256
D_FF = int(math.ceil(8 * D_MODEL / 3 / 256)) * 256
CHUNK = 128
EPS = 1e-6
ROPE_BASE = 10000.0

R_W = R_HEADS * R_DV
G_QK = G_HEADS * G_DK
G_W = G_HEADS * G_DV
M_QK = 2 * M_HEADS * M_DK
M_W = M_HEADS * M_DV
D_MIX = R_W + G_W + M_W

C_QR, C_KR, C_VR, C_GR = 0, 384, 768, 1152
C_QG, C_KG, C_VG, C_RG = 1536, 1728, 1920, 2304
C_QKM, C_VM, C_OM, C_SM = 2688, 3200, 3456, 3712
D_IN_PAD = 3840
SM_I, SM_F, SM_AG = 0, 4, 8
HEAD_W = 64
GLA_SAFE_LOG_RANGE = 60.0

V7X_LANES = 128
VMEM_LIMIT = 56 * 1024 * 1024


def _cparams(sem):
    return pltpu.CompilerParams(dimension_semantics=sem, vmem_limit_bytes=VMEM_LIMIT)


def _sigmoid(x):
    return 1.0 / (1.0 + jnp.exp(-x))


def _silu(x):
    return x * _sigmoid(x)


def _log_sigmoid(x):
    return jnp.minimum(x, 0.0) - jnp.log(1.0 + jnp.exp(-jnp.abs(x)))


def _dot(a, b):
    return jnp.dot(a, b, preferred_element_type=F32)


def _dot_nt(a, b):
    return lax.dot_general(a, b, (((1,), (1,)), ((), ())), preferred_element_type=F32)


def _dot_tn(a, b):
    return lax.dot_general(a, b, (((0,), (0,)), ((), ())), preferred_element_type=F32)


def _split3(x):
    hi = x.astype(BF16)
    r1 = x - hi.astype(F32)
    mid = r1.astype(BF16)
    lo = (r1 - mid.astype(F32)).astype(BF16)
    return hi, mid, lo


def _dot3(a, x):
    hi, mid, lo = _split3(x)
    return _dot(a, hi) + _dot(a, mid) + _dot(a, lo)


def _dot3_nt(a, x):
    hi, mid, lo = _split3(x)
    return _dot_nt(a, hi) + _dot_nt(a, mid) + _dot_nt(a, lo)


def _rms(x, g):
    return x * lax.rsqrt(jnp.mean(x * x, axis=-1, keepdims=True) + EPS) * g


def _rms_matmul_kernel(x_ref, g_ref, w_ref, o_ref, xn_ref):
    @pl.when(pl.program_id(1) == 0)
    def _():
        xn_ref[...] = _rms(x_ref[...], g_ref[...]).astype(BF16)

    o_ref[...] = _dot(xn_ref[...], w_ref[...]).astype(o_ref.dtype)


def rms_matmul(x, g, w, *, tm, tn, out_dtype=F32):
    M, D = x.shape
    N = w.shape[1]
    assert M % tm == 0 and N % tn == 0
    return pl.pallas_call(
        _rms_matmul_kernel,
        out_shape=jax.ShapeDtypeStruct((M, N), out_dtype),
        grid=(M // tm, N // tn),
        in_specs=[pl.BlockSpec((tm, D), lambda i, j: (i, 0)),
                  pl.BlockSpec((1, D), lambda i, j: (0, 0)),
                  pl.BlockSpec((D, tn), lambda i, j: (0, j))],
        out_specs=pl.BlockSpec((tm, tn), lambda i, j: (i, j)),
        scratch_shapes=[pltpu.VMEM((tm, D), BF16)],
        compiler_params=_cparams(("parallel", "arbitrary")),
        name="rms_matmul",
    )(x, g.reshape(1, D), w)


def _swiglu_kernel(x_ref, g_ref, wg_ref, wu_ref, wd_ref, gf_ref, o_ref, xn_ref, acc_ref, *, final_norm):
    j = pl.program_id(1)

    @pl.when(j == 0)
    def _():
        xn_ref[...] = _rms(x_ref[...], g_ref[...]).astype(BF16)
        acc_ref[...] = x_ref[...]

    xn = xn_ref[...]
    h = _silu(_dot(xn, wg_ref[...])) * _dot(xn, wu_ref[...])
    acc_ref[...] += _dot(h.astype(BF16), wd_ref[...])

    @pl.when(j == pl.num_programs(1) - 1)
    def _():
        y = acc_ref[...]
        if final_norm:
            y = _rms(y, gf_ref[...])
        o_ref[...] = y


def swiglu_res(x, g, wg, wu, wd, g_final, *, tm, tf, final_norm):
    M, D = x.shape
    FF = wg.shape[1]
    assert M % tm == 0 and FF % tf == 0
    return pl.pallas_call(
        functools.partial(_swiglu_kernel, final_norm=final_norm),
        out_shape=jax.ShapeDtypeStruct((M, D), F32),
        grid=(M // tm, FF // tf),
        in_specs=[pl.BlockSpec((tm, D), lambda i, j: (i, 0)),
                  pl.BlockSpec((1, D), lambda i, j: (0, 0)),
                  pl.BlockSpec((D, tf), lambda i, j: (0, j)),
                  pl.BlockSpec((D, tf), lambda i, j: (0, j)),
                  pl.BlockSpec((tf, D), lambda i, j: (j, 0)),
                  pl.BlockSpec((1, D), lambda i, j: (0, 0))],
        out_specs=pl.BlockSpec((tm, D), lambda i, j: (i, 0)),
        scratch_shapes=[pltpu.VMEM((tm, D), BF16), pltpu.VMEM((tm, D), F32)],
        compiler_params=_cparams(("parallel", "arbitrary")),
        name="swiglu_res",
    )(x, g.reshape(1, D), wg, wu, wd, g_final.reshape(1, D))


def _memkv_kernel(x_ref, g_ref, wk_ref, wv_ref, *refs):
    k5_ref, v5_ref, kb_ref, vb_ref = refs[-4:]
    xn = _rms(x_ref[...], g_ref[...]).astype(BF16)
    for w_ref, o5_ref, ob_ref in ((wk_ref, k5_ref, kb_ref), (wv_ref, v5_ref, vb_ref)):
        y = _dot(xn, w_ref[...])
        ob_ref[...] = y.astype(BF16)
        for h in range(X_HEADS):
            o5_ref[:, h, :] = y[:, h * X_HD:(h + 1) * X_HD]


def memory_kv(mem, g, wk, wv, l, depth, prev):
    B, _, D = mem.shape
    o5 = jax.ShapeDtypeStruct((depth, B, N_MEM, X_HEADS, X_HD), F32)
    ob = jax.ShapeDtypeStruct((B, N_MEM, D), BF16)
    in_specs = [pl.BlockSpec((None, N_MEM, D), lambda b: (b, 0, 0)),
                pl.BlockSpec((1, D), lambda b: (0, 0)),
                pl.BlockSpec((D, D), lambda b: (0, 0)),
                pl.BlockSpec((D, D), lambda b: (0, 0))]
    in_specs += [pl.BlockSpec(memory_space=pl.ANY)] * 2
    args = [mem, g.reshape(1, D), wk, wv] + list(prev)
    aliases = {4: 0, 5: 1}
    spec5 = pl.BlockSpec((None, None, N_MEM, X_HEADS, X_HD), lambda b: (l, b, 0, 0, 0))
    specb = pl.BlockSpec((None, N_MEM, D), lambda b: (b, 0, 0))
    return pl.pallas_call(
        _memkv_kernel,
        out_shape=(o5, o5, ob, ob),
        grid=(B,),
        in_specs=in_specs,
        out_specs=(spec5, spec5, specb, specb),
        input_output_aliases=aliases,
        compiler_params=_cparams(("parallel",)),
        name="memory_kv",
    )(*args)


def _out_and_query(a_ref, x_ref, wo_ref, g_ref, wq_ref):
    x1 = x_ref[...] + _dot(a_ref[...], wo_ref[...])
    q = _dot(_rms(x1, g_ref[...]).astype(BF16), wq_ref[...]).astype(BF16)
    return x1, q


def _post_mix_kernel(a_ref, x_ref, k_ref, v_ref, wo_ref, g_ref, wq_ref, wxo_ref, o_ref):
    x1, q = _out_and_query(a_ref, x_ref, wo_ref, g_ref, wq_ref)
    scale = X_HD ** -0.5
    parts = []
    for h in range(X_HEADS):
        sl = slice(h * X_HD, (h + 1) * X_HD)
        s = _dot_nt(q[:, sl], k_ref[:, sl]) * scale
        p = jnp.exp(s - jnp.max(s, axis=-1, keepdims=True))
        l = jnp.sum(p, axis=-1, keepdims=True)
        parts.append((_dot(p.astype(BF16), v_ref[:, sl]) / l).astype(BF16))
    o_ref[...] = x1 + _dot(jnp.concatenate(parts, axis=1), wxo_ref[...])


def post_mix(a, x, mk, mv, wo, g, wq, wxo, *, tq):
    B, T, D = x.shape
    assert T % tq == 0
    tok = lambda w: pl.BlockSpec((None, tq, w), lambda b, i: (b, i, 0))
    mem = pl.BlockSpec((None, N_MEM, D), lambda b, i: (b, 0, 0))
    wspec = lambda r: pl.BlockSpec((r, D), lambda b, i: (0, 0))
    return pl.pallas_call(
        _post_mix_kernel,
        out_shape=jax.ShapeDtypeStruct((B, T, D), F32),
        grid=(B, T // tq),
        in_specs=[tok(a.shape[2]), tok(D), mem, mem, wspec(a.shape[2]), wspec(1), wspec(D), wspec(D)],
        out_specs=tok(D),
        compiler_params=_cparams(("parallel", "arbitrary")),
        name="post_mix",
    )(a, x, mk, mv, wo, g.reshape(1, D), wq, wxo)


def _post_mix_cache_kernel(a_ref, x_ref, k_ref, v_ref, wo_ref, g_ref, wq_ref, wxo_ref, o_ref, *, nb, T):
    x1, q_all = _out_and_query(a_ref, x_ref, wo_ref, g_ref, wq_ref)
    R = X_HEADS * T
    rowh = lax.broadcasted_iota(jnp.int32, (R, N_MEM * X_HEADS), 0) // T
    colh = lax.broadcasted_iota(jnp.int32, (R, N_MEM * X_HEADS), 1) % X_HEADS
    own = rowh == colh
    outs = []
    for s in range(nb):
        q = q_all[s * T:(s + 1) * T]
        qf = jnp.concatenate([q[:, h * X_HD:(h + 1) * X_HD] for h in range(X_HEADS)], axis=0)
        kf = k_ref[s].reshape(N_MEM * X_HEADS, X_HD).astype(BF16)
        vf = v_ref[s].reshape(N_MEM * X_HEADS, X_HD).astype(BF16)
        sc = jnp.where(own, _dot_nt(qf, kf) * (X_HD ** -0.5), -jnp.inf)
        p = jnp.exp(sc - jnp.max(sc, axis=-1, keepdims=True))
        l = jnp.sum(p, axis=-1, keepdims=True)
        o = (_dot(p.astype(BF16), vf) / l).astype(BF16)
        outs.append(jnp.concatenate([o[h * T:(h + 1) * T] for h in range(X_HEADS)], axis=1))
    o_ref[...] = x1 + _dot(jnp.concatenate(outs, axis=0), wxo_ref[...])


def post_mix_cache(a, x, ck, cv, l, wo, g, wq, wxo, *, nb, T):
    M, D = x.shape
    rows = nb * T
    assert M % rows == 0
    cspec = pl.BlockSpec((None, nb, N_MEM, X_HEADS, X_HD), lambda b: (l, b, 0, 0, 0))
    tok = lambda w: pl.BlockSpec((rows, w), lambda b: (b, 0))
    wspec = lambda r: pl.BlockSpec((r, D), lambda b: (0, 0))
    return pl.pallas_call(
        functools.partial(_post_mix_cache_kernel, nb=nb, T=T),
        out_shape=jax.ShapeDtypeStruct((M, D), F32),
        grid=(M // rows,),
        in_specs=[tok(a.shape[1]), tok(D), cspec, cspec, wspec(a.shape[1]), wspec(1), wspec(D), wspec(D)],
        out_specs=tok(D),
        compiler_params=_cparams(("parallel",)),
        name="post_mix_cache",
    )(a, x, ck, cv, wo, g.reshape(1, D), wq, wxo)


def _eye(n):
    r = lax.broadcasted_iota(jnp.int32, (n, n), 0)
    c = lax.broadcasted_iota(jnp.int32, (n, n), 1)
    return jnp.where(r == c, 1.0, 0.0).astype(BF16)


_STATE_DIMS = ((R_HEADS, R_DK, R_DV), (G_HEADS, G_DK, G_DV), (M_HEADS, M_DK, M_DV), (M_HEADS, M_DK),
               (1, M_HEADS), (CONV_W - 1, M_QK))


def _split2(x):
    hi = x.astype(BF16)
    lo = (x - hi.astype(F32)).astype(BF16)
    return hi, lo


def _head_mean_sq(o, hmat):
    hi, lo = _split2(o * o)
    return (_dot(hi, hmat) + _dot(lo, hmat)) * (1.0 / HEAD_W)


def _norm_gate(o, hmat, g_row, gate):
    return (o * lax.rsqrt(_head_mean_sq(o, hmat) + EPS) * g_row * gate).astype(BF16)


def _lane_pick(cols, idx_lo, idx_hi, lo_mask):
    L = cols.shape[0]
    a = jnp.broadcast_to(cols[:, idx_lo:idx_lo + 1], (L, V7X_LANES))
    b = jnp.broadcast_to(cols[:, idx_hi:idx_hi + 1], (L, V7X_LANES))
    return jnp.where(lo_mask, a, b)


def _stack_masked(x, m_a, m_b):
    z = jnp.zeros_like(x)
    return jnp.concatenate([jnp.where(m_a, x, z), jnp.where(m_b, x, z)], axis=0)


def _mixer_prompt_kernel(proj_ref, cosq_ref, sinq_ref, cosk_ref, sink_ref, rdec_ref, qdec_ref, kdec_ref,
                         tri_ref, btri_ref, bones_ref, selr_ref, hmat_ref, emat_ref, pmask_ref,
                         wga_ref, bga_ref, convw_ref, convb_ref, bsm_ref, gcat_ref, *rest,
                         L, CS, NC, chunk_decay):
    (mix_ref, sr_out, sg_out, c_out, n_out, m_out, conv_out,
     srp_scr, sgt_scr, cp_scr, n_scr, m_scr, conv_scr, og_scr) = rest[-14:]
    c = pl.program_id(1)

    @pl.when(c == 0)
    def _():
        srp_scr[...] = jnp.zeros_like(srp_scr)
        sgt_scr[...] = jnp.zeros_like(sgt_scr)
        cp_scr[...] = jnp.zeros_like(cp_scr)
        n_scr[...] = jnp.zeros_like(n_scr)
        m_scr[...] = jnp.zeros_like(m_scr)
        conv_scr[0:8, :] = jnp.zeros((8, M_QK), F32)

    lane = lax.broadcasted_iota(jnp.int32, (L, V7X_LANES), 1)
    lo = lane < HEAD_W
    hi_m = lane >= HEAD_W
    first_half = (lane % R_DK) < (R_DK // 2)
    row2 = lax.broadcasted_iota(jnp.int32, (L, 2 * L), 0)
    col2 = lax.broadcasted_iota(jnp.int32, (L, 2 * L), 1) % L
    causal2 = col2 <= row2
    lo_row = lax.broadcasted_iota(jnp.int32, (1, V7X_LANES), 1) < HEAD_W
    lo8 = lax.broadcasted_iota(jnp.int32, (8, V7X_LANES), 1) < HEAD_W
    hmat = hmat_ref[...]
    pmask = pmask_ref[...]

    def rope(x, cos_ref, sin_ref):
        cos = cos_ref[...]
        sin = sin_ref[...]
        parts = []
        for t in range(R_W // V7X_LANES):
            xs = x[:, t * V7X_LANES:(t + 1) * V7X_LANES]
            rot = jnp.where(first_half, pltpu.roll(xs, V7X_LANES - R_DK // 2, 1), pltpu.roll(xs, R_DK // 2, 1))
            parts.append(xs * cos + rot * sin)
        return jnp.concatenate(parts, axis=1)

    q_r = rope(proj_ref[:, C_QR:C_QR + R_W], cosq_ref, sinq_ref)
    k_r = rope(proj_ref[:, C_KR:C_KR + R_W], cosk_ref, sink_ref)
    q_rb = q_r.astype(BF16)
    k_rb = k_r.astype(BF16)
    kd_rb = (k_r * kdec_ref[...]).astype(BF16)
    v_rb = proj_ref[:, C_VR:C_VR + R_W].astype(BF16)
    o_parts = []
    for p in range(R_HEADS // 2):
        sl = slice(p * V7X_LANES, (p + 1) * V7X_LANES)
        qs, ks, vs = q_rb[:, sl], k_rb[:, sl], v_rb[:, sl]
        sc = _dot_nt(qs, _stack_masked(ks, lo, hi_m)) * rdec_ref[p]
        sp = srp_scr[p]
        o = _dot(sc.astype(BF16), _stack_masked(vs, lo, hi_m)) + _dot(qs, sp.astype(BF16)) * qdec_ref[:, sl]
        cd = jnp.where(lo_row, chunk_decay[2 * p], chunk_decay[2 * p + 1])
        srp_scr[p] = sp * cd + _dot_tn(kd_rb[:, sl], vs) * pmask
        o_parts.append(o)
    o_r = jnp.concatenate(o_parts, axis=1)
    gate_r = _silu(proj_ref[:, C_GR:C_GR + R_W])
    mix_ref[:, 0:R_W] = _norm_gate(o_r, hmat, gcat_ref[:, 0:R_W], gate_r)

    small = proj_ref[:, C_SM:C_SM + V7X_LANES]

    u = proj_ref[:, C_QKM:C_QKM + M_QK]
    conv_scr[8:8 + L, :] = u
    y = convb_ref[...]
    for j in range(CONV_W - 1):
        y = y + conv_scr[5 + j:5 + j + L, :] * convw_ref[j:j + 1, :]
    y = y + u * convw_ref[CONV_W - 1:CONV_W, :]
    tail = conv_scr[5 + L:8 + L, :]
    conv_scr[5:8, :] = tail
    qk = _silu(y)
    q_m = qk[:, :M_W]
    k_m = qk[:, M_W:] * (M_DK ** -0.5)
    q_mb = q_m.astype(BF16)
    k_mb = k_m.astype(BF16)
    v_mb = proj_ref[:, C_VM:C_VM + M_W].astype(BF16)
    gates = small + bsm_ref[...]
    f_cum = _dot3(tri_ref[...], _log_sigmoid(gates))
    i_rows = _dot3_nt(selr_ref[...], gates)
    f_rows = _dot3_nt(selr_ref[...], f_cum)
    head_lane = (lane >= SM_I) & (lane < SM_I + M_HEADS)
    fc = jnp.where(head_lane, pltpu.roll(f_cum, V7X_LANES - (SM_F - SM_I), 1), 0.0)
    m_prev = m_scr[0:1, :]
    a = fc + m_prev
    mx = jnp.full((L, V7X_LANES), -jnp.inf, F32)
    row = lax.broadcasted_iota(jnp.int32, (L, L), 0)
    col = lax.broadcasted_iota(jnp.int32, (L, L), 1)
    causal = col <= row
    dms = []
    for h in range(M_HEADS):
        dm = jnp.where(causal, (fc[:, SM_I + h:SM_I + h + 1] - f_rows[M_HEADS + h:M_HEADS + h + 1, :])
                       + i_rows[h:h + 1, :], -jnp.inf)
        dms.append(dm)
        mx = jnp.where(lane == SM_I + h, jnp.max(dm, axis=-1, keepdims=True), mx)
    m_tok = jnp.maximum(a, mx)
    w_inter = jnp.exp(a - m_tok)
    n_full = n_scr[0:1, :]
    hs, lows = _split2(q_m * n_full)
    qn_b = _dot(hs, hmat_ref[0:M_W, 0:M_W]) + _dot(lows, hmat_ref[0:M_W, 0:M_W])
    rs_all = jnp.zeros((L, V7X_LANES), F32)
    qn_all = jnp.zeros((L, V7X_LANES), F32)
    scs = []
    for p in range(M_HEADS // 2):
        sl = slice(p * V7X_LANES, (p + 1) * V7X_LANES)
        sc = _dot_nt(q_mb[:, sl], _stack_masked(k_mb[:, sl], lo, hi_m))
        mt_a = jnp.broadcast_to(m_tok[:, SM_I + 2 * p:SM_I + 2 * p + 1], (L, L))
        mt_b = jnp.broadcast_to(m_tok[:, SM_I + 2 * p + 1:SM_I + 2 * p + 2], (L, L))
        sc_a = sc[:, :L] * jnp.exp(dms[2 * p] - mt_a)
        sc_b = sc[:, L:] * jnp.exp(dms[2 * p + 1] - mt_b)
        rs_all = jnp.where(lane == SM_I + 2 * p, jnp.sum(sc_a, axis=-1, keepdims=True), rs_all)
        rs_all = jnp.where(lane == SM_I + 2 * p + 1, jnp.sum(sc_b, axis=-1, keepdims=True), rs_all)
        qn_all = jnp.where(lane == SM_I + 2 * p, qn_b[:, p * V7X_LANES:p * V7X_LANES + 1], qn_all)
        qn_all = jnp.where(lane == SM_I + 2 * p + 1, qn_b[:, p * V7X_LANES + HEAD_W:p * V7X_LANES + HEAD_W + 1], qn_all)
        scs.append(jnp.concatenate([sc_a, sc_b], axis=1).astype(BF16))
    den = rs_all + qn_all * w_inter
    inv = 1.0 / jnp.maximum(jnp.abs(den), jnp.exp(-m_tok))
    m_new = jnp.where(head_lane, m_tok, 0.0)[L - 1:L, :]
    wk = jnp.exp(fc[L - 1:L, :] - fc + gates - m_new)
    scale = jnp.broadcast_to(jnp.exp(a[L - 1:L, :] - m_new), (8, V7X_LANES))
    h_parts = []
    kw_parts = []
    scale_parts = []
    for p in range(M_HEADS // 2):
        sl = slice(p * V7X_LANES, (p + 1) * V7X_LANES)
        ia, ib = SM_I + 2 * p, SM_I + 2 * p + 1
        cpair = cp_scr[p]
        num = (_dot(scs[p], _stack_masked(v_mb[:, sl], lo, hi_m))
               + _dot(q_mb[:, sl], cpair.astype(BF16)) * _lane_pick(w_inter, ia, ib, lo))
        h_parts.append(num * _lane_pick(inv, ia, ib, lo))
        kw = k_m[:, sl] * _lane_pick(wk, ia, ib, lo)
        kw_parts.append(kw)
        sc_row = _lane_pick(scale, ia, ib, lo8)[0:1]
        scale_parts.append(sc_row)
        cp_scr[p] = cpair * sc_row + _dot_tn(kw.astype(BF16), v_mb[:, sl]) * pmask
    kw_all = jnp.concatenate(kw_parts, axis=1)
    n_new = n_full * jnp.concatenate(scale_parts, axis=1) + jnp.sum(kw_all, axis=0, keepdims=True)
    n_scr[...] = jnp.broadcast_to(n_new, n_scr.shape)
    m_scr[...] = jnp.broadcast_to(m_new, m_scr.shape)
    gate_m = _sigmoid(proj_ref[:, C_OM:C_OM + M_W])
    mix_ref[:, R_W + G_W:D_MIX] = _norm_gate(jnp.concatenate(h_parts, axis=1), hmat_ref[0:M_W, 0:M_W],
                                             gcat_ref[:, R_W + G_W:D_MIX], gate_m)

    z = _dot(small.astype(BF16), wga_ref[...]) + bga_ref[...]
    log_a = _log_sigmoid(z) / G_NORMALIZER
    b = _dot3(tri_ref[...], log_a)
    b_last = b[L - 1:L, :]
    safe = jnp.max(-b_last) <= GLA_SAFE_LOG_RANGE
    q_g = proj_ref[:, C_QG:C_QG + G_QK] * (G_DK ** -0.5)
    k_g = proj_ref[:, C_KG:C_KG + G_QK]
    v_gb = proj_ref[:, C_VG:C_VG + G_W].astype(BF16)

    slot = lax.broadcasted_iota(jnp.int32, (V7X_LANES, V7X_LANES), 1) // G_DK
    row_head = lax.broadcasted_iota(jnp.int32, (V7X_LANES, V7X_LANES), 0) // HEAD_W
    lane_q = lane // G_DK

    def pad2(x):
        return jnp.concatenate([x, jnp.zeros((x.shape[0], 2 * V7X_LANES - G_QK), x.dtype)], axis=1)

    def slab(x_p, g):
        s0 = (2 * g * G_DK) // V7X_LANES * V7X_LANES
        return x_p[:, s0:s0 + V7X_LANES], (2 * g * G_DK - s0) // G_DK

    def gla_inter(q_p):
        return jnp.concatenate([_dot_nt(slab(q_p, g)[0], sgt_scr[g].astype(BF16)) for g in range(G_HEADS // 2)],
                               axis=1)

    def gla_update(decay_p, k_p, v_b):
        for g in range(G_HEADS // 2):
            ks, h_a = slab(k_p, g)
            kv = _dot_tn(v_b[:, g * V7X_LANES:(g + 1) * V7X_LANES], ks)
            sgt_scr[g] = sgt_scr[g] * slab(decay_p, g)[0] + jnp.where(slot == h_a + row_head, kv, 0.0)

    @pl.when(safe)
    def _():
        qt_p = pad2((q_g * jnp.exp(b)).astype(BF16))
        kt_p = pad2((k_g * jnp.exp(-b)).astype(BF16))
        kl_p = pad2((k_g * jnp.exp(b_last - b)).astype(BF16))
        o_inter = gla_inter(qt_p)
        parts = []
        for g in range(G_HEADS // 2):
            qs, h_a = slab(qt_p, g)
            ks, _ = slab(kt_p, g)
            sc = _dot_nt(qs, _stack_masked(ks, lane_q == h_a, lane_q == h_a + 1))
            sc = jnp.where(causal2, sc, 0.0).astype(BF16)
            vs = v_gb[:, g * V7X_LANES:(g + 1) * V7X_LANES]
            parts.append(_dot(sc, _stack_masked(vs, lo, hi_m)))
        og_scr[...] = o_inter + jnp.concatenate(parts, axis=1)
        gla_update(pad2(jnp.exp(b_last)), kl_p, v_gb)

    @pl.when(jnp.logical_not(safe))
    def _():
        v_g = proj_ref[:, C_VG:C_VG + G_W]
        b_loc = _dot3(btri_ref[...], log_a)
        b_tot = _dot3(bones_ref[...], log_a)
        qt_p = pad2((q_g * jnp.exp(b_loc)).astype(BF16))
        kt_p = pad2((k_g * jnp.exp(b_tot - b_loc)).astype(BF16))
        d_tot_p = pad2(jnp.exp(b_tot))
        emat = emat_ref[...]
        sub_row = lax.broadcasted_iota(jnp.int32, (CS, G_QK), 0)
        for blk in range(L // CS):
            r0 = blk * CS
            rs = slice(r0, r0 + CS)
            o_blk = gla_inter(qt_p[rs])
            bI, qI, kI = b_loc[rs], q_g[rs], k_g[rs]
            terms = []
            for j in range(CS):
                e = jnp.exp(jnp.where(sub_row >= j, bI - bI[j:j + 1], -jnp.inf))
                terms.append(e * qI * kI[j:j + 1])
            t = jnp.concatenate(terms, axis=0).astype(BF16)
            w = _dot(t, emat)
            for j in range(CS):
                o_blk = o_blk + w[j * CS:(j + 1) * CS] * v_g[r0 + j:r0 + j + 1]
            og_scr[rs, :] = o_blk
            gla_update(d_tot_p[r0:r0 + 1], kt_p[rs], v_gb[rs])

    gate_g = _silu(proj_ref[:, C_RG:C_RG + G_W])
    mix_ref[:, R_W:R_W + G_W] = _norm_gate(og_scr[...], hmat, gcat_ref[:, R_W:R_W + G_W], gate_g)

    @pl.when(c == NC - 1)
    def _():
        eye_k = _eye(G_DK)
        for p in range(R_HEADS // 2):
            sr_out[2 * p] = srp_scr[p, 0:HEAD_W, 0:HEAD_W]
            sr_out[2 * p + 1] = srp_scr[p, HEAD_W:, HEAD_W:]
        for h in range(G_HEADS):
            g, hh = h // 2, h % 2
            c0 = (h * G_DK) % V7X_LANES
            sg_out[h] = _dot3_nt(eye_k, sgt_scr[g, hh * G_DV:(hh + 1) * G_DV, c0:c0 + G_DK])
        for p in range(M_HEADS // 2):
            c_out[2 * p] = cp_scr[p, 0:HEAD_W, 0:HEAD_W]
            c_out[2 * p + 1] = cp_scr[p, HEAD_W:, HEAD_W:]
        for h in range(M_HEADS):
            n_out[h:h + 1, :] = n_scr[0:1, h * M_DK:(h + 1) * M_DK]
        m_out[...] = m_scr[0:1, SM_I:SM_I + M_HEADS]
        conv_out[...] = conv_scr[5:8, :]


def _prompt_tables(T, L, CS):
    half = R_DK // 2
    inv = ROPE_BASE ** (-jnp.arange(half, dtype=F32) * 2.0 / R_DK)
    pos = jnp.arange(T, dtype=F32)
    ang = pos[:, None] * inv[None, :]
    cos = jnp.tile(jnp.cos(ang), (1, V7X_LANES // half))
    sin_h = jnp.sin(ang)
    sin = jnp.tile(jnp.concatenate([-sin_h, sin_h], axis=1), (1, V7X_LANES // R_DK))
    qs = R_DK ** -0.5

    log_gamma = jnp.log(1.0 - 2.0 ** (-5.0 - jnp.arange(R_HEADS, dtype=F32)))
    idx = jnp.arange(L, dtype=F32)
    rel = idx[:, None] - idx[None, :]
    causal = rel >= 0
    rdecay = jnp.where(causal[None], jnp.exp(log_gamma[:, None, None] * jnp.where(causal, rel, 0.0)[None]), 0.0)
    rdec = jnp.concatenate([rdecay[0::2], rdecay[1::2]], axis=2)
    qdec = jnp.repeat(jnp.exp(log_gamma[:, None] * (idx + 1.0)).T, R_DK, axis=1)
    kdec = jnp.repeat(jnp.exp(log_gamma[:, None] * (L - 1.0 - idx)).T, R_DK, axis=1)
    lg32 = np.log(1.0 - 2.0 ** (-5.0 - np.arange(R_HEADS, dtype=np.float64))).astype(np.float32)
    chunk_decay = tuple(float(np.exp(v * np.float32(L))) for v in lg32)

    r = np.arange(L)
    tri = (r[None, :] <= r[:, None])
    same = (r[None, :] // CS) == (r[:, None] // CS)
    selr = np.zeros((16, V7X_LANES), np.float32)
    for h in range(M_HEADS):
        selr[h, SM_I + h] = 1.0
        selr[M_HEADS + h, SM_F + h] = 1.0
    hv = np.arange(G_W) // G_DV
    hc = np.arange(G_QK) // G_DK
    gmask = (hv[:, None] == hc[None, :]).astype(np.float32)
    hmat = (hv[:, None] == hv[None, :]).astype(np.float32)
    pm = np.arange(V7X_LANES) // HEAD_W
    pmask = (pm[:, None] == pm[None, :]).astype(np.float32)
    return dict(
        cosq=cos * qs, sinq=sin * qs, cosk=cos, sink=sin, rdec=rdec, qdec=qdec, kdec=kdec, chunk_decay=chunk_decay,
        tri=jnp.asarray(tri, BF16), btri=jnp.asarray(tri & same, BF16), bones=jnp.asarray(same, BF16),
        selr=jnp.asarray(selr, BF16), hmat=jnp.asarray(hmat, BF16), emat=jnp.asarray(gmask.T, BF16),
        pmask=jnp.asarray(pmask, F32))


def mixer_prompt(proj, tabs, lw, l_out, depth, prev, *, L, CS):
    B, T, _ = proj.shape
    NC = T // L
    const2 = lambda b, c: (0, 0)
    const3 = lambda b, c: (0, 0, 0)
    tspec = pl.BlockSpec((L, V7X_LANES), lambda b, c: (c, 0))
    in_specs = [
        pl.BlockSpec((None, L, D_IN_PAD), lambda b, c: (b, c, 0)),
        tspec, tspec, tspec, tspec,
        pl.BlockSpec((R_HEADS // 2, L, 2 * L), const3),
        pl.BlockSpec((L, R_W), const2),
        pl.BlockSpec((L, R_W), const2),
        pl.BlockSpec((L, L), const2),
        pl.BlockSpec((L, L), const2),
        pl.BlockSpec((L, L), const2),
        pl.BlockSpec((16, V7X_LANES), const2),
        pl.BlockSpec((G_W, G_W), const2),
        pl.BlockSpec((G_QK, G_W), const2),
        pl.BlockSpec((V7X_LANES, V7X_LANES), const2),
        pl.BlockSpec((V7X_LANES, G_QK), const2),
        pl.BlockSpec((1, G_QK), const2),
        pl.BlockSpec((CONV_W, M_QK), const2),
        pl.BlockSpec((1, M_QK), const2),
        pl.BlockSpec((1, V7X_LANES), const2),
        pl.BlockSpec((1, D_MIX), const2),
    ]
    args = [proj, tabs["cosq"], tabs["sinq"], tabs["cosk"], tabs["sink"], tabs["rdec"], tabs["qdec"], tabs["kdec"],
            tabs["tri"], tabs["btri"], tabs["bones"], tabs["selr"], tabs["hmat"], tabs["emat"],
            tabs["pmask"], lw["wga"], lw["bga"], lw["convw"], lw["convb"], lw["bsm"], lw["gcat"]]
    n_in = len(args)
    in_specs += [pl.BlockSpec(memory_space=pl.ANY)] * 6
    args += list(prev)
    aliases = {n_in + i: 1 + i for i in range(6)}

    def st_spec(dims):
        return pl.BlockSpec((None, None) + dims, lambda b, c: (l_out, b) + (0,) * len(dims))

    out_shape = (jax.ShapeDtypeStruct((B, T, D_MIX), BF16),) + tuple(
        jax.ShapeDtypeStruct((depth, B) + d, F32) for d in _STATE_DIMS)
    out_specs = (pl.BlockSpec((None, L, D_MIX), lambda b, c: (b, c, 0)),) + tuple(st_spec(d) for d in _STATE_DIMS)
    scratch = [
        pltpu.VMEM((R_HEADS // 2, V7X_LANES, V7X_LANES), F32),
        pltpu.VMEM((G_HEADS // 2, V7X_LANES, V7X_LANES), F32),
        pltpu.VMEM((M_HEADS // 2, V7X_LANES, V7X_LANES), F32),
        pltpu.VMEM((8, M_W), F32),
        pltpu.VMEM((8, V7X_LANES), F32),
        pltpu.VMEM((8 + L, M_QK), F32),
        pltpu.VMEM((L, G_W), F32),
    ]
    kern = functools.partial(_mixer_prompt_kernel, L=L, CS=CS, NC=NC, chunk_decay=tabs["chunk_decay"])
    outs = pl.pallas_call(
        kern, out_shape=out_shape, grid=(B, NC), in_specs=in_specs, out_specs=out_specs,
        scratch_shapes=scratch, input_output_aliases=aliases,
        compiler_params=_cparams(("parallel", "arbitrary")), name="mixer_prompt",
    )(*args)
    return outs[0], tuple(outs[1:])


NS = 16
ROWS = 128


def _seq_bcast(x, t, T):
    n, w = x.shape
    x3 = x.reshape(n // T, T, w)
    return jnp.broadcast_to(x3[:, t:t + 1, :], (n // T, T, w)).reshape(n, w)


def _mixer_sample_kernel(proj_ref, cosq_ref, sinq_ref, cosk_ref, sink_ref, rdec_ref, qdec_ref, kdec_ref,
                         tri_ref, segones_ref, selr_ref, hmat_ref, emat_ref, mseg_ref, msegt_ref,
                         wga_ref, bga_ref, convw_ref, convb_ref, bsm_ref, gcat_ref,
                         sr_ref, sg_ref, c_ref, n_ref, m_ref, conv_ref, *rest, T, chunk_decay):
    (mix_ref, sr_out, sg_out, c_out, n_out, m_out, conv_out, conv_scr) = rest[-8:]
    L = ROWS
    ns = L // T
    lane = lax.broadcasted_iota(jnp.int32, (L, V7X_LANES), 1)
    lo = lane < HEAD_W
    hi_m = lane >= HEAD_W
    first_half = (lane % R_DK) < (R_DK // 2)
    row = lax.broadcasted_iota(jnp.int32, (L, L), 0)
    col = lax.broadcasted_iota(jnp.int32, (L, L), 1)
    segcausal = (row // T == col // T) & (col <= row)
    hmat = hmat_ref[...]
    mseg = mseg_ref[...]
    msegt = msegt_ref[...]

    def rope(x, cos_ref, sin_ref):
        cos = cos_ref[...]
        sin = sin_ref[...]
        parts = []
        for t in range(R_W // V7X_LANES):
            xs = x[:, t * V7X_LANES:(t + 1) * V7X_LANES]
            rot = jnp.where(first_half, pltpu.roll(xs, V7X_LANES - R_DK // 2, 1), pltpu.roll(xs, R_DK // 2, 1))
            parts.append(xs * cos + rot * sin)
        return jnp.concatenate(parts, axis=1)

    def tile_lanes(x, n):
        return jnp.concatenate([x] * n, axis=1)

    def tile_rows(x, n):
        return jnp.concatenate([x] * n, axis=0)

    def pair_state_terms(qs, ks_f32, vs, st_ref, p, hd):
        r = st_ref[:, 2 * p:2 * p + 2].reshape(ns * 2 * hd, hd).astype(BF16)
        kt = tile_rows(ks_f32.T.astype(BF16), ns) * msegt
        inter, kv = [], []
        for hh, m in ((0, lo), (1, hi_m)):
            qh = jnp.where(m, qs, jnp.zeros_like(qs))
            inter.append(_dot(tile_lanes(qh, ns) * mseg, r))
            kv.append(_dot(kt, vs[:, hh * hd:(hh + 1) * hd]))
        return inter, kv

    q_r = rope(proj_ref[:, C_QR:C_QR + R_W], cosq_ref, sinq_ref)
    k_r = rope(proj_ref[:, C_KR:C_KR + R_W], cosk_ref, sink_ref)
    q_rb = q_r.astype(BF16)
    k_rb = k_r.astype(BF16)
    kd_r = k_r * kdec_ref[...]
    v_rb = proj_ref[:, C_VR:C_VR + R_W].astype(BF16)
    o_parts = []
    for p in range(R_HEADS // 2):
        sl = slice(p * V7X_LANES, (p + 1) * V7X_LANES)
        qs, ks, vs = q_rb[:, sl], k_rb[:, sl], v_rb[:, sl]
        sc = _dot_nt(qs, _stack_masked(ks, lo, hi_m)) * rdec_ref[p]
        inter, kv = pair_state_terms(qs, kd_r[:, sl], vs, sr_ref, p, R_DK)
        o = (_dot(sc.astype(BF16), _stack_masked(vs, lo, hi_m))
             + jnp.concatenate(inter, axis=1) * qdec_ref[:, sl])
        o_parts.append(o)
        for hh in range(2):
            h = 2 * p + hh
            sr_out[:, h] = sr_ref[:, h] * chunk_decay[h] + kv[hh].reshape(ns, 2, R_DK, R_DV)[:, hh]
    gate_r = _silu(proj_ref[:, C_GR:C_GR + R_W])
    mix_ref[:, 0:R_W] = _norm_gate(jnp.concatenate(o_parts, axis=1), hmat, gcat_ref[:, 0:R_W], gate_r)

    small = proj_ref[:, C_SM:C_SM + V7X_LANES]
    z = _dot(small.astype(BF16), wga_ref[...]) + bga_ref[...]
    log_a = _log_sigmoid(z) / G_NORMALIZER
    b = _dot3(tri_ref[...], log_a)
    b_tot = _seq_bcast(b, T - 1, T)
    q_g = proj_ref[:, C_QG:C_QG + G_QK] * (G_DK ** -0.5)
    k_g = proj_ref[:, C_KG:C_KG + G_QK]
    v_g = proj_ref[:, C_VG:C_VG + G_W]
    v_gb = v_g.astype(BF16)
    tok = lax.broadcasted_iota(jnp.int32, (L, G_QK), 0) % T
    emat = emat_ref[...]
    o_g = jnp.zeros((L, G_W), F32)
    for j in range(T):
        e = jnp.exp(jnp.where(tok >= j, b - _seq_bcast(b, j, T), -jnp.inf))
        tj = (e * q_g * _seq_bcast(k_g, j, T)).astype(BF16)
        o_g = o_g + _dot(tj, emat) * _seq_bcast(v_g, j, T)
    qt = (q_g * jnp.exp(b)).astype(BF16)
    kl = k_g * jnp.exp(b_tot - b)
    dtot = jnp.exp(b_tot)
    lane_h = lane // G_DK
    inter_parts = [None] * G_HEADS
    for h0, heads in ((0, (0, 1, 2, 3)), (2, (4, 5))):
        c0 = h0 * G_DK
        q_s = qt[:, c0:c0 + V7X_LANES]
        r = sg_ref[:, h0:h0 + 4].reshape(ns * V7X_LANES, G_DV)
        rb = r.astype(BF16)
        kt = tile_rows(kl[:, c0:c0 + V7X_LANES].T.astype(BF16), ns) * msegt
        dt = dtot[:, c0:c0 + V7X_LANES].T
        dcols = []
        for s in range(ns):
            dcols.append(jnp.broadcast_to(dt[:, s * T:s * T + 1], (V7X_LANES, G_DV)))
        dfull = jnp.concatenate(dcols, axis=0).reshape(ns, 4, G_DK, G_DV)
        for h in heads:
            qh = jnp.where(lane_h == h - h0, q_s, jnp.zeros_like(q_s))
            inter_parts[h] = _dot(tile_lanes(qh, ns) * mseg, rb)
            kv = _dot(kt, v_gb[:, h * G_DV:(h + 1) * G_DV])
            sg_out[:, h] = sg_ref[:, h] * dfull[:, h - h0] + kv.reshape(ns, 4, G_DK, G_DV)[:, h - h0]
    o_g = o_g + jnp.concatenate(inter_parts, axis=1)
    gate_g = _silu(proj_ref[:, C_RG:C_RG + G_W])
    mix_ref[:, R_W:R_W + G_W] = _norm_gate(o_g, hmat, gcat_ref[:, R_W:R_W + G_W], gate_g)

    u = proj_ref[:, C_QKM:C_QKM + M_QK]
    conv_scr[:, 5:8, :] = conv_ref[...]
    conv_scr[:, 8:8 + T, :] = u.reshape(ns, T, M_QK)
    y = convb_ref[...]
    for j in range(CONV_W - 1):
        y = y + conv_scr[:, 5 + j:5 + j + T, :].reshape(L, M_QK) * convw_ref[j:j + 1, :]
    y = y + u * convw_ref[CONV_W - 1:CONV_W, :]
    conv_out[...] = conv_scr[:, 5 + T:8 + T, :]
    qk = _silu(y)
    q_m = qk[:, :M_W]
    k_m = qk[:, M_W:] * (M_DK ** -0.5)
    q_mb = q_m.astype(BF16)
    k_mb = k_m.astype(BF16)
    v_mb = proj_ref[:, C_VM:C_VM + M_W].astype(BF16)
    gates = small + bsm_ref[...]
    f_cum = _dot3(tri_ref[...], _log_sigmoid(gates))
    i_rows = _dot3_nt(selr_ref[...], gates)
    f_rows = _dot3_nt(selr_ref[...], f_cum)
    head_lane = (lane >= SM_I) & (lane < SM_I + M_HEADS)
    fc = jnp.where(head_lane, pltpu.roll(f_cum, V7X_LANES - (SM_F - SM_I), 1), 0.0)
    m_prev = m_ref[...]
    a = fc + m_prev
    mx = jnp.full((L, V7X_LANES), -jnp.inf, F32)
    dms = []
    for h in range(M_HEADS):
        dm = jnp.where(segcausal, (fc[:, SM_I + h:SM_I + h + 1] - f_rows[M_HEADS + h:M_HEADS + h + 1, :])
                       + i_rows[h:h + 1, :], -jnp.inf)
        dms.append(dm)
        mx = jnp.where(lane == SM_I + h, jnp.max(dm, axis=-1, keepdims=True), mx)
    m_tok = jnp.maximum(a, mx)
    w_inter = jnp.exp(a - m_tok)
    n_rows = n_ref[...]
    hs, lows = _split2(q_m * n_rows)
    qn_b = _dot(hs, hmat_ref[0:M_W, 0:M_W]) + _dot(lows, hmat_ref[0:M_W, 0:M_W])
    rs_all = jnp.zeros((L, V7X_LANES), F32)
    qn_all = jnp.zeros((L, V7X_LANES), F32)
    scs = []
    for p in range(M_HEADS // 2):
        sl = slice(p * V7X_LANES, (p + 1) * V7X_LANES)
        sc = _dot_nt(q_mb[:, sl], _stack_masked(k_mb[:, sl], lo, hi_m))
        mt_a = jnp.broadcast_to(m_tok[:, SM_I + 2 * p:SM_I + 2 * p + 1], (L, L))
        mt_b = jnp.broadcast_to(m_tok[:, SM_I + 2 * p + 1:SM_I + 2 * p + 2], (L, L))
        sc_a = sc[:, :L] * jnp.exp(dms[2 * p] - mt_a)
        sc_b = sc[:, L:] * jnp.exp(dms[2 * p + 1] - mt_b)
        rs_all = jnp.where(lane == SM_I + 2 * p, jnp.sum(sc_a, axis=-1, keepdims=True), rs_all)
        rs_all = jnp.where(lane == SM_I + 2 * p + 1, jnp.sum(sc_b, axis=-1, keepdims=True), rs_all)
        qn_all = jnp.where(lane == SM_I + 2 * p, qn_b[:, p * V7X_LANES:p * V7X_LANES + 1], qn_all)
        qn_all = jnp.where(lane == SM_I + 2 * p + 1, qn_b[:, p * V7X_LANES + HEAD_W:p * V7X_LANES + HEAD_W + 1], qn_all)
        scs.append(jnp.concatenate([sc_a, sc_b], axis=1).astype(BF16))
    den = rs_all + qn_all * w_inter
    inv = 1.0 / jnp.maximum(jnp.abs(den), jnp.exp(-m_tok))
    m_new = _seq_bcast(jnp.where(head_lane, m_tok, 0.0), T - 1, T)
    wk = jnp.exp(_seq_bcast(fc, T - 1, T) - fc + gates - m_new)
    scale = jnp.exp(_seq_bcast(a, T - 1, T) - m_new)
    h_parts = []
    kw_parts = []
    scale_parts = []
    for p in range(M_HEADS // 2):
        sl = slice(p * V7X_LANES, (p + 1) * V7X_LANES)
        ia, ib = SM_I + 2 * p, SM_I + 2 * p + 1
        kw = k_m[:, sl] * _lane_pick(wk, ia, ib, lo)
        kw_parts.append(kw)
        inter, kv = pair_state_terms(q_mb[:, sl], kw, v_mb[:, sl], c_ref, p, M_DK)
        num = (_dot(scs[p], _stack_masked(v_mb[:, sl], lo, hi_m))
               + jnp.concatenate(inter, axis=1) * _lane_pick(w_inter, ia, ib, lo))
        h_parts.append(num * _lane_pick(inv, ia, ib, lo))
        scale_parts.append(_lane_pick(scale, ia, ib, lo))
        for hh in range(2):
            h = 2 * p + hh
            sc_rows = jnp.broadcast_to(scale[:, SM_I + h:SM_I + h + 1], (L, M_DV)).reshape(ns, T, M_DV)
            sc_h = jnp.broadcast_to(sc_rows[:, 0:1, :], (ns, M_DK, M_DV))
            c_out[:, h] = c_ref[:, h] * sc_h + kv[hh].reshape(ns, 2, M_DK, M_DV)[:, hh]
    kw_all = jnp.concatenate(kw_parts, axis=1)
    n_out[...] = n_rows * jnp.concatenate(scale_parts, axis=1) + _dot3(segones_ref[...], kw_all)
    m_out[...] = m_new
    gate_m = _sigmoid(proj_ref[:, C_OM:C_OM + M_W])
    mix_ref[:, R_W + G_W:D_MIX] = _norm_gate(jnp.concatenate(h_parts, axis=1), hmat_ref[0:M_W, 0:M_W],
                                             gcat_ref[:, R_W + G_W:D_MIX], gate_m)


def _sample_tables(T, pos0):
    L = ROWS
    half = R_DK // 2
    inv = ROPE_BASE ** (-jnp.arange(half, dtype=F32) * 2.0 / R_DK)
    tok = np.arange(L) % T
    seq = np.arange(L) // T
    pos = pos0 + jnp.asarray(tok, F32)
    ang = pos[:, None] * inv[None, :]
    cos = jnp.tile(jnp.cos(ang), (1, V7X_LANES // half))
    sin_h = jnp.sin(ang)
    sin = jnp.tile(jnp.concatenate([-sin_h, sin_h], axis=1), (1, V7X_LANES // R_DK))
    qs = R_DK ** -0.5

    log_gamma = jnp.log(1.0 - 2.0 ** (-5.0 - jnp.arange(R_HEADS, dtype=F32)))
    tf = jnp.asarray(tok, F32)
    rel = tf[:, None] - tf[None, :]
    ok_np = (seq[:, None] == seq[None, :]) & (tok[None, :] <= tok[:, None])
    ok = jnp.asarray(ok_np)
    rdecay = jnp.where(ok[None], jnp.exp(log_gamma[:, None, None] * jnp.where(ok, rel, 0.0)[None]), 0.0)
    rdec = jnp.concatenate([rdecay[0::2], rdecay[1::2]], axis=2)
    qdec = jnp.repeat(jnp.exp(log_gamma[:, None] * (tf + 1.0)).T, R_DK, axis=1)
    kdec = jnp.repeat(jnp.exp(log_gamma[:, None] * (T - 1.0 - tf)).T, R_DK, axis=1)
    lg32 = np.log(1.0 - 2.0 ** (-5.0 - np.arange(R_HEADS, dtype=np.float64))).astype(np.float32)
    chunk_decay = tuple(float(np.exp(v * np.float32(T))) for v in lg32)

    selr = np.zeros((16, V7X_LANES), np.float32)
    for h in range(M_HEADS):
        selr[h, SM_I + h] = 1.0
        selr[M_HEADS + h, SM_F + h] = 1.0
    hv = np.arange(G_W) // G_DV
    hc = np.arange(G_QK) // G_DK
    gmask = (hv[:, None] == hc[None, :]).astype(np.float32)
    hmat = (hv[:, None] == hv[None, :]).astype(np.float32)
    ns = L // T
    mseg = (seq[:, None] == (np.arange(ns * V7X_LANES) // V7X_LANES)[None, :]).astype(np.float32)
    return dict(
        cosq=cos * qs, sinq=sin * qs, cosk=cos, sink=sin, rdec=rdec, qdec=qdec, kdec=kdec, chunk_decay=chunk_decay,
        tri=jnp.asarray(ok_np, BF16), segones=jnp.asarray(seq[:, None] == seq[None, :], BF16),
        selr=jnp.asarray(selr, BF16), hmat=jnp.asarray(hmat, BF16),
        emat=jnp.asarray(gmask.T, BF16), mseg=jnp.asarray(mseg, BF16), msegt=jnp.asarray(mseg.T, BF16))


_SAMPLE_STATE_DIMS = ((R_HEADS, R_DK, R_DV), (G_HEADS, G_DK, G_DV), (M_HEADS, M_DK, M_DV), (M_W,),
                      (V7X_LANES,), (CONV_W - 1, M_QK))


def mixer_sample(proj, tabs, lw, state, l, depth, prev, *, T):
    M = proj.shape[0]
    B = M // T
    assert B % NS == 0 and NS * T == ROWS
    L = ROWS
    const2 = lambda b: (0, 0)
    const3 = lambda b: (0, 0, 0)

    def st_spec(dims):
        return pl.BlockSpec((None, NS) + dims, lambda b: (l, b) + (0,) * len(dims))

    def row_spec(w):
        return pl.BlockSpec((None, L, w), lambda b: (l, b, 0))

    st_specs = ([st_spec(d) for d in _SAMPLE_STATE_DIMS[:3]]
                + [row_spec(M_W), row_spec(V7X_LANES), st_spec(_SAMPLE_STATE_DIMS[5])])
    in_specs = [
        pl.BlockSpec((L, D_IN_PAD), lambda b: (b, 0)),
        pl.BlockSpec((L, V7X_LANES), const2), pl.BlockSpec((L, V7X_LANES), const2),
        pl.BlockSpec((L, V7X_LANES), const2), pl.BlockSpec((L, V7X_LANES), const2),
        pl.BlockSpec((R_HEADS // 2, L, 2 * L), const3),
        pl.BlockSpec((L, R_W), const2),
        pl.BlockSpec((L, R_W), const2),
        pl.BlockSpec((L, L), const2),
        pl.BlockSpec((L, L), const2),
        pl.BlockSpec((16, V7X_LANES), const2),
        pl.BlockSpec((G_W, G_W), const2),
        pl.BlockSpec((G_QK, G_W), const2),
        pl.BlockSpec((L, NS * V7X_LANES), const2),
        pl.BlockSpec((NS * V7X_LANES, L), const2),
        pl.BlockSpec((V7X_LANES, G_QK), const2),
        pl.BlockSpec((1, G_QK), const2),
        pl.BlockSpec((CONV_W, M_QK), const2),
        pl.BlockSpec((1, M_QK), const2),
        pl.BlockSpec((1, V7X_LANES), const2),
        pl.BlockSpec((1, D_MIX), const2),
    ] + st_specs
    args = [proj, tabs["cosq"], tabs["sinq"], tabs["cosk"], tabs["sink"], tabs["rdec"], tabs["qdec"], tabs["kdec"],
            tabs["tri"], tabs["segones"], tabs["selr"], tabs["hmat"], tabs["emat"], tabs["mseg"], tabs["msegt"],
            lw["wga"], lw["bga"], lw["convw"], lw["convb"], lw["bsm"], lw["gcat"]] + list(state)
    n_in = len(args)
    in_specs += [pl.BlockSpec(memory_space=pl.ANY)] * 6
    args += list(prev)
    aliases = {n_in + i: 1 + i for i in range(6)}
    shapes = [(depth, B) + d for d in _SAMPLE_STATE_DIMS]
    shapes[3], shapes[4] = (depth, M, M_W), (depth, M, V7X_LANES)
    out_shape = (jax.ShapeDtypeStruct((M, D_MIX), BF16),) + tuple(jax.ShapeDtypeStruct(sh, F32) for sh in shapes)
    out_specs = (pl.BlockSpec((L, D_MIX), lambda b: (b, 0)),) + tuple(st_specs)
    kern = functools.partial(_mixer_sample_kernel, T=T, chunk_decay=tabs["chunk_decay"])
    outs = pl.pallas_call(
        kern, out_shape=out_shape, grid=(B // NS,), in_specs=in_specs, out_specs=out_specs,
        scratch_shapes=[pltpu.VMEM((NS, 8 + T, M_QK), F32)],
        input_output_aliases=aliases,
        compiler_params=_cparams(("parallel",)), name="mixer_sample",
    )(*args)
    return outs[0], tuple(outs[1:])


def _prep_layer(l, g_mix, w_in, w_ga2, b_ga, conv_w, conv_b, b_i, b_f, g_ret, g_gla, g_mlstm, w_out,
                g_xattn, g_mem, w_xq, w_xk, w_xv, w_xo, g_ffn, w_gate, w_up, w_down):
    w = w_in[l]
    a0 = 2 * R_HEADS * R_DK + 2 * R_W + 2 * G_QK + 2 * G_W
    m0 = a0 + G_RANK
    g0 = m0 + M_QK + 2 * M_W
    w_pad = jnp.concatenate(
        [w[:, :a0], w[:, m0:g0], w[:, g0:g0 + 2 * M_HEADS], w[:, a0:m0],
         jnp.zeros((D_MODEL, D_IN_PAD - C_SM - G_RANK - 2 * M_HEADS), F32)], axis=1).astype(BF16)
    wga = jnp.zeros((V7X_LANES, G_QK), F32).at[SM_AG:SM_AG + G_RANK].set(w_ga2[l]).astype(BF16)
    bsm = (jnp.zeros((1, V7X_LANES), F32).at[0, SM_I:SM_I + M_HEADS].set(b_i[l])
           .at[0, SM_F:SM_F + M_HEADS].set(b_f[l]))
    return dict(
        g_mix=g_mix[l], w_in=w_pad, wga=wga, bga=b_ga[l].reshape(1, G_QK), convw=conv_w[l],
        convb=conv_b[l].reshape(1, M_QK), bsm=bsm,
        gcat=jnp.concatenate([g_ret[l], g_gla[l], g_mlstm[l]]).reshape(1, D_MIX),
        w_out=w_out[l].astype(BF16), g_xattn=g_xattn[l], g_mem=g_mem[l],
        w_xq=w_xq[l].astype(BF16), w_xk=w_xk[l].astype(BF16), w_xv=w_xv[l].astype(BF16),
        w_xo=w_xo[l].astype(BF16), g_ffn=g_ffn[l], w_gate=w_gate[l].astype(BF16),
        w_up=w_up[l].astype(BF16), w_down=w_down[l].astype(BF16))


def _layer(x, mix_fn, attend, lw, g_final, *, tm, final_norm):
    B, T, D = x.shape
    M = B * T
    x2 = x.reshape(M, D)
    proj = rms_matmul(x2, lw["g_mix"], lw["w_in"], tm=tm, tn=D_IN_PAD)
    mix, new_state = mix_fn(proj)
    x2 = attend(mix, x2)
    x2 = swiglu_res(x2, lw["g_ffn"], lw["w_gate"], lw["w_up"], lw["w_down"], g_final,
                    tm=tm, tf=D_FF, final_norm=final_norm)
    return x2.reshape(B, T, D), new_state


def kernel(x_prompt, x_sample, state_ret, state_gla, state_mlstm_C, state_mlstm_n, state_mlstm_m, state_mlstm_conv, cache_mem_k, cache_mem_v, mem_prompt, g_mix, w_in, w_ga2, b_ga, conv_w, conv_b, b_i, b_f, g_ret, g_gla, g_mlstm, w_out, g_xattn, g_mem, w_xq, w_xk, w_xv, w_xo, g_ffn, w_gate, w_up, w_down, g_final):
    B, T, D = x_prompt.shape
    Bs, Ts, _ = x_sample.shape
    depth = w_in.shape[0]
    assert T % CHUNK == 0 and Ts * NS == ROWS and Bs % NS == 0
    tabs_p = _prompt_tables(T, CHUNK, 16)
    tabs_s = _sample_tables(Ts, float(PAST_LEN))
    tm_p = 512 if (B * T) % 512 == 0 else B * T
    tm_s = 512 if (Bs * Ts) % 512 == 0 else Bs * Ts
    tq_p = 512 if T % 512 == 0 else T
    nb_x = 8 if Bs % 8 == 0 else 1

    sample_state = (state_ret, state_gla, state_mlstm_C,
                    jnp.repeat(state_mlstm_n.reshape(depth, Bs, M_W), Ts, axis=1),
                    jnp.repeat(jnp.pad(state_mlstm_m, ((0, 0), (0, 0), (0, V7X_LANES - M_HEADS))), Ts, axis=1),
                    state_mlstm_conv)
    s_shapes = [(depth, Bs) + d for d in _SAMPLE_STATE_DIMS]
    s_shapes[3], s_shapes[4] = (depth, Bs * Ts, M_W), (depth, Bs * Ts, V7X_LANES)
    s_st = tuple(jnp.zeros(sh, F32) for sh in s_shapes)
    p_st = tuple(jnp.zeros((depth, B) + d, F32) for d in _STATE_DIMS)
    p_mem = tuple(jnp.zeros((depth, B, N_MEM, X_HEADS, X_HD), F32) for _ in range(2))
    hp, hs = x_prompt, x_sample
    for l in range(depth):
        lw = _prep_layer(l, g_mix, w_in, w_ga2, b_ga, conv_w, conv_b, b_i, b_f, g_ret, g_gla, g_mlstm,
                         w_out, g_xattn, g_mem, w_xq, w_xk, w_xv, w_xo, g_ffn, w_gate, w_up, w_down)
        last = l == depth - 1
        k5, v5, kb, vb = memory_kv(mem_prompt, lw["g_mem"], lw["w_xk"], lw["w_xv"], l, depth, p_mem)
        p_mem = (k5, v5)

        def mix_p(proj):
            mix, st = mixer_prompt(proj.reshape(B, T, D_IN_PAD), tabs_p, lw, l, depth, p_st, L=CHUNK, CS=16)
            return mix.reshape(B * T, D_MIX), st

        def attend_p(mix, x):
            y = post_mix(mix.reshape(B, T, D_MIX), x.reshape(B, T, D), kb, vb,
                         lw["w_out"], lw["g_xattn"], lw["w_xq"], lw["w_xo"], tq=tq_p)
            return y.reshape(B * T, D)

        hp, p_st = _layer(hp, mix_p, attend_p, lw, g_final, tm=tm_p, final_norm=last)
        hs, s_st = _layer(hs, lambda proj: mixer_sample(proj, tabs_s, lw, sample_state, l, depth, s_st, T=Ts),
                          lambda mix, x: post_mix_cache(mix, x, cache_mem_k, cache_mem_v, l, lw["w_out"],
                                                        lw["g_xattn"], lw["w_xq"], lw["w_xo"], nb=nb_x, T=Ts),
                          lw, g_final, tm=tm_s, final_norm=last)

    p_out = p_st[:4] + (p_st[4].reshape(depth, B, M_HEADS), p_st[5])
    s_out = s_st[:3] + (s_st[3][:, ::Ts].reshape(depth, Bs, M_HEADS, M_DK), s_st[4][:, ::Ts, :M_HEADS], s_st[5])
    return (hp, hs, *p_out, *p_mem, *s_out)
```

```python
import functools
import math

import numpy as np
import jax
import jax.numpy as jnp
from jax import lax
from jax.experimental import pallas as pl
from jax.experimental.pallas import tpu as pltpu

F32 = jnp.float32
BF16 = jnp.bfloat16

D_MODEL = 1024
PAST_LEN = 16384
R_HEADS, R_DK, R_DV = 6, 64, 64
G_HEADS, G_DK, G_DV, G_RANK = 6, 32, 64, 16
G_NORMALIZER = 16.0
M_HEADS, M_DK, M_DV = 4, 64, 64
CONV_W = 4
X_HEADS = 4
X_HD = D_MODEL // X_HEADS
N_MEM = 256
D_FF = int(math.ceil(8 * D_MODEL / 3 / 256)) * 256
CHUNK = 128
EPS = 1e-6
ROPE_BASE = 10000.0

R_W = R_HEADS * R_DV
G_QK = G_HEADS * G_DK
G_W = G_HEADS * G_DV
M_QK = 2 * M_HEADS * M_DK
M_W = M_HEADS * M_DV
D_MIX = R_W + G_W + M_W

C_QR, C_KR, C_VR, C_GR = 0, 384, 768, 1152
C_QG, C_KG, C_VG, C_RG = 1536, 1728, 1920, 2304
C_QKM, C_VM, C_OM, C_SM = 2688, 3200, 3456, 3712
D_IN_PAD = 3840
SM_I, SM_F, SM_AG = 0, 4, 8
HEAD_W = 64
GLA_SAFE_LOG_RANGE = 60.0

V7X_LANES = 128
VMEM_LIMIT = 56 * 1024 * 1024


def _cparams(sem):
    return pltpu.CompilerParams(dimension_semantics=sem, vmem_limit_bytes=VMEM_LIMIT)


def _sigmoid(x):
    return 1.0 / (1.0 + jnp.exp(-x))


def _silu(x):
    return x * _sigmoid(x)


def _log_sigmoid(x):
    return jnp.minimum(x, 0.0) - jnp.log(1.0 + jnp.exp(-jnp.abs(x)))


def _dot(a, b):
    return jnp.dot(a, b, preferred_element_type=F32)


def _dot_nt(a, b):
    return lax.dot_general(a, b, (((1,), (1,)), ((), ())), preferred_element_type=F32)


def _dot_tn(a, b):
    return lax.dot_general(a, b, (((0,), (0,)), ((), ())), preferred_element_type=F32)


def _split3(x):
    hi = x.astype(BF16)
    r1 = x - hi.astype(F32)
    mid = r1.astype(BF16)
    lo = (r1 - mid.astype(F32)).astype(BF16)
    return hi, mid, lo


def _dot3(a, x):
    hi, mid, lo = _split3(x)
    return _dot(a, hi) + _dot(a, mid) + _dot(a, lo)


def _dot3_nt(a, x):
    hi, mid, lo = _split3(x)
    return _dot_nt(a, hi) + _dot_nt(a, mid) + _dot_nt(a, lo)


def _rms(x, g):
    return x * lax.rsqrt(jnp.mean(x * x, axis=-1, keepdims=True) + EPS) * g


def _wspec(w, l):
    return pl.BlockSpec((None,) + w.shape[1:], lambda *_: (l, 0, 0))


def _rms_matmul_kernel(x_ref, g_ref, w_ref, o_ref):
    o_ref[...] = _dot(_rms(x_ref[...], g_ref[...]).astype(BF16), w_ref[...]).astype(o_ref.dtype)


def rms_matmul(x, g, w, l, *, tm, out_dtype=F32):
    M, D = x.shape
    N = w.shape[2]
    assert M % tm == 0
    return pl.pallas_call(
        _rms_matmul_kernel,
        out_shape=jax.ShapeDtypeStruct((M, N), out_dtype),
        grid=(M // tm,),
        in_specs=[pl.BlockSpec((tm, D), lambda i: (i, 0)),
                  pl.BlockSpec((1, D), lambda i: (0, 0)),
                  _wspec(w, l)],
        out_specs=pl.BlockSpec((tm, N), lambda i: (i, 0)),
        compiler_params=_cparams(("parallel",)),
        name="rms_matmul",
    )(x, g.reshape(1, D), w)


def _swiglu_kernel(x_ref, g_ref, wg_ref, wu_ref, wd_ref, gf_ref, o_ref, *, final_norm):
    x = x_ref[...]
    xn = _rms(x, g_ref[...]).astype(BF16)
    h = _silu(_dot(xn, wg_ref[...])) * _dot(xn, wu_ref[...])
    y = x + _dot(h.astype(BF16), wd_ref[...])
    if final_norm:
        y = _rms(y, gf_ref[...])
    o_ref[...] = y


def swiglu_res(x, g, wg, wu, wd, l, g_final, *, tm, final_norm):
    M, D = x.shape
    assert M % tm == 0
    row = pl.BlockSpec((tm, D), lambda i: (i, 0))
    vec = pl.BlockSpec((1, D), lambda i: (0, 0))
    return pl.pallas_call(
        functools.partial(_swiglu_kernel, final_norm=final_norm),
        out_shape=jax.ShapeDtypeStruct((M, D), F32),
        grid=(M // tm,),
        in_specs=[row, vec, _wspec(wg, l), _wspec(wu, l), _wspec(wd, l), vec],
        out_specs=row,
        compiler_params=_cparams(("parallel",)),
        name="swiglu_res",
    )(x, g.reshape(1, D), wg, wu, wd, g_final.reshape(1, D))


def _memkv_kernel(x_ref, g_ref, wk_ref, wv_ref, *refs):
    k5_ref, v5_ref, kb_ref, vb_ref = refs[-4:]
    xn = _rms(x_ref[...], g_ref[...]).astype(BF16)
    for w_ref, o5_ref, ob_ref in ((wk_ref, k5_ref, kb_ref), (wv_ref, v5_ref, vb_ref)):
        y = _dot(xn, w_ref[...])
        ob_ref[...] = y.astype(BF16)
        for h in range(X_HEADS):
            o5_ref[:, h, :] = y[:, h * X_HD:(h + 1) * X_HD]


def memory_kv(mem, g, wk, wv, l, depth, prev):
    B, _, D = mem.shape
    o5 = jax.ShapeDtypeStruct((depth, B, N_MEM, X_HEADS, X_HD), F32)
    ob = jax.ShapeDtypeStruct((B, N_MEM, D), BF16)
    in_specs = [pl.BlockSpec((None, N_MEM, D), lambda b: (b, 0, 0)),
                pl.BlockSpec((1, D), lambda b: (0, 0)),
                _wspec(wk, l), _wspec(wv, l)]
    in_specs += [pl.BlockSpec(memory_space=pl.ANY)] * 2
    args = [mem, g.reshape(1, D), wk, wv] + list(prev)
    aliases = {4: 0, 5: 1}
    spec5 = pl.BlockSpec((None, None, N_MEM, X_HEADS, X_HD), lambda b: (l, b, 0, 0, 0))
    specb = pl.BlockSpec((None, N_MEM, D), lambda b: (b, 0, 0))
    return pl.pallas_call(
        _memkv_kernel,
        out_shape=(o5, o5, ob, ob),
        grid=(B,),
        in_specs=in_specs,
        out_specs=(spec5, spec5, specb, specb),
        input_output_aliases=aliases,
        compiler_params=_cparams(("parallel",)),
        name="memory_kv",
    )(*args)


def _out_and_query(a_ref, x_ref, wo_ref, g_ref, wq_ref):
    x1 = x_ref[...] + _dot(a_ref[...], wo_ref[...])
    q = _dot(_rms(x1, g_ref[...]).astype(BF16), wq_ref[...]).astype(BF16)
    return x1, q


def _post_mix_kernel(a_ref, x_ref, k_ref, v_ref, wo_ref, g_ref, wq_ref, wxo_ref, o_ref):
    x1, q = _out_and_query(a_ref, x_ref, wo_ref, g_ref, wq_ref)
    scale = X_HD ** -0.5
    parts = []
    for h in range(X_HEADS):
        sl = slice(h * X_HD, (h + 1) * X_HD)
        s = _dot_nt(q[:, sl], k_ref[:, sl]) * scale
        p = jnp.exp(s - jnp.max(s, axis=-1, keepdims=True))
        l = jnp.sum(p, axis=-1, keepdims=True)
        parts.append((_dot(p.astype(BF16), v_ref[:, sl]) / l).astype(BF16))
    o_ref[...] = x1 + _dot(jnp.concatenate(parts, axis=1), wxo_ref[...])


def post_mix(a, x, mk, mv, wo, g, wq, wxo, l, *, tq):
    B, T, D = x.shape
    assert T % tq == 0
    tok = lambda w: pl.BlockSpec((None, tq, w), lambda b, i: (b, i, 0))
    mem = pl.BlockSpec((None, N_MEM, D), lambda b, i: (b, 0, 0))
    return pl.pallas_call(
        _post_mix_kernel,
        out_shape=jax.ShapeDtypeStruct((B, T, D), F32),
        grid=(B, T // tq),
        in_specs=[tok(a.shape[2]), tok(D), mem, mem, _wspec(wo, l), pl.BlockSpec((1, D), lambda b, i: (0, 0)),
                  _wspec(wq, l), _wspec(wxo, l)],
        out_specs=tok(D),
        compiler_params=_cparams(("parallel", "arbitrary")),
        name="post_mix",
    )(a, x, mk, mv, wo, g.reshape(1, D), wq, wxo)


def _post_mix_cache_kernel(a_ref, x_ref, k_ref, v_ref, wo_ref, g_ref, wq_ref, wxo_ref, o_ref, *, nb, T):
    x1, q_all = _out_and_query(a_ref, x_ref, wo_ref, g_ref, wq_ref)
    R = X_HEADS * T
    rowh = lax.broadcasted_iota(jnp.int32, (R, N_MEM * X_HEADS), 0) // T
    colh = lax.broadcasted_iota(jnp.int32, (R, N_MEM * X_HEADS), 1) % X_HEADS
    own = rowh == colh
    outs = []
    for s in range(nb):
        q = q_all[s * T:(s + 1) * T]
        qf = jnp.concatenate([q[:, h * X_HD:(h + 1) * X_HD] for h in range(X_HEADS)], axis=0)
        kf = k_ref[s].reshape(N_MEM * X_HEADS, X_HD).astype(BF16)
        vf = v_ref[s].reshape(N_MEM * X_HEADS, X_HD).astype(BF16)
        sc = jnp.where(own, _dot_nt(qf, kf) * (X_HD ** -0.5), -jnp.inf)
        p = jnp.exp(sc - jnp.max(sc, axis=-1, keepdims=True))
        l = jnp.sum(p, axis=-1, keepdims=True)
        o = (_dot(p.astype(BF16), vf) / l).astype(BF16)
        outs.append(jnp.concatenate([o[h * T:(h + 1) * T] for h in range(X_HEADS)], axis=1))
    o_ref[...] = x1 + _dot(jnp.concatenate(outs, axis=0), wxo_ref[...])


def post_mix_cache(a, x, ck, cv, wo, g, wq, wxo, l, *, nb, T):
    M, D = x.shape
    rows = nb * T
    assert M % rows == 0
    cspec = pl.BlockSpec((None, nb, N_MEM, X_HEADS, X_HD), lambda b: (l, b, 0, 0, 0))
    tok = lambda w: pl.BlockSpec((rows, w), lambda b: (b, 0))
    return pl.pallas_call(
        functools.partial(_post_mix_cache_kernel, nb=nb, T=T),
        out_shape=jax.ShapeDtypeStruct((M, D), F32),
        grid=(M // rows,),
        in_specs=[tok(a.shape[1]), tok(D), cspec, cspec, _wspec(wo, l), pl.BlockSpec((1, D), lambda b: (0, 0)),
                  _wspec(wq, l), _wspec(wxo, l)],
        out_specs=tok(D),
        compiler_params=_cparams(("parallel",)),
        name="post_mix_cache",
    )(a, x, ck, cv, wo, g.reshape(1, D), wq, wxo)


def _eye(n):
    r = lax.broadcasted_iota(jnp.int32, (n, n), 0)
    c = lax.broadcasted_iota(jnp.int32, (n, n), 1)
    return jnp.where(r == c, 1.0, 0.0).astype(BF16)


_STATE_DIMS = ((R_HEADS, R_DK, R_DV), (G_HEADS, G_DK, G_DV), (M_HEADS, M_DK, M_DV), (M_HEADS, M_DK),
               (1, M_HEADS), (CONV_W - 1, M_QK))


def _split2(x):
    hi = x.astype(BF16)
    lo = (x - hi.astype(F32)).astype(BF16)
    return hi, lo


def _head_mean_sq(o, hmat):
    hi, lo = _split2(o * o)
    return (_dot(hi, hmat) + _dot(lo, hmat)) * (1.0 / HEAD_W)


def _norm_gate(o, hmat, g_row, gate):
    return (o * lax.rsqrt(_head_mean_sq(o, hmat) + EPS) * g_row * gate).astype(BF16)


def _lane_pick(cols, idx_lo, idx_hi, lo_mask):
    L = cols.shape[0]
    a = jnp.broadcast_to(cols[:, idx_lo:idx_lo + 1], (L, V7X_LANES))
    b = jnp.broadcast_to(cols[:, idx_hi:idx_hi + 1], (L, V7X_LANES))
    return jnp.where(lo_mask, a, b)


def _stack_masked(x, m_a, m_b):
    z = jnp.zeros_like(x)
    return jnp.concatenate([jnp.where(m_a, x, z), jnp.where(m_b, x, z)], axis=0)


def _mixer_prompt_kernel(proj_ref, cosq_ref, sinq_ref, cosk_ref, sink_ref, rdec_ref, qdec_ref, kdec_ref,
                         tri_ref, btri_ref, bones_ref, selr_ref, hmat_ref, emat_ref, pmask_ref,
                         wga_ref, bga_ref, convw_ref, convb_ref, bsm_ref, gcat_ref, *rest,
                         L, CS, NC, chunk_decay):
    (mix_ref, sr_out, sg_out, c_out, n_out, m_out, conv_out,
     srp_scr, sgt_scr, cp_scr, n_scr, m_scr, conv_scr, og_scr) = rest[-14:]
    c = pl.program_id(1)

    @pl.when(c == 0)
    def _():
        srp_scr[...] = jnp.zeros_like(srp_scr)
        sgt_scr[...] = jnp.zeros_like(sgt_scr)
        cp_scr[...] = jnp.zeros_like(cp_scr)
        n_scr[...] = jnp.zeros_like(n_scr)
        m_scr[...] = jnp.zeros_like(m_scr)
        conv_scr[0:8, :] = jnp.zeros((8, M_QK), F32)

    lane = lax.broadcasted_iota(jnp.int32, (L, V7X_LANES), 1)
    lo = lane < HEAD_W
    hi_m = lane >= HEAD_W
    first_half = (lane % R_DK) < (R_DK // 2)
    row2 = lax.broadcasted_iota(jnp.int32, (L, 2 * L), 0)
    col2 = lax.broadcasted_iota(jnp.int32, (L, 2 * L), 1) % L
    causal2 = col2 <= row2
    lo_row = lax.broadcasted_iota(jnp.int32, (1, V7X_LANES), 1) < HEAD_W
    lo8 = lax.broadcasted_iota(jnp.int32, (8, V7X_LANES), 1) < HEAD_W
    hmat = hmat_ref[...]
    pmask = pmask_ref[...]

    def rope(x, cos_ref, sin_ref):
        cos = cos_ref[...]
        sin = sin_ref[...]
        parts = []
        for t in range(R_W // V7X_LANES):
            xs = x[:, t * V7X_LANES:(t + 1) * V7X_LANES]
            rot = jnp.where(first_half, pltpu.roll(xs, V7X_LANES - R_DK // 2, 1), pltpu.roll(xs, R_DK // 2, 1))
            parts.append(xs * cos + rot * sin)
        return jnp.concatenate(parts, axis=1)

    q_r = rope(proj_ref[:, C_QR:C_QR + R_W], cosq_ref, sinq_ref)
    k_r = rope(proj_ref[:, C_KR:C_KR + R_W], cosk_ref, sink_ref)
    q_rb = q_r.astype(BF16)
    k_rb = k_r.astype(BF16)
    kd_rb = (k_r * kdec_ref[...]).astype(BF16)
    v_rb = proj_ref[:, C_VR:C_VR + R_W].astype(BF16)
    o_parts = []
    for p in range(R_HEADS // 2):
        sl = slice(p * V7X_LANES, (p + 1) * V7X_LANES)
        qs, ks, vs = q_rb[:, sl], k_rb[:, sl], v_rb[:, sl]
        sc = _dot_nt(qs, _stack_masked(ks, lo, hi_m)) * rdec_ref[p]
        sp = srp_scr[p]
        o = _dot(sc.astype(BF16), _stack_masked(vs, lo, hi_m)) + _dot(qs, sp.astype(BF16)) * qdec_ref[:, sl]
        cd = jnp.where(lo_row, chunk_decay[2 * p], chunk_decay[2 * p + 1])
        srp_scr[p] = sp * cd + _dot_tn(kd_rb[:, sl], vs) * pmask
        o_parts.append(o)
    o_r = jnp.concatenate(o_parts, axis=1)
    gate_r = _silu(proj_ref[:, C_GR:C_GR + R_W])
    mix_ref[:, 0:R_W] = _norm_gate(o_r, hmat, gcat_ref[:, 0:R_W], gate_r)

    small = proj_ref[:, C_SM:C_SM + V7X_LANES]

    u = proj_ref[:, C_QKM:C_QKM + M_QK]
    conv_scr[8:8 + L, :] = u
    y = convb_ref[...]
    for j in range(CONV_W - 1):
        y = y + conv_scr[5 + j:5 + j + L, :] * convw_ref[j:j + 1, :]
    y = y + u * convw_ref[CONV_W - 1:CONV_W, :]
    tail = conv_scr[5 + L:8 + L, :]
    conv_scr[5:8, :] = tail
    qk = _silu(y)
    q_m = qk[:, :M_W]
    k_m = qk[:, M_W:] * (M_DK ** -0.5)
    q_mb = q_m.astype(BF16)
    k_mb = k_m.astype(BF16)
    v_mb = proj_ref[:, C_VM:C_VM + M_W].astype(BF16)
    gates = small + bsm_ref[...]
    f_cum = _dot3(tri_ref[...], _log_sigmoid(gates))
    i_rows = _dot3_nt(selr_ref[...], gates)
    f_rows = _dot3_nt(selr_ref[...], f_cum)
    head_lane = (lane >= SM_I) & (lane < SM_I + M_HEADS)
    fc = jnp.where(head_lane, pltpu.roll(f_cum, V7X_LANES - (SM_F - SM_I), 1), 0.0)
    m_prev = m_scr[0:1, :]
    a = fc + m_prev
    mx = jnp.full((L, V7X_LANES), -jnp.inf, F32)
    row = lax.broadcasted_iota(jnp.int32, (L, L), 0)
    col = lax.broadcasted_iota(jnp.int32, (L, L), 1)
    causal = col <= row
    dms = []
    for h in range(M_HEADS):
        dm = jnp.where(causal, (fc[:, SM_I + h:SM_I + h + 1] - f_rows[M_HEADS + h:M_HEADS + h + 1, :])
                       + i_rows[h:h + 1, :], -jnp.inf)
        dms.append(dm)
        mx = jnp.where(lane == SM_I + h, jnp.max(dm, axis=-1, keepdims=True), mx)
    m_tok = jnp.maximum(a, mx)
    w_inter = jnp.exp(a - m_tok)
    n_full = n_scr[0:1, :]
    hs, lows = _split2(q_m * n_full)
    qn_b = _dot(hs, hmat_ref[0:M_W, 0:M_W]) + _dot(lows, hmat_ref[0:M_W, 0:M_W])
    rs_all = jnp.zeros((L, V7X_LANES), F32)
    qn_all = jnp.zeros((L, V7X_LANES), F32)
    scs = []
    for p in range(M_HEADS // 2):
        sl = slice(p * V7X_LANES, (p + 1) * V7X_LANES)
        sc = _dot_nt(q_mb[:, sl], _stack_masked(k_mb[:, sl], lo, hi_m))
        mt_a = jnp.broadcast_to(m_tok[:, SM_I + 2 * p:SM_I + 2 * p + 1], (L, L))
        mt_b = jnp.broadcast_to(m_tok[:, SM_I + 2 * p + 1:SM_I + 2 * p + 2], (L, L))
        sc_a = sc[:, :L] * jnp.exp(dms[2 * p] - mt_a)
        sc_b = sc[:, L:] * jnp.exp(dms[2 * p + 1] - mt_b)
        rs_all = jnp.where(lane == SM_I + 2 * p, jnp.sum(sc_a, axis=-1, keepdims=True), rs_all)
        rs_all = jnp.where(lane == SM_I + 2 * p + 1, jnp.sum(sc_b, axis=-1, keepdims=True), rs_all)
        qn_all = jnp.where(lane == SM_I + 2 * p, qn_b[:, p * V7X_LANES:p * V7X_LANES + 1], qn_all)
        qn_all = jnp.where(lane == SM_I + 2 * p + 1, qn_b[:, p * V7X_LANES + HEAD_W:p * V7X_LANES + HEAD_W + 1], qn_all)
        scs.append(jnp.concatenate([sc_a, sc_b], axis=1).astype(BF16))
    den = rs_all + qn_all * w_inter
    inv = 1.0 / jnp.maximum(jnp.abs(den), jnp.exp(-m_tok))
    m_new = jnp.where(head_lane, m_tok, 0.0)[L - 1:L, :]
    wk = jnp.exp(fc[L - 1:L, :] - fc + gates - m_new)
    scale = jnp.broadcast_to(jnp.exp(a[L - 1:L, :] - m_new), (8, V7X_LANES))
    h_parts = []
    kw_parts = []
    scale_parts = []
    for p in range(M_HEADS // 2):
        sl = slice(p * V7X_LANES, (p + 1) * V7X_LANES)
        ia, ib = SM_I + 2 * p, SM_I + 2 * p + 1
        cpair = cp_scr[p]
        num = (_dot(scs[p], _stack_masked(v_mb[:, sl], lo, hi_m))
               + _dot(q_mb[:, sl], cpair.astype(BF16)) * _lane_pick(w_inter, ia, ib, lo))
        h_parts.append(num * _lane_pick(inv, ia, ib, lo))
        kw = k_m[:, sl] * _lane_pick(wk, ia, ib, lo)
        kw_parts.append(kw)
        sc_row = _lane_pick(scale, ia, ib, lo8)[0:1]
        scale_parts.append(sc_row)
        cp_scr[p] = cpair * sc_row + _dot_tn(kw.astype(BF16), v_mb[:, sl]) * pmask
    kw_all = jnp.concatenate(kw_parts, axis=1)
    n_new = n_full * jnp.concatenate(scale_parts, axis=1) + jnp.sum(kw_all, axis=0, keepdims=True)
    n_scr[...] = jnp.broadcast_to(n_new, n_scr.shape)
    m_scr[...] = jnp.broadcast_to(m_new, m_scr.shape)
    gate_m = _sigmoid(proj_ref[:, C_OM:C_OM + M_W])
    mix_ref[:, R_W + G_W:D_MIX] = _norm_gate(jnp.concatenate(h_parts, axis=1), hmat_ref[0:M_W, 0:M_W],
                                             gcat_ref[:, R_W + G_W:D_MIX], gate_m)

    z = _dot(small.astype(BF16), wga_ref[...]) + bga_ref[...]
    log_a = _log_sigmoid(z) / G_NORMALIZER
    b = _dot3(tri_ref[...], log_a)
    b_last = b[L - 1:L, :]
    safe = jnp.max(-b_last) <= GLA_SAFE_LOG_RANGE
    q_g = proj_ref[:, C_QG:C_QG + G_QK] * (G_DK ** -0.5)
    k_g = proj_ref[:, C_KG:C_KG + G_QK]
    v_gb = proj_ref[:, C_VG:C_VG + G_W].astype(BF16)

    slot = lax.broadcasted_iota(jnp.int32, (V7X_LANES, V7X_LANES), 1) // G_DK
    row_head = lax.broadcasted_iota(jnp.int32, (V7X_LANES, V7X_LANES), 0) // HEAD_W
    lane_q = lane // G_DK

    def pad2(x):
        return jnp.concatenate([x, jnp.zeros((x.shape[0], 2 * V7X_LANES - G_QK), x.dtype)], axis=1)

    def slab(x_p, g):
        s0 = (2 * g * G_DK) // V7X_LANES * V7X_LANES
        return x_p[:, s0:s0 + V7X_LANES], (2 * g * G_DK - s0) // G_DK

    def gla_inter(q_p):
        return jnp.concatenate([_dot_nt(slab(q_p, g)[0], sgt_scr[g].astype(BF16)) for g in range(G_HEADS // 2)],
                               axis=1)

    def gla_update(decay_p, k_p, v_b):
        for g in range(G_HEADS // 2):
            ks, h_a = slab(k_p, g)
            kv = _dot_tn(v_b[:, g * V7X_LANES:(g + 1) * V7X_LANES], ks)
            sgt_scr[g] = sgt_scr[g] * slab(decay_p, g)[0] + jnp.where(slot == h_a + row_head, kv, 0.0)

    @pl.when(safe)
    def _():
        qt_p = pad2((q_g * jnp.exp(b)).astype(BF16))
        kt_p = pad2((k_g * jnp.exp(-b)).astype(BF16))
        kl_p = pad2((k_g * jnp.exp(b_last - b)).astype(BF16))
        o_inter = gla_inter(qt_p)
        parts = []
        for g in range(G_HEADS // 2):
            qs, h_a = slab(qt_p, g)
            ks, _ = slab(kt_p, g)
            sc = _dot_nt(qs, _stack_masked(ks, lane_q == h_a, lane_q == h_a + 1))
            sc = jnp.where(causal2, sc, 0.0).astype(BF16)
            vs = v_gb[:, g * V7X_LANES:(g + 1) * V7X_LANES]
            parts.append(_dot(sc, _stack_masked(vs, lo, hi_m)))
        og_scr[...] = o_inter + jnp.concatenate(parts, axis=1)
        gla_update(pad2(jnp.exp(b_last)), kl_p, v_gb)

    @pl.when(jnp.logical_not(safe))
    def _():
        v_g = proj_ref[:, C_VG:C_VG + G_W]
        b_loc = _dot3(btri_ref[...], log_a)
        b_tot = _dot3(bones_ref[...], log_a)
        qt_p = pad2((q_g * jnp.exp(b_loc)).astype(BF16))
        kt_p = pad2((k_g * jnp.exp(b_tot - b_loc)).astype(BF16))
        d_tot_p = pad2(jnp.exp(b_tot))
        emat = emat_ref[...]
        sub_row = lax.broadcasted_iota(jnp.int32, (CS, G_QK), 0)
        for blk in range(L // CS):
            r0 = blk * CS
            rs = slice(r0, r0 + CS)
            o_blk = gla_inter(qt_p[rs])
            bI, qI, kI = b_loc[rs], q_g[rs], k_g[rs]
            terms = []
            for j in range(CS):
                e = jnp.exp(jnp.where(sub_row >= j, bI - bI[j:j + 1], -jnp.inf))
                terms.append(e * qI * kI[j:j + 1])
            t = jnp.concatenate(terms, axis=0).astype(BF16)
            w = _dot(t, emat)
            for j in range(CS):
                o_blk = o_blk + w[j * CS:(j + 1) * CS] * v_g[r0 + j:r0 + j + 1]
            og_scr[rs, :] = o_blk
            gla_update(d_tot_p[r0:r0 + 1], kt_p[rs], v_gb[rs])

    gate_g = _silu(proj_ref[:, C_RG:C_RG + G_W])
    mix_ref[:, R_W:R_W + G_W] = _norm_gate(og_scr[...], hmat, gcat_ref[:, R_W:R_W + G_W], gate_g)

    @pl.when(c == NC - 1)
    def _():
        eye_k = _eye(G_DK)
        for p in range(R_HEADS // 2):
            sr_out[2 * p] = srp_scr[p, 0:HEAD_W, 0:HEAD_W]
            sr_out[2 * p + 1] = srp_scr[p, HEAD_W:, HEAD_W:]
        for h in range(G_HEADS):
            g, hh = h // 2, h % 2
            c0 = (h * G_DK) % V7X_LANES
            sg_out[h] = _dot3_nt(eye_k, sgt_scr[g, hh * G_DV:(hh + 1) * G_DV, c0:c0 + G_DK])
        for p in range(M_HEADS // 2):
            c_out[2 * p] = cp_scr[p, 0:HEAD_W, 0:HEAD_W]
            c_out[2 * p + 1] = cp_scr[p, HEAD_W:, HEAD_W:]
        for h in range(M_HEADS):
            n_out[h:h + 1, :] = n_scr[0:1, h * M_DK:(h + 1) * M_DK]
        m_out[...] = m_scr[0:1, SM_I:SM_I + M_HEADS]
        conv_out[...] = conv_scr[5:8, :]


def _prompt_tables(T, L, CS):
    half = R_DK // 2
    inv = ROPE_BASE ** (-jnp.arange(half, dtype=F32) * 2.0 / R_DK)
    pos = jnp.arange(T, dtype=F32)
    ang = pos[:, None] * inv[None, :]
    cos = jnp.tile(jnp.cos(ang), (1, V7X_LANES // half))
    sin_h = jnp.sin(ang)
    sin = jnp.tile(jnp.concatenate([-sin_h, sin_h], axis=1), (1, V7X_LANES // R_DK))
    qs = R_DK ** -0.5

    log_gamma = jnp.log(1.0 - 2.0 ** (-5.0 - jnp.arange(R_HEADS, dtype=F32)))
    idx = jnp.arange(L, dtype=F32)
    rel = idx[:, None] - idx[None, :]
    causal = rel >= 0
    rdecay = jnp.where(causal[None], jnp.exp(log_gamma[:, None, None] * jnp.where(causal, rel, 0.0)[None]), 0.0)
    rdec = jnp.concatenate([rdecay[0::2], rdecay[1::2]], axis=2)
    qdec = jnp.repeat(jnp.exp(log_gamma[:, None] * (idx + 1.0)).T, R_DK, axis=1)
    kdec = jnp.repeat(jnp.exp(log_gamma[:, None] * (L - 1.0 - idx)).T, R_DK, axis=1)
    lg32 = np.log(1.0 - 2.0 ** (-5.0 - np.arange(R_HEADS, dtype=np.float64))).astype(np.float32)
    chunk_decay = tuple(float(np.exp(v * np.float32(L))) for v in lg32)

    r = np.arange(L)
    tri = (r[None, :] <= r[:, None])
    same = (r[None, :] // CS) == (r[:, None] // CS)
    selr = np.zeros((16, V7X_LANES), np.float32)
    for h in range(M_HEADS):
        selr[h, SM_I + h] = 1.0
        selr[M_HEADS + h, SM_F + h] = 1.0
    hv = np.arange(G_W) // G_DV
    hc = np.arange(G_QK) // G_DK
    gmask = (hv[:, None] == hc[None, :]).astype(np.float32)
    hmat = (hv[:, None] == hv[None, :]).astype(np.float32)
    pm = np.arange(V7X_LANES) // HEAD_W
    pmask = (pm[:, None] == pm[None, :]).astype(np.float32)
    return dict(
        cosq=cos * qs, sinq=sin * qs, cosk=cos, sink=sin, rdec=rdec, qdec=qdec, kdec=kdec, chunk_decay=chunk_decay,
        tri=jnp.asarray(tri, BF16), btri=jnp.asarray(tri & same, BF16), bones=jnp.asarray(same, BF16),
        selr=jnp.asarray(selr, BF16), hmat=jnp.asarray(hmat, BF16), emat=jnp.asarray(gmask.T, BF16),
        pmask=jnp.asarray(pmask, F32))


def mixer_prompt(proj, tabs, lw, l_out, depth, prev, *, L, CS):
    B, T, _ = proj.shape
    NC = T // L
    const2 = lambda b, c: (0, 0)
    const3 = lambda b, c: (0, 0, 0)
    tspec = pl.BlockSpec((L, V7X_LANES), lambda b, c: (c, 0))
    in_specs = [
        pl.BlockSpec((None, L, D_IN_PAD), lambda b, c: (b, c, 0)),
        tspec, tspec, tspec, tspec,
        pl.BlockSpec((R_HEADS // 2, L, 2 * L), const3),
        pl.BlockSpec((L, R_W), const2),
        pl.BlockSpec((L, R_W), const2),
        pl.BlockSpec((L, L), const2),
        pl.BlockSpec((L, L), const2),
        pl.BlockSpec((L, L), const2),
        pl.BlockSpec((16, V7X_LANES), const2),
        pl.BlockSpec((G_W, G_W), const2),
        pl.BlockSpec((G_QK, G_W), const2),
        pl.BlockSpec((V7X_LANES, V7X_LANES), const2),
        pl.BlockSpec((V7X_LANES, G_QK), const2),
        pl.BlockSpec((1, G_QK), const2),
        pl.BlockSpec((CONV_W, M_QK), const2),
        pl.BlockSpec((1, M_QK), const2),
        pl.BlockSpec((1, V7X_LANES), const2),
        pl.BlockSpec((1, D_MIX), const2),
    ]
    args = [proj, tabs["cosq"], tabs["sinq"], tabs["cosk"], tabs["sink"], tabs["rdec"], tabs["qdec"], tabs["kdec"],
            tabs["tri"], tabs["btri"], tabs["bones"], tabs["selr"], tabs["hmat"], tabs["emat"],
            tabs["pmask"], lw["wga"], lw["bga"], lw["convw"], lw["convb"], lw["bsm"], lw["gcat"]]
    n_in = len(args)
    in_specs += [pl.BlockSpec(memory_space=pl.ANY)] * 6
    args += list(prev)
    aliases = {n_in + i: 1 + i for i in range(6)}

    def st_spec(dims):
        return pl.BlockSpec((None, None) + dims, lambda b, c: (l_out, b) + (0,) * len(dims))

    out_shape = (jax.ShapeDtypeStruct((B, T, D_MIX), BF16),) + tuple(
        jax.ShapeDtypeStruct((depth, B) + d, F32) for d in _STATE_DIMS)
    out_specs = (pl.BlockSpec((None, L, D_MIX), lambda b, c: (b, c, 0)),) + tuple(st_spec(d) for d in _STATE_DIMS)
    scratch = [
        pltpu.VMEM((R_HEADS // 2, V7X_LANES, V7X_LANES), F32),
        pltpu.VMEM((G_HEADS // 2, V7X_LANES, V7X_LANES), F32),
        pltpu.VMEM((M_HEADS // 2, V7X_LANES, V7X_LANES), F32),
        pltpu.VMEM((8, M_W), F32),
        pltpu.VMEM((8, V7X_LANES), F32),
        pltpu.VMEM((8 + L, M_QK), F32),
        pltpu.VMEM((L, G_W), F32),
    ]
    kern = functools.partial(_mixer_prompt_kernel, L=L, CS=CS, NC=NC, chunk_decay=tabs["chunk_decay"])
    outs = pl.pallas_call(
        kern, out_shape=out_shape, grid=(B, NC), in_specs=in_specs, out_specs=out_specs,
        scratch_shapes=scratch, input_output_aliases=aliases,
        compiler_params=_cparams(("parallel", "arbitrary")), name="mixer_prompt",
    )(*args)
    return outs[0], tuple(outs[1:])


NS = 16
ROWS = 128


def _seq_bcast(x, t, T):
    n, w = x.shape
    x3 = x.reshape(n // T, T, w)
    return jnp.broadcast_to(x3[:, t:t + 1, :], (n // T, T, w)).reshape(n, w)


def _mixer_sample_kernel(proj_ref, cosq_ref, sinq_ref, cosk_ref, sink_ref, rdec_ref, qdec_ref, kdec_ref,
                         tri_ref, segones_ref, selr_ref, hmat_ref, emat_ref, mseg_ref, msegt_ref,
                         wga_ref, bga_ref, convw_ref, convb_ref, bsm_ref, gcat_ref,
                         sr_ref, sg_ref, c_ref, n_ref, m_ref, conv_ref, *rest, T, chunk_decay):
    (mix_ref, sr_out, sg_out, c_out, n_out, m_out, conv_out, conv_scr) = rest[-8:]
    L = ROWS
    ns = L // T
    lane = lax.broadcasted_iota(jnp.int32, (L, V7X_LANES), 1)
    lo = lane < HEAD_W
    hi_m = lane >= HEAD_W
    first_half = (lane % R_DK) < (R_DK // 2)
    row = lax.broadcasted_iota(jnp.int32, (L, L), 0)
    col = lax.broadcasted_iota(jnp.int32, (L, L), 1)
    segcausal = (row // T == col // T) & (col <= row)
    hmat = hmat_ref[...]
    mseg = mseg_ref[...]
    msegt = msegt_ref[...]

    def rope(x, cos_ref, sin_ref):
        cos = cos_ref[...]
        sin = sin_ref[...]
        parts = []
        for t in range(R_W // V7X_LANES):
            xs = x[:, t * V7X_LANES:(t + 1) * V7X_LANES]
            rot = jnp.where(first_half, pltpu.roll(xs, V7X_LANES - R_DK // 2, 1), pltpu.roll(xs, R_DK // 2, 1))
            parts.append(xs * cos + rot * sin)
        return jnp.concatenate(parts, axis=1)

    def tile_lanes(x, n):
        return jnp.concatenate([x] * n, axis=1)

    def tile_rows(x, n):
        return jnp.concatenate([x] * n, axis=0)

    def pair_state_terms(qs, ks_f32, vs, st_ref, p, hd):
        r = st_ref[:, 2 * p:2 * p + 2].reshape(ns * 2 * hd, hd).astype(BF16)
        kt = tile_rows(ks_f32.T.astype(BF16), ns) * msegt
        inter, kv = [], []
        for hh, m in ((0, lo), (1, hi_m)):
            qh = jnp.where(m, qs, jnp.zeros_like(qs))
            inter.append(_dot(tile_lanes(qh, ns) * mseg, r))
            kv.append(_dot(kt, vs[:, hh * hd:(hh + 1) * hd]))
        return inter, kv

    q_r = rope(proj_ref[:, C_QR:C_QR + R_W], cosq_ref, sinq_ref)
    k_r = rope(proj_ref[:, C_KR:C_KR + R_W], cosk_ref, sink_ref)
    q_rb = q_r.astype(BF16)
    k_rb = k_r.astype(BF16)
    kd_r = k_r * kdec_ref[...]
    v_rb = proj_ref[:, C_VR:C_VR + R_W].astype(BF16)
    o_parts = []
    for p in range(R_HEADS // 2):
        sl = slice(p * V7X_LANES, (p + 1) * V7X_LANES)
        qs, ks, vs = q_rb[:, sl], k_rb[:, sl], v_rb[:, sl]
        sc = _dot_nt(qs, _stack_masked(ks, lo, hi_m)) * rdec_ref[p]
        inter, kv = pair_state_terms(qs, kd_r[:, sl], vs, sr_ref, p, R_DK)
        o = (_dot(sc.astype(BF16), _stack_masked(vs, lo, hi_m))
             + jnp.concatenate(inter, axis=1) * qdec_ref[:, sl])
        o_parts.append(o)
        for hh in range(2):
            h = 2 * p + hh
            sr_out[:, h] = sr_ref[:, h] * chunk_decay[h] + kv[hh].reshape(ns, 2, R_DK, R_DV)[:, hh]
    gate_r = _silu(proj_ref[:, C_GR:C_GR + R_W])
    mix_ref[:, 0:R_W] = _norm_gate(jnp.concatenate(o_parts, axis=1), hmat, gcat_ref[:, 0:R_W], gate_r)

    small = proj_ref[:, C_SM:C_SM + V7X_LANES]
    z = _dot(small.astype(BF16), wga_ref[...]) + bga_ref[...]
    log_a = _log_sigmoid(z) / G_NORMALIZER
    b = _dot3(tri_ref[...], log_a)
    b_tot = _seq_bcast(b, T - 1, T)
    q_g = proj_ref[:, C_QG:C_QG + G_QK] * (G_DK ** -0.5)
    k_g = proj_ref[:, C_KG:C_KG + G_QK]
    v_g = proj_ref[:, C_VG:C_VG + G_W]
    v_gb = v_g.astype(BF16)
    tok = lax.broadcasted_iota(jnp.int32, (L, G_QK), 0) % T
    emat = emat_ref[...]
    o_g = jnp.zeros((L, G_W), F32)
    for j in range(T):
        e = jnp.exp(jnp.where(tok >= j, b - _seq_bcast(b, j, T), -jnp.inf))
        tj = (e * q_g * _seq_bcast(k_g, j, T)).astype(BF16)
        o_g = o_g + _dot(tj, emat) * _seq_bcast(v_g, j, T)
    qt = (q_g * jnp.exp(b)).astype(BF16)
    kl = k_g * jnp.exp(b_tot - b)
    dtot = jnp.exp(b_tot)
    lane_h = lane // G_DK
    inter_parts = [None] * G_HEADS
    for h0, heads in ((0, (0, 1, 2, 3)), (2, (4, 5))):
        c0 = h0 * G_DK
        q_s = qt[:, c0:c0 + V7X_LANES]
        r = sg_ref[:, h0:h0 + 4].reshape(ns * V7X_LANES, G_DV)
        rb = r.astype(BF16)
        kt = tile_rows(kl[:, c0:c0 + V7X_LANES].T.astype(BF16), ns) * msegt
        dt = dtot[:, c0:c0 + V7X_LANES].T
        dcols = []
        for s in range(ns):
            dcols.append(jnp.broadcast_to(dt[:, s * T:s * T + 1], (V7X_LANES, G_DV)))
        dfull = jnp.concatenate(dcols, axis=0).reshape(ns, 4, G_DK, G_DV)
        for h in heads:
            qh = jnp.where(lane_h == h - h0, q_s, jnp.zeros_like(q_s))
            inter_parts[h] = _dot(tile_lanes(qh, ns) * mseg, rb)
            kv = _dot(kt, v_gb[:, h * G_DV:(h + 1) * G_DV])
            sg_out[:, h] = sg_ref[:, h] * dfull[:, h - h0] + kv.reshape(ns, 4, G_DK, G_DV)[:, h - h0]
    o_g = o_g + jnp.concatenate(inter_parts, axis=1)
    gate_g = _silu(proj_ref[:, C_RG:C_RG + G_W])
    mix_ref[:, R_W:R_W + G_W] = _norm_gate(o_g, hmat, gcat_ref[:, R_W:R_W + G_W], gate_g)

    u = proj_ref[:, C_QKM:C_QKM + M_QK]
    conv_scr[:, 5:8, :] = conv_ref[...]
    conv_scr[:, 8:8 + T, :] = u.reshape(ns, T, M_QK)
    y = convb_ref[...]
    for j in range(CONV_W - 1):
        y = y + conv_scr[:, 5 + j:5 + j + T, :].reshape(L, M_QK) * convw_ref[j:j + 1, :]
    y = y + u * convw_ref[CONV_W - 1:CONV_W, :]
    conv_out[...] = conv_scr[:, 5 + T:8 + T, :]
    qk = _silu(y)
    q_m = qk[:, :M_W]
    k_m = qk[:, M_W:] * (M_DK ** -0.5)
    q_mb = q_m.astype(BF16)
    k_mb = k_m.astype(BF16)
    v_mb = proj_ref[:, C_VM:C_VM + M_W].astype(BF16)
    gates = small + bsm_ref[...]
    f_cum = _dot3(tri_ref[...], _log_sigmoid(gates))
    i_rows = _dot3_nt(selr_ref[...], gates)
    f_rows = _dot3_nt(selr_ref[...], f_cum)
    head_lane = (lane >= SM_I) & (lane < SM_I + M_HEADS)
    fc = jnp.where(head_lane, pltpu.roll(f_cum, V7X_LANES - (SM_F - SM_I), 1), 0.0)
    m_prev = m_ref[...]
    a = fc + m_prev
    mx = jnp.full((L, V7X_LANES), -jnp.inf, F32)
    dms = []
    for h in range(M_HEADS):
        dm = jnp.where(segcausal, (fc[:, SM_I + h:SM_I + h + 1] - f_rows[M_HEADS + h:M_HEADS + h + 1, :])
                       + i_rows[h:h + 1, :], -jnp.inf)
        dms.append(dm)
        mx = jnp.where(lane == SM_I + h, jnp.max(dm, axis=-1, keepdims=True), mx)
    m_tok = jnp.maximum(a, mx)
    w_inter = jnp.exp(a - m_tok)
    n_rows = n_ref[...]
    hs, lows = _split2(q_m * n_rows)
    qn_b = _dot(hs, hmat_ref[0:M_W, 0:M_W]) + _dot(lows, hmat_ref[0:M_W, 0:M_W])
    rs_all = jnp.zeros((L, V7X_LANES), F32)
    qn_all = jnp.zeros((L, V7X_LANES), F32)
    scs = []
    for p in range(M_HEADS // 2):
        sl = slice(p * V7X_LANES, (p + 1) * V7X_LANES)
        sc = _dot_nt(q_mb[:, sl], _stack_masked(k_mb[:, sl], lo, hi_m))
        mt_a = jnp.broadcast_to(m_tok[:, SM_I + 2 * p:SM_I + 2 * p + 1], (L, L))
        mt_b = jnp.broadcast_to(m_tok[:, SM_I + 2 * p + 1:SM_I + 2 * p + 2], (L, L))
        sc_a = sc[:, :L] * jnp.exp(dms[2 * p] - mt_a)
        sc_b = sc[:, L:] * jnp.exp(dms[2 * p + 1] - mt_b)
        rs_all = jnp.where(lane == SM_I + 2 * p, jnp.sum(sc_a, axis=-1, keepdims=True), rs_all)
        rs_all = jnp.where(lane == SM_I + 2 * p + 1, jnp.sum(sc_b, axis=-1, keepdims=True), rs_all)
        qn_all = jnp.where(lane == SM_I + 2 * p, qn_b[:, p * V7X_LANES:p * V7X_LANES + 1], qn_all)
        qn_all = jnp.where(lane == SM_I + 2 * p + 1, qn_b[:, p * V7X_LANES + HEAD_W:p * V7X_LANES + HEAD_W + 1], qn_all)
        scs.append(jnp.concatenate([sc_a, sc_b], axis=1).astype(BF16))
    den = rs_all + qn_all * w_inter
    inv = 1.0 / jnp.maximum(jnp.abs(den), jnp.exp(-m_tok))
    m_new = _seq_bcast(jnp.where(head_lane, m_tok, 0.0), T - 1, T)
    wk = jnp.exp(_seq_bcast(fc, T - 1, T) - fc + gates - m_new)
    scale = jnp.exp(_seq_bcast(a, T - 1, T) - m_new)
    h_parts = []
    kw_parts = []
    scale_parts = []
    for p in range(M_HEADS // 2):
        sl = slice(p * V7X_LANES, (p + 1) * V7X_LANES)
        ia, ib = SM_I + 2 * p, SM_I + 2 * p + 1
        kw = k_m[:, sl] * _lane_pick(wk, ia, ib, lo)
        kw_parts.append(kw)
        inter, kv = pair_state_terms(q_mb[:, sl], kw, v_mb[:, sl], c_ref, p, M_DK)
        num = (_dot(scs[p], _stack_masked(v_mb[:, sl], lo, hi_m))
               + jnp.concatenate(inter, axis=1) * _lane_pick(w_inter, ia, ib, lo))
        h_parts.append(num * _lane_pick(inv, ia, ib, lo))
        scale_parts.append(_lane_pick(scale, ia, ib, lo))
        for hh in range(2):
            h = 2 * p + hh
            sc_rows = jnp.broadcast_to(scale[:, SM_I + h:SM_I + h + 1], (L, M_DV)).reshape(ns, T, M_DV)
            sc_h = jnp.broadcast_to(sc_rows[:, 0:1, :], (ns, M_DK, M_DV))
            c_out[:, h] = c_ref[:, h] * sc_h + kv[hh].reshape(ns, 2, M_DK, M_DV)[:, hh]
    kw_all = jnp.concatenate(kw_parts, axis=1)
    n_out[...] = n_rows * jnp.concatenate(scale_parts, axis=1) + _dot3(segones_ref[...], kw_all)
    m_out[...] = m_new
    gate_m = _sigmoid(proj_ref[:, C_OM:C_OM + M_W])
    mix_ref[:, R_W + G_W:D_MIX] = _norm_gate(jnp.concatenate(h_parts, axis=1), hmat_ref[0:M_W, 0:M_W],
                                             gcat_ref[:, R_W + G_W:D_MIX], gate_m)


def _sample_tables(T, pos0):
    L = ROWS
    half = R_DK // 2
    inv = ROPE_BASE ** (-jnp.arange(half, dtype=F32) * 2.0 / R_DK)
    tok = np.arange(L) % T
    seq = np.arange(L) // T
    pos = pos0 + jnp.asarray(tok, F32)
    ang = pos[:, None] * inv[None, :]
    cos = jnp.tile(jnp.cos(ang), (1, V7X_LANES // half))
    sin_h = jnp.sin(ang)
    sin = jnp.tile(jnp.concatenate([-sin_h, sin_h], axis=1), (1, V7X_LANES // R_DK))
    qs = R_DK ** -0.5

    log_gamma = jnp.log(1.0 - 2.0 ** (-5.0 - jnp.arange(R_HEADS, dtype=F32)))
    tf = jnp.asarray(tok, F32)
    rel = tf[:, None] - tf[None, :]
    ok_np = (seq[:, None] == seq[None, :]) & (tok[None, :] <= tok[:, None])
    ok = jnp.asarray(ok_np)
    rdecay = jnp.where(ok[None], jnp.exp(log_gamma[:, None, None] * jnp.where(ok, rel, 0.0)[None]), 0.0)
    rdec = jnp.concatenate([rdecay[0::2], rdecay[1::2]], axis=2)
    qdec = jnp.repeat(jnp.exp(log_gamma[:, None] * (tf + 1.0)).T, R_DK, axis=1)
    kdec = jnp.repeat(jnp.exp(log_gamma[:, None] * (T - 1.0 - tf)).T, R_DK, axis=1)
    lg32 = np.log(1.0 - 2.0 ** (-5.0 - np.arange(R_HEADS, dtype=np.float64))).astype(np.float32)
    chunk_decay = tuple(float(np.exp(v * np.float32(T))) for v in lg32)

    selr = np.zeros((16, V7X_LANES), np.float32)
    for h in range(M_HEADS):
        selr[h, SM_I + h] = 1.0
        selr[M_HEADS + h, SM_F + h] = 1.0
    hv = np.arange(G_W) // G_DV
    hc = np.arange(G_QK) // G_DK
    gmask = (hv[:, None] == hc[None, :]).astype(np.float32)
    hmat = (hv[:, None] == hv[None, :]).astype(np.float32)
    ns = L // T
    mseg = (seq[:, None] == (np.arange(ns * V7X_LANES) // V7X_LANES)[None, :]).astype(np.float32)
    return dict(
        cosq=cos * qs, sinq=sin * qs, cosk=cos, sink=sin, rdec=rdec, qdec=qdec, kdec=kdec, chunk_decay=chunk_decay,
        tri=jnp.asarray(ok_np, BF16), segones=jnp.asarray(seq[:, None] == seq[None, :], BF16),
        selr=jnp.asarray(selr, BF16), hmat=jnp.asarray(hmat, BF16),
        emat=jnp.asarray(gmask.T, BF16), mseg=jnp.asarray(mseg, BF16), msegt=jnp.asarray(mseg.T, BF16))


_SAMPLE_STATE_DIMS = ((R_HEADS, R_DK, R_DV), (G_HEADS, G_DK, G_DV), (M_HEADS, M_DK, M_DV), (M_W,),
                      (V7X_LANES,), (CONV_W - 1, M_QK))


def mixer_sample(proj, tabs, lw, state, l, depth, prev, *, T):
    M = proj.shape[0]
    B = M // T
    assert B % NS == 0 and NS * T == ROWS
    L = ROWS
    const2 = lambda b: (0, 0)
    const3 = lambda b: (0, 0, 0)

    def st_spec(dims):
        return pl.BlockSpec((None, NS) + dims, lambda b: (l, b) + (0,) * len(dims))

    def row_spec(w):
        return pl.BlockSpec((None, L, w), lambda b: (l, b, 0))

    st_specs = ([st_spec(d) for d in _SAMPLE_STATE_DIMS[:3]]
                + [row_spec(M_W), row_spec(V7X_LANES), st_spec(_SAMPLE_STATE_DIMS[5])])
    in_specs = [
        pl.BlockSpec((L, D_IN_PAD), lambda b: (b, 0)),
        pl.BlockSpec((L, V7X_LANES), const2), pl.BlockSpec((L, V7X_LANES), const2),
        pl.BlockSpec((L, V7X_LANES), const2), pl.BlockSpec((L, V7X_LANES), const2),
        pl.BlockSpec((R_HEADS // 2, L, 2 * L), const3),
        pl.BlockSpec((L, R_W), const2),
        pl.BlockSpec((L, R_W), const2),
        pl.BlockSpec((L, L), const2),
        pl.BlockSpec((L, L), const2),
        pl.BlockSpec((16, V7X_LANES), const2),
        pl.BlockSpec((G_W, G_W), const2),
        pl.BlockSpec((G_QK, G_W), const2),
        pl.BlockSpec((L, NS * V7X_LANES), const2),
        pl.BlockSpec((NS * V7X_LANES, L), const2),
        pl.BlockSpec((V7X_LANES, G_QK), const2),
        pl.BlockSpec((1, G_QK), const2),
        pl.BlockSpec((CONV_W, M_QK), const2),
        pl.BlockSpec((1, M_QK), const2),
        pl.BlockSpec((1, V7X_LANES), const2),
        pl.BlockSpec((1, D_MIX), const2),
    ] + st_specs
    args = [proj, tabs["cosq"], tabs["sinq"], tabs["cosk"], tabs["sink"], tabs["rdec"], tabs["qdec"], tabs["kdec"],
            tabs["tri"], tabs["segones"], tabs["selr"], tabs["hmat"], tabs["emat"], tabs["mseg"], tabs["msegt"],
            lw["wga"], lw["bga"], lw["convw"], lw["convb"], lw["bsm"], lw["gcat"]] + list(state)
    n_in = len(args)
    in_specs += [pl.BlockSpec(memory_space=pl.ANY)] * 6
    args += list(prev)
    aliases = {n_in + i: 1 + i for i in range(6)}
    shapes = [(depth, B) + d for d in _SAMPLE_STATE_DIMS]
    shapes[3], shapes[4] = (depth, M, M_W), (depth, M, V7X_LANES)
    out_shape = (jax.ShapeDtypeStruct((M, D_MIX), BF16),) + tuple(jax.ShapeDtypeStruct(sh, F32) for sh in shapes)
    out_specs = (pl.BlockSpec((L, D_MIX), lambda b: (b, 0)),) + tuple(st_specs)
    kern = functools.partial(_mixer_sample_kernel, T=T, chunk_decay=tabs["chunk_decay"])
    outs = pl.pallas_call(
        kern, out_shape=out_shape, grid=(B // NS,), in_specs=in_specs, out_specs=out_specs,
        scratch_shapes=[pltpu.VMEM((NS, 8 + T, M_QK), F32)],
        input_output_aliases=aliases,
        compiler_params=_cparams(("parallel",)), name="mixer_sample",
    )(*args)
    return outs[0], tuple(outs[1:])


def _prep_weights(w_in, w_out, w_xq, w_xk, w_xv, w_xo, w_gate, w_up, w_down):
    a0 = 2 * R_HEADS * R_DK + 2 * R_W + 2 * G_QK + 2 * G_W
    m0 = a0 + G_RANK
    g0 = m0 + M_QK + 2 * M_W
    pad = jnp.zeros(w_in.shape[:2] + (D_IN_PAD - C_SM - G_RANK - 2 * M_HEADS,), F32)
    w_pad = jnp.concatenate(
        [w_in[..., :a0], w_in[..., m0:g0], w_in[..., g0:g0 + 2 * M_HEADS], w_in[..., a0:m0], pad], axis=2)
    cast = lambda w: w.astype(BF16)
    return dict(w_in=cast(w_pad), w_out=cast(w_out), w_xq=cast(w_xq), w_xk=cast(w_xk), w_xv=cast(w_xv),
                w_xo=cast(w_xo), w_gate=cast(w_gate), w_up=cast(w_up), w_down=cast(w_down))


def _prep_layer(l, g_mix, w_ga2, b_ga, conv_w, conv_b, b_i, b_f, g_ret, g_gla, g_mlstm, g_xattn, g_mem, g_ffn):
    wga = jnp.zeros((V7X_LANES, G_QK), F32).at[SM_AG:SM_AG + G_RANK].set(w_ga2[l]).astype(BF16)
    bsm = (jnp.zeros((1, V7X_LANES), F32).at[0, SM_I:SM_I + M_HEADS].set(b_i[l])
           .at[0, SM_F:SM_F + M_HEADS].set(b_f[l]))
    return dict(
        g_mix=g_mix[l], wga=wga, bga=b_ga[l].reshape(1, G_QK), convw=conv_w[l],
        convb=conv_b[l].reshape(1, M_QK), bsm=bsm,
        gcat=jnp.concatenate([g_ret[l], g_gla[l], g_mlstm[l]]).reshape(1, D_MIX),
        g_xattn=g_xattn[l], g_mem=g_mem[l], g_ffn=g_ffn[l])


def _layer(x, mix_fn, attend, W, lw, l, g_final, *, tm, final_norm):
    B, T, D = x.shape
    M = B * T
    x2 = x.reshape(M, D)
    proj = rms_matmul(x2, lw["g_mix"], W["w_in"], l, tm=tm)
    mix, new_state = mix_fn(proj)
    x2 = attend(mix, x2)
    x2 = swiglu_res(x2, lw["g_ffn"], W["w_gate"], W["w_up"], W["w_down"], l, g_final,
                    tm=tm, final_norm=final_norm)
    return x2.reshape(B, T, D), new_state


def kernel(x_prompt, x_sample, state_ret, state_gla, state_mlstm_C, state_mlstm_n, state_mlstm_m, state_mlstm_conv, cache_mem_k, cache_mem_v, mem_prompt, g_mix, w_in, w_ga2, b_ga, conv_w, conv_b, b_i, b_f, g_ret, g_gla, g_mlstm, w_out, g_xattn, g_mem, w_xq, w_xk, w_xv, w_xo, g_ffn, w_gate, w_up, w_down, g_final):
    B, T, D = x_prompt.shape
    Bs, Ts, _ = x_sample.shape
    depth = w_in.shape[0]
    assert T % CHUNK == 0 and Ts * NS == ROWS and Bs % NS == 0
    tabs_p = _prompt_tables(T, CHUNK, 16)
    tabs_s = _sample_tables(Ts, float(PAST_LEN))
    tm_p = 512 if (B * T) % 512 == 0 else B * T
    tm_s = 512 if (Bs * Ts) % 512 == 0 else Bs * Ts
    tq_p = 512 if T % 512 == 0 else T
    nb_x = 8 if Bs % 8 == 0 else 1

    sample_state = (state_ret, state_gla, state_mlstm_C,
                    jnp.repeat(state_mlstm_n.reshape(depth, Bs, M_W), Ts, axis=1),
                    jnp.repeat(jnp.pad(state_mlstm_m, ((0, 0), (0, 0), (0, V7X_LANES - M_HEADS))), Ts, axis=1),
                    state_mlstm_conv)
    s_shapes = [(depth, Bs) + d for d in _SAMPLE_STATE_DIMS]
    s_shapes[3], s_shapes[4] = (depth, Bs * Ts, M_W), (depth, Bs * Ts, V7X_LANES)
    s_st = tuple(jnp.zeros(sh, F32) for sh in s_shapes)
    p_st = tuple(jnp.zeros((depth, B) + d, F32) for d in _STATE_DIMS)
    p_mem = tuple(jnp.zeros((depth, B, N_MEM, X_HEADS, X_HD), F32) for _ in range(2))
    hp, hs = x_prompt, x_sample
    W = _prep_weights(w_in, w_out, w_xq, w_xk, w_xv, w_xo, w_gate, w_up, w_down)
    for l in range(depth):
        lw = _prep_layer(l, g_mix, w_ga2, b_ga, conv_w, conv_b, b_i, b_f, g_ret, g_gla, g_mlstm,
                         g_xattn, g_mem, g_ffn)
        last = l == depth - 1
        k5, v5, kb, vb = memory_kv(mem_prompt, lw["g_mem"], W["w_xk"], W["w_xv"], l, depth, p_mem)
        p_mem = (k5, v5)

        def mix_p(proj):
            mix, st = mixer_prompt(proj.reshape(B, T, D_IN_PAD), tabs_p, lw, l, depth, p_st, L=CHUNK, CS=16)
            return mix.reshape(B * T, D_MIX), st

        def attend_p(mix, x):
            y = post_mix(mix.reshape(B, T, D_MIX), x.reshape(B, T, D), kb, vb,
                         W["w_out"], lw["g_xattn"], W["w_xq"], W["w_xo"], l, tq=tq_p)
            return y.reshape(B * T, D)

        hp, p_st = _layer(hp, mix_p, attend_p, W, lw, l, g_final, tm=tm_p, final_norm=last)
        hs, s_st = _layer(hs, lambda proj: mixer_sample(proj, tabs_s, lw, sample_state, l, depth, s_st, T=Ts),
                          lambda mix, x: post_mix_cache(mix, x, cache_mem_k, cache_mem_v, W["w_out"],
                                                        lw["g_xattn"], W["w_xq"], W["w_xo"], l, nb=nb_x, T=Ts),
                          W, lw, l, g_final, tm=tm_s, final_norm=last)

    p_out = p_st[:4] + (p_st[4].reshape(depth, B, M_HEADS), p_st[5])
    s_out = s_st[:3] + (s_st[3][:, ::Ts].reshape(depth, Bs, M_HEADS, M_DK), s_st[4][:, ::Ts, :M_HEADS], s_st[5])
    return (hp, hs, *p_out, *p_mem, *s_out)
```

```python
import functools
import math

import numpy as np
import jax
import jax.numpy as jnp
from jax import lax
from jax.experimental import pallas as pl
from jax.experimental.pallas import tpu as pltpu

F32 = jnp.float32
BF16 = jnp.bfloat16

D_MODEL = 1024
PAST_LEN = 16384
R_HEADS, R_DK, R_DV = 6, 64, 64
G_HEADS, G_DK, G_DV, G_RANK = 6, 32, 64, 16
G_NORMALIZER = 16.0
M_HEADS, M_DK, M_DV = 4, 64, 64
CONV_W = 4
X_HEADS = 4
X_HD = D_MODEL // X_HEADS
N_MEM = 256
D_FF = int(math.ceil(8 * D_MODEL / 3 / 256)) * 256
CHUNK = 128
EPS = 1e-6
ROPE_BASE = 10000.0

R_W = R_HEADS * R_DV
G_QK = G_HEADS * G_DK
G_W = G_HEADS * G_DV
M_QK = 2 * M_HEADS * M_DK
M_W = M_HEADS * M_DV
D_MIX = R_W + G_W + M_W

C_QR, C_KR, C_VR, C_GR = 0, 384, 768, 1152
C_QG, C_KG, C_VG, C_RG = 1536, 1728, 1920, 2304
C_QKM, C_VM, C_OM, C_SM = 2688, 3200, 3456, 3712
D_IN_PAD = 3840
SM_I, SM_F, SM_AG = 0, 4, 8
HEAD_W = 64
GLA_SAFE_LOG_RANGE = 60.0

V7X_LANES = 128
VMEM_LIMIT = 56 * 1024 * 1024


def _cparams(sem):
    return pltpu.CompilerParams(dimension_semantics=sem, vmem_limit_bytes=VMEM_LIMIT)


def _sigmoid(x):
    return 1.0 / (1.0 + jnp.exp(-x))


def _silu(x):
    return x * _sigmoid(x)


def _log_sigmoid(x):
    return jnp.minimum(x, 0.0) - jnp.log(1.0 + jnp.exp(-jnp.abs(x)))


def _dot(a, b):
    return jnp.dot(a, b, preferred_element_type=F32)


def _dot_nt(a, b):
    return lax.dot_general(a, b, (((1,), (1,)), ((), ())), preferred_element_type=F32)


def _dot_tn(a, b):
    return lax.dot_general(a, b, (((0,), (0,)), ((), ())), preferred_element_type=F32)


def _split3(x):
    hi = x.astype(BF16)
    r1 = x - hi.astype(F32)
    mid = r1.astype(BF16)
    lo = (r1 - mid.astype(F32)).astype(BF16)
    return hi, mid, lo


def _dot3(a, x):
    hi, mid, lo = _split3(x)
    return _dot(a, hi) + _dot(a, mid) + _dot(a, lo)


def _dot3_nt(a, x):
    hi, mid, lo = _split3(x)
    return _dot_nt(a, hi) + _dot_nt(a, mid) + _dot_nt(a, lo)


def _rms(x, g):
    return x * lax.rsqrt(jnp.mean(x * x, axis=-1, keepdims=True) + EPS) * g


def _wspec(w, l):
    return pl.BlockSpec((None,) + w.shape[1:], lambda *_: (l, 0, 0))


def _rms_matmul_kernel(x_ref, g_ref, w_ref, o_ref):
    o_ref[...] = _dot(_rms(x_ref[...], g_ref[...]).astype(BF16), w_ref[...]).astype(o_ref.dtype)


def rms_matmul(x, g, w, l, *, tm, out_dtype=F32):
    M, D = x.shape
    N = w.shape[2]
    assert M % tm == 0
    return pl.pallas_call(
        _rms_matmul_kernel,
        out_shape=jax.ShapeDtypeStruct((M, N), out_dtype),
        grid=(M // tm,),
        in_specs=[pl.BlockSpec((tm, D), lambda i: (i, 0)),
                  pl.BlockSpec((1, D), lambda i: (0, 0)),
                  _wspec(w, l)],
        out_specs=pl.BlockSpec((tm, N), lambda i: (i, 0)),
        compiler_params=_cparams(("parallel",)),
        name="rms_matmul",
    )(x, g.reshape(1, D), w)


def _swiglu_kernel(x_ref, g_ref, wg_ref, wu_ref, wd_ref, gf_ref, o_ref, *, final_norm):
    x = x_ref[...]
    xn = _rms(x, g_ref[...]).astype(BF16)
    h = _silu(_dot(xn, wg_ref[...])) * _dot(xn, wu_ref[...])
    y = x + _dot(h.astype(BF16), wd_ref[...])
    if final_norm:
        y = _rms(y, gf_ref[...])
    o_ref[...] = y


def swiglu_res(x, g, wg, wu, wd, l, g_final, *, tm, final_norm):
    M, D = x.shape
    assert M % tm == 0
    row = pl.BlockSpec((tm, D), lambda i: (i, 0))
    vec = pl.BlockSpec((1, D), lambda i: (0, 0))
    return pl.pallas_call(
        functools.partial(_swiglu_kernel, final_norm=final_norm),
        out_shape=jax.ShapeDtypeStruct((M, D), F32),
        grid=(M // tm,),
        in_specs=[row, vec, _wspec(wg, l), _wspec(wu, l), _wspec(wd, l), vec],
        out_specs=row,
        compiler_params=_cparams(("parallel",)),
        name="swiglu_res",
    )(x, g.reshape(1, D), wg, wu, wd, g_final.reshape(1, D))


def _memkv_kernel(x_ref, g_ref, wk_ref, wv_ref, *refs, slot, depth):
    k5_ref, v5_ref, kb_ref, vb_ref = refs[-4:]
    if slot is not None:
        for e in range(depth):
            if e != slot:
                k5_ref[e] = jnp.zeros(k5_ref.shape[1:], F32)
                v5_ref[e] = jnp.zeros(v5_ref.shape[1:], F32)
        k5_ref, v5_ref = k5_ref.at[slot], v5_ref.at[slot]
    xn = _rms(x_ref[...], g_ref[...]).astype(BF16)
    for w_ref, o5_ref, ob_ref in ((wk_ref, k5_ref, kb_ref), (wv_ref, v5_ref, vb_ref)):
        y = _dot(xn, w_ref[...])
        ob_ref[...] = y.astype(BF16)
        for h in range(X_HEADS):
            o5_ref[:, h, :] = y[:, h * X_HD:(h + 1) * X_HD]


def memory_kv(mem, g, wk, wv, l, depth, prev):
    B, _, D = mem.shape
    in_specs = [pl.BlockSpec((None, N_MEM, D), lambda b: (b, 0, 0)),
                pl.BlockSpec((1, D), lambda b: (0, 0)),
                _wspec(wk, l), _wspec(wv, l)]
    args = [mem, g.reshape(1, D), wk, wv]
    aliases = {}
    if prev is None:
        spec5 = pl.BlockSpec((depth, None, N_MEM, X_HEADS, X_HD), lambda b: (0, b, 0, 0, 0))
    else:
        spec5 = pl.BlockSpec((None, None, N_MEM, X_HEADS, X_HD), lambda b: (l, b, 0, 0, 0))
        in_specs += [pl.BlockSpec(memory_space=pl.ANY)] * 2
        args += list(prev)
        aliases = {4: 0, 5: 1}
    o5 = jax.ShapeDtypeStruct((depth, B, N_MEM, X_HEADS, X_HD), F32)
    ob = jax.ShapeDtypeStruct((B, N_MEM, D), BF16)
    specb = pl.BlockSpec((None, N_MEM, D), lambda b: (b, 0, 0))
    return pl.pallas_call(
        functools.partial(_memkv_kernel, slot=l if prev is None else None, depth=depth),
        out_shape=(o5, o5, ob, ob),
        grid=(B,),
        in_specs=in_specs,
        out_specs=(spec5, spec5, specb, specb),
        input_output_aliases=aliases,
        compiler_params=_cparams(("parallel",)),
        name="memory_kv",
    )(*args)


def _out_and_query(a_ref, x_ref, wo_ref, g_ref, wq_ref):
    x1 = x_ref[...] + _dot(a_ref[...], wo_ref[...])
    q = _dot(_rms(x1, g_ref[...]).astype(BF16), wq_ref[...]).astype(BF16)
    return x1, q


def _post_mix_kernel(a_ref, x_ref, k_ref, v_ref, wo_ref, g_ref, wq_ref, wxo_ref, o_ref):
    x1, q = _out_and_query(a_ref, x_ref, wo_ref, g_ref, wq_ref)
    scale = X_HD ** -0.5
    parts = []
    for h in range(X_HEADS):
        sl = slice(h * X_HD, (h + 1) * X_HD)
        s = _dot_nt(q[:, sl], k_ref[:, sl]) * scale
        p = jnp.exp(s - jnp.max(s, axis=-1, keepdims=True))
        l = jnp.sum(p, axis=-1, keepdims=True)
        parts.append((_dot(p.astype(BF16), v_ref[:, sl]) / l).astype(BF16))
    o_ref[...] = x1 + _dot(jnp.concatenate(parts, axis=1), wxo_ref[...])


def post_mix(a, x, mk, mv, wo, g, wq, wxo, l, *, tq):
    B, T, D = x.shape
    assert T % tq == 0
    tok = lambda w: pl.BlockSpec((None, tq, w), lambda b, i: (b, i, 0))
    mem = pl.BlockSpec((None, N_MEM, D), lambda b, i: (b, 0, 0))
    return pl.pallas_call(
        _post_mix_kernel,
        out_shape=jax.ShapeDtypeStruct((B, T, D), F32),
        grid=(B, T // tq),
        in_specs=[tok(a.shape[2]), tok(D), mem, mem, _wspec(wo, l), pl.BlockSpec((1, D), lambda b, i: (0, 0)),
                  _wspec(wq, l), _wspec(wxo, l)],
        out_specs=tok(D),
        compiler_params=_cparams(("parallel", "arbitrary")),
        name="post_mix",
    )(a, x, mk, mv, wo, g.reshape(1, D), wq, wxo)


def _post_mix_cache_kernel(a_ref, x_ref, k_ref, v_ref, wo_ref, g_ref, wq_ref, wxo_ref, o_ref, *, nb, T):
    x1, q_all = _out_and_query(a_ref, x_ref, wo_ref, g_ref, wq_ref)
    R = X_HEADS * T
    rowh = lax.broadcasted_iota(jnp.int32, (R, N_MEM * X_HEADS), 0) // T
    colh = lax.broadcasted_iota(jnp.int32, (R, N_MEM * X_HEADS), 1) % X_HEADS
    own = rowh == colh
    outs = []
    for s in range(nb):
        q = q_all[s * T:(s + 1) * T]
        qf = jnp.concatenate([q[:, h * X_HD:(h + 1) * X_HD] for h in range(X_HEADS)], axis=0)
        kf = k_ref[s].reshape(N_MEM * X_HEADS, X_HD).astype(BF16)
        vf = v_ref[s].reshape(N_MEM * X_HEADS, X_HD).astype(BF16)
        sc = jnp.where(own, _dot_nt(qf, kf) * (X_HD ** -0.5), -jnp.inf)
        p = jnp.exp(sc - jnp.max(sc, axis=-1, keepdims=True))
        l = jnp.sum(p, axis=-1, keepdims=True)
        o = (_dot(p.astype(BF16), vf) / l).astype(BF16)
        outs.append(jnp.concatenate([o[h * T:(h + 1) * T] for h in range(X_HEADS)], axis=1))
    o_ref[...] = x1 + _dot(jnp.concatenate(outs, axis=0), wxo_ref[...])


def post_mix_cache(a, x, ck, cv, wo, g, wq, wxo, l, *, nb, T):
    M, D = x.shape
    rows = nb * T
    assert M % rows == 0
    cspec = pl.BlockSpec((None, nb, N_MEM, X_HEADS, X_HD), lambda b: (l, b, 0, 0, 0))
    tok = lambda w: pl.BlockSpec((rows, w), lambda b: (b, 0))
    return pl.pallas_call(
        functools.partial(_post_mix_cache_kernel, nb=nb, T=T),
        out_shape=jax.ShapeDtypeStruct((M, D), F32),
        grid=(M // rows,),
        in_specs=[tok(a.shape[1]), tok(D), cspec, cspec, _wspec(wo, l), pl.BlockSpec((1, D), lambda b: (0, 0)),
                  _wspec(wq, l), _wspec(wxo, l)],
        out_specs=tok(D),
        compiler_params=_cparams(("parallel",)),
        name="post_mix_cache",
    )(a, x, ck, cv, wo, g.reshape(1, D), wq, wxo)


def _eye(n):
    r = lax.broadcasted_iota(jnp.int32, (n, n), 0)
    c = lax.broadcasted_iota(jnp.int32, (n, n), 1)
    return jnp.where(r == c, 1.0, 0.0).astype(BF16)


_STATE_DIMS = ((R_HEADS, R_DK, R_DV), (G_HEADS, G_DK, G_DV), (M_HEADS, M_DK, M_DV), (M_HEADS, M_DK),
               (1, M_HEADS), (CONV_W - 1, M_QK))


def _split2(x):
    hi = x.astype(BF16)
    lo = (x - hi.astype(F32)).astype(BF16)
    return hi, lo


def _head_mean_sq(o, hmat):
    hi, lo = _split2(o * o)
    return (_dot(hi, hmat) + _dot(lo, hmat)) * (1.0 / HEAD_W)


def _norm_gate(o, hmat, g_row, gate):
    return (o * lax.rsqrt(_head_mean_sq(o, hmat) + EPS) * g_row * gate).astype(BF16)


def _lane_pick(cols, idx_lo, idx_hi, lo_mask):
    L = cols.shape[0]
    a = jnp.broadcast_to(cols[:, idx_lo:idx_lo + 1], (L, V7X_LANES))
    b = jnp.broadcast_to(cols[:, idx_hi:idx_hi + 1], (L, V7X_LANES))
    return jnp.where(lo_mask, a, b)


def _stack_masked(x, m_a, m_b):
    z = jnp.zeros_like(x)
    return jnp.concatenate([jnp.where(m_a, x, z), jnp.where(m_b, x, z)], axis=0)


def _mixer_prompt_kernel(proj_ref, cosq_ref, sinq_ref, cosk_ref, sink_ref, rdec_ref, qdec_ref, kdec_ref,
                         tri_ref, btri_ref, bones_ref, selr_ref, hmat_ref, emat_ref, pmask_ref,
                         wga_ref, bga_ref, convw_ref, convb_ref, bsm_ref, gcat_ref, *rest,
                         L, CS, NC, chunk_decay):
    (mix_ref, sr_out, sg_out, c_out, n_out, m_out, conv_out,
     srp_scr, sgt_scr, cp_scr, n_scr, m_scr, conv_scr, og_scr) = rest[-14:]
    c = pl.program_id(1)

    @pl.when(c == 0)
    def _():
        srp_scr[...] = jnp.zeros_like(srp_scr)
        sgt_scr[...] = jnp.zeros_like(sgt_scr)
        cp_scr[...] = jnp.zeros_like(cp_scr)
        n_scr[...] = jnp.zeros_like(n_scr)
        m_scr[...] = jnp.zeros_like(m_scr)
        conv_scr[0:8, :] = jnp.zeros((8, M_QK), F32)

    lane = lax.broadcasted_iota(jnp.int32, (L, V7X_LANES), 1)
    lo = lane < HEAD_W
    hi_m = lane >= HEAD_W
    first_half = (lane % R_DK) < (R_DK // 2)
    row2 = lax.broadcasted_iota(jnp.int32, (L, 2 * L), 0)
    col2 = lax.broadcasted_iota(jnp.int32, (L, 2 * L), 1) % L
    causal2 = col2 <= row2
    lo_row = lax.broadcasted_iota(jnp.int32, (1, V7X_LANES), 1) < HEAD_W
    lo8 = lax.broadcasted_iota(jnp.int32, (8, V7X_LANES), 1) < HEAD_W
    hmat = hmat_ref[...]
    pmask = pmask_ref[...]

    def rope(x, cos_ref, sin_ref):
        cos = cos_ref[...]
        sin = sin_ref[...]
        parts = []
        for t in range(R_W // V7X_LANES):
            xs = x[:, t * V7X_LANES:(t + 1) * V7X_LANES]
            rot = jnp.where(first_half, pltpu.roll(xs, V7X_LANES - R_DK // 2, 1), pltpu.roll(xs, R_DK // 2, 1))
            parts.append(xs * cos + rot * sin)
        return jnp.concatenate(parts, axis=1)

    q_r = rope(proj_ref[:, C_QR:C_QR + R_W], cosq_ref, sinq_ref)
    k_r = rope(proj_ref[:, C_KR:C_KR + R_W], cosk_ref, sink_ref)
    q_rb = q_r.astype(BF16)
    k_rb = k_r.astype(BF16)
    kd_rb = (k_r * kdec_ref[...]).astype(BF16)
    v_rb = proj_ref[:, C_VR:C_VR + R_W].astype(BF16)
    o_parts = []
    for p in range(R_HEADS // 2):
        sl = slice(p * V7X_LANES, (p + 1) * V7X_LANES)
        qs, ks, vs = q_rb[:, sl], k_rb[:, sl], v_rb[:, sl]
        sc = _dot_nt(qs, _stack_masked(ks, lo, hi_m)) * rdec_ref[p]
        sp = srp_scr[p]
        o = _dot(sc.astype(BF16), _stack_masked(vs, lo, hi_m)) + _dot(qs, sp.astype(BF16)) * qdec_ref[:, sl]
        cd = jnp.where(lo_row, chunk_decay[2 * p], chunk_decay[2 * p + 1])
        srp_scr[p] = sp * cd + _dot_tn(kd_rb[:, sl], vs) * pmask
        o_parts.append(o)
    o_r = jnp.concatenate(o_parts, axis=1)
    gate_r = _silu(proj_ref[:, C_GR:C_GR + R_W])
    mix_ref[:, 0:R_W] = _norm_gate(o_r, hmat, gcat_ref[:, 0:R_W], gate_r)

    small = proj_ref[:, C_SM:C_SM + V7X_LANES]

    u = proj_ref[:, C_QKM:C_QKM + M_QK]
    conv_scr[8:8 + L, :] = u
    y = convb_ref[...]
    for j in range(CONV_W - 1):
        y = y + conv_scr[5 + j:5 + j + L, :] * convw_ref[j:j + 1, :]
    y = y + u * convw_ref[CONV_W - 1:CONV_W, :]
    tail = conv_scr[5 + L:8 + L, :]
    conv_scr[5:8, :] = tail
    qk = _silu(y)
    q_m = qk[:, :M_W]
    k_m = qk[:, M_W:] * (M_DK ** -0.5)
    q_mb = q_m.astype(BF16)
    k_mb = k_m.astype(BF16)
    v_mb = proj_ref[:, C_VM:C_VM + M_W].astype(BF16)
    gates = small + bsm_ref[...]
    f_cum = _dot3(tri_ref[...], _log_sigmoid(gates))
    i_rows = _dot3_nt(selr_ref[...], gates)
    f_rows = _dot3_nt(selr_ref[...], f_cum)
    head_lane = (lane >= SM_I) & (lane < SM_I + M_HEADS)
    fc = jnp.where(head_lane, pltpu.roll(f_cum, V7X_LANES - (SM_F - SM_I), 1), 0.0)
    m_prev = m_scr[0:1, :]
    a = fc + m_prev
    mx = jnp.full((L, V7X_LANES), -jnp.inf, F32)
    row = lax.broadcasted_iota(jnp.int32, (L, L), 0)
    col = lax.broadcasted_iota(jnp.int32, (L, L), 1)
    causal = col <= row
    dms = []
    for h in range(M_HEADS):
        dm = jnp.where(causal, (fc[:, SM_I + h:SM_I + h + 1] - f_rows[M_HEADS + h:M_HEADS + h + 1, :])
                       + i_rows[h:h + 1, :], -jnp.inf)
        dms.append(dm)
        mx = jnp.where(lane == SM_I + h, jnp.max(dm, axis=-1, keepdims=True), mx)
    m_tok = jnp.maximum(a, mx)
    w_inter = jnp.exp(a - m_tok)
    n_full = n_scr[0:1, :]
    hs, lows = _split2(q_m * n_full)
    qn_b = _dot(hs, hmat_ref[0:M_W, 0:M_W]) + _dot(lows, hmat_ref[0:M_W, 0:M_W])
    rs_all = jnp.zeros((L, V7X_LANES), F32)
    qn_all = jnp.zeros((L, V7X_LANES), F32)
    scs = []
    for p in range(M_HEADS // 2):
        sl = slice(p * V7X_LANES, (p + 1) * V7X_LANES)
        sc = _dot_nt(q_mb[:, sl], _stack_masked(k_mb[:, sl], lo, hi_m))
        mt_a = jnp.broadcast_to(m_tok[:, SM_I + 2 * p:SM_I + 2 * p + 1], (L, L))
        mt_b = jnp.broadcast_to(m_tok[:, SM_I + 2 * p + 1:SM_I + 2 * p + 2], (L, L))
        sc_a = sc[:, :L] * jnp.exp(dms[2 * p] - mt_a)
        sc_b = sc[:, L:] * jnp.exp(dms[2 * p + 1] - mt_b)
        rs_all = jnp.where(lane == SM_I + 2 * p, jnp.sum(sc_a, axis=-1, keepdims=True), rs_all)
        rs_all = jnp.where(lane == SM_I + 2 * p + 1, jnp.sum(sc_b, axis=-1, keepdims=True), rs_all)
        qn_all = jnp.where(lane == SM_I + 2 * p, qn_b[:, p * V7X_LANES:p * V7X_LANES + 1], qn_all)
        qn_all = jnp.where(lane == SM_I + 2 * p + 1, qn_b[:, p * V7X_LANES + HEAD_W:p * V7X_LANES + HEAD_W + 1], qn_all)
        scs.append(jnp.concatenate([sc_a, sc_b], axis=1).astype(BF16))
    den = rs_all + qn_all * w_inter
    inv = 1.0 / jnp.maximum(jnp.abs(den), jnp.exp(-m_tok))
    m_new = jnp.where(head_lane, m_tok, 0.0)[L - 1:L, :]
    wk = jnp.exp(fc[L - 1:L, :] - fc + gates - m_new)
    scale = jnp.broadcast_to(jnp.exp(a[L - 1:L, :] - m_new), (8, V7X_LANES))
    h_parts = []
    kw_parts = []
    scale_parts = []
    for p in range(M_HEADS // 2):
        sl = slice(p * V7X_LANES, (p + 1) * V7X_LANES)
        ia, ib = SM_I + 2 * p, SM_I + 2 * p + 1
        cpair = cp_scr[p]
        num = (_dot(scs[p], _stack_masked(v_mb[:, sl], lo, hi_m))
               + _dot(q_mb[:, sl], cpair.astype(BF16)) * _lane_pick(w_inter, ia, ib, lo))
        h_parts.append(num * _lane_pick(inv, ia, ib, lo))
        kw = k_m[:, sl] * _lane_pick(wk, ia, ib, lo)
        kw_parts.append(kw)
        sc_row = _lane_pick(scale, ia, ib, lo8)[0:1]
        scale_parts.append(sc_row)
        cp_scr[p] = cpair * sc_row + _dot_tn(kw.astype(BF16), v_mb[:, sl]) * pmask
    kw_all = jnp.concatenate(kw_parts, axis=1)
    n_new = n_full * jnp.concatenate(scale_parts, axis=1) + jnp.sum(kw_all, axis=0, keepdims=True)
    n_scr[...] = jnp.broadcast_to(n_new, n_scr.shape)
    m_scr[...] = jnp.broadcast_to(m_new, m_scr.shape)
    gate_m = _sigmoid(proj_ref[:, C_OM:C_OM + M_W])
    mix_ref[:, R_W + G_W:D_MIX] = _norm_gate(jnp.concatenate(h_parts, axis=1), hmat_ref[0:M_W, 0:M_W],
                                             gcat_ref[:, R_W + G_W:D_MIX], gate_m)

    z = _dot(small.astype(BF16), wga_ref[...]) + bga_ref[...]
    log_a = _log_sigmoid(z) / G_NORMALIZER
    b = _dot3(tri_ref[...], log_a)
    b_last = b[L - 1:L, :]
    safe = jnp.max(-b_last) <= GLA_SAFE_LOG_RANGE
    q_g = proj_ref[:, C_QG:C_QG + G_QK] * (G_DK ** -0.5)
    k_g = proj_ref[:, C_KG:C_KG + G_QK]
    v_gb = proj_ref[:, C_VG:C_VG + G_W].astype(BF16)

    slot = lax.broadcasted_iota(jnp.int32, (V7X_LANES, V7X_LANES), 1) // G_DK
    row_head = lax.broadcasted_iota(jnp.int32, (V7X_LANES, V7X_LANES), 0) // HEAD_W
    lane_q = lane // G_DK

    def pad2(x):
        return jnp.concatenate([x, jnp.zeros((x.shape[0], 2 * V7X_LANES - G_QK), x.dtype)], axis=1)

    def slab(x_p, g):
        s0 = (2 * g * G_DK) // V7X_LANES * V7X_LANES
        return x_p[:, s0:s0 + V7X_LANES], (2 * g * G_DK - s0) // G_DK

    def gla_inter(q_p):
        return jnp.concatenate([_dot_nt(slab(q_p, g)[0], sgt_scr[g].astype(BF16)) for g in range(G_HEADS // 2)],
                               axis=1)

    def gla_update(decay_p, k_p, v_b):
        for g in range(G_HEADS // 2):
            ks, h_a = slab(k_p, g)
            kv = _dot_tn(v_b[:, g * V7X_LANES:(g + 1) * V7X_LANES], ks)
            sgt_scr[g] = sgt_scr[g] * slab(decay_p, g)[0] + jnp.where(slot == h_a + row_head, kv, 0.0)

    @pl.when(safe)
    def _():
        qt_p = pad2((q_g * jnp.exp(b)).astype(BF16))
        kt_p = pad2((k_g * jnp.exp(-b)).astype(BF16))
        kl_p = pad2((k_g * jnp.exp(b_last - b)).astype(BF16))
        o_inter = gla_inter(qt_p)
        parts = []
        for g in range(G_HEADS // 2):
            qs, h_a = slab(qt_p, g)
            ks, _ = slab(kt_p, g)
            sc = _dot_nt(qs, _stack_masked(ks, lane_q == h_a, lane_q == h_a + 1))
            sc = jnp.where(causal2, sc, 0.0).astype(BF16)
            vs = v_gb[:, g * V7X_LANES:(g + 1) * V7X_LANES]
            parts.append(_dot(sc, _stack_masked(vs, lo, hi_m)))
        og_scr[...] = o_inter + jnp.concatenate(parts, axis=1)
        gla_update(pad2(jnp.exp(b_last)), kl_p, v_gb)

    @pl.when(jnp.logical_not(safe))
    def _():
        v_g = proj_ref[:, C_VG:C_VG + G_W]
        b_loc = _dot3(btri_ref[...], log_a)
        b_tot = _dot3(bones_ref[...], log_a)
        qt_p = pad2((q_g * jnp.exp(b_loc)).astype(BF16))
        kt_p = pad2((k_g * jnp.exp(b_tot - b_loc)).astype(BF16))
        d_tot_p = pad2(jnp.exp(b_tot))
        emat = emat_ref[...]
        sub_row = lax.broadcasted_iota(jnp.int32, (CS, G_QK), 0)
        for blk in range(L // CS):
            r0 = blk * CS
            rs = slice(r0, r0 + CS)
            o_blk = gla_inter(qt_p[rs])
            bI, qI, kI = b_loc[rs], q_g[rs], k_g[rs]
            terms = []
            for j in range(CS):
                e = jnp.exp(jnp.where(sub_row >= j, bI - bI[j:j + 1], -jnp.inf))
                terms.append(e * qI * kI[j:j + 1])
            t = jnp.concatenate(terms, axis=0).astype(BF16)
            w = _dot(t, emat)
            for j in range(CS):
                o_blk = o_blk + w[j * CS:(j + 1) * CS] * v_g[r0 + j:r0 + j + 1]
            og_scr[rs, :] = o_blk
            gla_update(d_tot_p[r0:r0 + 1], kt_p[rs], v_gb[rs])

    gate_g = _silu(proj_ref[:, C_RG:C_RG + G_W])
    mix_ref[:, R_W:R_W + G_W] = _norm_gate(og_scr[...], hmat, gcat_ref[:, R_W:R_W + G_W], gate_g)

    @pl.when(c == NC - 1)
    def _():
        eye_k = _eye(G_DK)
        for p in range(R_HEADS // 2):
            sr_out[2 * p] = srp_scr[p, 0:HEAD_W, 0:HEAD_W]
            sr_out[2 * p + 1] = srp_scr[p, HEAD_W:, HEAD_W:]
        for h in range(G_HEADS):
            g, hh = h // 2, h % 2
            c0 = (h * G_DK) % V7X_LANES
            sg_out[h] = _dot3_nt(eye_k, sgt_scr[g, hh * G_DV:(hh + 1) * G_DV, c0:c0 + G_DK])
        for p in range(M_HEADS // 2):
            c_out[2 * p] = cp_scr[p, 0:HEAD_W, 0:HEAD_W]
            c_out[2 * p + 1] = cp_scr[p, HEAD_W:, HEAD_W:]
        for h in range(M_HEADS):
            n_out[h:h + 1, :] = n_scr[0:1, h * M_DK:(h + 1) * M_DK]
        m_out[...] = m_scr[0:1, SM_I:SM_I + M_HEADS]
        conv_out[...] = conv_scr[5:8, :]


def _prompt_tables(T, L, CS):
    half = R_DK // 2
    inv = ROPE_BASE ** (-jnp.arange(half, dtype=F32) * 2.0 / R_DK)
    pos = jnp.arange(T, dtype=F32)
    ang = pos[:, None] * inv[None, :]
    cos = jnp.tile(jnp.cos(ang), (1, V7X_LANES // half))
    sin_h = jnp.sin(ang)
    sin = jnp.tile(jnp.concatenate([-sin_h, sin_h], axis=1), (1, V7X_LANES // R_DK))
    qs = R_DK ** -0.5

    log_gamma = jnp.log(1.0 - 2.0 ** (-5.0 - jnp.arange(R_HEADS, dtype=F32)))
    idx = jnp.arange(L, dtype=F32)
    rel = idx[:, None] - idx[None, :]
    causal = rel >= 0
    rdecay = jnp.where(causal[None], jnp.exp(log_gamma[:, None, None] * jnp.where(causal, rel, 0.0)[None]), 0.0)
    rdec = jnp.concatenate([rdecay[0::2], rdecay[1::2]], axis=2)
    qdec = jnp.repeat(jnp.exp(log_gamma[:, None] * (idx + 1.0)).T, R_DK, axis=1)
    kdec = jnp.repeat(jnp.exp(log_gamma[:, None] * (L - 1.0 - idx)).T, R_DK, axis=1)
    lg32 = np.log(1.0 - 2.0 ** (-5.0 - np.arange(R_HEADS, dtype=np.float64))).astype(np.float32)
    chunk_decay = tuple(float(np.exp(v * np.float32(L))) for v in lg32)

    r = np.arange(L)
    tri = (r[None, :] <= r[:, None])
    same = (r[None, :] // CS) == (r[:, None] // CS)
    selr = np.zeros((16, V7X_LANES), np.float32)
    for h in range(M_HEADS):
        selr[h, SM_I + h] = 1.0
        selr[M_HEADS + h, SM_F + h] = 1.0
    hv = np.arange(G_W) // G_DV
    hc = np.arange(G_QK) // G_DK
    gmask = (hv[:, None] == hc[None, :]).astype(np.float32)
    hmat = (hv[:, None] == hv[None, :]).astype(np.float32)
    pm = np.arange(V7X_LANES) // HEAD_W
    pmask = (pm[:, None] == pm[None, :]).astype(np.float32)
    return dict(
        cosq=cos * qs, sinq=sin * qs, cosk=cos, sink=sin, rdec=rdec, qdec=qdec, kdec=kdec, chunk_decay=chunk_decay,
        tri=jnp.asarray(tri, BF16), btri=jnp.asarray(tri & same, BF16), bones=jnp.asarray(same, BF16),
        selr=jnp.asarray(selr, BF16), hmat=jnp.asarray(hmat, BF16), emat=jnp.asarray(gmask.T, BF16),
        pmask=jnp.asarray(pmask, F32))


def mixer_prompt(proj, tabs, lw, l_out, depth, prev, *, L, CS):
    B, T, _ = proj.shape
    NC = T // L
    const2 = lambda b, c: (0, 0)
    const3 = lambda b, c: (0, 0, 0)
    tspec = pl.BlockSpec((L, V7X_LANES), lambda b, c: (c, 0))
    in_specs = [
        pl.BlockSpec((None, L, D_IN_PAD), lambda b, c: (b, c, 0)),
        tspec, tspec, tspec, tspec,
        pl.BlockSpec((R_HEADS // 2, L, 2 * L), const3),
        pl.BlockSpec((L, R_W), const2),
        pl.BlockSpec((L, R_W), const2),
        pl.BlockSpec((L, L), const2),
        pl.BlockSpec((L, L), const2),
        pl.BlockSpec((L, L), const2),
        pl.BlockSpec((16, V7X_LANES), const2),
        pl.BlockSpec((G_W, G_W), const2),
        pl.BlockSpec((G_QK, G_W), const2),
        pl.BlockSpec((V7X_LANES, V7X_LANES), const2),
        pl.BlockSpec((V7X_LANES, G_QK), const2),
        pl.BlockSpec((1, G_QK), const2),
        pl.BlockSpec((CONV_W, M_QK), const2),
        pl.BlockSpec((1, M_QK), const2),
        pl.BlockSpec((1, V7X_LANES), const2),
        pl.BlockSpec((1, D_MIX), const2),
    ]
    args = [proj, tabs["cosq"], tabs["sinq"], tabs["cosk"], tabs["sink"], tabs["rdec"], tabs["qdec"], tabs["kdec"],
            tabs["tri"], tabs["btri"], tabs["bones"], tabs["selr"], tabs["hmat"], tabs["emat"],
            tabs["pmask"], lw["wga"], lw["bga"], lw["convw"], lw["convb"], lw["bsm"], lw["gcat"]]
    n_in = len(args)
    in_specs += [pl.BlockSpec(memory_space=pl.ANY)] * 6
    args += list(prev)
    aliases = {n_in + i: 1 + i for i in range(6)}

    def st_spec(dims):
        return pl.BlockSpec((None, None) + dims, lambda b, c: (l_out, b) + (0,) * len(dims))

    out_shape = (jax.ShapeDtypeStruct((B, T, D_MIX), BF16),) + tuple(
        jax.ShapeDtypeStruct((depth, B) + d, F32) for d in _STATE_DIMS)
    out_specs = (pl.BlockSpec((None, L, D_MIX), lambda b, c: (b, c, 0)),) + tuple(st_spec(d) for d in _STATE_DIMS)
    scratch = [
        pltpu.VMEM((R_HEADS // 2, V7X_LANES, V7X_LANES), F32),
        pltpu.VMEM((G_HEADS // 2, V7X_LANES, V7X_LANES), F32),
        pltpu.VMEM((M_HEADS // 2, V7X_LANES, V7X_LANES), F32),
        pltpu.VMEM((8, M_W), F32),
        pltpu.VMEM((8, V7X_LANES), F32),
        pltpu.VMEM((8 + L, M_QK), F32),
        pltpu.VMEM((L, G_W), F32),
    ]
    kern = functools.partial(_mixer_prompt_kernel, L=L, CS=CS, NC=NC, chunk_decay=tabs["chunk_decay"])
    outs = pl.pallas_call(
        kern, out_shape=out_shape, grid=(B, NC), in_specs=in_specs, out_specs=out_specs,
        scratch_shapes=scratch, input_output_aliases=aliases,
        compiler_params=_cparams(("parallel", "arbitrary")), name="mixer_prompt",
    )(*args)
    return outs[0], tuple(outs[1:])


NS = 16
ROWS = 128


def _seq_bcast(x, t, T):
    n, w = x.shape
    x3 = x.reshape(n // T, T, w)
    return jnp.broadcast_to(x3[:, t:t + 1, :], (n // T, T, w)).reshape(n, w)


def _mixer_sample_kernel(proj_ref, cosq_ref, sinq_ref, cosk_ref, sink_ref, rdec_ref, qdec_ref, kdec_ref,
                         tri_ref, segones_ref, selr_ref, hmat_ref, emat_ref, mseg_ref, msegt_ref,
                         wga_ref, bga_ref, convw_ref, convb_ref, bsm_ref, gcat_ref,
                         sr_ref, sg_ref, c_ref, n_ref, m_ref, conv_ref, *rest, T, chunk_decay, slot, depth):
    mix_ref = rest[-8]
    st_outs = rest[-7:-1]
    conv_scr = rest[-1]
    if slot is not None:
        for e in range(depth):
            if e != slot:
                for dst in st_outs:
                    dst[e] = jnp.zeros(dst.shape[1:], F32)
        st_outs = [o.at[slot] for o in st_outs]
    sr_out, sg_out, c_out, n_out, m_out, conv_out = st_outs
    L = ROWS
    ns = L // T
    lane = lax.broadcasted_iota(jnp.int32, (L, V7X_LANES), 1)
    lo = lane < HEAD_W
    hi_m = lane >= HEAD_W
    first_half = (lane % R_DK) < (R_DK // 2)
    row = lax.broadcasted_iota(jnp.int32, (L, L), 0)
    col = lax.broadcasted_iota(jnp.int32, (L, L), 1)
    segcausal = (row // T == col // T) & (col <= row)
    hmat = hmat_ref[...]
    mseg = mseg_ref[...]
    msegt = msegt_ref[...]

    def rope(x, cos_ref, sin_ref):
        cos = cos_ref[...]
        sin = sin_ref[...]
        parts = []
        for t in range(R_W // V7X_LANES):
            xs = x[:, t * V7X_LANES:(t + 1) * V7X_LANES]
            rot = jnp.where(first_half, pltpu.roll(xs, V7X_LANES - R_DK // 2, 1), pltpu.roll(xs, R_DK // 2, 1))
            parts.append(xs * cos + rot * sin)
        return jnp.concatenate(parts, axis=1)

    def tile_lanes(x, n):
        return jnp.concatenate([x] * n, axis=1)

    def tile_rows(x, n):
        return jnp.concatenate([x] * n, axis=0)

    def pair_state_terms(qs, ks_f32, vs, st_ref, p, hd):
        r = st_ref[:, 2 * p:2 * p + 2].reshape(ns * 2 * hd, hd).astype(BF16)
        kt = tile_rows(ks_f32.T.astype(BF16), ns) * msegt
        inter, kv = [], []
        for hh, m in ((0, lo), (1, hi_m)):
            qh = jnp.where(m, qs, jnp.zeros_like(qs))
            inter.append(_dot(tile_lanes(qh, ns) * mseg, r))
            kv.append(_dot(kt, vs[:, hh * hd:(hh + 1) * hd]))
        return inter, kv

    q_r = rope(proj_ref[:, C_QR:C_QR + R_W], cosq_ref, sinq_ref)
    k_r = rope(proj_ref[:, C_KR:C_KR + R_W], cosk_ref, sink_ref)
    q_rb = q_r.astype(BF16)
    k_rb = k_r.astype(BF16)
    kd_r = k_r * kdec_ref[...]
    v_rb = proj_ref[:, C_VR:C_VR + R_W].astype(BF16)
    o_parts = []
    for p in range(R_HEADS // 2):
        sl = slice(p * V7X_LANES, (p + 1) * V7X_LANES)
        qs, ks, vs = q_rb[:, sl], k_rb[:, sl], v_rb[:, sl]
        sc = _dot_nt(qs, _stack_masked(ks, lo, hi_m)) * rdec_ref[p]
        inter, kv = pair_state_terms(qs, kd_r[:, sl], vs, sr_ref, p, R_DK)
        o = (_dot(sc.astype(BF16), _stack_masked(vs, lo, hi_m))
             + jnp.concatenate(inter, axis=1) * qdec_ref[:, sl])
        o_parts.append(o)
        for hh in range(2):
            h = 2 * p + hh
            sr_out[:, h] = sr_ref[:, h] * chunk_decay[h] + kv[hh].reshape(ns, 2, R_DK, R_DV)[:, hh]
    gate_r = _silu(proj_ref[:, C_GR:C_GR + R_W])
    mix_ref[:, 0:R_W] = _norm_gate(jnp.concatenate(o_parts, axis=1), hmat, gcat_ref[:, 0:R_W], gate_r)

    small = proj_ref[:, C_SM:C_SM + V7X_LANES]
    z = _dot(small.astype(BF16), wga_ref[...]) + bga_ref[...]
    log_a = _log_sigmoid(z) / G_NORMALIZER
    b = _dot3(tri_ref[...], log_a)
    b_tot = _seq_bcast(b, T - 1, T)
    q_g = proj_ref[:, C_QG:C_QG + G_QK] * (G_DK ** -0.5)
    k_g = proj_ref[:, C_KG:C_KG + G_QK]
    v_g = proj_ref[:, C_VG:C_VG + G_W]
    v_gb = v_g.astype(BF16)
    tok = lax.broadcasted_iota(jnp.int32, (L, G_QK), 0) % T
    emat = emat_ref[...]
    o_g = jnp.zeros((L, G_W), F32)
    for j in range(T):
        e = jnp.exp(jnp.where(tok >= j, b - _seq_bcast(b, j, T), -jnp.inf))
        tj = (e * q_g * _seq_bcast(k_g, j, T)).astype(BF16)
        o_g = o_g + _dot(tj, emat) * _seq_bcast(v_g, j, T)
    qt = (q_g * jnp.exp(b)).astype(BF16)
    kl = k_g * jnp.exp(b_tot - b)
    dtot = jnp.exp(b_tot)
    lane_h = lane // G_DK
    inter_parts = [None] * G_HEADS
    for h0, heads in ((0, (0, 1, 2, 3)), (2, (4, 5))):
        c0 = h0 * G_DK
        q_s = qt[:, c0:c0 + V7X_LANES]
        r = sg_ref[:, h0:h0 + 4].reshape(ns * V7X_LANES, G_DV)
        rb = r.astype(BF16)
        kt = tile_rows(kl[:, c0:c0 + V7X_LANES].T.astype(BF16), ns) * msegt
        dt = dtot[:, c0:c0 + V7X_LANES].T
        dcols = []
        for s in range(ns):
            dcols.append(jnp.broadcast_to(dt[:, s * T:s * T + 1], (V7X_LANES, G_DV)))
        dfull = jnp.concatenate(dcols, axis=0).reshape(ns, 4, G_DK, G_DV)
        for h in heads:
            qh = jnp.where(lane_h == h - h0, q_s, jnp.zeros_like(q_s))
            inter_parts[h] = _dot(tile_lanes(qh, ns) * mseg, rb)
            kv = _dot(kt, v_gb[:, h * G_DV:(h + 1) * G_DV])
            sg_out[:, h] = sg_ref[:, h] * dfull[:, h - h0] + kv.reshape(ns, 4, G_DK, G_DV)[:, h - h0]
    o_g = o_g + jnp.concatenate(inter_parts, axis=1)
    gate_g = _silu(proj_ref[:, C_RG:C_RG + G_W])
    mix_ref[:, R_W:R_W + G_W] = _norm_gate(o_g, hmat, gcat_ref[:, R_W:R_W + G_W], gate_g)

    u = proj_ref[:, C_QKM:C_QKM + M_QK]
    conv_scr[:, 5:8, :] = conv_ref[...]
    conv_scr[:, 8:8 + T, :] = u.reshape(ns, T, M_QK)
    y = convb_ref[...]
    for j in range(CONV_W - 1):
        y = y + conv_scr[:, 5 + j:5 + j + T, :].reshape(L, M_QK) * convw_ref[j:j + 1, :]
    y = y + u * convw_ref[CONV_W - 1:CONV_W, :]
    conv_out[...] = conv_scr[:, 5 + T:8 + T, :]
    qk = _silu(y)
    q_m = qk[:, :M_W]
    k_m = qk[:, M_W:] * (M_DK ** -0.5)
    q_mb = q_m.astype(BF16)
    k_mb = k_m.astype(BF16)
    v_mb = proj_ref[:, C_VM:C_VM + M_W].astype(BF16)
    gates = small + bsm_ref[...]
    f_cum = _dot3(tri_ref[...], _log_sigmoid(gates))
    i_rows = _dot3_nt(selr_ref[...], gates)
    f_rows = _dot3_nt(selr_ref[...], f_cum)
    head_lane = (lane >= SM_I) & (lane < SM_I + M_HEADS)
    fc = jnp.where(head_lane, pltpu.roll(f_cum, V7X_LANES - (SM_F - SM_I), 1), 0.0)
    m_prev = m_ref[...]
    a = fc + m_prev
    mx = jnp.full((L, V7X_LANES), -jnp.inf, F32)
    dms = []
    for h in range(M_HEADS):
        dm = jnp.where(segcausal, (fc[:, SM_I + h:SM_I + h + 1] - f_rows[M_HEADS + h:M_HEADS + h + 1, :])
                       + i_rows[h:h + 1, :], -jnp.inf)
        dms.append(dm)
        mx = jnp.where(lane == SM_I + h, jnp.max(dm, axis=-1, keepdims=True), mx)
    m_tok = jnp.maximum(a, mx)
    w_inter = jnp.exp(a - m_tok)
    n_rows = n_ref[...]
    hs, lows = _split2(q_m * n_rows)
    qn_b = _dot(hs, hmat_ref[0:M_W, 0:M_W]) + _dot(lows, hmat_ref[0:M_W, 0:M_W])
    rs_all = jnp.zeros((L, V7X_LANES), F32)
    qn_all = jnp.zeros((L, V7X_LANES), F32)
    scs = []
    for p in range(M_HEADS // 2):
        sl = slice(p * V7X_LANES, (p + 1) * V7X_LANES)
        sc = _dot_nt(q_mb[:, sl], _stack_masked(k_mb[:, sl], lo, hi_m))
        mt_a = jnp.broadcast_to(m_tok[:, SM_I + 2 * p:SM_I + 2 * p + 1], (L, L))
        mt_b = jnp.broadcast_to(m_tok[:, SM_I + 2 * p + 1:SM_I + 2 * p + 2], (L, L))
        sc_a = sc[:, :L] * jnp.exp(dms[2 * p] - mt_a)
        sc_b = sc[:, L:] * jnp.exp(dms[2 * p + 1] - mt_b)
        rs_all = jnp.where(lane == SM_I + 2 * p, jnp.sum(sc_a, axis=-1, keepdims=True), rs_all)
        rs_all = jnp.where(lane == SM_I + 2 * p + 1, jnp.sum(sc_b, axis=-1, keepdims=True), rs_all)
        qn_all = jnp.where(lane == SM_I + 2 * p, qn_b[:, p * V7X_LANES:p * V7X_LANES + 1], qn_all)
        qn_all = jnp.where(lane == SM_I + 2 * p + 1, qn_b[:, p * V7X_LANES + HEAD_W:p * V7X_LANES + HEAD_W + 1], qn_all)
        scs.append(jnp.concatenate([sc_a, sc_b], axis=1).astype(BF16))
    den = rs_all + qn_all * w_inter
    inv = 1.0 / jnp.maximum(jnp.abs(den), jnp.exp(-m_tok))
    m_new = _seq_bcast(jnp.where(head_lane, m_tok, 0.0), T - 1, T)
    wk = jnp.exp(_seq_bcast(fc, T - 1, T) - fc + gates - m_new)
    scale = jnp.exp(_seq_bcast(a, T - 1, T) - m_new)
    h_parts = []
    kw_parts = []
    scale_parts = []
    for p in range(M_HEADS // 2):
        sl = slice(p * V7X_LANES, (p + 1) * V7X_LANES)
        ia, ib = SM_I + 2 * p, SM_I + 2 * p + 1
        kw = k_m[:, sl] * _lane_pick(wk, ia, ib, lo)
        kw_parts.append(kw)
        inter, kv = pair_state_terms(q_mb[:, sl], kw, v_mb[:, sl], c_ref, p, M_DK)
        num = (_dot(scs[p], _stack_masked(v_mb[:, sl], lo, hi_m))
               + jnp.concatenate(inter, axis=1) * _lane_pick(w_inter, ia, ib, lo))
        h_parts.append(num * _lane_pick(inv, ia, ib, lo))
        scale_parts.append(_lane_pick(scale, ia, ib, lo))
        for hh in range(2):
            h = 2 * p + hh
            sc_rows = jnp.broadcast_to(scale[:, SM_I + h:SM_I + h + 1], (L, M_DV)).reshape(ns, T, M_DV)
            sc_h = jnp.broadcast_to(sc_rows[:, 0:1, :], (ns, M_DK, M_DV))
            c_out[:, h] = c_ref[:, h] * sc_h + kv[hh].reshape(ns, 2, M_DK, M_DV)[:, hh]
    kw_all = jnp.concatenate(kw_parts, axis=1)
    n_out[...] = n_rows * jnp.concatenate(scale_parts, axis=1) + _dot3(segones_ref[...], kw_all)
    m_out[...] = m_new
    gate_m = _sigmoid(proj_ref[:, C_OM:C_OM + M_W])
    mix_ref[:, R_W + G_W:D_MIX] = _norm_gate(jnp.concatenate(h_parts, axis=1), hmat_ref[0:M_W, 0:M_W],
                                             gcat_ref[:, R_W + G_W:D_MIX], gate_m)


def _sample_tables(T, pos0):
    L = ROWS
    half = R_DK // 2
    inv = ROPE_BASE ** (-jnp.arange(half, dtype=F32) * 2.0 / R_DK)
    tok = np.arange(L) % T
    seq = np.arange(L) // T
    pos = pos0 + jnp.asarray(tok, F32)
    ang = pos[:, None] * inv[None, :]
    cos = jnp.tile(jnp.cos(ang), (1, V7X_LANES // half))
    sin_h = jnp.sin(ang)
    sin = jnp.tile(jnp.concatenate([-sin_h, sin_h], axis=1), (1, V7X_LANES // R_DK))
    qs = R_DK ** -0.5

    log_gamma = jnp.log(1.0 - 2.0 ** (-5.0 - jnp.arange(R_HEADS, dtype=F32)))
    tf = jnp.asarray(tok, F32)
    rel = tf[:, None] - tf[None, :]
    ok_np = (seq[:, None] == seq[None, :]) & (tok[None, :] <= tok[:, None])
    ok = jnp.asarray(ok_np)
    rdecay = jnp.where(ok[None], jnp.exp(log_gamma[:, None, None] * jnp.where(ok, rel, 0.0)[None]), 0.0)
    rdec = jnp.concatenate([rdecay[0::2], rdecay[1::2]], axis=2)
    qdec = jnp.repeat(jnp.exp(log_gamma[:, None] * (tf + 1.0)).T, R_DK, axis=1)
    kdec = jnp.repeat(jnp.exp(log_gamma[:, None] * (T - 1.0 - tf)).T, R_DK, axis=1)
    lg32 = np.log(1.0 - 2.0 ** (-5.0 - np.arange(R_HEADS, dtype=np.float64))).astype(np.float32)
    chunk_decay = tuple(float(np.exp(v * np.float32(T))) for v in lg32)

    selr = np.zeros((16, V7X_LANES), np.float32)
    for h in range(M_HEADS):
        selr[h, SM_I + h] = 1.0
        selr[M_HEADS + h, SM_F + h] = 1.0
    hv = np.arange(G_W) // G_DV
    hc = np.arange(G_QK) // G_DK
    gmask = (hv[:, None] == hc[None, :]).astype(np.float32)
    hmat = (hv[:, None] == hv[None, :]).astype(np.float32)
    ns = L // T
    mseg = (seq[:, None] == (np.arange(ns * V7X_LANES) // V7X_LANES)[None, :]).astype(np.float32)
    return dict(
        cosq=cos * qs, sinq=sin * qs, cosk=cos, sink=sin, rdec=rdec, qdec=qdec, kdec=kdec, chunk_decay=chunk_decay,
        tri=jnp.asarray(ok_np, BF16), segones=jnp.asarray(seq[:, None] == seq[None, :], BF16),
        selr=jnp.asarray(selr, BF16), hmat=jnp.asarray(hmat, BF16),
        emat=jnp.asarray(gmask.T, BF16), mseg=jnp.asarray(mseg, BF16), msegt=jnp.asarray(mseg.T, BF16))


_SAMPLE_STATE_DIMS = ((R_HEADS, R_DK, R_DV), (G_HEADS, G_DK, G_DV), (M_HEADS, M_DK, M_DV), (M_W,),
                      (V7X_LANES,), (CONV_W - 1, M_QK))


def mixer_sample(proj, tabs, lw, state, l, depth, prev, *, T):
    M = proj.shape[0]
    B = M // T
    assert B % NS == 0 and NS * T == ROWS
    L = ROWS
    const2 = lambda b: (0, 0)
    const3 = lambda b: (0, 0, 0)
    dims = [(NS,) + d for d in _SAMPLE_STATE_DIMS]
    dims[3], dims[4] = (L, M_W), (L, V7X_LANES)
    full = [(B,) + d for d in _SAMPLE_STATE_DIMS]
    full[3], full[4] = (M, M_W), (M, V7X_LANES)

    def spec(d, lead):
        tail = (0,) * (len(d) - 1)
        if lead == "all":
            return pl.BlockSpec((depth,) + d, lambda b: (0, b) + tail)
        return pl.BlockSpec((None,) + d, lambda b: (lead, b) + tail)

    in_specs = [
        pl.BlockSpec((L, D_IN_PAD), lambda b: (b, 0)),
        pl.BlockSpec((L, V7X_LANES), const2), pl.BlockSpec((L, V7X_LANES), const2),
        pl.BlockSpec((L, V7X_LANES), const2), pl.BlockSpec((L, V7X_LANES), const2),
        pl.BlockSpec((R_HEADS // 2, L, 2 * L), const3),
        pl.BlockSpec((L, R_W), const2),
        pl.BlockSpec((L, R_W), const2),
        pl.BlockSpec((L, L), const2),
        pl.BlockSpec((L, L), const2),
        pl.BlockSpec((16, V7X_LANES), const2),
        pl.BlockSpec((G_W, G_W), const2),
        pl.BlockSpec((G_QK, G_W), const2),
        pl.BlockSpec((L, NS * V7X_LANES), const2),
        pl.BlockSpec((NS * V7X_LANES, L), const2),
        pl.BlockSpec((V7X_LANES, G_QK), const2),
        pl.BlockSpec((1, G_QK), const2),
        pl.BlockSpec((CONV_W, M_QK), const2),
        pl.BlockSpec((1, M_QK), const2),
        pl.BlockSpec((1, V7X_LANES), const2),
        pl.BlockSpec((1, D_MIX), const2),
    ] + [spec(d, l) for d in dims]
    args = [proj, tabs["cosq"], tabs["sinq"], tabs["cosk"], tabs["sink"], tabs["rdec"], tabs["qdec"], tabs["kdec"],
            tabs["tri"], tabs["segones"], tabs["selr"], tabs["hmat"], tabs["emat"], tabs["mseg"], tabs["msegt"],
            lw["wga"], lw["bga"], lw["convw"], lw["convb"], lw["bsm"], lw["gcat"]] + list(state)
    aliases = {}
    if prev is not None:
        n_in = len(args)
        in_specs += [pl.BlockSpec(memory_space=pl.ANY)] * 6
        args += list(prev)
        aliases = {n_in + i: 1 + i for i in range(6)}
    out_shape = (jax.ShapeDtypeStruct((M, D_MIX), BF16),) + tuple(jax.ShapeDtypeStruct((depth,) + f, F32) for f in full)
    out_specs = (pl.BlockSpec((L, D_MIX), lambda b: (b, 0)),) + tuple(
        spec(d, "all" if prev is None else l) for d in dims)
    kern = functools.partial(_mixer_sample_kernel, T=T, chunk_decay=tabs["chunk_decay"],
                             slot=l if prev is None else None, depth=depth)
    outs = pl.pallas_call(
        kern, out_shape=out_shape, grid=(B // NS,), in_specs=in_specs, out_specs=out_specs,
        scratch_shapes=[pltpu.VMEM((NS, 8 + T, M_QK), F32)],
        input_output_aliases=aliases,
        compiler_params=_cparams(("parallel",)), name="mixer_sample",
    )(*args)
    return outs[0], tuple(outs[1:])


def _prep_weights(w_in, w_out, w_xq, w_xk, w_xv, w_xo, w_gate, w_up, w_down):
    a0 = 2 * R_HEADS * R_DK + 2 * R_W + 2 * G_QK + 2 * G_W
    m0 = a0 + G_RANK
    g0 = m0 + M_QK + 2 * M_W
    cast = lambda w: w.astype(BF16)
    wb = cast(w_in)
    pad = jnp.zeros(w_in.shape[:2] + (D_IN_PAD - C_SM - G_RANK - 2 * M_HEADS,), BF16)
    w_pad = jnp.concatenate([wb[..., :a0], wb[..., m0:g0], wb[..., g0:g0 + 2 * M_HEADS], wb[..., a0:m0], pad], axis=2)
    return dict(w_in=w_pad, w_out=cast(w_out), w_xq=cast(w_xq), w_xk=cast(w_xk), w_xv=cast(w_xv),
                w_xo=cast(w_xo), w_gate=cast(w_gate), w_up=cast(w_up), w_down=cast(w_down))


def _prep_layer(l, g_mix, w_ga2, b_ga, conv_w, conv_b, b_i, b_f, g_ret, g_gla, g_mlstm, g_xattn, g_mem, g_ffn):
    wga = jnp.zeros((V7X_LANES, G_QK), F32).at[SM_AG:SM_AG + G_RANK].set(w_ga2[l]).astype(BF16)
    bsm = (jnp.zeros((1, V7X_LANES), F32).at[0, SM_I:SM_I + M_HEADS].set(b_i[l])
           .at[0, SM_F:SM_F + M_HEADS].set(b_f[l]))
    return dict(
        g_mix=g_mix[l], wga=wga, bga=b_ga[l].reshape(1, G_QK), convw=conv_w[l],
        convb=conv_b[l].reshape(1, M_QK), bsm=bsm,
        gcat=jnp.concatenate([g_ret[l], g_gla[l], g_mlstm[l]]).reshape(1, D_MIX),
        g_xattn=g_xattn[l], g_mem=g_mem[l], g_ffn=g_ffn[l])


def _layer(x, mix_fn, attend, W, lw, l, g_final, *, tm, final_norm):
    B, T, D = x.shape
    M = B * T
    x2 = x.reshape(M, D)
    proj = rms_matmul(x2, lw["g_mix"], W["w_in"], l, tm=tm)
    mix, new_state = mix_fn(proj)
    x2 = attend(mix, x2)
    x2 = swiglu_res(x2, lw["g_ffn"], W["w_gate"], W["w_up"], W["w_down"], l, g_final,
                    tm=tm, final_norm=final_norm)
    return x2.reshape(B, T, D), new_state


def kernel(x_prompt, x_sample, state_ret, state_gla, state_mlstm_C, state_mlstm_n, state_mlstm_m, state_mlstm_conv, cache_mem_k, cache_mem_v, mem_prompt, g_mix, w_in, w_ga2, b_ga, conv_w, conv_b, b_i, b_f, g_ret, g_gla, g_mlstm, w_out, g_xattn, g_mem, w_xq, w_xk, w_xv, w_xo, g_ffn, w_gate, w_up, w_down, g_final):
    B, T, D = x_prompt.shape
    Bs, Ts, _ = x_sample.shape
    depth = w_in.shape[0]
    assert T % CHUNK == 0 and Ts * NS == ROWS and Bs % NS == 0
    tabs_p = _prompt_tables(T, CHUNK, 16)
    tabs_s = _sample_tables(Ts, float(PAST_LEN))
    tm_p = 512 if (B * T) % 512 == 0 else B * T
    tm_s = 512 if (Bs * Ts) % 512 == 0 else Bs * Ts
    tq_p = 512 if T % 512 == 0 else T
    nb_x = 8 if Bs % 8 == 0 else 1

    sample_state = (state_ret, state_gla, state_mlstm_C,
                    jnp.repeat(state_mlstm_n.reshape(depth, Bs, M_W), Ts, axis=1),
                    jnp.repeat(jnp.pad(state_mlstm_m, ((0, 0), (0, 0), (0, V7X_LANES - M_HEADS))), Ts, axis=1),
                    state_mlstm_conv)
    p_st = tuple(jnp.zeros((depth, B) + d, F32) for d in _STATE_DIMS)
    s_st = p_mem = None
    hp, hs = x_prompt, x_sample
    W = _prep_weights(w_in, w_out, w_xq, w_xk, w_xv, w_xo, w_gate, w_up, w_down)
    for l in range(depth):
        lw = _prep_layer(l, g_mix, w_ga2, b_ga, conv_w, conv_b, b_i, b_f, g_ret, g_gla, g_mlstm,
                         g_xattn, g_mem, g_ffn)
        last = l == depth - 1
        k5, v5, kb, vb = memory_kv(mem_prompt, lw["g_mem"], W["w_xk"], W["w_xv"], l, depth, p_mem)
        p_mem = (k5, v5)

        def mix_p(proj):
            mix, st = mixer_prompt(proj.reshape(B, T, D_IN_PAD), tabs_p, lw, l, depth, p_st, L=CHUNK, CS=16)
            return mix.reshape(B * T, D_MIX), st

        def attend_p(mix, x):
            y = post_mix(mix.reshape(B, T, D_MIX), x.reshape(B, T, D), kb, vb,
                         W["w_out"], lw["g_xattn"], W["w_xq"], W["w_xo"], l, tq=tq_p)
            return y.reshape(B * T, D)

        hp, p_st = _layer(hp, mix_p, attend_p, W, lw, l, g_final, tm=tm_p, final_norm=last)
        hs, s_st = _layer(hs, lambda proj: mixer_sample(proj, tabs_s, lw, sample_state, l, depth, s_st, T=Ts),
                          lambda mix, x: post_mix_cache(mix, x, cache_mem_k, cache_mem_v, W["w_out"],
                                                        lw["g_xattn"], W["w_xq"], W["w_xo"], l, nb=nb_x, T=Ts),
                          W, lw, l, g_final, tm=tm_s, final_norm=last)

    p_out = p_st[:4] + (p_st[4].reshape(depth, B, M_HEADS), p_st[5])
    s_out = s_st[:3] + (s_st[3][:, ::Ts].reshape(depth, Bs, M_HEADS, M_DK), s_st[4][:, ::Ts, :M_HEADS], s_st[5])
    return (hp, hs, *p_out, *p_mem, *s_out)
```

```python
import functools

import numpy as np
import jax
import jax.numpy as jnp
from jax import lax
from jax.experimental import pallas as pl
from jax.experimental.pallas import tpu as pltpu

F32 = jnp.float32
BF16 = jnp.bfloat16

D_MODEL = 1024
PAST_LEN = 16384
R_HEADS, R_DK, R_DV = 6, 64, 64
G_HEADS, G_DK, G_DV, G_RANK = 6, 32, 64, 16
G_NORMALIZER = 16.0
M_HEADS, M_DK, M_DV = 4, 64, 64
CONV_W = 4
X_HEADS = 4
X_HD = D_MODEL // X_HEADS
N_MEM = 256
CHUNK = 256
EPS = 1e-6
ROPE_BASE = 10000.0

R_W = R_HEADS * R_DV
G_QK = G_HEADS * G_DK
G_W = G_HEADS * G_DV
M_QK = 2 * M_HEADS * M_DK
M_W = M_HEADS * M_DV
D_MIX = R_W + G_W + M_W

C_QR, C_KR, C_VR, C_GR = 0, 384, 768, 1152
C_QG, C_KG, C_VG, C_RG = 1536, 1728, 1920, 2304
C_QKM, C_VM, C_OM, C_SM = 2688, 3200, 3456, 3712
D_IN_PAD = 3840
SM_I, SM_F, SM_AG = 0, 4, 8
HEAD_W = 64
GLA_SAFE_LOG_RANGE = 60.0

V7X_LANES = 128
VMEM_LIMIT = 56 * 1024 * 1024


def _cparams(sem):
    return pltpu.CompilerParams(dimension_semantics=sem, vmem_limit_bytes=VMEM_LIMIT)


def _sigmoid(x):
    return 1.0 / (1.0 + jnp.exp(-x))


def _silu(x):
    return x * _sigmoid(x)


def _log_sigmoid(x):
    return jnp.minimum(x, 0.0) - jnp.log(1.0 + jnp.exp(-jnp.abs(x)))


def _dot(a, b):
    return jnp.dot(a, b, preferred_element_type=F32)


def _dot_nt(a, b):
    return lax.dot_general(a, b, (((1,), (1,)), ((), ())), preferred_element_type=F32)


def _dot_tn(a, b):
    return lax.dot_general(a, b, (((0,), (0,)), ((), ())), preferred_element_type=F32)


def _split3(x):
    hi = x.astype(BF16)
    r1 = x - hi.astype(F32)
    mid = r1.astype(BF16)
    lo = (r1 - mid.astype(F32)).astype(BF16)
    return hi, mid, lo


def _dot3(a, x):
    hi, mid, lo = _split3(x)
    return _dot(a, hi) + _dot(a, mid) + _dot(a, lo)


def _dot3_nt(a, x):
    hi, mid, lo = _split3(x)
    return _dot_nt(a, hi) + _dot_nt(a, mid) + _dot_nt(a, lo)


def _rms(x, g):
    return x * lax.rsqrt(jnp.mean(x * x, axis=-1, keepdims=True) + EPS) * g


def _wspec(w, l):
    return pl.BlockSpec((None,) + w.shape[1:], lambda *_: (l, 0, 0))


def _rms_matmul_kernel(x_ref, g_ref, w_ref, o_ref):
    o_ref[...] = _dot(_rms(x_ref[...], g_ref[...]).astype(BF16), w_ref[...]).astype(o_ref.dtype)


def rms_matmul(x, g, w, l, *, tm, out_dtype=F32):
    M, D = x.shape
    N = w.shape[2]
    assert M % tm == 0
    return pl.pallas_call(
        _rms_matmul_kernel,
        out_shape=jax.ShapeDtypeStruct((M, N), out_dtype),
        grid=(M // tm,),
        in_specs=[pl.BlockSpec((tm, D), lambda i: (i, 0)),
                  pl.BlockSpec((1, D), lambda i: (0, 0)),
                  _wspec(w, l)],
        out_specs=pl.BlockSpec((tm, N), lambda i: (i, 0)),
        compiler_params=_cparams(("parallel",)),
        name="rms_matmul",
    )(x, g.reshape(1, D), w)


def _swiglu_kernel(x_ref, g_ref, wg_ref, wu_ref, wd_ref, gf_ref, o_ref, *, final_norm):
    x = x_ref[...]
    xn = _rms(x, g_ref[...]).astype(BF16)
    h = _silu(_dot(xn, wg_ref[...])) * _dot(xn, wu_ref[...])
    y = x + _dot(h.astype(BF16), wd_ref[...])
    if final_norm:
        y = _rms(y, gf_ref[...])
    o_ref[...] = y


def swiglu_res(x, g, wg, wu, wd, l, g_final, *, tm, final_norm):
    M, D = x.shape
    assert M % tm == 0
    row = pl.BlockSpec((tm, D), lambda i: (i, 0))
    vec = pl.BlockSpec((1, D), lambda i: (0, 0))
    return pl.pallas_call(
        functools.partial(_swiglu_kernel, final_norm=final_norm),
        out_shape=jax.ShapeDtypeStruct((M, D), F32),
        grid=(M // tm,),
        in_specs=[row, vec, _wspec(wg, l), _wspec(wu, l), _wspec(wd, l), vec],
        out_specs=row,
        compiler_params=_cparams(("parallel",)),
        name="swiglu_res",
    )(x, g.reshape(1, D), wg, wu, wd, g_final.reshape(1, D))


def _memkv_kernel(x_ref, g_ref, wk_ref, wv_ref, *refs, slot, depth):
    k5_ref, v5_ref, kb_ref, vb_ref = refs[-4:]
    if slot is not None:
        for e in range(depth):
            if e != slot:
                k5_ref[e] = jnp.zeros(k5_ref.shape[1:], F32)
                v5_ref[e] = jnp.zeros(v5_ref.shape[1:], F32)
        k5_ref, v5_ref = k5_ref.at[slot], v5_ref.at[slot]
    xn = _rms(x_ref[...], g_ref[...]).astype(BF16)
    for w_ref, o5_ref, ob_ref in ((wk_ref, k5_ref, kb_ref), (wv_ref, v5_ref, vb_ref)):
        y = _dot(xn, w_ref[...])
        ob_ref[...] = y.astype(BF16)
        for h in range(X_HEADS):
            o5_ref[:, h, :] = y[:, h * X_HD:(h + 1) * X_HD]


def memory_kv(mem, g, wk, wv, l, depth, prev):
    B, _, D = mem.shape
    in_specs = [pl.BlockSpec((None, N_MEM, D), lambda b: (b, 0, 0)),
                pl.BlockSpec((1, D), lambda b: (0, 0)),
                _wspec(wk, l), _wspec(wv, l)]
    args = [mem, g.reshape(1, D), wk, wv]
    aliases = {}
    if prev is None:
        spec5 = pl.BlockSpec((depth, None, N_MEM, X_HEADS, X_HD), lambda b: (0, b, 0, 0, 0))
    else:
        spec5 = pl.BlockSpec((None, None, N_MEM, X_HEADS, X_HD), lambda b: (l, b, 0, 0, 0))
        in_specs += [pl.BlockSpec(memory_space=pl.ANY)] * 2
        args += list(prev)
        aliases = {4: 0, 5: 1}
    o5 = jax.ShapeDtypeStruct((depth, B, N_MEM, X_HEADS, X_HD), F32)
    ob = jax.ShapeDtypeStruct((B, N_MEM, D), BF16)
    specb = pl.BlockSpec((None, N_MEM, D), lambda b: (b, 0, 0))
    return pl.pallas_call(
        functools.partial(_memkv_kernel, slot=l if prev is None else None, depth=depth),
        out_shape=(o5, o5, ob, ob),
        grid=(B,),
        in_specs=in_specs,
        out_specs=(spec5, spec5, specb, specb),
        input_output_aliases=aliases,
        compiler_params=_cparams(("parallel",)),
        name="memory_kv",
    )(*args)


def _out_and_query(a_ref, x_ref, wo_ref, g_ref, wq_ref):
    x1 = x_ref[...] + _dot(a_ref[...], wo_ref[...])
    q = _dot(_rms(x1, g_ref[...]).astype(BF16), wq_ref[...]).astype(BF16)
    return x1, q


def _post_mix_kernel(a_ref, x_ref, k_ref, v_ref, wo_ref, g_ref, wq_ref, wxo_ref, o_ref):
    x1, q = _out_and_query(a_ref, x_ref, wo_ref, g_ref, wq_ref)
    scale = X_HD ** -0.5
    parts = []
    for h in range(X_HEADS):
        sl = slice(h * X_HD, (h + 1) * X_HD)
        s = _dot_nt(q[:, sl], k_ref[:, sl]) * scale
        p = jnp.exp(s - jnp.max(s, axis=-1, keepdims=True))
        l = jnp.sum(p, axis=-1, keepdims=True)
        parts.append((_dot(p.astype(BF16), v_ref[:, sl]) / l).astype(BF16))
    o_ref[...] = x1 + _dot(jnp.concatenate(parts, axis=1), wxo_ref[...])


def post_mix(a, x, mk, mv, wo, g, wq, wxo, l, *, tq):
    B, T, D = x.shape
    assert T % tq == 0
    tok = lambda w: pl.BlockSpec((None, tq, w), lambda b, i: (b, i, 0))
    mem = pl.BlockSpec((None, N_MEM, D), lambda b, i: (b, 0, 0))
    return pl.pallas_call(
        _post_mix_kernel,
        out_shape=jax.ShapeDtypeStruct((B, T, D), F32),
        grid=(B, T // tq),
        in_specs=[tok(a.shape[2]), tok(D), mem, mem, _wspec(wo, l), pl.BlockSpec((1, D), lambda b, i: (0, 0)),
                  _wspec(wq, l), _wspec(wxo, l)],
        out_specs=tok(D),
        compiler_params=_cparams(("parallel", "arbitrary")),
        name="post_mix",
    )(a, x, mk, mv, wo, g.reshape(1, D), wq, wxo)


def _post_mix_cache_kernel(a_ref, x_ref, k_ref, v_ref, wo_ref, g_ref, wq_ref, wxo_ref, o_ref, *, nb, T):
    x1, q_all = _out_and_query(a_ref, x_ref, wo_ref, g_ref, wq_ref)
    R = X_HEADS * T
    rowh = lax.broadcasted_iota(jnp.int32, (R, N_MEM * X_HEADS), 0) // T
    colh = lax.broadcasted_iota(jnp.int32, (R, N_MEM * X_HEADS), 1) % X_HEADS
    own = rowh == colh
    outs = []
    for s in range(nb):
        q = q_all[s * T:(s + 1) * T]
        qf = jnp.concatenate([q[:, h * X_HD:(h + 1) * X_HD] for h in range(X_HEADS)], axis=0)
        kf = k_ref[s].reshape(N_MEM * X_HEADS, X_HD).astype(BF16)
        vf = v_ref[s].reshape(N_MEM * X_HEADS, X_HD).astype(BF16)
        sc = jnp.where(own, _dot_nt(qf, kf) * (X_HD ** -0.5), -jnp.inf)
        p = jnp.exp(sc - jnp.max(sc, axis=-1, keepdims=True))
        l = jnp.sum(p, axis=-1, keepdims=True)
        o = (_dot(p.astype(BF16), vf) / l).astype(BF16)
        outs.append(jnp.concatenate([o[h * T:(h + 1) * T] for h in range(X_HEADS)], axis=1))
    o_ref[...] = x1 + _dot(jnp.concatenate(outs, axis=0), wxo_ref[...])


def post_mix_cache(a, x, ck, cv, wo, g, wq, wxo, l, *, nb, T):
    M, D = x.shape
    rows = nb * T
    assert M % rows == 0
    cspec = pl.BlockSpec((None, nb, N_MEM, X_HEADS, X_HD), lambda b: (l, b, 0, 0, 0))
    tok = lambda w: pl.BlockSpec((rows, w), lambda b: (b, 0))
    return pl.pallas_call(
        functools.partial(_post_mix_cache_kernel, nb=nb, T=T),
        out_shape=jax.ShapeDtypeStruct((M, D), F32),
        grid=(M // rows,),
        in_specs=[tok(a.shape[1]), tok(D), cspec, cspec, _wspec(wo, l), pl.BlockSpec((1, D), lambda b: (0, 0)),
                  _wspec(wq, l), _wspec(wxo, l)],
        out_specs=tok(D),
        compiler_params=_cparams(("parallel",)),
        name="post_mix_cache",
    )(a, x, ck, cv, wo, g.reshape(1, D), wq, wxo)


def _eye(n):
    r = lax.broadcasted_iota(jnp.int32, (n, n), 0)
    c = lax.broadcasted_iota(jnp.int32, (n, n), 1)
    return jnp.where(r == c, 1.0, 0.0).astype(BF16)


_STATE_DIMS = ((R_HEADS, R_DK, R_DV), (G_HEADS, G_DK, G_DV), (M_HEADS, M_DK, M_DV), (M_HEADS, M_DK),
               (1, M_HEADS), (CONV_W - 1, M_QK))


def _split2(x):
    hi = x.astype(BF16)
    lo = (x - hi.astype(F32)).astype(BF16)
    return hi, lo


def _head_mean_sq(o, hmat):
    hi, lo = _split2(o * o)
    return (_dot(hi, hmat) + _dot(lo, hmat)) * (1.0 / HEAD_W)


def _norm_gate(o, hmat, g_row, gate):
    return (o * lax.rsqrt(_head_mean_sq(o, hmat) + EPS) * g_row * gate).astype(BF16)


def _lane_pick(cols, idx_lo, idx_hi, lo_mask):
    L = cols.shape[0]
    a = jnp.broadcast_to(cols[:, idx_lo:idx_lo + 1], (L, V7X_LANES))
    b = jnp.broadcast_to(cols[:, idx_hi:idx_hi + 1], (L, V7X_LANES))
    return jnp.where(lo_mask, a, b)


def _stack_masked(x, m_a, m_b):
    z = jnp.zeros_like(x)
    return jnp.concatenate([jnp.where(m_a, x, z), jnp.where(m_b, x, z)], axis=0)


def _mixer_prompt_kernel(proj_ref, cosq_ref, sinq_ref, cosk_ref, sink_ref, rdec_ref, qdec_ref, kdec_ref,
                         tri_ref, btri_ref, bones_ref, selr_ref, hmat_ref, emat_ref, pmask_ref,
                         wga_ref, bga_ref, convw_ref, convb_ref, bsm_ref, gcat_ref, *rest,
                         L, CS, NC, chunk_decay):
    (mix_ref, sr_out, sg_out, c_out, n_out, m_out, conv_out,
     srp_scr, sgt_scr, cp_scr, n_scr, m_scr, conv_scr, og_scr) = rest[-14:]
    c = pl.program_id(1)

    @pl.when(c == 0)
    def _():
        srp_scr[...] = jnp.zeros_like(srp_scr)
        sgt_scr[...] = jnp.zeros_like(sgt_scr)
        cp_scr[...] = jnp.zeros_like(cp_scr)
        n_scr[...] = jnp.zeros_like(n_scr)
        m_scr[...] = jnp.zeros_like(m_scr)
        conv_scr[0:8, :] = jnp.zeros((8, M_QK), F32)

    lane = lax.broadcasted_iota(jnp.int32, (L, V7X_LANES), 1)
    lo = lane < HEAD_W
    hi_m = lane >= HEAD_W
    first_half = (lane % R_DK) < (R_DK // 2)
    row2 = lax.broadcasted_iota(jnp.int32, (L, 2 * L), 0)
    col2 = lax.broadcasted_iota(jnp.int32, (L, 2 * L), 1) % L
    causal2 = col2 <= row2
    lo_row = lax.broadcasted_iota(jnp.int32, (1, V7X_LANES), 1) < HEAD_W
    lo8 = lax.broadcasted_iota(jnp.int32, (8, V7X_LANES), 1) < HEAD_W
    hmat = hmat_ref[...]
    pmask = pmask_ref[...]

    def rope(x, cos_ref, sin_ref):
        cos = cos_ref[...]
        sin = sin_ref[...]
        parts = []
        for t in range(R_W // V7X_LANES):
            xs = x[:, t * V7X_LANES:(t + 1) * V7X_LANES]
            rot = jnp.where(first_half, pltpu.roll(xs, V7X_LANES - R_DK // 2, 1), pltpu.roll(xs, R_DK // 2, 1))
            parts.append(xs * cos + rot * sin)
        return jnp.concatenate(parts, axis=1)

    q_r = rope(proj_ref[:, C_QR:C_QR + R_W], cosq_ref, sinq_ref)
    k_r = rope(proj_ref[:, C_KR:C_KR + R_W], cosk_ref, sink_ref)
    q_rb = q_r.astype(BF16)
    k_rb = k_r.astype(BF16)
    kd_rb = (k_r * kdec_ref[...]).astype(BF16)
    v_rb = proj_ref[:, C_VR:C_VR + R_W].astype(BF16)
    o_parts = []
    for p in range(R_HEADS // 2):
        sl = slice(p * V7X_LANES, (p + 1) * V7X_LANES)
        qs, ks, vs = q_rb[:, sl], k_rb[:, sl], v_rb[:, sl]
        sc = _dot_nt(qs, _stack_masked(ks, lo, hi_m)) * rdec_ref[p]
        sp = srp_scr[p]
        o = _dot(sc.astype(BF16), _stack_masked(vs, lo, hi_m)) + _dot(qs, sp.astype(BF16)) * qdec_ref[:, sl]
        cd = jnp.where(lo_row, chunk_decay[2 * p], chunk_decay[2 * p + 1])
        srp_scr[p] = sp * cd + _dot_tn(kd_rb[:, sl], vs) * pmask
        o_parts.append(o)
    o_r = jnp.concatenate(o_parts, axis=1)
    gate_r = _silu(proj_ref[:, C_GR:C_GR + R_W])
    mix_ref[:, 0:R_W] = _norm_gate(o_r, hmat, gcat_ref[:, 0:R_W], gate_r)

    small = proj_ref[:, C_SM:C_SM + V7X_LANES]

    u = proj_ref[:, C_QKM:C_QKM + M_QK]
    conv_scr[8:8 + L, :] = u
    y = convb_ref[...]
    for j in range(CONV_W - 1):
        y = y + conv_scr[5 + j:5 + j + L, :] * convw_ref[j:j + 1, :]
    y = y + u * convw_ref[CONV_W - 1:CONV_W, :]
    tail = conv_scr[5 + L:8 + L, :]
    conv_scr[5:8, :] = tail
    qk = _silu(y)
    q_m = qk[:, :M_W]
    k_m = qk[:, M_W:] * (M_DK ** -0.5)
    q_mb = q_m.astype(BF16)
    k_mb = k_m.astype(BF16)
    v_mb = proj_ref[:, C_VM:C_VM + M_W].astype(BF16)
    gates = small + bsm_ref[...]
    f_cum = _dot3(tri_ref[...], _log_sigmoid(gates))
    i_rows = _dot3_nt(selr_ref[...], gates)
    f_rows = _dot3_nt(selr_ref[...], f_cum)
    head_lane = (lane >= SM_I) & (lane < SM_I + M_HEADS)
    fc = jnp.where(head_lane, pltpu.roll(f_cum, V7X_LANES - (SM_F - SM_I), 1), 0.0)
    m_prev = m_scr[0:1, :]
    a = fc + m_prev
    mx = jnp.full((L, V7X_LANES), -jnp.inf, F32)
    row = lax.broadcasted_iota(jnp.int32, (L, L), 0)
    col = lax.broadcasted_iota(jnp.int32, (L, L), 1)
    causal = col <= row
    dms = []
    for h in range(M_HEADS):
        dm = jnp.where(causal, (fc[:, SM_I + h:SM_I + h + 1] - f_rows[M_HEADS + h:M_HEADS + h + 1, :])
                       + i_rows[h:h + 1, :], -jnp.inf)
        dms.append(dm)
        mx = jnp.where(lane == SM_I + h, jnp.max(dm, axis=-1, keepdims=True), mx)
    m_tok = jnp.maximum(a, mx)
    w_inter = jnp.exp(a - m_tok)
    n_full = n_scr[0:1, :]
    hs, lows = _split2(q_m * n_full)
    qn_b = _dot(hs, hmat_ref[0:M_W, 0:M_W]) + _dot(lows, hmat_ref[0:M_W, 0:M_W])
    rs_all = jnp.zeros((L, V7X_LANES), F32)
    qn_all = jnp.zeros((L, V7X_LANES), F32)
    scs = []
    for p in range(M_HEADS // 2):
        sl = slice(p * V7X_LANES, (p + 1) * V7X_LANES)
        sc = _dot_nt(q_mb[:, sl], _stack_masked(k_mb[:, sl], lo, hi_m))
        mt_a = jnp.broadcast_to(m_tok[:, SM_I + 2 * p:SM_I + 2 * p + 1], (L, L))
        mt_b = jnp.broadcast_to(m_tok[:, SM_I + 2 * p + 1:SM_I + 2 * p + 2], (L, L))
        sc_a = sc[:, :L] * jnp.exp(dms[2 * p] - mt_a)
        sc_b = sc[:, L:] * jnp.exp(dms[2 * p + 1] - mt_b)
        rs_all = jnp.where(lane == SM_I + 2 * p, jnp.sum(sc_a, axis=-1, keepdims=True), rs_all)
        rs_all = jnp.where(lane == SM_I + 2 * p + 1, jnp.sum(sc_b, axis=-1, keepdims=True), rs_all)
        qn_all = jnp.where(lane == SM_I + 2 * p, qn_b[:, p * V7X_LANES:p * V7X_LANES + 1], qn_all)
        qn_all = jnp.where(lane == SM_I + 2 * p + 1, qn_b[:, p * V7X_LANES + HEAD_W:p * V7X_LANES + HEAD_W + 1], qn_all)
        scs.append(jnp.concatenate([sc_a, sc_b], axis=1).astype(BF16))
    den = rs_all + qn_all * w_inter
    inv = 1.0 / jnp.maximum(jnp.abs(den), jnp.exp(-m_tok))
    m_new = jnp.where(head_lane, m_tok, 0.0)[L - 1:L, :]
    wk = jnp.exp(fc[L - 1:L, :] - fc + gates - m_new)
    scale = jnp.broadcast_to(jnp.exp(a[L - 1:L, :] - m_new), (8, V7X_LANES))
    h_parts = []
    kw_parts = []
    scale_parts = []
    for p in range(M_HEADS // 2):
        sl = slice(p * V7X_LANES, (p + 1) * V7X_LANES)
        ia, ib = SM_I + 2 * p, SM_I + 2 * p + 1
        cpair = cp_scr[p]
        num = (_dot(scs[p], _stack_masked(v_mb[:, sl], lo, hi_m))
               + _dot(q_mb[:, sl], cpair.astype(BF16)) * _lane_pick(w_inter, ia, ib, lo))
        h_parts.append(num * _lane_pick(inv, ia, ib, lo))
        kw = k_m[:, sl] * _lane_pick(wk, ia, ib, lo)
        kw_parts.append(kw)
        sc_row = _lane_pick(scale, ia, ib, lo8)[0:1]
        scale_parts.append(sc_row)
        cp_scr[p] = cpair * sc_row + _dot_tn(kw.astype(BF16), v_mb[:, sl]) * pmask
    kw_all = jnp.concatenate(kw_parts, axis=1)
    n_new = n_full * jnp.concatenate(scale_parts, axis=1) + jnp.sum(kw_all, axis=0, keepdims=True)
    n_scr[...] = jnp.broadcast_to(n_new, n_scr.shape)
    m_scr[...] = jnp.broadcast_to(m_new, m_scr.shape)
    gate_m = _sigmoid(proj_ref[:, C_OM:C_OM + M_W])
    mix_ref[:, R_W + G_W:D_MIX] = _norm_gate(jnp.concatenate(h_parts, axis=1), hmat_ref[0:M_W, 0:M_W],
                                             gcat_ref[:, R_W + G_W:D_MIX], gate_m)

    z = _dot(small.astype(BF16), wga_ref[...]) + bga_ref[...]
    log_a = _log_sigmoid(z) / G_NORMALIZER
    b = _dot3(tri_ref[...], log_a)
    b_last = b[L - 1:L, :]
    safe = jnp.max(-b_last) <= GLA_SAFE_LOG_RANGE
    q_g = proj_ref[:, C_QG:C_QG + G_QK] * (G_DK ** -0.5)
    k_g = proj_ref[:, C_KG:C_KG + G_QK]
    v_gb = proj_ref[:, C_VG:C_VG + G_W].astype(BF16)

    slot = lax.broadcasted_iota(jnp.int32, (V7X_LANES, V7X_LANES), 1) // G_DK
    row_head = lax.broadcasted_iota(jnp.int32, (V7X_LANES, V7X_LANES), 0) // HEAD_W
    lane_q = lane // G_DK

    def pad2(x):
        return jnp.concatenate([x, jnp.zeros((x.shape[0], 2 * V7X_LANES - G_QK), x.dtype)], axis=1)

    def slab(x_p, g):
        s0 = (2 * g * G_DK) // V7X_LANES * V7X_LANES
        return x_p[:, s0:s0 + V7X_LANES], (2 * g * G_DK - s0) // G_DK

    def gla_inter(q_p):
        return jnp.concatenate([_dot_nt(slab(q_p, g)[0], sgt_scr[g].astype(BF16)) for g in range(G_HEADS // 2)],
                               axis=1)

    def gla_update(decay_p, k_p, v_b):
        for g in range(G_HEADS // 2):
            ks, h_a = slab(k_p, g)
            kv = _dot_tn(v_b[:, g * V7X_LANES:(g + 1) * V7X_LANES], ks)
            sgt_scr[g] = sgt_scr[g] * slab(decay_p, g)[0] + jnp.where(slot == h_a + row_head, kv, 0.0)

    @pl.when(safe)
    def _():
        qt_p = pad2((q_g * jnp.exp(b)).astype(BF16))
        kt_p = pad2((k_g * jnp.exp(-b)).astype(BF16))
        kl_p = pad2((k_g * jnp.exp(b_last - b)).astype(BF16))
        o_inter = gla_inter(qt_p)
        parts = []
        for g in range(G_HEADS // 2):
            qs, h_a = slab(qt_p, g)
            ks, _ = slab(kt_p, g)
            sc = _dot_nt(qs, _stack_masked(ks, lane_q == h_a, lane_q == h_a + 1))
            sc = jnp.where(causal2, sc, 0.0).astype(BF16)
            vs = v_gb[:, g * V7X_LANES:(g + 1) * V7X_LANES]
            parts.append(_dot(sc, _stack_masked(vs, lo, hi_m)))
        og_scr[...] = o_inter + jnp.concatenate(parts, axis=1)
        gla_update(pad2(jnp.exp(b_last)), kl_p, v_gb)

    @pl.when(jnp.logical_not(safe))
    def _():
        v_g = proj_ref[:, C_VG:C_VG + G_W]
        b_loc = _dot3(btri_ref[...], log_a)
        b_tot = _dot3(bones_ref[...], log_a)
        qt_p = pad2((q_g * jnp.exp(b_loc)).astype(BF16))
        kt_p = pad2((k_g * jnp.exp(b_tot - b_loc)).astype(BF16))
        d_tot_p = pad2(jnp.exp(b_tot))
        emat = emat_ref[...]
        sub_row = lax.broadcasted_iota(jnp.int32, (CS, G_QK), 0)
        for blk in range(L // CS):
            r0 = blk * CS
            rs = slice(r0, r0 + CS)
            o_blk = gla_inter(qt_p[rs])
            bI, qI, kI = b_loc[rs], q_g[rs], k_g[rs]
            terms = []
            for j in range(CS):
                e = jnp.exp(jnp.where(sub_row >= j, bI - bI[j:j + 1], -jnp.inf))
                terms.append(e * qI * kI[j:j + 1])
            t = jnp.concatenate(terms, axis=0).astype(BF16)
            w = _dot(t, emat)
            for j in range(CS):
                o_blk = o_blk + w[j * CS:(j + 1) * CS] * v_g[r0 + j:r0 + j + 1]
            og_scr[rs, :] = o_blk
            gla_update(d_tot_p[r0:r0 + 1], kt_p[rs], v_gb[rs])

    gate_g = _silu(proj_ref[:, C_RG:C_RG + G_W])
    mix_ref[:, R_W:R_W + G_W] = _norm_gate(og_scr[...], hmat, gcat_ref[:, R_W:R_W + G_W], gate_g)

    @pl.when(c == NC - 1)
    def _():
        eye_k = _eye(G_DK)
        for p in range(R_HEADS // 2):
            sr_out[2 * p] = srp_scr[p, 0:HEAD_W, 0:HEAD_W]
            sr_out[2 * p + 1] = srp_scr[p, HEAD_W:, HEAD_W:]
        for h in range(G_HEADS):
            g, hh = h // 2, h % 2
            c0 = (h * G_DK) % V7X_LANES
            sg_out[h] = _dot3_nt(eye_k, sgt_scr[g, hh * G_DV:(hh + 1) * G_DV, c0:c0 + G_DK])
        for p in range(M_HEADS // 2):
            c_out[2 * p] = cp_scr[p, 0:HEAD_W, 0:HEAD_W]
            c_out[2 * p + 1] = cp_scr[p, HEAD_W:, HEAD_W:]
        for h in range(M_HEADS):
            n_out[h:h + 1, :] = n_scr[0:1, h * M_DK:(h + 1) * M_DK]
        m_out[...] = m_scr[0:1, SM_I:SM_I + M_HEADS]
        conv_out[...] = conv_scr[5:8, :]


def _prompt_tables(T, L, CS):
    half = R_DK // 2
    inv = ROPE_BASE ** (-jnp.arange(half, dtype=F32) * 2.0 / R_DK)
    pos = jnp.arange(T, dtype=F32)
    ang = pos[:, None] * inv[None, :]
    cos = jnp.tile(jnp.cos(ang), (1, V7X_LANES // half))
    sin_h = jnp.sin(ang)
    sin = jnp.tile(jnp.concatenate([-sin_h, sin_h], axis=1), (1, V7X_LANES // R_DK))
    qs = R_DK ** -0.5

    log_gamma = jnp.log(1.0 - 2.0 ** (-5.0 - jnp.arange(R_HEADS, dtype=F32)))
    idx = jnp.arange(L, dtype=F32)
    rel = idx[:, None] - idx[None, :]
    causal = rel >= 0
    rdecay = jnp.where(causal[None], jnp.exp(log_gamma[:, None, None] * jnp.where(causal, rel, 0.0)[None]), 0.0)
    rdec = jnp.concatenate([rdecay[0::2], rdecay[1::2]], axis=2)
    qdec = jnp.repeat(jnp.exp(log_gamma[:, None] * (idx + 1.0)).T, R_DK, axis=1)
    kdec = jnp.repeat(jnp.exp(log_gamma[:, None] * (L - 1.0 - idx)).T, R_DK, axis=1)
    lg32 = np.log(1.0 - 2.0 ** (-5.0 - np.arange(R_HEADS, dtype=np.float64))).astype(np.float32)
    chunk_decay = tuple(float(np.exp(v * np.float32(L))) for v in lg32)

    r = np.arange(L)
    tri = (r[None, :] <= r[:, None])
    same = (r[None, :] // CS) == (r[:, None] // CS)
    selr = np.zeros((16, V7X_LANES), np.float32)
    for h in range(M_HEADS):
        selr[h, SM_I + h] = 1.0
        selr[M_HEADS + h, SM_F + h] = 1.0
    hv = np.arange(G_W) // G_DV
    hc = np.arange(G_QK) // G_DK
    gmask = (hv[:, None] == hc[None, :]).astype(np.float32)
    hmat = (hv[:, None] == hv[None, :]).astype(np.float32)
    pm = np.arange(V7X_LANES) // HEAD_W
    pmask = (pm[:, None] == pm[None, :]).astype(np.float32)
    return dict(
        cosq=cos * qs, sinq=sin * qs, cosk=cos, sink=sin, rdec=rdec, qdec=qdec, kdec=kdec, chunk_decay=chunk_decay,
        tri=jnp.asarray(tri, BF16), btri=jnp.asarray(tri & same, BF16), bones=jnp.asarray(same, BF16),
        selr=jnp.asarray(selr, BF16), hmat=jnp.asarray(hmat, BF16), emat=jnp.asarray(gmask.T, BF16),
        pmask=jnp.asarray(pmask, F32))


def mixer_prompt(proj, tabs, lw, l_out, depth, prev, *, L, CS):
    B, T, _ = proj.shape
    NC = T // L
    const2 = lambda b, c: (0, 0)
    const3 = lambda b, c: (0, 0, 0)
    tspec = pl.BlockSpec((L, V7X_LANES), lambda b, c: (c, 0))
    in_specs = [
        pl.BlockSpec((None, L, D_IN_PAD), lambda b, c: (b, c, 0)),
        tspec, tspec, tspec, tspec,
        pl.BlockSpec((R_HEADS // 2, L, 2 * L), const3),
        pl.BlockSpec((L, R_W), const2),
        pl.BlockSpec((L, R_W), const2),
        pl.BlockSpec((L, L), const2),
        pl.BlockSpec((L, L), const2),
        pl.BlockSpec((L, L), const2),
        pl.BlockSpec((16, V7X_LANES), const2),
        pl.BlockSpec((G_W, G_W), const2),
        pl.BlockSpec((G_QK, G_W), const2),
        pl.BlockSpec((V7X_LANES, V7X_LANES), const2),
        pl.BlockSpec((V7X_LANES, G_QK), const2),
        pl.BlockSpec((1, G_QK), const2),
        pl.BlockSpec((CONV_W, M_QK), const2),
        pl.BlockSpec((1, M_QK), const2),
        pl.BlockSpec((1, V7X_LANES), const2),
        pl.BlockSpec((1, D_MIX), const2),
    ]
    args = [proj, tabs["cosq"], tabs["sinq"], tabs["cosk"], tabs["sink"], tabs["rdec"], tabs["qdec"], tabs["kdec"],
            tabs["tri"], tabs["btri"], tabs["bones"], tabs["selr"], tabs["hmat"], tabs["emat"],
            tabs["pmask"], lw["wga"], lw["bga"], lw["convw"], lw["convb"], lw["bsm"], lw["gcat"]]
    n_in = len(args)
    in_specs += [pl.BlockSpec(memory_space=pl.ANY)] * 6
    args += list(prev)
    aliases = {n_in + i: 1 + i for i in range(6)}

    def st_spec(dims):
        return pl.BlockSpec((None, None) + dims, lambda b, c: (l_out, b) + (0,) * len(dims))

    out_shape = (jax.ShapeDtypeStruct((B, T, D_MIX), BF16),) + tuple(
        jax.ShapeDtypeStruct((depth, B) + d, F32) for d in _STATE_DIMS)
    out_specs = (pl.BlockSpec((None, L, D_MIX), lambda b, c: (b, c, 0)),) + tuple(st_spec(d) for d in _STATE_DIMS)
    scratch = [
        pltpu.VMEM((R_HEADS // 2, V7X_LANES, V7X_LANES), F32),
        pltpu.VMEM((G_HEADS // 2, V7X_LANES, V7X_LANES), F32),
        pltpu.VMEM((M_HEADS // 2, V7X_LANES, V7X_LANES), F32),
        pltpu.VMEM((8, M_W), F32),
        pltpu.VMEM((8, V7X_LANES), F32),
        pltpu.VMEM((8 + L, M_QK), F32),
        pltpu.VMEM((L, G_W), F32),
    ]
    kern = functools.partial(_mixer_prompt_kernel, L=L, CS=CS, NC=NC, chunk_decay=tabs["chunk_decay"])
    outs = pl.pallas_call(
        kern, out_shape=out_shape, grid=(B, NC), in_specs=in_specs, out_specs=out_specs,
        scratch_shapes=scratch, input_output_aliases=aliases,
        compiler_params=_cparams(("parallel", "arbitrary")), name="mixer_prompt",
    )(*args)
    return outs[0], tuple(outs[1:])


NS = 16
ROWS = 128


def _seq_bcast(x, t, T):
    n, w = x.shape
    x3 = x.reshape(n // T, T, w)
    return jnp.broadcast_to(x3[:, t:t + 1, :], (n // T, T, w)).reshape(n, w)


def _mixer_sample_kernel(proj_ref, cosq_ref, sinq_ref, cosk_ref, sink_ref, rdec_ref, qdec_ref, kdec_ref,
                         tri_ref, segones_ref, selr_ref, hmat_ref, emat_ref, mseg_ref, msegt_ref,
                         wga_ref, bga_ref, convw_ref, convb_ref, bsm_ref, gcat_ref,
                         sr_ref, sg_ref, c_ref, n_ref, m_ref, conv_ref, *rest, T, chunk_decay, slot, depth):
    mix_ref = rest[-8]
    st_outs = rest[-7:-1]
    conv_scr = rest[-1]
    if slot is not None:
        for e in range(depth):
            if e != slot:
                for dst in st_outs:
                    dst[e] = jnp.zeros(dst.shape[1:], F32)
        st_outs = [o.at[slot] for o in st_outs]
    sr_out, sg_out, c_out, n_out, m_out, conv_out = st_outs
    L = ROWS
    ns = L // T
    lane = lax.broadcasted_iota(jnp.int32, (L, V7X_LANES), 1)
    lo = lane < HEAD_W
    hi_m = lane >= HEAD_W
    first_half = (lane % R_DK) < (R_DK // 2)
    row = lax.broadcasted_iota(jnp.int32, (L, L), 0)
    col = lax.broadcasted_iota(jnp.int32, (L, L), 1)
    segcausal = (row // T == col // T) & (col <= row)
    hmat = hmat_ref[...]
    mseg = mseg_ref[...]
    msegt = msegt_ref[...]

    def rope(x, cos_ref, sin_ref):
        cos = cos_ref[...]
        sin = sin_ref[...]
        parts = []
        for t in range(R_W // V7X_LANES):
            xs = x[:, t * V7X_LANES:(t + 1) * V7X_LANES]
            rot = jnp.where(first_half, pltpu.roll(xs, V7X_LANES - R_DK // 2, 1), pltpu.roll(xs, R_DK // 2, 1))
            parts.append(xs * cos + rot * sin)
        return jnp.concatenate(parts, axis=1)

    def tile_lanes(x, n):
        return jnp.concatenate([x] * n, axis=1)

    def tile_rows(x, n):
        return jnp.concatenate([x] * n, axis=0)

    def pair_state_terms(qs, ks_f32, vs, st_ref, p, hd):
        r = st_ref[:, 2 * p:2 * p + 2].reshape(ns * 2 * hd, hd).astype(BF16)
        kt = tile_rows(ks_f32.T.astype(BF16), ns) * msegt
        inter, kv = [], []
        for hh, m in ((0, lo), (1, hi_m)):
            qh = jnp.where(m, qs, jnp.zeros_like(qs))
            inter.append(_dot(tile_lanes(qh, ns) * mseg, r))
            kv.append(_dot(kt, vs[:, hh * hd:(hh + 1) * hd]))
        return inter, kv

    q_r = rope(proj_ref[:, C_QR:C_QR + R_W], cosq_ref, sinq_ref)
    k_r = rope(proj_ref[:, C_KR:C_KR + R_W], cosk_ref, sink_ref)
    q_rb = q_r.astype(BF16)
    k_rb = k_r.astype(BF16)
    kd_r = k_r * kdec_ref[...]
    v_rb = proj_ref[:, C_VR:C_VR + R_W].astype(BF16)
    o_parts = []
    for p in range(R_HEADS // 2):
        sl = slice(p * V7X_LANES, (p + 1) * V7X_LANES)
        qs, ks, vs = q_rb[:, sl], k_rb[:, sl], v_rb[:, sl]
        sc = _dot_nt(qs, _stack_masked(ks, lo, hi_m)) * rdec_ref[p]
        inter, kv = pair_state_terms(qs, kd_r[:, sl], vs, sr_ref, p, R_DK)
        o = (_dot(sc.astype(BF16), _stack_masked(vs, lo, hi_m))
             + jnp.concatenate(inter, axis=1) * qdec_ref[:, sl])
        o_parts.append(o)
        for hh in range(2):
            h = 2 * p + hh
            sr_out[:, h] = sr_ref[:, h] * chunk_decay[h] + kv[hh].reshape(ns, 2, R_DK, R_DV)[:, hh]
    gate_r = _silu(proj_ref[:, C_GR:C_GR + R_W])
    mix_ref[:, 0:R_W] = _norm_gate(jnp.concatenate(o_parts, axis=1), hmat, gcat_ref[:, 0:R_W], gate_r)

    small = proj_ref[:, C_SM:C_SM + V7X_LANES]
    z = _dot(small.astype(BF16), wga_ref[...]) + bga_ref[...]
    log_a = _log_sigmoid(z) / G_NORMALIZER
    b = _dot3(tri_ref[...], log_a)
    b_tot = _seq_bcast(b, T - 1, T)
    q_g = proj_ref[:, C_QG:C_QG + G_QK] * (G_DK ** -0.5)
    k_g = proj_ref[:, C_KG:C_KG + G_QK]
    v_g = proj_ref[:, C_VG:C_VG + G_W]
    v_gb = v_g.astype(BF16)
    tok = lax.broadcasted_iota(jnp.int32, (L, G_QK), 0) % T
    emat = emat_ref[...]
    o_g = jnp.zeros((L, G_W), F32)
    for j in range(T):
        e = jnp.exp(jnp.where(tok >= j, b - _seq_bcast(b, j, T), -jnp.inf))
        tj = (e * q_g * _seq_bcast(k_g, j, T)).astype(BF16)
        o_g = o_g + _dot(tj, emat) * _seq_bcast(v_g, j, T)
    qt = (q_g * jnp.exp(b)).astype(BF16)
    kl = k_g * jnp.exp(b_tot - b)
    dtot = jnp.exp(b_tot)
    lane_h = lane // G_DK
    inter_parts = [None] * G_HEADS
    for h0, heads in ((0, (0, 1, 2, 3)), (2, (4, 5))):
        c0 = h0 * G_DK
        q_s = qt[:, c0:c0 + V7X_LANES]
        r = sg_ref[:, h0:h0 + 4].reshape(ns * V7X_LANES, G_DV)
        rb = r.astype(BF16)
        kt = tile_rows(kl[:, c0:c0 + V7X_LANES].T.astype(BF16), ns) * msegt
        dt = dtot[:, c0:c0 + V7X_LANES].T
        dcols = []
        for s in range(ns):
            dcols.append(jnp.broadcast_to(dt[:, s * T:s * T + 1], (V7X_LANES, G_DV)))
        dfull = jnp.concatenate(dcols, axis=0).reshape(ns, 4, G_DK, G_DV)
        for h in heads:
            qh = jnp.where(lane_h == h - h0, q_s, jnp.zeros_like(q_s))
            inter_parts[h] = _dot(tile_lanes(qh, ns) * mseg, rb)
            kv = _dot(kt, v_gb[:, h * G_DV:(h + 1) * G_DV])
            sg_out[:, h] = sg_ref[:, h] * dfull[:, h - h0] + kv.reshape(ns, 4, G_DK, G_DV)[:, h - h0]
    o_g = o_g + jnp.concatenate(inter_parts, axis=1)
    gate_g = _silu(proj_ref[:, C_RG:C_RG + G_W])
    mix_ref[:, R_W:R_W + G_W] = _norm_gate(o_g, hmat, gcat_ref[:, R_W:R_W + G_W], gate_g)

    u = proj_ref[:, C_QKM:C_QKM + M_QK]
    conv_scr[:, 5:8, :] = conv_ref[...]
    conv_scr[:, 8:8 + T, :] = u.reshape(ns, T, M_QK)
    y = convb_ref[...]
    for j in range(CONV_W - 1):
        y = y + conv_scr[:, 5 + j:5 + j + T, :].reshape(L, M_QK) * convw_ref[j:j + 1, :]
    y = y + u * convw_ref[CONV_W - 1:CONV_W, :]
    conv_out[...] = conv_scr[:, 5 + T:8 + T, :]
    qk = _silu(y)
    q_m = qk[:, :M_W]
    k_m = qk[:, M_W:] * (M_DK ** -0.5)
    q_mb = q_m.astype(BF16)
    k_mb = k_m.astype(BF16)
    v_mb = proj_ref[:, C_VM:C_VM + M_W].astype(BF16)
    gates = small + bsm_ref[...]
    f_cum = _dot3(tri_ref[...], _log_sigmoid(gates))
    i_rows = _dot3_nt(selr_ref[...], gates)
    f_rows = _dot3_nt(selr_ref[...], f_cum)
    head_lane = (lane >= SM_I) & (lane < SM_I + M_HEADS)
    fc = jnp.where(head_lane, pltpu.roll(f_cum, V7X_LANES - (SM_F - SM_I), 1), 0.0)
    m_prev = m_ref[...]
    a = fc + m_prev
    mx = jnp.full((L, V7X_LANES), -jnp.inf, F32)
    dms = []
    for h in range(M_HEADS):
        dm = jnp.where(segcausal, (fc[:, SM_I + h:SM_I + h + 1] - f_rows[M_HEADS + h:M_HEADS + h + 1, :])
                       + i_rows[h:h + 1, :], -jnp.inf)
        dms.append(dm)
        mx = jnp.where(lane == SM_I + h, jnp.max(dm, axis=-1, keepdims=True), mx)
    m_tok = jnp.maximum(a, mx)
    w_inter = jnp.exp(a - m_tok)
    n_rows = n_ref[...]
    hs, lows = _split2(q_m * n_rows)
    qn_b = _dot(hs, hmat_ref[0:M_W, 0:M_W]) + _dot(lows, hmat_ref[0:M_W, 0:M_W])
    rs_all = jnp.zeros((L, V7X_LANES), F32)
    qn_all = jnp.zeros((L, V7X_LANES), F32)
    scs = []
    for p in range(M_HEADS // 2):
        sl = slice(p * V7X_LANES, (p + 1) * V7X_LANES)
        sc = _dot_nt(q_mb[:, sl], _stack_masked(k_mb[:, sl], lo, hi_m))
        mt_a = jnp.broadcast_to(m_tok[:, SM_I + 2 * p:SM_I + 2 * p + 1], (L, L))
        mt_b = jnp.broadcast_to(m_tok[:, SM_I + 2 * p + 1:SM_I + 2 * p + 2], (L, L))
        sc_a = sc[:, :L] * jnp.exp(dms[2 * p] - mt_a)
        sc_b = sc[:, L:] * jnp.exp(dms[2 * p + 1] - mt_b)
        rs_all = jnp.where(lane == SM_I + 2 * p, jnp.sum(sc_a, axis=-1, keepdims=True), rs_all)
        rs_all = jnp.where(lane == SM_I + 2 * p + 1, jnp.sum(sc_b, axis=-1, keepdims=True), rs_all)
        qn_all = jnp.where(lane == SM_I + 2 * p, qn_b[:, p * V7X_LANES:p * V7X_LANES + 1], qn_all)
        qn_all = jnp.where(lane == SM_I + 2 * p + 1, qn_b[:, p * V7X_LANES + HEAD_W:p * V7X_LANES + HEAD_W + 1], qn_all)
        scs.append(jnp.concatenate([sc_a, sc_b], axis=1).astype(BF16))
    den = rs_all + qn_all * w_inter
    inv = 1.0 / jnp.maximum(jnp.abs(den), jnp.exp(-m_tok))
    m_new = _seq_bcast(jnp.where(head_lane, m_tok, 0.0), T - 1, T)
    wk = jnp.exp(_seq_bcast(fc, T - 1, T) - fc + gates - m_new)
    scale = jnp.exp(_seq_bcast(a, T - 1, T) - m_new)
    h_parts = []
    kw_parts = []
    scale_parts = []
    for p in range(M_HEADS // 2):
        sl = slice(p * V7X_LANES, (p + 1) * V7X_LANES)
        ia, ib = SM_I + 2 * p, SM_I + 2 * p + 1
        kw = k_m[:, sl] * _lane_pick(wk, ia, ib, lo)
        kw_parts.append(kw)
        inter, kv = pair_state_terms(q_mb[:, sl], kw, v_mb[:, sl], c_ref, p, M_DK)
        num = (_dot(scs[p], _stack_masked(v_mb[:, sl], lo, hi_m))
               + jnp.concatenate(inter, axis=1) * _lane_pick(w_inter, ia, ib, lo))
        h_parts.append(num * _lane_pick(inv, ia, ib, lo))
        scale_parts.append(_lane_pick(scale, ia, ib, lo))
        for hh in range(2):
            h = 2 * p + hh
            sc_rows = jnp.broadcast_to(scale[:, SM_I + h:SM_I + h + 1], (L, M_DV)).reshape(ns, T, M_DV)
            sc_h = jnp.broadcast_to(sc_rows[:, 0:1, :], (ns, M_DK, M_DV))
            c_out[:, h] = c_ref[:, h] * sc_h + kv[hh].reshape(ns, 2, M_DK, M_DV)[:, hh]
    kw_all = jnp.concatenate(kw_parts, axis=1)
    n_out[...] = n_rows * jnp.concatenate(scale_parts, axis=1) + _dot3(segones_ref[...], kw_all)
    m_out[...] = m_new
    gate_m = _sigmoid(proj_ref[:, C_OM:C_OM + M_W])
    mix_ref[:, R_W + G_W:D_MIX] = _norm_gate(jnp.concatenate(h_parts, axis=1), hmat_ref[0:M_W, 0:M_W],
                                             gcat_ref[:, R_W + G_W:D_MIX], gate_m)


def _sample_tables(T, pos0):
    L = ROWS
    half = R_DK // 2
    inv = ROPE_BASE ** (-jnp.arange(half, dtype=F32) * 2.0 / R_DK)
    tok = np.arange(L) % T
    seq = np.arange(L) // T
    pos = pos0 + jnp.asarray(tok, F32)
    ang = pos[:, None] * inv[None, :]
    cos = jnp.tile(jnp.cos(ang), (1, V7X_LANES // half))
    sin_h = jnp.sin(ang)
    sin = jnp.tile(jnp.concatenate([-sin_h, sin_h], axis=1), (1, V7X_LANES // R_DK))
    qs = R_DK ** -0.5

    log_gamma = jnp.log(1.0 - 2.0 ** (-5.0 - jnp.arange(R_HEADS, dtype=F32)))
    tf = jnp.asarray(tok, F32)
    rel = tf[:, None] - tf[None, :]
    ok_np = (seq[:, None] == seq[None, :]) & (tok[None, :] <= tok[:, None])
    ok = jnp.asarray(ok_np)
    rdecay = jnp.where(ok[None], jnp.exp(log_gamma[:, None, None] * jnp.where(ok, rel, 0.0)[None]), 0.0)
    rdec = jnp.concatenate([rdecay[0::2], rdecay[1::2]], axis=2)
    qdec = jnp.repeat(jnp.exp(log_gamma[:, None] * (tf + 1.0)).T, R_DK, axis=1)
    kdec = jnp.repeat(jnp.exp(log_gamma[:, None] * (T - 1.0 - tf)).T, R_DK, axis=1)
    lg32 = np.log(1.0 - 2.0 ** (-5.0 - np.arange(R_HEADS, dtype=np.float64))).astype(np.float32)
    chunk_decay = tuple(float(np.exp(v * np.float32(T))) for v in lg32)

    selr = np.zeros((16, V7X_LANES), np.float32)
    for h in range(M_HEADS):
        selr[h, SM_I + h] = 1.0
        selr[M_HEADS + h, SM_F + h] = 1.0
    hv = np.arange(G_W) // G_DV
    hc = np.arange(G_QK) // G_DK
    gmask = (hv[:, None] == hc[None, :]).astype(np.float32)
    hmat = (hv[:, None] == hv[None, :]).astype(np.float32)
    ns = L // T
    mseg = (seq[:, None] == (np.arange(ns * V7X_LANES) // V7X_LANES)[None, :]).astype(np.float32)
    return dict(
        cosq=cos * qs, sinq=sin * qs, cosk=cos, sink=sin, rdec=rdec, qdec=qdec, kdec=kdec, chunk_decay=chunk_decay,
        tri=jnp.asarray(ok_np, BF16), segones=jnp.asarray(seq[:, None] == seq[None, :], BF16),
        selr=jnp.asarray(selr, BF16), hmat=jnp.asarray(hmat, BF16),
        emat=jnp.asarray(gmask.T, BF16), mseg=jnp.asarray(mseg, BF16), msegt=jnp.asarray(mseg.T, BF16))


_SAMPLE_STATE_DIMS = ((R_HEADS, R_DK, R_DV), (G_HEADS, G_DK, G_DV), (M_HEADS, M_DK, M_DV), (M_W,),
                      (V7X_LANES,), (CONV_W - 1, M_QK))


def mixer_sample(proj, tabs, lw, state, l, depth, prev, *, T):
    M = proj.shape[0]
    B = M // T
    assert B % NS == 0 and NS * T == ROWS
    L = ROWS
    const2 = lambda b: (0, 0)
    const3 = lambda b: (0, 0, 0)
    dims = [(NS,) + d for d in _SAMPLE_STATE_DIMS]
    dims[3], dims[4] = (L, M_W), (L, V7X_LANES)
    full = [(B,) + d for d in _SAMPLE_STATE_DIMS]
    full[3], full[4] = (M, M_W), (M, V7X_LANES)

    def spec(d, lead):
        tail = (0,) * (len(d) - 1)
        if lead == "all":
            return pl.BlockSpec((depth,) + d, lambda b: (0, b) + tail)
        return pl.BlockSpec((None,) + d, lambda b: (lead, b) + tail)

    in_specs = [
        pl.BlockSpec((L, D_IN_PAD), lambda b: (b, 0)),
        pl.BlockSpec((L, V7X_LANES), const2), pl.BlockSpec((L, V7X_LANES), const2),
        pl.BlockSpec((L, V7X_LANES), const2), pl.BlockSpec((L, V7X_LANES), const2),
        pl.BlockSpec((R_HEADS // 2, L, 2 * L), const3),
        pl.BlockSpec((L, R_W), const2),
        pl.BlockSpec((L, R_W), const2),
        pl.BlockSpec((L, L), const2),
        pl.BlockSpec((L, L), const2),
        pl.BlockSpec((16, V7X_LANES), const2),
        pl.BlockSpec((G_W, G_W), const2),
        pl.BlockSpec((G_QK, G_W), const2),
        pl.BlockSpec((L, NS * V7X_LANES), const2),
        pl.BlockSpec((NS * V7X_LANES, L), const2),
        pl.BlockSpec((V7X_LANES, G_QK), const2),
        pl.BlockSpec((1, G_QK), const2),
        pl.BlockSpec((CONV_W, M_QK), const2),
        pl.BlockSpec((1, M_QK), const2),
        pl.BlockSpec((1, V7X_LANES), const2),
        pl.BlockSpec((1, D_MIX), const2),
    ] + [spec(d, l) for d in dims]
    args = [proj, tabs["cosq"], tabs["sinq"], tabs["cosk"], tabs["sink"], tabs["rdec"], tabs["qdec"], tabs["kdec"],
            tabs["tri"], tabs["segones"], tabs["selr"], tabs["hmat"], tabs["emat"], tabs["mseg"], tabs["msegt"],
            lw["wga"], lw["bga"], lw["convw"], lw["convb"], lw["bsm"], lw["gcat"]] + list(state)
    aliases = {}
    if prev is not None:
        n_in = len(args)
        in_specs += [pl.BlockSpec(memory_space=pl.ANY)] * 6
        args += list(prev)
        aliases = {n_in + i: 1 + i for i in range(6)}
    out_shape = (jax.ShapeDtypeStruct((M, D_MIX), BF16),) + tuple(jax.ShapeDtypeStruct((depth,) + f, F32) for f in full)
    out_specs = (pl.BlockSpec((L, D_MIX), lambda b: (b, 0)),) + tuple(
        spec(d, "all" if prev is None else l) for d in dims)
    kern = functools.partial(_mixer_sample_kernel, T=T, chunk_decay=tabs["chunk_decay"],
                             slot=l if prev is None else None, depth=depth)
    outs = pl.pallas_call(
        kern, out_shape=out_shape, grid=(B // NS,), in_specs=in_specs, out_specs=out_specs,
        scratch_shapes=[pltpu.VMEM((NS, 8 + T, M_QK), F32)],
        input_output_aliases=aliases,
        compiler_params=_cparams(("parallel",)), name="mixer_sample",
    )(*args)
    return outs[0], tuple(outs[1:])


def _prep_weights(w_in, w_out, w_xq, w_xk, w_xv, w_xo, w_gate, w_up, w_down):
    a0 = 2 * R_HEADS * R_DK + 2 * R_W + 2 * G_QK + 2 * G_W
    m0 = a0 + G_RANK
    g0 = m0 + M_QK + 2 * M_W
    cast = lambda w: w.astype(BF16)
    wb = cast(w_in)
    pad = jnp.zeros(w_in.shape[:2] + (D_IN_PAD - C_SM - G_RANK - 2 * M_HEADS,), BF16)
    w_pad = jnp.concatenate([wb[..., :a0], wb[..., m0:g0], wb[..., g0:g0 + 2 * M_HEADS], wb[..., a0:m0], pad], axis=2)
    return dict(w_in=w_pad, w_out=cast(w_out), w_xq=cast(w_xq), w_xk=cast(w_xk), w_xv=cast(w_xv),
                w_xo=cast(w_xo), w_gate=cast(w_gate), w_up=cast(w_up), w_down=cast(w_down))


def _prep_layer(l, g_mix, w_ga2, b_ga, conv_w, conv_b, b_i, b_f, g_ret, g_gla, g_mlstm, g_xattn, g_mem, g_ffn):
    wga = jnp.zeros((V7X_LANES, G_QK), F32).at[SM_AG:SM_AG + G_RANK].set(w_ga2[l]).astype(BF16)
    bsm = (jnp.zeros((1, V7X_LANES), F32).at[0, SM_I:SM_I + M_HEADS].set(b_i[l])
           .at[0, SM_F:SM_F + M_HEADS].set(b_f[l]))
    return dict(
        g_mix=g_mix[l], wga=wga, bga=b_ga[l].reshape(1, G_QK), convw=conv_w[l],
        convb=conv_b[l].reshape(1, M_QK), bsm=bsm,
        gcat=jnp.concatenate([g_ret[l], g_gla[l], g_mlstm[l]]).reshape(1, D_MIX),
        g_xattn=g_xattn[l], g_mem=g_mem[l], g_ffn=g_ffn[l])


def _layer(x, mix_fn, attend, W, lw, l, g_final, *, tm, final_norm):
    B, T, D = x.shape
    M = B * T
    x2 = x.reshape(M, D)
    proj = rms_matmul(x2, lw["g_mix"], W["w_in"], l, tm=tm)
    mix, new_state = mix_fn(proj)
    x2 = attend(mix, x2)
    x2 = swiglu_res(x2, lw["g_ffn"], W["w_gate"], W["w_up"], W["w_down"], l, g_final,
                    tm=tm, final_norm=final_norm)
    return x2.reshape(B, T, D), new_state


def kernel(x_prompt, x_sample, state_ret, state_gla, state_mlstm_C, state_mlstm_n, state_mlstm_m, state_mlstm_conv, cache_mem_k, cache_mem_v, mem_prompt, g_mix, w_in, w_ga2, b_ga, conv_w, conv_b, b_i, b_f, g_ret, g_gla, g_mlstm, w_out, g_xattn, g_mem, w_xq, w_xk, w_xv, w_xo, g_ffn, w_gate, w_up, w_down, g_final):
    B, T, D = x_prompt.shape
    Bs, Ts, _ = x_sample.shape
    depth = w_in.shape[0]
    assert T % CHUNK == 0 and Ts * NS == ROWS and Bs % NS == 0
    tabs_p = _prompt_tables(T, CHUNK, 16)
    tabs_s = _sample_tables(Ts, float(PAST_LEN))
    tm_p = 512 if (B * T) % 512 == 0 else B * T
    tm_s = 512 if (Bs * Ts) % 512 == 0 else Bs * Ts
    tq_p = 512 if T % 512 == 0 else T
    nb_x = 8 if Bs % 8 == 0 else 1

    sample_state = (state_ret, state_gla, state_mlstm_C,
                    jnp.repeat(state_mlstm_n.reshape(depth, Bs, M_W), Ts, axis=1),
                    jnp.repeat(jnp.pad(state_mlstm_m, ((0, 0), (0, 0), (0, V7X_LANES - M_HEADS))), Ts, axis=1),
                    state_mlstm_conv)
    p_st = tuple(jnp.zeros((depth, B) + d, F32) for d in _STATE_DIMS)
    s_st = p_mem = None
    hp, hs = x_prompt, x_sample
    W = _prep_weights(w_in, w_out, w_xq, w_xk, w_xv, w_xo, w_gate, w_up, w_down)
    for l in range(depth):
        lw = _prep_layer(l, g_mix, w_ga2, b_ga, conv_w, conv_b, b_i, b_f, g_ret, g_gla, g_mlstm,
                         g_xattn, g_mem, g_ffn)
        last = l == depth - 1
        k5, v5, kb, vb = memory_kv(mem_prompt, lw["g_mem"], W["w_xk"], W["w_xv"], l, depth, p_mem)
        p_mem = (k5, v5)

        def mix_p(proj):
            mix, st = mixer_prompt(proj.reshape(B, T, D_IN_PAD), tabs_p, lw, l, depth, p_st, L=CHUNK, CS=16)
            return mix.reshape(B * T, D_MIX), st

        def attend_p(mix, x):
            y = post_mix(mix.reshape(B, T, D_MIX), x.reshape(B, T, D), kb, vb,
                         W["w_out"], lw["g_xattn"], W["w_xq"], W["w_xo"], l, tq=tq_p)
            return y.reshape(B * T, D)

        hp, p_st = _layer(hp, mix_p, attend_p, W, lw, l, g_final, tm=tm_p, final_norm=last)
        hs, s_st = _layer(hs, lambda proj: mixer_sample(proj, tabs_s, lw, sample_state, l, depth, s_st, T=Ts),
                          lambda mix, x: post_mix_cache(mix, x, cache_mem_k, cache_mem_v, W["w_out"],
                                                        lw["g_xattn"], W["w_xq"], W["w_xo"], l, nb=nb_x, T=Ts),
                          W, lw, l, g_final, tm=tm_s, final_norm=last)

    p_out = p_st[:4] + (p_st[4].reshape(depth, B, M_HEADS), p_st[5])
    s_out = s_st[:3] + (s_st[3][:, ::Ts].reshape(depth, Bs, M_HEADS, M_DK), s_st[4][:, ::Ts, :M_HEADS], s_st[5])
    return (hp, hs, *p_out, *p_mem, *s_out)
```

```python
import functools

import numpy as np
import jax
import jax.numpy as jnp
from jax import lax
from jax.experimental import pallas as pl
from jax.experimental.pallas import tpu as pltpu

F32 = jnp.float32
BF16 = jnp.bfloat16

D_MODEL = 1024
PAST_LEN = 16384
R_HEADS, R_DK, R_DV = 6, 64, 64
G_HEADS, G_DK, G_DV, G_RANK = 6, 32, 64, 16
G_NORMALIZER = 16.0
M_HEADS, M_DK, M_DV = 4, 64, 64
CONV_W = 4
X_HEADS = 4
X_HD = D_MODEL // X_HEADS
N_MEM = 256
CHUNK = 256
EPS = 1e-6
ROPE_BASE = 10000.0

R_W = R_HEADS * R_DV
G_QK = G_HEADS * G_DK
G_W = G_HEADS * G_DV
M_QK = 2 * M_HEADS * M_DK
M_W = M_HEADS * M_DV
D_MIX = R_W + G_W + M_W

C_QR, C_KR, C_VR, C_GR = 0, 384, 768, 1152
C_QG, C_KG, C_VG, C_RG = 1536, 1728, 1920, 2304
C_QKM, C_VM, C_OM, C_SM = 2688, 3200, 3456, 3712
D_IN_PAD = 3840
SM_I, SM_F, SM_AG = 0, 4, 8
HEAD_W = 64
GLA_SAFE_LOG_RANGE = 60.0

V7X_LANES = 128
VMEM_LIMIT = 56 * 1024 * 1024


def _cparams(sem):
    return pltpu.CompilerParams(dimension_semantics=sem, vmem_limit_bytes=VMEM_LIMIT)


def _sigmoid(x):
    return 1.0 / (1.0 + jnp.exp(-x))


def _silu(x):
    return x * _sigmoid(x)


def _log_sigmoid(x):
    return jnp.minimum(x, 0.0) - jnp.log(1.0 + jnp.exp(-jnp.abs(x)))


def _dot(a, b):
    return jnp.dot(a, b, preferred_element_type=F32)


def _dot_nt(a, b):
    return lax.dot_general(a, b, (((1,), (1,)), ((), ())), preferred_element_type=F32)


def _dot_tn(a, b):
    return lax.dot_general(a, b, (((0,), (0,)), ((), ())), preferred_element_type=F32)


def _split3(x):
    hi = x.astype(BF16)
    r1 = x - hi.astype(F32)
    mid = r1.astype(BF16)
    lo = (r1 - mid.astype(F32)).astype(BF16)
    return hi, mid, lo


def _dot3(a, x):
    hi, mid, lo = _split3(x)
    return _dot(a, hi) + _dot(a, mid) + _dot(a, lo)


def _dot3_nt(a, x):
    hi, mid, lo = _split3(x)
    return _dot_nt(a, hi) + _dot_nt(a, mid) + _dot_nt(a, lo)


def _rms(x, g):
    return x * lax.rsqrt(jnp.mean(x * x, axis=-1, keepdims=True) + EPS) * g


def _wspec(w, l):
    return pl.BlockSpec((None,) + w.shape[1:], lambda *_: (l, 0, 0))


def _rms_matmul_kernel(x_ref, g_ref, w_ref, o_ref):
    o_ref[...] = _dot(_rms(x_ref[...], g_ref[...]).astype(BF16), w_ref[...]).astype(o_ref.dtype)


def rms_matmul(x, g, w, l, *, tm, out_dtype=F32):
    M, D = x.shape
    N = w.shape[2]
    assert M % tm == 0
    return pl.pallas_call(
        _rms_matmul_kernel,
        out_shape=jax.ShapeDtypeStruct((M, N), out_dtype),
        grid=(M // tm,),
        in_specs=[pl.BlockSpec((tm, D), lambda i: (i, 0)),
                  pl.BlockSpec((1, D), lambda i: (0, 0)),
                  _wspec(w, l)],
        out_specs=pl.BlockSpec((tm, N), lambda i: (i, 0)),
        compiler_params=_cparams(("parallel",)),
        name="rms_matmul",
    )(x, g.reshape(1, D), w)


def _swiglu_kernel(x_ref, g_ref, wg_ref, wu_ref, wd_ref, gf_ref, o_ref, *, final_norm):
    x = x_ref[...]
    xn = _rms(x, g_ref[...]).astype(BF16)
    h = _silu(_dot(xn, wg_ref[...])) * _dot(xn, wu_ref[...])
    y = x + _dot(h.astype(BF16), wd_ref[...])
    if final_norm:
        y = _rms(y, gf_ref[...])
    o_ref[...] = y


def swiglu_res(x, g, wg, wu, wd, l, g_final, *, tm, final_norm):
    M, D = x.shape
    assert M % tm == 0
    row = pl.BlockSpec((tm, D), lambda i: (i, 0))
    vec = pl.BlockSpec((1, D), lambda i: (0, 0))
    return pl.pallas_call(
        functools.partial(_swiglu_kernel, final_norm=final_norm),
        out_shape=jax.ShapeDtypeStruct((M, D), F32),
        grid=(M // tm,),
        in_specs=[row, vec, _wspec(wg, l), _wspec(wu, l), _wspec(wd, l), vec],
        out_specs=row,
        compiler_params=_cparams(("parallel",)),
        name="swiglu_res",
    )(x, g.reshape(1, D), wg, wu, wd, g_final.reshape(1, D))


def _memkv_kernel(x_ref, g_ref, wk_ref, wv_ref, *refs, slot, depth):
    k5_ref, v5_ref, kb_ref, vb_ref = refs[-4:]
    if slot is not None:
        for e in range(depth):
            if e != slot:
                k5_ref[e] = jnp.zeros(k5_ref.shape[1:], F32)
                v5_ref[e] = jnp.zeros(v5_ref.shape[1:], F32)
        k5_ref, v5_ref = k5_ref.at[slot], v5_ref.at[slot]
    xn = _rms(x_ref[...], g_ref[...]).astype(BF16)
    for w_ref, o5_ref, ob_ref in ((wk_ref, k5_ref, kb_ref), (wv_ref, v5_ref, vb_ref)):
        y = _dot(xn, w_ref[...])
        ob_ref[...] = y.astype(BF16)
        for h in range(X_HEADS):
            o5_ref[:, h, :] = y[:, h * X_HD:(h + 1) * X_HD]


def memory_kv(mem, g, wk, wv, l, depth, prev):
    B, _, D = mem.shape
    in_specs = [pl.BlockSpec((None, N_MEM, D), lambda b: (b, 0, 0)),
                pl.BlockSpec((1, D), lambda b: (0, 0)),
                _wspec(wk, l), _wspec(wv, l)]
    args = [mem, g.reshape(1, D), wk, wv]
    aliases = {}
    if prev is None:
        spec5 = pl.BlockSpec((depth, None, N_MEM, X_HEADS, X_HD), lambda b: (0, b, 0, 0, 0))
    else:
        spec5 = pl.BlockSpec((None, None, N_MEM, X_HEADS, X_HD), lambda b: (l, b, 0, 0, 0))
        in_specs += [pl.BlockSpec(memory_space=pl.ANY)] * 2
        args += list(prev)
        aliases = {4: 0, 5: 1}
    o5 = jax.ShapeDtypeStruct((depth, B, N_MEM, X_HEADS, X_HD), F32)
    ob = jax.ShapeDtypeStruct((B, N_MEM, D), BF16)
    specb = pl.BlockSpec((None, N_MEM, D), lambda b: (b, 0, 0))
    return pl.pallas_call(
        functools.partial(_memkv_kernel, slot=l if prev is None else None, depth=depth),
        out_shape=(o5, o5, ob, ob),
        grid=(B,),
        in_specs=in_specs,
        out_specs=(spec5, spec5, specb, specb),
        input_output_aliases=aliases,
        compiler_params=_cparams(("parallel",)),
        name="memory_kv",
    )(*args)


def _out_and_query(a_ref, x_ref, wo_ref, g_ref, wq_ref):
    x1 = x_ref[...] + _dot(a_ref[...], wo_ref[...])
    q = _dot(_rms(x1, g_ref[...]).astype(BF16), wq_ref[...]).astype(BF16)
    return x1, q


def _post_mix_kernel(a_ref, x_ref, k_ref, v_ref, wo_ref, g_ref, wq_ref, wxo_ref, o_ref):
    x1, q = _out_and_query(a_ref, x_ref, wo_ref, g_ref, wq_ref)
    scale = X_HD ** -0.5
    parts = []
    for h in range(X_HEADS):
        sl = slice(h * X_HD, (h + 1) * X_HD)
        s = _dot_nt(q[:, sl], k_ref[:, sl]) * scale
        p = jnp.exp(s - jnp.max(s, axis=-1, keepdims=True))
        l = jnp.sum(p, axis=-1, keepdims=True)
        parts.append((_dot(p.astype(BF16), v_ref[:, sl]) / l).astype(BF16))
    o_ref[...] = x1 + _dot(jnp.concatenate(parts, axis=1), wxo_ref[...])


def post_mix(a, x, mk, mv, wo, g, wq, wxo, l, *, tq):
    B, T, D = x.shape
    assert T % tq == 0
    tok = lambda w: pl.BlockSpec((None, tq, w), lambda b, i: (b, i, 0))
    mem = pl.BlockSpec((None, N_MEM, D), lambda b, i: (b, 0, 0))
    return pl.pallas_call(
        _post_mix_kernel,
        out_shape=jax.ShapeDtypeStruct((B, T, D), F32),
        grid=(B, T // tq),
        in_specs=[tok(a.shape[2]), tok(D), mem, mem, _wspec(wo, l), pl.BlockSpec((1, D), lambda b, i: (0, 0)),
                  _wspec(wq, l), _wspec(wxo, l)],
        out_specs=tok(D),
        compiler_params=_cparams(("parallel", "arbitrary")),
        name="post_mix",
    )(a, x, mk, mv, wo, g.reshape(1, D), wq, wxo)


def _post_mix_cache_kernel(a_ref, x_ref, k_ref, v_ref, wo_ref, g_ref, wq_ref, wxo_ref, o_ref, *, nb, T):
    x1, q_all = _out_and_query(a_ref, x_ref, wo_ref, g_ref, wq_ref)
    R = X_HEADS * T
    rowh = lax.broadcasted_iota(jnp.int32, (R, N_MEM * X_HEADS), 0) // T
    colh = lax.broadcasted_iota(jnp.int32, (R, N_MEM * X_HEADS), 1) % X_HEADS
    own = rowh == colh
    outs = []
    for s in range(nb):
        q = q_all[s * T:(s + 1) * T]
        qf = jnp.concatenate([q[:, h * X_HD:(h + 1) * X_HD] for h in range(X_HEADS)], axis=0)
        kf = k_ref[s].reshape(N_MEM * X_HEADS, X_HD).astype(BF16)
        vf = v_ref[s].reshape(N_MEM * X_HEADS, X_HD).astype(BF16)
        sc = jnp.where(own, _dot_nt(qf, kf) * (X_HD ** -0.5), -jnp.inf)
        p = jnp.exp(sc - jnp.max(sc, axis=-1, keepdims=True))
        l = jnp.sum(p, axis=-1, keepdims=True)
        o = (_dot(p.astype(BF16), vf) / l).astype(BF16)
        outs.append(jnp.concatenate([o[h * T:(h + 1) * T] for h in range(X_HEADS)], axis=1))
    o_ref[...] = x1 + _dot(jnp.concatenate(outs, axis=0), wxo_ref[...])


def post_mix_cache(a, x, ck, cv, wo, g, wq, wxo, l, *, nb, T):
    M, D = x.shape
    rows = nb * T
    assert M % rows == 0
    cspec = pl.BlockSpec((None, nb, N_MEM, X_HEADS, X_HD), lambda b: (l, b, 0, 0, 0))
    tok = lambda w: pl.BlockSpec((rows, w), lambda b: (b, 0))
    return pl.pallas_call(
        functools.partial(_post_mix_cache_kernel, nb=nb, T=T),
        out_shape=jax.ShapeDtypeStruct((M, D), F32),
        grid=(M // rows,),
        in_specs=[tok(a.shape[1]), tok(D), cspec, cspec, _wspec(wo, l), pl.BlockSpec((1, D), lambda b: (0, 0)),
                  _wspec(wq, l), _wspec(wxo, l)],
        out_specs=tok(D),
        compiler_params=_cparams(("parallel",)),
        name="post_mix_cache",
    )(a, x, ck, cv, wo, g.reshape(1, D), wq, wxo)


def _eye(n):
    r = lax.broadcasted_iota(jnp.int32, (n, n), 0)
    c = lax.broadcasted_iota(jnp.int32, (n, n), 1)
    return jnp.where(r == c, 1.0, 0.0).astype(BF16)


_STATE_DIMS = ((R_HEADS, R_DK, R_DV), (G_HEADS, G_DK, G_DV), (M_HEADS, M_DK, M_DV), (M_HEADS, M_DK),
               (1, M_HEADS), (CONV_W - 1, M_QK))


def _split2(x):
    hi = x.astype(BF16)
    lo = (x - hi.astype(F32)).astype(BF16)
    return hi, lo


def _head_mean_sq(o, hmat):
    hi, lo = _split2(o * o)
    return (_dot(hi, hmat) + _dot(lo, hmat)) * (1.0 / HEAD_W)


def _norm_gate(o, hmat, g_row, gate):
    return (o * lax.rsqrt(_head_mean_sq(o, hmat) + EPS) * g_row * gate).astype(BF16)


def _lane_pick(cols, idx_lo, idx_hi, lo_mask):
    L = cols.shape[0]
    a = jnp.broadcast_to(cols[:, idx_lo:idx_lo + 1], (L, V7X_LANES))
    b = jnp.broadcast_to(cols[:, idx_hi:idx_hi + 1], (L, V7X_LANES))
    return jnp.where(lo_mask, a, b)


def _stack_masked(x, m_a, m_b):
    z = jnp.zeros_like(x)
    return jnp.concatenate([jnp.where(m_a, x, z), jnp.where(m_b, x, z)], axis=0)


def _mixer_prompt_kernel(proj_ref, cosq_ref, sinq_ref, cosk_ref, sink_ref, rdec_ref, qdec_ref, kdec_ref,
                         tri_ref, btri_ref, bones_ref, selr_ref, hmat_ref, emat_ref, pmask_ref,
                         wga_ref, bga_ref, convw_ref, convb_ref, bsm_ref, gcat_ref, *rest,
                         L, CS, NC, chunk_decay):
    (mix_ref, sr_out, sg_out, c_out, n_out, m_out, conv_out,
     srp_scr, sgt_scr, cp_scr, n_scr, m_scr, conv_scr, og_scr) = rest[-14:]
    c = pl.program_id(1)

    @pl.when(c == 0)
    def _():
        srp_scr[...] = jnp.zeros_like(srp_scr)
        sgt_scr[...] = jnp.zeros_like(sgt_scr)
        cp_scr[...] = jnp.zeros_like(cp_scr)
        n_scr[...] = jnp.zeros_like(n_scr)
        m_scr[...] = jnp.zeros_like(m_scr)
        conv_scr[0:8, :] = jnp.zeros((8, M_QK), F32)

    lane = lax.broadcasted_iota(jnp.int32, (L, V7X_LANES), 1)
    lo = lane < HEAD_W
    hi_m = lane >= HEAD_W
    first_half = (lane % R_DK) < (R_DK // 2)
    row2 = lax.broadcasted_iota(jnp.int32, (L, 2 * L), 0)
    col2 = lax.broadcasted_iota(jnp.int32, (L, 2 * L), 1) % L
    causal2 = col2 <= row2
    lo_row = lax.broadcasted_iota(jnp.int32, (1, V7X_LANES), 1) < HEAD_W
    lo8 = lax.broadcasted_iota(jnp.int32, (8, V7X_LANES), 1) < HEAD_W
    hmat = hmat_ref[...]
    pmask = pmask_ref[...]

    def rope(x, cos_ref, sin_ref):
        cos = cos_ref[...]
        sin = sin_ref[...]
        parts = []
        for t in range(R_W // V7X_LANES):
            xs = x[:, t * V7X_LANES:(t + 1) * V7X_LANES]
            rot = jnp.where(first_half, pltpu.roll(xs, V7X_LANES - R_DK // 2, 1), pltpu.roll(xs, R_DK // 2, 1))
            parts.append(xs * cos + rot * sin)
        return jnp.concatenate(parts, axis=1)

    q_r = rope(proj_ref[:, C_QR:C_QR + R_W], cosq_ref, sinq_ref)
    k_r = rope(proj_ref[:, C_KR:C_KR + R_W], cosk_ref, sink_ref)
    q_rb = q_r.astype(BF16)
    k_rb = k_r.astype(BF16)
    kd_rb = (k_r * kdec_ref[...]).astype(BF16)
    v_rb = proj_ref[:, C_VR:C_VR + R_W].astype(BF16)
    o_parts = []
    for p in range(R_HEADS // 2):
        sl = slice(p * V7X_LANES, (p + 1) * V7X_LANES)
        qs, ks, vs = q_rb[:, sl], k_rb[:, sl], v_rb[:, sl]
        sc = _dot_nt(qs, _stack_masked(ks, lo, hi_m)) * rdec_ref[p]
        sp = srp_scr[p]
        o = _dot(sc.astype(BF16), _stack_masked(vs, lo, hi_m)) + _dot(qs, sp.astype(BF16)) * qdec_ref[:, sl]
        cd = jnp.where(lo_row, chunk_decay[2 * p], chunk_decay[2 * p + 1])
        srp_scr[p] = sp * cd + _dot_tn(kd_rb[:, sl], vs) * pmask
        o_parts.append(o)
    o_r = jnp.concatenate(o_parts, axis=1)
    gate_r = _silu(proj_ref[:, C_GR:C_GR + R_W])
    mix_ref[:, 0:R_W] = _norm_gate(o_r, hmat, gcat_ref[:, 0:R_W], gate_r)

    small = proj_ref[:, C_SM:C_SM + V7X_LANES]

    u = proj_ref[:, C_QKM:C_QKM + M_QK]
    conv_scr[8:8 + L, :] = u
    y = convb_ref[...]
    for j in range(CONV_W - 1):
        y = y + conv_scr[5 + j:5 + j + L, :] * convw_ref[j:j + 1, :]
    y = y + u * convw_ref[CONV_W - 1:CONV_W, :]
    tail = conv_scr[5 + L:8 + L, :]
    conv_scr[5:8, :] = tail
    qk = _silu(y)
    q_m = qk[:, :M_W]
    k_m = qk[:, M_W:] * (M_DK ** -0.5)
    q_mb = q_m.astype(BF16)
    k_mb = k_m.astype(BF16)
    v_mb = proj_ref[:, C_VM:C_VM + M_W].astype(BF16)
    gates = small + bsm_ref[...]
    f_cum = _dot3(tri_ref[...], _log_sigmoid(gates))
    i_rows = _dot3_nt(selr_ref[...], gates)
    f_rows = _dot3_nt(selr_ref[...], f_cum)
    head_lane = (lane >= SM_I) & (lane < SM_I + M_HEADS)
    fc = jnp.where(head_lane, pltpu.roll(f_cum, V7X_LANES - (SM_F - SM_I), 1), 0.0)
    m_prev = m_scr[0:1, :]
    a = fc + m_prev
    mx = jnp.full((L, V7X_LANES), -jnp.inf, F32)
    row = lax.broadcasted_iota(jnp.int32, (L, L), 0)
    col = lax.broadcasted_iota(jnp.int32, (L, L), 1)
    causal = col <= row
    dms = []
    for h in range(M_HEADS):
        dm = jnp.where(causal, (fc[:, SM_I + h:SM_I + h + 1] - f_rows[M_HEADS + h:M_HEADS + h + 1, :])
                       + i_rows[h:h + 1, :], -jnp.inf)
        dms.append(dm)
        mx = jnp.where(lane == SM_I + h, jnp.max(dm, axis=-1, keepdims=True), mx)
    m_tok = jnp.maximum(a, mx)
    w_inter = jnp.exp(a - m_tok)
    n_full = n_scr[0:1, :]
    hs, lows = _split2(q_m * n_full)
    qn_b = _dot(hs, hmat_ref[0:M_W, 0:M_W]) + _dot(lows, hmat_ref[0:M_W, 0:M_W])
    rs_all = jnp.zeros((L, V7X_LANES), F32)
    qn_all = jnp.zeros((L, V7X_LANES), F32)
    scs = []
    for p in range(M_HEADS // 2):
        sl = slice(p * V7X_LANES, (p + 1) * V7X_LANES)
        sc = _dot_nt(q_mb[:, sl], _stack_masked(k_mb[:, sl], lo, hi_m))
        mt_a = jnp.broadcast_to(m_tok[:, SM_I + 2 * p:SM_I + 2 * p + 1], (L, L))
        mt_b = jnp.broadcast_to(m_tok[:, SM_I + 2 * p + 1:SM_I + 2 * p + 2], (L, L))
        sc_a = sc[:, :L] * jnp.exp(dms[2 * p] - mt_a)
        sc_b = sc[:, L:] * jnp.exp(dms[2 * p + 1] - mt_b)
        rs_all = jnp.where(lane == SM_I + 2 * p, jnp.sum(sc_a, axis=-1, keepdims=True), rs_all)
        rs_all = jnp.where(lane == SM_I + 2 * p + 1, jnp.sum(sc_b, axis=-1, keepdims=True), rs_all)
        qn_all = jnp.where(lane == SM_I + 2 * p, qn_b[:, p * V7X_LANES:p * V7X_LANES + 1], qn_all)
        qn_all = jnp.where(lane == SM_I + 2 * p + 1, qn_b[:, p * V7X_LANES + HEAD_W:p * V7X_LANES + HEAD_W + 1], qn_all)
        scs.append(jnp.concatenate([sc_a, sc_b], axis=1).astype(BF16))
    den = rs_all + qn_all * w_inter
    inv = 1.0 / jnp.maximum(jnp.abs(den), jnp.exp(-m_tok))
    m_new = jnp.where(head_lane, m_tok, 0.0)[L - 1:L, :]
    wk = jnp.exp(fc[L - 1:L, :] - fc + gates - m_new)
    scale = jnp.broadcast_to(jnp.exp(a[L - 1:L, :] - m_new), (8, V7X_LANES))
    h_parts = []
    kw_parts = []
    scale_parts = []
    for p in range(M_HEADS // 2):
        sl = slice(p * V7X_LANES, (p + 1) * V7X_LANES)
        ia, ib = SM_I + 2 * p, SM_I + 2 * p + 1
        cpair = cp_scr[p]
        num = (_dot(scs[p], _stack_masked(v_mb[:, sl], lo, hi_m))
               + _dot(q_mb[:, sl], cpair.astype(BF16)) * _lane_pick(w_inter, ia, ib, lo))
        h_parts.append(num * _lane_pick(inv, ia, ib, lo))
        kw = k_m[:, sl] * _lane_pick(wk, ia, ib, lo)
        kw_parts.append(kw)
        sc_row = _lane_pick(scale, ia, ib, lo8)[0:1]
        scale_parts.append(sc_row)
        cp_scr[p] = cpair * sc_row + _dot_tn(kw.astype(BF16), v_mb[:, sl]) * pmask
    kw_all = jnp.concatenate(kw_parts, axis=1)
    n_new = n_full * jnp.concatenate(scale_parts, axis=1) + jnp.sum(kw_all, axis=0, keepdims=True)
    n_scr[...] = jnp.broadcast_to(n_new, n_scr.shape)
    m_scr[...] = jnp.broadcast_to(m_new, m_scr.shape)
    gate_m = _sigmoid(proj_ref[:, C_OM:C_OM + M_W])
    mix_ref[:, R_W + G_W:D_MIX] = _norm_gate(jnp.concatenate(h_parts, axis=1), hmat_ref[0:M_W, 0:M_W],
                                             gcat_ref[:, R_W + G_W:D_MIX], gate_m)

    z = _dot(small.astype(BF16), wga_ref[...]) + bga_ref[...]
    log_a = _log_sigmoid(z) / G_NORMALIZER
    b = _dot3(tri_ref[...], log_a)
    b_last = b[L - 1:L, :]
    safe = jnp.max(-b_last) <= GLA_SAFE_LOG_RANGE
    q_g = proj_ref[:, C_QG:C_QG + G_QK] * (G_DK ** -0.5)
    k_g = proj_ref[:, C_KG:C_KG + G_QK]
    v_gb = proj_ref[:, C_VG:C_VG + G_W].astype(BF16)

    slot = lax.broadcasted_iota(jnp.int32, (V7X_LANES, V7X_LANES), 1) // G_DK
    row_head = lax.broadcasted_iota(jnp.int32, (V7X_LANES, V7X_LANES), 0) // HEAD_W
    lane_q = lane // G_DK

    def pad2(x):
        return jnp.concatenate([x, jnp.zeros((x.shape[0], 2 * V7X_LANES - G_QK), x.dtype)], axis=1)

    def slab(x_p, g):
        s0 = (2 * g * G_DK) // V7X_LANES * V7X_LANES
        return x_p[:, s0:s0 + V7X_LANES], (2 * g * G_DK - s0) // G_DK

    def gla_inter(q_p):
        return jnp.concatenate([_dot_nt(slab(q_p, g)[0], sgt_scr[g].astype(BF16)) for g in range(G_HEADS // 2)],
                               axis=1)

    def gla_update(decay_p, k_p, v_b):
        for g in range(G_HEADS // 2):
            ks, h_a = slab(k_p, g)
            kv = _dot_tn(v_b[:, g * V7X_LANES:(g + 1) * V7X_LANES], ks)
            sgt_scr[g] = sgt_scr[g] * slab(decay_p, g)[0] + jnp.where(slot == h_a + row_head, kv, 0.0)

    @pl.when(safe)
    def _():
        qt_p = pad2((q_g * jnp.exp(b)).astype(BF16))
        kt_p = pad2((k_g * jnp.exp(-b)).astype(BF16))
        kl_p = pad2((k_g * jnp.exp(b_last - b)).astype(BF16))
        o_inter = gla_inter(qt_p)
        parts = []
        for g in range(G_HEADS // 2):
            qs, h_a = slab(qt_p, g)
            ks, _ = slab(kt_p, g)
            sc = _dot_nt(qs, _stack_masked(ks, lane_q == h_a, lane_q == h_a + 1))
            sc = jnp.where(causal2, sc, 0.0).astype(BF16)
            vs = v_gb[:, g * V7X_LANES:(g + 1) * V7X_LANES]
            parts.append(_dot(sc, _stack_masked(vs, lo, hi_m)))
        og_scr[...] = o_inter + jnp.concatenate(parts, axis=1)
        gla_update(pad2(jnp.exp(b_last)), kl_p, v_gb)

    @pl.when(jnp.logical_not(safe))
    def _():
        v_g = proj_ref[:, C_VG:C_VG + G_W]
        b_loc = _dot3(btri_ref[...], log_a)
        b_tot = _dot3(bones_ref[...], log_a)
        qt_p = pad2((q_g * jnp.exp(b_loc)).astype(BF16))
        kt_p = pad2((k_g * jnp.exp(b_tot - b_loc)).astype(BF16))
        d_tot_p = pad2(jnp.exp(b_tot))
        emat = emat_ref[...]
        sub_row = lax.broadcasted_iota(jnp.int32, (CS, G_QK), 0)
        for blk in range(L // CS):
            r0 = blk * CS
            rs = slice(r0, r0 + CS)
            o_blk = gla_inter(qt_p[rs])
            bI, qI, kI = b_loc[rs], q_g[rs], k_g[rs]
            terms = []
            for j in range(CS):
                e = jnp.exp(jnp.where(sub_row >= j, bI - bI[j:j + 1], -jnp.inf))
                terms.append(e * qI * kI[j:j + 1])
            t = jnp.concatenate(terms, axis=0).astype(BF16)
            w = _dot(t, emat)
            for j in range(CS):
                o_blk = o_blk + w[j * CS:(j + 1) * CS] * v_g[r0 + j:r0 + j + 1]
            og_scr[rs, :] = o_blk
            gla_update(d_tot_p[r0:r0 + 1], kt_p[rs], v_gb[rs])

    gate_g = _silu(proj_ref[:, C_RG:C_RG + G_W])
    mix_ref[:, R_W:R_W + G_W] = _norm_gate(og_scr[...], hmat, gcat_ref[:, R_W:R_W + G_W], gate_g)

    @pl.when(c == NC - 1)
    def _():
        eye_k = _eye(G_DK)
        for p in range(R_HEADS // 2):
            sr_out[2 * p] = srp_scr[p, 0:HEAD_W, 0:HEAD_W]
            sr_out[2 * p + 1] = srp_scr[p, HEAD_W:, HEAD_W:]
        for h in range(G_HEADS):
            g, hh = h // 2, h % 2
            c0 = (h * G_DK) % V7X_LANES
            sg_out[h] = _dot3_nt(eye_k, sgt_scr[g, hh * G_DV:(hh + 1) * G_DV, c0:c0 + G_DK])
        for p in range(M_HEADS // 2):
            c_out[2 * p] = cp_scr[p, 0:HEAD_W, 0:HEAD_W]
            c_out[2 * p + 1] = cp_scr[p, HEAD_W:, HEAD_W:]
        for h in range(M_HEADS):
            n_out[h:h + 1, :] = n_scr[0:1, h * M_DK:(h + 1) * M_DK]
        m_out[...] = m_scr[0:1, SM_I:SM_I + M_HEADS]
        conv_out[...] = conv_scr[5:8, :]


def _prompt_tables(T, L, CS):
    half = R_DK // 2
    inv = ROPE_BASE ** (-jnp.arange(half, dtype=F32) * 2.0 / R_DK)
    pos = jnp.arange(T, dtype=F32)
    ang = pos[:, None] * inv[None, :]
    cos = jnp.tile(jnp.cos(ang), (1, V7X_LANES // half))
    sin_h = jnp.sin(ang)
    sin = jnp.tile(jnp.concatenate([-sin_h, sin_h], axis=1), (1, V7X_LANES // R_DK))
    qs = R_DK ** -0.5

    log_gamma = jnp.log(1.0 - 2.0 ** (-5.0 - jnp.arange(R_HEADS, dtype=F32)))
    idx = jnp.arange(L, dtype=F32)
    rel = idx[:, None] - idx[None, :]
    causal = rel >= 0
    rdecay = jnp.where(causal[None], jnp.exp(log_gamma[:, None, None] * jnp.where(causal, rel, 0.0)[None]), 0.0)
    rdec = jnp.concatenate([rdecay[0::2], rdecay[1::2]], axis=2)
    qdec = jnp.repeat(jnp.exp(log_gamma[:, None] * (idx + 1.0)).T, R_DK, axis=1)
    kdec = jnp.repeat(jnp.exp(log_gamma[:, None] * (L - 1.0 - idx)).T, R_DK, axis=1)
    lg32 = np.log(1.0 - 2.0 ** (-5.0 - np.arange(R_HEADS, dtype=np.float64))).astype(np.float32)
    chunk_decay = tuple(float(np.exp(v * np.float32(L))) for v in lg32)

    r = np.arange(L)
    tri = (r[None, :] <= r[:, None])
    same = (r[None, :] // CS) == (r[:, None] // CS)
    selr = np.zeros((16, V7X_LANES), np.float32)
    for h in range(M_HEADS):
        selr[h, SM_I + h] = 1.0
        selr[M_HEADS + h, SM_F + h] = 1.0
    hv = np.arange(G_W) // G_DV
    hc = np.arange(G_QK) // G_DK
    gmask = (hv[:, None] == hc[None, :]).astype(np.float32)
    hmat = (hv[:, None] == hv[None, :]).astype(np.float32)
    pm = np.arange(V7X_LANES) // HEAD_W
    pmask = (pm[:, None] == pm[None, :]).astype(np.float32)
    return dict(
        cosq=cos * qs, sinq=sin * qs, cosk=cos, sink=sin, rdec=rdec, qdec=qdec, kdec=kdec, chunk_decay=chunk_decay,
        tri=jnp.asarray(tri, BF16), btri=jnp.asarray(tri & same, BF16), bones=jnp.asarray(same, BF16),
        selr=jnp.asarray(selr, BF16), hmat=jnp.asarray(hmat, BF16), emat=jnp.asarray(gmask.T, BF16),
        pmask=jnp.asarray(pmask, F32))


def mixer_prompt(proj, tabs, lw, l_out, depth, prev, *, L, CS):
    B, T, _ = proj.shape
    NC = T // L
    const2 = lambda b, c: (0, 0)
    const3 = lambda b, c: (0, 0, 0)
    tspec = pl.BlockSpec((L, V7X_LANES), lambda b, c: (c, 0))
    in_specs = [
        pl.BlockSpec((None, L, D_IN_PAD), lambda b, c: (b, c, 0)),
        tspec, tspec, tspec, tspec,
        pl.BlockSpec((R_HEADS // 2, L, 2 * L), const3),
        pl.BlockSpec((L, R_W), const2),
        pl.BlockSpec((L, R_W), const2),
        pl.BlockSpec((L, L), const2),
        pl.BlockSpec((L, L), const2),
        pl.BlockSpec((L, L), const2),
        pl.BlockSpec((16, V7X_LANES), const2),
        pl.BlockSpec((G_W, G_W), const2),
        pl.BlockSpec((G_QK, G_W), const2),
        pl.BlockSpec((V7X_LANES, V7X_LANES), const2),
        pl.BlockSpec((V7X_LANES, G_QK), const2),
        pl.BlockSpec((1, G_QK), const2),
        pl.BlockSpec((CONV_W, M_QK), const2),
        pl.BlockSpec((1, M_QK), const2),
        pl.BlockSpec((1, V7X_LANES), const2),
        pl.BlockSpec((1, D_MIX), const2),
    ]
    args = [proj, tabs["cosq"], tabs["sinq"], tabs["cosk"], tabs["sink"], tabs["rdec"], tabs["qdec"], tabs["kdec"],
            tabs["tri"], tabs["btri"], tabs["bones"], tabs["selr"], tabs["hmat"], tabs["emat"],
            tabs["pmask"], lw["wga"], lw["bga"], lw["convw"], lw["convb"], lw["bsm"], lw["gcat"]]
    n_in = len(args)
    in_specs += [pl.BlockSpec(memory_space=pl.ANY)] * 6
    args += list(prev)
    aliases = {n_in + i: 1 + i for i in range(6)}

    def st_spec(dims):
        return pl.BlockSpec((None, None) + dims, lambda b, c: (l_out, b) + (0,) * len(dims))

    out_shape = (jax.ShapeDtypeStruct((B, T, D_MIX), BF16),) + tuple(
        jax.ShapeDtypeStruct((depth, B) + d, F32) for d in _STATE_DIMS)
    out_specs = (pl.BlockSpec((None, L, D_MIX), lambda b, c: (b, c, 0)),) + tuple(st_spec(d) for d in _STATE_DIMS)
    scratch = [
        pltpu.VMEM((R_HEADS // 2, V7X_LANES, V7X_LANES), F32),
        pltpu.VMEM((G_HEADS // 2, V7X_LANES, V7X_LANES), F32),
        pltpu.VMEM((M_HEADS // 2, V7X_LANES, V7X_LANES), F32),
        pltpu.VMEM((8, M_W), F32),
        pltpu.VMEM((8, V7X_LANES), F32),
        pltpu.VMEM((8 + L, M_QK), F32),
        pltpu.VMEM((L, G_W), F32),
    ]
    kern = functools.partial(_mixer_prompt_kernel, L=L, CS=CS, NC=NC, chunk_decay=tabs["chunk_decay"])
    outs = pl.pallas_call(
        kern, out_shape=out_shape, grid=(B, NC), in_specs=in_specs, out_specs=out_specs,
        scratch_shapes=scratch, input_output_aliases=aliases,
        compiler_params=_cparams(("parallel", "arbitrary")), name="mixer_prompt",
    )(*args)
    return outs[0], tuple(outs[1:])


NS = 16
ROWS = 128


def _seq_bcast(x, t, T):
    n, w = x.shape
    x3 = x.reshape(n // T, T, w)
    return jnp.broadcast_to(x3[:, t:t + 1, :], (n // T, T, w)).reshape(n, w)


def _mixer_sample_kernel(proj_ref, cosq_ref, sinq_ref, cosk_ref, sink_ref, rdec_ref, qdec_ref, kdec_ref,
                         tri_ref, segones_ref, selr_ref, hmat_ref, emat_ref, mseg_ref, msegt_ref,
                         wga_ref, bga_ref, convw_ref, convb_ref, bsm_ref, gcat_ref,
                         sr_ref, sg_ref, c_ref, n_ref, m_ref, conv_ref, *rest, T, chunk_decay, slot, depth):
    mix_ref = rest[-8]
    st_outs = rest[-7:-1]
    conv_scr = rest[-1]
    if slot is not None:
        for e in range(depth):
            if e != slot:
                for dst in st_outs:
                    dst[e] = jnp.zeros(dst.shape[1:], F32)
        st_outs = [o.at[slot] for o in st_outs]
    sr_out, sg_out, c_out, n_out, m_out, conv_out = st_outs
    L = ROWS
    ns = L // T
    lane = lax.broadcasted_iota(jnp.int32, (L, V7X_LANES), 1)
    lo = lane < HEAD_W
    hi_m = lane >= HEAD_W
    first_half = (lane % R_DK) < (R_DK // 2)
    row = lax.broadcasted_iota(jnp.int32, (L, L), 0)
    col = lax.broadcasted_iota(jnp.int32, (L, L), 1)
    segcausal = (row // T == col // T) & (col <= row)
    hmat = hmat_ref[...]
    mseg = mseg_ref[...]
    msegt = msegt_ref[...]

    def rope(x, cos_ref, sin_ref):
        cos = cos_ref[...]
        sin = sin_ref[...]
        parts = []
        for t in range(R_W // V7X_LANES):
            xs = x[:, t * V7X_LANES:(t + 1) * V7X_LANES]
            rot = jnp.where(first_half, pltpu.roll(xs, V7X_LANES - R_DK // 2, 1), pltpu.roll(xs, R_DK // 2, 1))
            parts.append(xs * cos + rot * sin)
        return jnp.concatenate(parts, axis=1)

    def tile_lanes(x, n):
        return jnp.concatenate([x] * n, axis=1)

    def tile_rows(x, n):
        return jnp.concatenate([x] * n, axis=0)

    def pair_state_terms(qs, ks_f32, vs, st_ref, p, hd):
        r = st_ref[:, 2 * p:2 * p + 2].reshape(ns * 2 * hd, hd).astype(BF16)
        kt = tile_rows(ks_f32.T.astype(BF16), ns) * msegt
        inter, kv = [], []
        for hh, m in ((0, lo), (1, hi_m)):
            qh = jnp.where(m, qs, jnp.zeros_like(qs))
            inter.append(_dot(tile_lanes(qh, ns) * mseg, r))
            kv.append(_dot(kt, vs[:, hh * hd:(hh + 1) * hd]))
        return inter, kv

    q_r = rope(proj_ref[:, C_QR:C_QR + R_W], cosq_ref, sinq_ref)
    k_r = rope(proj_ref[:, C_KR:C_KR + R_W], cosk_ref, sink_ref)
    q_rb = q_r.astype(BF16)
    k_rb = k_r.astype(BF16)
    kd_r = k_r * kdec_ref[...]
    v_rb = proj_ref[:, C_VR:C_VR + R_W].astype(BF16)
    o_parts = []
    for p in range(R_HEADS // 2):
        sl = slice(p * V7X_LANES, (p + 1) * V7X_LANES)
        qs, ks, vs = q_rb[:, sl], k_rb[:, sl], v_rb[:, sl]
        sc = _dot_nt(qs, _stack_masked(ks, lo, hi_m)) * rdec_ref[p]
        inter, kv = pair_state_terms(qs, kd_r[:, sl], vs, sr_ref, p, R_DK)
        o = (_dot(sc.astype(BF16), _stack_masked(vs, lo, hi_m))
             + jnp.concatenate(inter, axis=1) * qdec_ref[:, sl])
        o_parts.append(o)
        for hh in range(2):
            h = 2 * p + hh
            sr_out[:, h] = sr_ref[:, h] * chunk_decay[h] + kv[hh].reshape(ns, 2, R_DK, R_DV)[:, hh]
    gate_r = _silu(proj_ref[:, C_GR:C_GR + R_W])
    mix_ref[:, 0:R_W] = _norm_gate(jnp.concatenate(o_parts, axis=1), hmat, gcat_ref[:, 0:R_W], gate_r)

    small = proj_ref[:, C_SM:C_SM + V7X_LANES]
    z = _dot(small.astype(BF16), wga_ref[...]) + bga_ref[...]
    log_a = _log_sigmoid(z) / G_NORMALIZER
    b = _dot3(tri_ref[...], log_a)
    b_tot = _seq_bcast(b, T - 1, T)
    q_g = proj_ref[:, C_QG:C_QG + G_QK] * (G_DK ** -0.5)
    k_g = proj_ref[:, C_KG:C_KG + G_QK]
    v_g = proj_ref[:, C_VG:C_VG + G_W]
    v_gb = v_g.astype(BF16)
    tok = lax.broadcasted_iota(jnp.int32, (L, G_QK), 0) % T
    emat = emat_ref[...]
    o_g = jnp.zeros((L, G_W), F32)
    for j in range(T):
        e = jnp.exp(jnp.where(tok >= j, b - _seq_bcast(b, j, T), -jnp.inf))
        tj = (e * q_g * _seq_bcast(k_g, j, T)).astype(BF16)
        o_g = o_g + _dot(tj, emat) * _seq_bcast(v_g, j, T)
    qt = (q_g * jnp.exp(b)).astype(BF16)
    kl = k_g * jnp.exp(b_tot - b)
    dtot = jnp.exp(b_tot)
    lane_h = lane // G_DK
    inter_parts = [None] * G_HEADS
    for h0, heads in ((0, (0, 1, 2, 3)), (2, (4, 5))):
        c0 = h0 * G_DK
        q_s = qt[:, c0:c0 + V7X_LANES]
        r = sg_ref[:, h0:h0 + 4].reshape(ns * V7X_LANES, G_DV)
        rb = r.astype(BF16)
        kt = tile_rows(kl[:, c0:c0 + V7X_LANES].T.astype(BF16), ns) * msegt
        dt = dtot[:, c0:c0 + V7X_LANES].T
        dcols = []
        for s in range(ns):
            dcols.append(jnp.broadcast_to(dt[:, s * T:s * T + 1], (V7X_LANES, G_DV)))
        dfull = jnp.concatenate(dcols, axis=0).reshape(ns, 4, G_DK, G_DV)
        for h in heads:
            qh = jnp.where(lane_h == h - h0, q_s, jnp.zeros_like(q_s))
            inter_parts[h] = _dot(tile_lanes(qh, ns) * mseg, rb)
            kv = _dot(kt, v_gb[:, h * G_DV:(h + 1) * G_DV])
            sg_out[:, h] = sg_ref[:, h] * dfull[:, h - h0] + kv.reshape(ns, 4, G_DK, G_DV)[:, h - h0]
    o_g = o_g + jnp.concatenate(inter_parts, axis=1)
    gate_g = _silu(proj_ref[:, C_RG:C_RG + G_W])
    mix_ref[:, R_W:R_W + G_W] = _norm_gate(o_g, hmat, gcat_ref[:, R_W:R_W + G_W], gate_g)

    u = proj_ref[:, C_QKM:C_QKM + M_QK]
    conv_scr[:, 5:8, :] = conv_ref[...]
    conv_scr[:, 8:8 + T, :] = u.reshape(ns, T, M_QK)
    y = convb_ref[...]
    for j in range(CONV_W - 1):
        y = y + conv_scr[:, 5 + j:5 + j + T, :].reshape(L, M_QK) * convw_ref[j:j + 1, :]
    y = y + u * convw_ref[CONV_W - 1:CONV_W, :]
    conv_out[...] = conv_scr[:, 5 + T:8 + T, :]
    qk = _silu(y)
    q_m = qk[:, :M_W]
    k_m = qk[:, M_W:] * (M_DK ** -0.5)
    q_mb = q_m.astype(BF16)
    k_mb = k_m.astype(BF16)
    v_mb = proj_ref[:, C_VM:C_VM + M_W].astype(BF16)
    gates = small + bsm_ref[...]
    f_cum = _dot3(tri_ref[...], _log_sigmoid(gates))
    i_rows = _dot3_nt(selr_ref[...], gates)
    f_rows = _dot3_nt(selr_ref[...], f_cum)
    head_lane = (lane >= SM_I) & (lane < SM_I + M_HEADS)
    fc = jnp.where(head_lane, pltpu.roll(f_cum, V7X_LANES - (SM_F - SM_I), 1), 0.0)
    m_prev = m_ref[...]
    a = fc + m_prev
    mx = jnp.full((L, V7X_LANES), -jnp.inf, F32)
    dms = []
    for h in range(M_HEADS):
        dm = jnp.where(segcausal, (fc[:, SM_I + h:SM_I + h + 1] - f_rows[M_HEADS + h:M_HEADS + h + 1, :])
                       + i_rows[h:h + 1, :], -jnp.inf)
        dms.append(dm)
        mx = jnp.where(lane == SM_I + h, jnp.max(dm, axis=-1, keepdims=True), mx)
    m_tok = jnp.maximum(a, mx)
    w_inter = jnp.exp(a - m_tok)
    n_rows = n_ref[...]
    hs, lows = _split2(q_m * n_rows)
    qn_b = _dot(hs, hmat_ref[0:M_W, 0:M_W]) + _dot(lows, hmat_ref[0:M_W, 0:M_W])
    rs_all = jnp.zeros((L, V7X_LANES), F32)
    qn_all = jnp.zeros((L, V7X_LANES), F32)
    scs = []
    for p in range(M_HEADS // 2):
        sl = slice(p * V7X_LANES, (p + 1) * V7X_LANES)
        sc = _dot_nt(q_mb[:, sl], _stack_masked(k_mb[:, sl], lo, hi_m))
        mt_a = jnp.broadcast_to(m_tok[:, SM_I + 2 * p:SM_I + 2 * p + 1], (L, L))
        mt_b = jnp.broadcast_to(m_tok[:, SM_I + 2 * p + 1:SM_I + 2 * p + 2], (L, L))
        sc_a = sc[:, :L] * jnp.exp(dms[2 * p] - mt_a)
        sc_b = sc[:, L:] * jnp.exp(dms[2 * p + 1] - mt_b)
        rs_all = jnp.where(lane == SM_I + 2 * p, jnp.sum(sc_a, axis=-1, keepdims=True), rs_all)
        rs_all = jnp.where(lane == SM_I + 2 * p + 1, jnp.sum(sc_b, axis=-1, keepdims=True), rs_all)
        qn_all = jnp.where(lane == SM_I + 2 * p, qn_b[:, p * V7X_LANES:p * V7X_LANES + 1], qn_all)
        qn_all = jnp.where(lane == SM_I + 2 * p + 1, qn_b[:, p * V7X_LANES + HEAD_W:p * V7X_LANES + HEAD_W + 1], qn_all)
        scs.append(jnp.concatenate([sc_a, sc_b], axis=1).astype(BF16))
    den = rs_all + qn_all * w_inter
    inv = 1.0 / jnp.maximum(jnp.abs(den), jnp.exp(-m_tok))
    m_new = _seq_bcast(jnp.where(head_lane, m_tok, 0.0), T - 1, T)
    wk = jnp.exp(_seq_bcast(fc, T - 1, T) - fc + gates - m_new)
    scale = jnp.exp(_seq_bcast(a, T - 1, T) - m_new)
    h_parts = []
    kw_parts = []
    scale_parts = []
    for p in range(M_HEADS // 2):
        sl = slice(p * V7X_LANES, (p + 1) * V7X_LANES)
        ia, ib = SM_I + 2 * p, SM_I + 2 * p + 1
        kw = k_m[:, sl] * _lane_pick(wk, ia, ib, lo)
        kw_parts.append(kw)
        inter, kv = pair_state_terms(q_mb[:, sl], kw, v_mb[:, sl], c_ref, p, M_DK)
        num = (_dot(scs[p], _stack_masked(v_mb[:, sl], lo, hi_m))
               + jnp.concatenate(inter, axis=1) * _lane_pick(w_inter, ia, ib, lo))
        h_parts.append(num * _lane_pick(inv, ia, ib, lo))
        scale_parts.append(_lane_pick(scale, ia, ib, lo))
        for hh in range(2):
            h = 2 * p + hh
            sc_rows = jnp.broadcast_to(scale[:, SM_I + h:SM_I + h + 1], (L, M_DV)).reshape(ns, T, M_DV)
            sc_h = jnp.broadcast_to(sc_rows[:, 0:1, :], (ns, M_DK, M_DV))
            c_out[:, h] = c_ref[:, h] * sc_h + kv[hh].reshape(ns, 2, M_DK, M_DV)[:, hh]
    kw_all = jnp.concatenate(kw_parts, axis=1)
    n_out[...] = n_rows * jnp.concatenate(scale_parts, axis=1) + _dot3(segones_ref[...], kw_all)
    m_out[...] = m_new
    gate_m = _sigmoid(proj_ref[:, C_OM:C_OM + M_W])
    mix_ref[:, R_W + G_W:D_MIX] = _norm_gate(jnp.concatenate(h_parts, axis=1), hmat_ref[0:M_W, 0:M_W],
                                             gcat_ref[:, R_W + G_W:D_MIX], gate_m)


def _sample_tables(T, pos0):
    L = ROWS
    half = R_DK // 2
    inv = ROPE_BASE ** (-jnp.arange(half, dtype=F32) * 2.0 / R_DK)
    tok = np.arange(L) % T
    seq = np.arange(L) // T
    pos = pos0 + jnp.asarray(tok, F32)
    ang = pos[:, None] * inv[None, :]
    cos = jnp.tile(jnp.cos(ang), (1, V7X_LANES // half))
    sin_h = jnp.sin(ang)
    sin = jnp.tile(jnp.concatenate([-sin_h, sin_h], axis=1), (1, V7X_LANES // R_DK))
    qs = R_DK ** -0.5

    log_gamma = jnp.log(1.0 - 2.0 ** (-5.0 - jnp.arange(R_HEADS, dtype=F32)))
    tf = jnp.asarray(tok, F32)
    rel = tf[:, None] - tf[None, :]
    ok_np = (seq[:, None] == seq[None, :]) & (tok[None, :] <= tok[:, None])
    ok = jnp.asarray(ok_np)
    rdecay = jnp.where(ok[None], jnp.exp(log_gamma[:, None, None] * jnp.where(ok, rel, 0.0)[None]), 0.0)
    rdec = jnp.concatenate([rdecay[0::2], rdecay[1::2]], axis=2)
    qdec = jnp.repeat(jnp.exp(log_gamma[:, None] * (tf + 1.0)).T, R_DK, axis=1)
    kdec = jnp.repeat(jnp.exp(log_gamma[:, None] * (T - 1.0 - tf)).T, R_DK, axis=1)
    lg32 = np.log(1.0 - 2.0 ** (-5.0 - np.arange(R_HEADS, dtype=np.float64))).astype(np.float32)
    chunk_decay = tuple(float(np.exp(v * np.float32(T))) for v in lg32)

    selr = np.zeros((16, V7X_LANES), np.float32)
    for h in range(M_HEADS):
        selr[h, SM_I + h] = 1.0
        selr[M_HEADS + h, SM_F + h] = 1.0
    hv = np.arange(G_W) // G_DV
    hc = np.arange(G_QK) // G_DK
    gmask = (hv[:, None] == hc[None, :]).astype(np.float32)
    hmat = (hv[:, None] == hv[None, :]).astype(np.float32)
    ns = L // T
    mseg = (seq[:, None] == (np.arange(ns * V7X_LANES) // V7X_LANES)[None, :]).astype(np.float32)
    return dict(
        cosq=cos * qs, sinq=sin * qs, cosk=cos, sink=sin, rdec=rdec, qdec=qdec, kdec=kdec, chunk_decay=chunk_decay,
        tri=jnp.asarray(ok_np, BF16), segones=jnp.asarray(seq[:, None] == seq[None, :], BF16),
        selr=jnp.asarray(selr, BF16), hmat=jnp.asarray(hmat, BF16),
        emat=jnp.asarray(gmask.T, BF16), mseg=jnp.asarray(mseg, BF16), msegt=jnp.asarray(mseg.T, BF16))


_SAMPLE_STATE_DIMS = ((R_HEADS, R_DK, R_DV), (G_HEADS, G_DK, G_DV), (M_HEADS, M_DK, M_DV), (M_W,),
                      (V7X_LANES,), (CONV_W - 1, M_QK))


def mixer_sample(proj, tabs, lw, state, l, depth, prev, *, T):
    M = proj.shape[0]
    B = M // T
    assert B % NS == 0 and NS * T == ROWS
    L = ROWS
    const2 = lambda b: (0, 0)
    const3 = lambda b: (0, 0, 0)
    dims = [(NS,) + d for d in _SAMPLE_STATE_DIMS]
    dims[3], dims[4] = (L, M_W), (L, V7X_LANES)
    full = [(B,) + d for d in _SAMPLE_STATE_DIMS]
    full[3], full[4] = (M, M_W), (M, V7X_LANES)

    def spec(d, lead):
        tail = (0,) * (len(d) - 1)
        if lead == "all":
            return pl.BlockSpec((depth,) + d, lambda b: (0, b) + tail)
        return pl.BlockSpec((None,) + d, lambda b: (lead, b) + tail)

    in_specs = [
        pl.BlockSpec((L, D_IN_PAD), lambda b: (b, 0)),
        pl.BlockSpec((L, V7X_LANES), const2), pl.BlockSpec((L, V7X_LANES), const2),
        pl.BlockSpec((L, V7X_LANES), const2), pl.BlockSpec((L, V7X_LANES), const2),
        pl.BlockSpec((R_HEADS // 2, L, 2 * L), const3),
        pl.BlockSpec((L, R_W), const2),
        pl.BlockSpec((L, R_W), const2),
        pl.BlockSpec((L, L), const2),
        pl.BlockSpec((L, L), const2),
        pl.BlockSpec((16, V7X_LANES), const2),
        pl.BlockSpec((G_W, G_W), const2),
        pl.BlockSpec((G_QK, G_W), const2),
        pl.BlockSpec((L, NS * V7X_LANES), const2),
        pl.BlockSpec((NS * V7X_LANES, L), const2),
        pl.BlockSpec((V7X_LANES, G_QK), const2),
        pl.BlockSpec((1, G_QK), const2),
        pl.BlockSpec((CONV_W, M_QK), const2),
        pl.BlockSpec((1, M_QK), const2),
        pl.BlockSpec((1, V7X_LANES), const2),
        pl.BlockSpec((1, D_MIX), const2),
    ] + [spec(d, l) for d in dims]
    args = [proj, tabs["cosq"], tabs["sinq"], tabs["cosk"], tabs["sink"], tabs["rdec"], tabs["qdec"], tabs["kdec"],
            tabs["tri"], tabs["segones"], tabs["selr"], tabs["hmat"], tabs["emat"], tabs["mseg"], tabs["msegt"],
            lw["wga"], lw["bga"], lw["convw"], lw["convb"], lw["bsm"], lw["gcat"]] + list(state)
    aliases = {}
    if prev is not None:
        n_in = len(args)
        in_specs += [pl.BlockSpec(memory_space=pl.ANY)] * 6
        args += list(prev)
        aliases = {n_in + i: 1 + i for i in range(6)}
    out_shape = (jax.ShapeDtypeStruct((M, D_MIX), BF16),) + tuple(jax.ShapeDtypeStruct((depth,) + f, F32) for f in full)
    out_specs = (pl.BlockSpec((L, D_MIX), lambda b: (b, 0)),) + tuple(
        spec(d, "all" if prev is None else l) for d in dims)
    kern = functools.partial(_mixer_sample_kernel, T=T, chunk_decay=tabs["chunk_decay"],
                             slot=l if prev is None else None, depth=depth)
    outs = pl.pallas_call(
        kern, out_shape=out_shape, grid=(B // NS,), in_specs=in_specs, out_specs=out_specs,
        scratch_shapes=[pltpu.VMEM((NS, 8 + T, M_QK), F32)],
        input_output_aliases=aliases,
        compiler_params=_cparams(("parallel",)), name="mixer_sample",
    )(*args)
    return outs[0], tuple(outs[1:])


def _prep_weights(w_in, w_out, w_xq, w_xk, w_xv, w_xo, w_gate, w_up, w_down):
    a0 = 2 * R_HEADS * R_DK + 2 * R_W + 2 * G_QK + 2 * G_W
    m0 = a0 + G_RANK
    g0 = m0 + M_QK + 2 * M_W
    cast = lambda w: w.astype(BF16)
    wb = cast(w_in)
    pad = jnp.zeros(w_in.shape[:2] + (D_IN_PAD - C_SM - G_RANK - 2 * M_HEADS,), BF16)
    w_pad = jnp.concatenate([wb[..., :a0], wb[..., m0:g0], wb[..., g0:g0 + 2 * M_HEADS], wb[..., a0:m0], pad], axis=2)
    return dict(w_in=w_pad, w_out=cast(w_out), w_xq=cast(w_xq), w_xk=cast(w_xk), w_xv=cast(w_xv),
                w_xo=cast(w_xo), w_gate=cast(w_gate), w_up=cast(w_up), w_down=cast(w_down))


def _prep_layer(l, g_mix, w_ga2, b_ga, conv_w, conv_b, b_i, b_f, g_ret, g_gla, g_mlstm, g_xattn, g_mem, g_ffn):
    wga = jnp.zeros((V7X_LANES, G_QK), F32).at[SM_AG:SM_AG + G_RANK].set(w_ga2[l]).astype(BF16)
    bsm = (jnp.zeros((1, V7X_LANES), F32).at[0, SM_I:SM_I + M_HEADS].set(b_i[l])
           .at[0, SM_F:SM_F + M_HEADS].set(b_f[l]))
    return dict(
        g_mix=g_mix[l], wga=wga, bga=b_ga[l].reshape(1, G_QK), convw=conv_w[l],
        convb=conv_b[l].reshape(1, M_QK), bsm=bsm,
        gcat=jnp.concatenate([g_ret[l], g_gla[l], g_mlstm[l]]).reshape(1, D_MIX),
        g_xattn=g_xattn[l], g_mem=g_mem[l], g_ffn=g_ffn[l])


def _layer(x, mix_fn, attend, W, lw, l, g_final, *, tm, final_norm):
    B, T, D = x.shape
    M = B * T
    x2 = x.reshape(M, D)
    proj = rms_matmul(x2, lw["g_mix"], W["w_in"], l, tm=tm)
    mix, new_state = mix_fn(proj)
    x2 = attend(mix, x2)
    x2 = swiglu_res(x2, lw["g_ffn"], W["w_gate"], W["w_up"], W["w_down"], l, g_final,
                    tm=tm, final_norm=final_norm)
    return x2.reshape(B, T, D), new_state


def kernel(x_prompt, x_sample, state_ret, state_gla, state_mlstm_C, state_mlstm_n, state_mlstm_m, state_mlstm_conv, cache_mem_k, cache_mem_v, mem_prompt, g_mix, w_in, w_ga2, b_ga, conv_w, conv_b, b_i, b_f, g_ret, g_gla, g_mlstm, w_out, g_xattn, g_mem, w_xq, w_xk, w_xv, w_xo, g_ffn, w_gate, w_up, w_down, g_final):
    B, T, D = x_prompt.shape
    Bs, Ts, _ = x_sample.shape
    depth = w_in.shape[0]
    assert T % CHUNK == 0 and Ts * NS == ROWS and Bs % NS == 0
    tabs_p = _prompt_tables(T, CHUNK, 16)
    tabs_s = _sample_tables(Ts, float(PAST_LEN))
    tm_p = 512 if (B * T) % 512 == 0 else B * T
    tm_s = 512 if (Bs * Ts) % 512 == 0 else Bs * Ts
    tq_p = 1024 if T % 1024 == 0 else T
    nb_x = 8 if Bs % 8 == 0 else 1

    sample_state = (state_ret, state_gla, state_mlstm_C,
                    jnp.repeat(state_mlstm_n.reshape(depth, Bs, M_W), Ts, axis=1),
                    jnp.repeat(jnp.pad(state_mlstm_m, ((0, 0), (0, 0), (0, V7X_LANES - M_HEADS))), Ts, axis=1),
                    state_mlstm_conv)
    p_st = tuple(jnp.zeros((depth, B) + d, F32) for d in _STATE_DIMS)
    s_st = p_mem = None
    hp, hs = x_prompt, x_sample
    W = _prep_weights(w_in, w_out, w_xq, w_xk, w_xv, w_xo, w_gate, w_up, w_down)
    for l in range(depth):
        lw = _prep_layer(l, g_mix, w_ga2, b_ga, conv_w, conv_b, b_i, b_f, g_ret, g_gla, g_mlstm,
                         g_xattn, g_mem, g_ffn)
        last = l == depth - 1
        k5, v5, kb, vb = memory_kv(mem_prompt, lw["g_mem"], W["w_xk"], W["w_xv"], l, depth, p_mem)
        p_mem = (k5, v5)

        def mix_p(proj):
            mix, st = mixer_prompt(proj.reshape(B, T, D_IN_PAD), tabs_p, lw, l, depth, p_st, L=CHUNK, CS=16)
            return mix.reshape(B * T, D_MIX), st

        def attend_p(mix, x):
            y = post_mix(mix.reshape(B, T, D_MIX), x.reshape(B, T, D), kb, vb,
                         W["w_out"], lw["g_xattn"], W["w_xq"], W["w_xo"], l, tq=tq_p)
            return y.reshape(B * T, D)

        hp, p_st = _layer(hp, mix_p, attend_p, W, lw, l, g_final, tm=tm_p, final_norm=last)
        hs, s_st = _layer(hs, lambda proj: mixer_sample(proj, tabs_s, lw, sample_state, l, depth, s_st, T=Ts),
                          lambda mix, x: post_mix_cache(mix, x, cache_mem_k, cache_mem_v, W["w_out"],
                                                        lw["g_xattn"], W["w_xq"], W["w_xo"], l, nb=nb_x, T=Ts),
                          W, lw, l, g_final, tm=tm_s, final_norm=last)

    p_out = p_st[:4] + (p_st[4].reshape(depth, B, M_HEADS), p_st[5])
    s_out = s_st[:3] + (s_st[3][:, ::Ts].reshape(depth, Bs, M_HEADS, M_DK), s_st[4][:, ::Ts, :M_HEADS], s_st[5])
    return (hp, hs, *p_out, *p_mem, *s_out)
```

```python
import functools

import numpy as np
import jax
import jax.numpy as jnp
from jax import lax
from jax.experimental import pallas as pl
from jax.experimental.pallas import tpu as pltpu

F32 = jnp.float32
BF16 = jnp.bfloat16

D_MODEL = 1024
PAST_LEN = 16384
R_HEADS, R_DK, R_DV = 6, 64, 64
G_HEADS, G_DK, G_DV, G_RANK = 6, 32, 64, 16
G_NORMALIZER = 16.0
M_HEADS, M_DK, M_DV = 4, 64, 64
CONV_W = 4
X_HEADS = 4
X_HD = D_MODEL // X_HEADS
N_MEM = 256
CHUNK = 256
EPS = 1e-6
ROPE_BASE = 10000.0

R_W = R_HEADS * R_DV
G_QK = G_HEADS * G_DK
G_W = G_HEADS * G_DV
M_QK = 2 * M_HEADS * M_DK
M_W = M_HEADS * M_DV
D_MIX = R_W + G_W + M_W

C_QR, C_KR, C_VR, C_GR = 0, 384, 768, 1152
C_QG, C_KG, C_VG, C_RG = 1536, 1728, 1920, 2304
C_QKM, C_VM, C_OM, C_SM = 2688, 3200, 3456, 3712
D_IN_PAD = 3840
SM_I, SM_F, SM_AG = 0, 4, 8
HEAD_W = 64
GLA_SAFE_LOG_RANGE = 60.0

V7X_LANES = 128
V7X_SUBLANES = 8
VMEM_LIMIT = 56 * 1024 * 1024
CONV_AT = V7X_SUBLANES
CONV_LO = CONV_AT - (CONV_W - 1)
SEL_ROWS = 2 * V7X_SUBLANES
GLA_SUB = 16
XATTN_SEQS = 8


def _cparams(sem):
    return pltpu.CompilerParams(dimension_semantics=sem, vmem_limit_bytes=VMEM_LIMIT)


def _sigmoid(x):
    return 1.0 / (1.0 + jnp.exp(-x))


def _silu(x):
    return x * _sigmoid(x)


def _log_sigmoid(x):
    return jnp.minimum(x, 0.0) - jnp.log(1.0 + jnp.exp(-jnp.abs(x)))


def _dot(a, b):
    return jnp.dot(a, b, preferred_element_type=F32)


def _dot_nt(a, b):
    return lax.dot_general(a, b, (((1,), (1,)), ((), ())), preferred_element_type=F32)


def _dot_tn(a, b):
    return lax.dot_general(a, b, (((0,), (0,)), ((), ())), preferred_element_type=F32)


def _split3(x):
    hi = x.astype(BF16)
    r1 = x - hi.astype(F32)
    mid = r1.astype(BF16)
    lo = (r1 - mid.astype(F32)).astype(BF16)
    return hi, mid, lo


def _dot3(a, x):
    hi, mid, lo = _split3(x)
    return _dot(a, hi) + _dot(a, mid) + _dot(a, lo)


def _dot3_nt(a, x):
    hi, mid, lo = _split3(x)
    return _dot_nt(a, hi) + _dot_nt(a, mid) + _dot_nt(a, lo)


def _rms(x, g):
    return x * lax.rsqrt(jnp.mean(x * x, axis=-1, keepdims=True) + EPS) * g


def _wspec(w, l):
    return pl.BlockSpec((None,) + w.shape[1:], lambda *_: (l, 0, 0))


def _rms_matmul_kernel(x_ref, g_ref, w_ref, o_ref):
    o_ref[...] = _dot(_rms(x_ref[...], g_ref[...]).astype(BF16), w_ref[...]).astype(o_ref.dtype)


def rms_matmul(x, g, w, l, *, tm, out_dtype=F32):
    M, D = x.shape
    N = w.shape[2]
    assert M % tm == 0
    return pl.pallas_call(
        _rms_matmul_kernel,
        out_shape=jax.ShapeDtypeStruct((M, N), out_dtype),
        grid=(M // tm,),
        in_specs=[pl.BlockSpec((tm, D), lambda i: (i, 0)),
                  pl.BlockSpec((1, D), lambda i: (0, 0)),
                  _wspec(w, l)],
        out_specs=pl.BlockSpec((tm, N), lambda i: (i, 0)),
        compiler_params=_cparams(("parallel",)),
        name="rms_matmul",
    )(x, g.reshape(1, D), w)


def _swiglu_kernel(x_ref, g_ref, wg_ref, wu_ref, wd_ref, gf_ref, o_ref, *, final_norm):
    x = x_ref[...]
    xn = _rms(x, g_ref[...]).astype(BF16)
    h = _silu(_dot(xn, wg_ref[...])) * _dot(xn, wu_ref[...])
    y = x + _dot(h.astype(BF16), wd_ref[...])
    if final_norm:
        y = _rms(y, gf_ref[...])
    o_ref[...] = y


def swiglu_res(x, g, wg, wu, wd, l, g_final, *, tm, final_norm):
    M, D = x.shape
    assert M % tm == 0
    row = pl.BlockSpec((tm, D), lambda i: (i, 0))
    vec = pl.BlockSpec((1, D), lambda i: (0, 0))
    return pl.pallas_call(
        functools.partial(_swiglu_kernel, final_norm=final_norm),
        out_shape=jax.ShapeDtypeStruct((M, D), F32),
        grid=(M // tm,),
        in_specs=[row, vec, _wspec(wg, l), _wspec(wu, l), _wspec(wd, l), vec],
        out_specs=row,
        compiler_params=_cparams(("parallel",)),
        name="swiglu_res",
    )(x, g.reshape(1, D), wg, wu, wd, g_final.reshape(1, D))


def _memkv_kernel(x_ref, g_ref, wk_ref, wv_ref, *refs, slot, depth):
    k5_ref, v5_ref, kb_ref, vb_ref = refs[-4:]
    if slot is not None:
        for e in range(depth):
            if e != slot:
                k5_ref[e] = jnp.zeros(k5_ref.shape[1:], F32)
                v5_ref[e] = jnp.zeros(v5_ref.shape[1:], F32)
        k5_ref, v5_ref = k5_ref.at[slot], v5_ref.at[slot]
    xn = _rms(x_ref[...], g_ref[...]).astype(BF16)
    for w_ref, o5_ref, ob_ref in ((wk_ref, k5_ref, kb_ref), (wv_ref, v5_ref, vb_ref)):
        y = _dot(xn, w_ref[...])
        ob_ref[...] = y.astype(BF16)
        for h in range(X_HEADS):
            o5_ref[:, h, :] = y[:, h * X_HD:(h + 1) * X_HD]


def memory_kv(mem, g, wk, wv, l, depth, prev):
    B, _, D = mem.shape
    in_specs = [pl.BlockSpec((None, N_MEM, D), lambda b: (b, 0, 0)),
                pl.BlockSpec((1, D), lambda b: (0, 0)),
                _wspec(wk, l), _wspec(wv, l)]
    args = [mem, g.reshape(1, D), wk, wv]
    aliases = {}
    if prev is None:
        spec5 = pl.BlockSpec((depth, None, N_MEM, X_HEADS, X_HD), lambda b: (0, b, 0, 0, 0))
    else:
        spec5 = pl.BlockSpec((None, None, N_MEM, X_HEADS, X_HD), lambda b: (l, b, 0, 0, 0))
        in_specs += [pl.BlockSpec(memory_space=pl.ANY)] * 2
        args += list(prev)
        aliases = {4: 0, 5: 1}
    o5 = jax.ShapeDtypeStruct((depth, B, N_MEM, X_HEADS, X_HD), F32)
    ob = jax.ShapeDtypeStruct((B, N_MEM, D), BF16)
    specb = pl.BlockSpec((None, N_MEM, D), lambda b: (b, 0, 0))
    return pl.pallas_call(
        functools.partial(_memkv_kernel, slot=l if prev is None else None, depth=depth),
        out_shape=(o5, o5, ob, ob),
        grid=(B,),
        in_specs=in_specs,
        out_specs=(spec5, spec5, specb, specb),
        input_output_aliases=aliases,
        compiler_params=_cparams(("parallel",)),
        name="memory_kv",
    )(*args)


def _out_and_query(a_ref, x_ref, wo_ref, g_ref, wq_ref):
    x1 = x_ref[...] + _dot(a_ref[...], wo_ref[...])
    q = _dot(_rms(x1, g_ref[...]).astype(BF16), wq_ref[...]).astype(BF16)
    return x1, q


def _post_mix_kernel(a_ref, x_ref, k_ref, v_ref, wo_ref, g_ref, wq_ref, wxo_ref, o_ref):
    x1, q = _out_and_query(a_ref, x_ref, wo_ref, g_ref, wq_ref)
    scale = X_HD ** -0.5
    parts = []
    for h in range(X_HEADS):
        sl = slice(h * X_HD, (h + 1) * X_HD)
        s = _dot_nt(q[:, sl], k_ref[:, sl]) * scale
        p = jnp.exp(s - jnp.max(s, axis=-1, keepdims=True))
        l = jnp.sum(p, axis=-1, keepdims=True)
        parts.append((_dot(p.astype(BF16), v_ref[:, sl]) / l).astype(BF16))
    o_ref[...] = x1 + _dot(jnp.concatenate(parts, axis=1), wxo_ref[...])


def post_mix(a, x, mk, mv, wo, g, wq, wxo, l, *, tq):
    B, T, D = x.shape
    assert T % tq == 0
    tok = lambda w: pl.BlockSpec((None, tq, w), lambda b, i: (b, i, 0))
    mem = pl.BlockSpec((None, N_MEM, D), lambda b, i: (b, 0, 0))
    return pl.pallas_call(
        _post_mix_kernel,
        out_shape=jax.ShapeDtypeStruct((B, T, D), F32),
        grid=(B, T // tq),
        in_specs=[tok(a.shape[2]), tok(D), mem, mem, _wspec(wo, l), pl.BlockSpec((1, D), lambda b, i: (0, 0)),
                  _wspec(wq, l), _wspec(wxo, l)],
        out_specs=tok(D),
        compiler_params=_cparams(("parallel", "arbitrary")),
        name="post_mix",
    )(a, x, mk, mv, wo, g.reshape(1, D), wq, wxo)


def _post_mix_cache_kernel(a_ref, x_ref, k_ref, v_ref, wo_ref, g_ref, wq_ref, wxo_ref, o_ref, *, nb, T):
    x1, q_all = _out_and_query(a_ref, x_ref, wo_ref, g_ref, wq_ref)
    R = X_HEADS * T
    rowh = lax.broadcasted_iota(jnp.int32, (R, N_MEM * X_HEADS), 0) // T
    colh = lax.broadcasted_iota(jnp.int32, (R, N_MEM * X_HEADS), 1) % X_HEADS
    own = rowh == colh
    outs = []
    for s in range(nb):
        q = q_all[s * T:(s + 1) * T]
        qf = jnp.concatenate([q[:, h * X_HD:(h + 1) * X_HD] for h in range(X_HEADS)], axis=0)
        kf = k_ref[s].reshape(N_MEM * X_HEADS, X_HD).astype(BF16)
        vf = v_ref[s].reshape(N_MEM * X_HEADS, X_HD).astype(BF16)
        sc = jnp.where(own, _dot_nt(qf, kf) * (X_HD ** -0.5), -jnp.inf)
        p = jnp.exp(sc - jnp.max(sc, axis=-1, keepdims=True))
        l = jnp.sum(p, axis=-1, keepdims=True)
        o = (_dot(p.astype(BF16), vf) / l).astype(BF16)
        outs.append(jnp.concatenate([o[h * T:(h + 1) * T] for h in range(X_HEADS)], axis=1))
    o_ref[...] = x1 + _dot(jnp.concatenate(outs, axis=0), wxo_ref[...])


def post_mix_cache(a, x, ck, cv, wo, g, wq, wxo, l, *, nb, T):
    M, D = x.shape
    rows = nb * T
    assert M % rows == 0
    cspec = pl.BlockSpec((None, nb, N_MEM, X_HEADS, X_HD), lambda b: (l, b, 0, 0, 0))
    tok = lambda w: pl.BlockSpec((rows, w), lambda b: (b, 0))
    return pl.pallas_call(
        functools.partial(_post_mix_cache_kernel, nb=nb, T=T),
        out_shape=jax.ShapeDtypeStruct((M, D), F32),
        grid=(M // rows,),
        in_specs=[tok(a.shape[1]), tok(D), cspec, cspec, _wspec(wo, l), pl.BlockSpec((1, D), lambda b: (0, 0)),
                  _wspec(wq, l), _wspec(wxo, l)],
        out_specs=tok(D),
        compiler_params=_cparams(("parallel",)),
        name="post_mix_cache",
    )(a, x, ck, cv, wo, g.reshape(1, D), wq, wxo)


def _eye(n):
    r = lax.broadcasted_iota(jnp.int32, (n, n), 0)
    c = lax.broadcasted_iota(jnp.int32, (n, n), 1)
    return jnp.where(r == c, 1.0, 0.0).astype(BF16)


_STATE_DIMS = ((R_HEADS, R_DK, R_DV), (G_HEADS, G_DK, G_DV), (M_HEADS, M_DK, M_DV), (M_HEADS, M_DK),
               (1, M_HEADS), (CONV_W - 1, M_QK))


def _split2(x):
    hi = x.astype(BF16)
    lo = (x - hi.astype(F32)).astype(BF16)
    return hi, lo


def _head_mean_sq(o, hmat):
    hi, lo = _split2(o * o)
    return (_dot(hi, hmat) + _dot(lo, hmat)) * (1.0 / HEAD_W)


def _norm_gate(o, hmat, g_row, gate):
    return (o * lax.rsqrt(_head_mean_sq(o, hmat) + EPS) * g_row * gate).astype(BF16)


def _lane_pick(cols, idx_lo, idx_hi, lo_mask):
    L = cols.shape[0]
    a = jnp.broadcast_to(cols[:, idx_lo:idx_lo + 1], (L, V7X_LANES))
    b = jnp.broadcast_to(cols[:, idx_hi:idx_hi + 1], (L, V7X_LANES))
    return jnp.where(lo_mask, a, b)


def _stack_masked(x, m_a, m_b):
    z = jnp.zeros_like(x)
    return jnp.concatenate([jnp.where(m_a, x, z), jnp.where(m_b, x, z)], axis=0)


def _mixer_prompt_kernel(proj_ref, cosq_ref, sinq_ref, cosk_ref, sink_ref, rdec_ref, qdec_ref, kdec_ref,
                         tri_ref, btri_ref, bones_ref, selr_ref, hmat_ref, emat_ref, pmask_ref,
                         wga_ref, bga_ref, convw_ref, convb_ref, bsm_ref, gcat_ref, *rest,
                         L, CS, NC, chunk_decay):
    (mix_ref, sr_out, sg_out, c_out, n_out, m_out, conv_out,
     srp_scr, sgt_scr, cp_scr, n_scr, m_scr, conv_scr, og_scr) = rest[-14:]
    c = pl.program_id(1)

    @pl.when(c == 0)
    def _():
        srp_scr[...] = jnp.zeros_like(srp_scr)
        sgt_scr[...] = jnp.zeros_like(sgt_scr)
        cp_scr[...] = jnp.zeros_like(cp_scr)
        n_scr[...] = jnp.zeros_like(n_scr)
        m_scr[...] = jnp.zeros_like(m_scr)
        conv_scr[0:CONV_AT, :] = jnp.zeros((CONV_AT, M_QK), F32)

    lane = lax.broadcasted_iota(jnp.int32, (L, V7X_LANES), 1)
    lo = lane < HEAD_W
    hi_m = lane >= HEAD_W
    first_half = (lane % R_DK) < (R_DK // 2)
    row2 = lax.broadcasted_iota(jnp.int32, (L, 2 * L), 0)
    col2 = lax.broadcasted_iota(jnp.int32, (L, 2 * L), 1) % L
    causal2 = col2 <= row2
    lo_row = lax.broadcasted_iota(jnp.int32, (1, V7X_LANES), 1) < HEAD_W
    lo8 = lax.broadcasted_iota(jnp.int32, (V7X_SUBLANES, V7X_LANES), 1) < HEAD_W
    hmat = hmat_ref[...]
    pmask = pmask_ref[...]

    def rope(x, cos_ref, sin_ref):
        cos = cos_ref[...]
        sin = sin_ref[...]
        parts = []
        for t in range(R_W // V7X_LANES):
            xs = x[:, t * V7X_LANES:(t + 1) * V7X_LANES]
            rot = jnp.where(first_half, pltpu.roll(xs, V7X_LANES - R_DK // 2, 1), pltpu.roll(xs, R_DK // 2, 1))
            parts.append(xs * cos + rot * sin)
        return jnp.concatenate(parts, axis=1)

    q_r = rope(proj_ref[:, C_QR:C_QR + R_W], cosq_ref, sinq_ref)
    k_r = rope(proj_ref[:, C_KR:C_KR + R_W], cosk_ref, sink_ref)
    q_rb = q_r.astype(BF16)
    k_rb = k_r.astype(BF16)
    kd_rb = (k_r * kdec_ref[...]).astype(BF16)
    v_rb = proj_ref[:, C_VR:C_VR + R_W].astype(BF16)
    o_parts = []
    for p in range(R_HEADS // 2):
        sl = slice(p * V7X_LANES, (p + 1) * V7X_LANES)
        qs, ks, vs = q_rb[:, sl], k_rb[:, sl], v_rb[:, sl]
        sc = _dot_nt(qs, _stack_masked(ks, lo, hi_m)) * rdec_ref[p]
        sp = srp_scr[p]
        o = _dot(sc.astype(BF16), _stack_masked(vs, lo, hi_m)) + _dot(qs, sp.astype(BF16)) * qdec_ref[:, sl]
        cd = jnp.where(lo_row, chunk_decay[2 * p], chunk_decay[2 * p + 1])
        srp_scr[p] = sp * cd + _dot_tn(kd_rb[:, sl], vs) * pmask
        o_parts.append(o)
    o_r = jnp.concatenate(o_parts, axis=1)
    gate_r = _silu(proj_ref[:, C_GR:C_GR + R_W])
    mix_ref[:, 0:R_W] = _norm_gate(o_r, hmat, gcat_ref[:, 0:R_W], gate_r)

    small = proj_ref[:, C_SM:C_SM + V7X_LANES]

    u = proj_ref[:, C_QKM:C_QKM + M_QK]
    conv_scr[CONV_AT:CONV_AT + L, :] = u
    y = convb_ref[...]
    for j in range(CONV_W - 1):
        y = y + conv_scr[CONV_LO + j:CONV_LO + j + L, :] * convw_ref[j:j + 1, :]
    y = y + u * convw_ref[CONV_W - 1:CONV_W, :]
    tail = conv_scr[CONV_LO + L:CONV_AT + L, :]
    conv_scr[CONV_LO:CONV_AT, :] = tail
    qk = _silu(y)
    q_m = qk[:, :M_W]
    k_m = qk[:, M_W:] * (M_DK ** -0.5)
    q_mb = q_m.astype(BF16)
    k_mb = k_m.astype(BF16)
    v_mb = proj_ref[:, C_VM:C_VM + M_W].astype(BF16)
    gates = small + bsm_ref[...]
    f_cum = _dot3(tri_ref[...], _log_sigmoid(gates))
    i_rows = _dot3_nt(selr_ref[...], gates)
    f_rows = _dot3_nt(selr_ref[...], f_cum)
    head_lane = (lane >= SM_I) & (lane < SM_I + M_HEADS)
    fc = jnp.where(head_lane, pltpu.roll(f_cum, V7X_LANES - (SM_F - SM_I), 1), 0.0)
    m_prev = m_scr[0:1, :]
    a = fc + m_prev
    mx = jnp.full((L, V7X_LANES), -jnp.inf, F32)
    row = lax.broadcasted_iota(jnp.int32, (L, L), 0)
    col = lax.broadcasted_iota(jnp.int32, (L, L), 1)
    causal = col <= row
    dms = []
    for h in range(M_HEADS):
        dm = jnp.where(causal, (fc[:, SM_I + h:SM_I + h + 1] - f_rows[M_HEADS + h:M_HEADS + h + 1, :])
                       + i_rows[h:h + 1, :], -jnp.inf)
        dms.append(dm)
        mx = jnp.where(lane == SM_I + h, jnp.max(dm, axis=-1, keepdims=True), mx)
    m_tok = jnp.maximum(a, mx)
    w_inter = jnp.exp(a - m_tok)
    n_full = n_scr[0:1, :]
    hs, lows = _split2(q_m * n_full)
    qn_b = _dot(hs, hmat_ref[0:M_W, 0:M_W]) + _dot(lows, hmat_ref[0:M_W, 0:M_W])
    rs_all = jnp.zeros((L, V7X_LANES), F32)
    qn_all = jnp.zeros((L, V7X_LANES), F32)
    scs = []
    for p in range(M_HEADS // 2):
        sl = slice(p * V7X_LANES, (p + 1) * V7X_LANES)
        sc = _dot_nt(q_mb[:, sl], _stack_masked(k_mb[:, sl], lo, hi_m))
        mt_a = jnp.broadcast_to(m_tok[:, SM_I + 2 * p:SM_I + 2 * p + 1], (L, L))
        mt_b = jnp.broadcast_to(m_tok[:, SM_I + 2 * p + 1:SM_I + 2 * p + 2], (L, L))
        sc_a = sc[:, :L] * jnp.exp(dms[2 * p] - mt_a)
        sc_b = sc[:, L:] * jnp.exp(dms[2 * p + 1] - mt_b)
        rs_all = jnp.where(lane == SM_I + 2 * p, jnp.sum(sc_a, axis=-1, keepdims=True), rs_all)
        rs_all = jnp.where(lane == SM_I + 2 * p + 1, jnp.sum(sc_b, axis=-1, keepdims=True), rs_all)
        qn_all = jnp.where(lane == SM_I + 2 * p, qn_b[:, p * V7X_LANES:p * V7X_LANES + 1], qn_all)
        qn_all = jnp.where(lane == SM_I + 2 * p + 1, qn_b[:, p * V7X_LANES + HEAD_W:p * V7X_LANES + HEAD_W + 1], qn_all)
        scs.append(jnp.concatenate([sc_a, sc_b], axis=1).astype(BF16))
    den = rs_all + qn_all * w_inter
    inv = 1.0 / jnp.maximum(jnp.abs(den), jnp.exp(-m_tok))
    m_new = jnp.where(head_lane, m_tok, 0.0)[L - 1:L, :]
    wk = jnp.exp(fc[L - 1:L, :] - fc + gates - m_new)
    scale = jnp.broadcast_to(jnp.exp(a[L - 1:L, :] - m_new), (V7X_SUBLANES, V7X_LANES))
    h_parts = []
    kw_parts = []
    scale_parts = []
    for p in range(M_HEADS // 2):
        sl = slice(p * V7X_LANES, (p + 1) * V7X_LANES)
        ia, ib = SM_I + 2 * p, SM_I + 2 * p + 1
        cpair = cp_scr[p]
        num = (_dot(scs[p], _stack_masked(v_mb[:, sl], lo, hi_m))
               + _dot(q_mb[:, sl], cpair.astype(BF16)) * _lane_pick(w_inter, ia, ib, lo))
        h_parts.append(num * _lane_pick(inv, ia, ib, lo))
        kw = k_m[:, sl] * _lane_pick(wk, ia, ib, lo)
        kw_parts.append(kw)
        sc_row = _lane_pick(scale, ia, ib, lo8)[0:1]
        scale_parts.append(sc_row)
        cp_scr[p] = cpair * sc_row + _dot_tn(kw.astype(BF16), v_mb[:, sl]) * pmask
    kw_all = jnp.concatenate(kw_parts, axis=1)
    n_new = n_full * jnp.concatenate(scale_parts, axis=1) + jnp.sum(kw_all, axis=0, keepdims=True)
    n_scr[...] = jnp.broadcast_to(n_new, n_scr.shape)
    m_scr[...] = jnp.broadcast_to(m_new, m_scr.shape)
    gate_m = _sigmoid(proj_ref[:, C_OM:C_OM + M_W])
    mix_ref[:, R_W + G_W:D_MIX] = _norm_gate(jnp.concatenate(h_parts, axis=1), hmat_ref[0:M_W, 0:M_W],
                                             gcat_ref[:, R_W + G_W:D_MIX], gate_m)

    z = _dot(small.astype(BF16), wga_ref[...]) + bga_ref[...]
    log_a = _log_sigmoid(z) / G_NORMALIZER
    b = _dot3(tri_ref[...], log_a)
    b_last = b[L - 1:L, :]
    safe = jnp.max(-b_last) <= GLA_SAFE_LOG_RANGE
    q_g = proj_ref[:, C_QG:C_QG + G_QK] * (G_DK ** -0.5)
    k_g = proj_ref[:, C_KG:C_KG + G_QK]
    v_gb = proj_ref[:, C_VG:C_VG + G_W].astype(BF16)

    slot = lax.broadcasted_iota(jnp.int32, (V7X_LANES, V7X_LANES), 1) // G_DK
    row_head = lax.broadcasted_iota(jnp.int32, (V7X_LANES, V7X_LANES), 0) // HEAD_W
    lane_q = lane // G_DK

    def pad2(x):
        return jnp.concatenate([x, jnp.zeros((x.shape[0], 2 * V7X_LANES - G_QK), x.dtype)], axis=1)

    def slab(x_p, g):
        s0 = (2 * g * G_DK) // V7X_LANES * V7X_LANES
        return x_p[:, s0:s0 + V7X_LANES], (2 * g * G_DK - s0) // G_DK

    def gla_inter(q_p):
        return jnp.concatenate([_dot_nt(slab(q_p, g)[0], sgt_scr[g].astype(BF16)) for g in range(G_HEADS // 2)],
                               axis=1)

    def gla_update(decay_p, k_p, v_b):
        for g in range(G_HEADS // 2):
            ks, h_a = slab(k_p, g)
            kv = _dot_tn(v_b[:, g * V7X_LANES:(g + 1) * V7X_LANES], ks)
            sgt_scr[g] = sgt_scr[g] * slab(decay_p, g)[0] + jnp.where(slot == h_a + row_head, kv, 0.0)

    @pl.when(safe)
    def _():
        qt_p = pad2((q_g * jnp.exp(b)).astype(BF16))
        kt_p = pad2((k_g * jnp.exp(-b)).astype(BF16))
        kl_p = pad2((k_g * jnp.exp(b_last - b)).astype(BF16))
        o_inter = gla_inter(qt_p)
        parts = []
        for g in range(G_HEADS // 2):
            qs, h_a = slab(qt_p, g)
            ks, _ = slab(kt_p, g)
            sc = _dot_nt(qs, _stack_masked(ks, lane_q == h_a, lane_q == h_a + 1))
            sc = jnp.where(causal2, sc, 0.0).astype(BF16)
            vs = v_gb[:, g * V7X_LANES:(g + 1) * V7X_LANES]
            parts.append(_dot(sc, _stack_masked(vs, lo, hi_m)))
        og_scr[...] = o_inter + jnp.concatenate(parts, axis=1)
        gla_update(pad2(jnp.exp(b_last)), kl_p, v_gb)

    @pl.when(jnp.logical_not(safe))
    def _():
        v_g = proj_ref[:, C_VG:C_VG + G_W]
        b_loc = _dot3(btri_ref[...], log_a)
        b_tot = _dot3(bones_ref[...], log_a)
        qt_p = pad2((q_g * jnp.exp(b_loc)).astype(BF16))
        kt_p = pad2((k_g * jnp.exp(b_tot - b_loc)).astype(BF16))
        d_tot_p = pad2(jnp.exp(b_tot))
        emat = emat_ref[...]
        sub_row = lax.broadcasted_iota(jnp.int32, (CS, G_QK), 0)
        for blk in range(L // CS):
            r0 = blk * CS
            rs = slice(r0, r0 + CS)
            o_blk = gla_inter(qt_p[rs])
            bI, qI, kI = b_loc[rs], q_g[rs], k_g[rs]
            terms = []
            for j in range(CS):
                e = jnp.exp(jnp.where(sub_row >= j, bI - bI[j:j + 1], -jnp.inf))
                terms.append(e * qI * kI[j:j + 1])
            t = jnp.concatenate(terms, axis=0).astype(BF16)
            w = _dot(t, emat)
            for j in range(CS):
                o_blk = o_blk + w[j * CS:(j + 1) * CS] * v_g[r0 + j:r0 + j + 1]
            og_scr[rs, :] = o_blk
            gla_update(d_tot_p[r0:r0 + 1], kt_p[rs], v_gb[rs])

    gate_g = _silu(proj_ref[:, C_RG:C_RG + G_W])
    mix_ref[:, R_W:R_W + G_W] = _norm_gate(og_scr[...], hmat, gcat_ref[:, R_W:R_W + G_W], gate_g)

    @pl.when(c == NC - 1)
    def _():
        eye_k = _eye(G_DK)
        for p in range(R_HEADS // 2):
            sr_out[2 * p] = srp_scr[p, 0:HEAD_W, 0:HEAD_W]
            sr_out[2 * p + 1] = srp_scr[p, HEAD_W:, HEAD_W:]
        for h in range(G_HEADS):
            g, hh = h // 2, h % 2
            c0 = (h * G_DK) % V7X_LANES
            sg_out[h] = _dot3_nt(eye_k, sgt_scr[g, hh * G_DV:(hh + 1) * G_DV, c0:c0 + G_DK])
        for p in range(M_HEADS // 2):
            c_out[2 * p] = cp_scr[p, 0:HEAD_W, 0:HEAD_W]
            c_out[2 * p + 1] = cp_scr[p, HEAD_W:, HEAD_W:]
        for h in range(M_HEADS):
            n_out[h:h + 1, :] = n_scr[0:1, h * M_DK:(h + 1) * M_DK]
        m_out[...] = m_scr[0:1, SM_I:SM_I + M_HEADS]
        conv_out[...] = conv_scr[CONV_LO:CONV_AT, :]


def _prompt_tables(T, L, CS):
    half = R_DK // 2
    inv = ROPE_BASE ** (-jnp.arange(half, dtype=F32) * 2.0 / R_DK)
    pos = jnp.arange(T, dtype=F32)
    ang = pos[:, None] * inv[None, :]
    cos = jnp.tile(jnp.cos(ang), (1, V7X_LANES // half))
    sin_h = jnp.sin(ang)
    sin = jnp.tile(jnp.concatenate([-sin_h, sin_h], axis=1), (1, V7X_LANES // R_DK))
    qs = R_DK ** -0.5

    log_gamma = jnp.log(1.0 - 2.0 ** (-5.0 - jnp.arange(R_HEADS, dtype=F32)))
    idx = jnp.arange(L, dtype=F32)
    rel = idx[:, None] - idx[None, :]
    causal = rel >= 0
    rdecay = jnp.where(causal[None], jnp.exp(log_gamma[:, None, None] * jnp.where(causal, rel, 0.0)[None]), 0.0)
    rdec = jnp.concatenate([rdecay[0::2], rdecay[1::2]], axis=2)
    qdec = jnp.repeat(jnp.exp(log_gamma[:, None] * (idx + 1.0)).T, R_DK, axis=1)
    kdec = jnp.repeat(jnp.exp(log_gamma[:, None] * (L - 1.0 - idx)).T, R_DK, axis=1)
    lg32 = np.log(1.0 - 2.0 ** (-5.0 - np.arange(R_HEADS, dtype=np.float64))).astype(np.float32)
    chunk_decay = tuple(float(np.exp(v * np.float32(L))) for v in lg32)

    r = np.arange(L)
    tri = (r[None, :] <= r[:, None])
    same = (r[None, :] // CS) == (r[:, None] // CS)
    selr = np.zeros((SEL_ROWS, V7X_LANES), np.float32)
    for h in range(M_HEADS):
        selr[h, SM_I + h] = 1.0
        selr[M_HEADS + h, SM_F + h] = 1.0
    hv = np.arange(G_W) // G_DV
    hc = np.arange(G_QK) // G_DK
    gmask = (hv[:, None] == hc[None, :]).astype(np.float32)
    hmat = (hv[:, None] == hv[None, :]).astype(np.float32)
    pm = np.arange(V7X_LANES) // HEAD_W
    pmask = (pm[:, None] == pm[None, :]).astype(np.float32)
    return dict(
        cosq=cos * qs, sinq=sin * qs, cosk=cos, sink=sin, rdec=rdec, qdec=qdec, kdec=kdec, chunk_decay=chunk_decay,
        tri=jnp.asarray(tri, BF16), btri=jnp.asarray(tri & same, BF16), bones=jnp.asarray(same, BF16),
        selr=jnp.asarray(selr, BF16), hmat=jnp.asarray(hmat, BF16), emat=jnp.asarray(gmask.T, BF16),
        pmask=jnp.asarray(pmask, F32))


def mixer_prompt(proj, tabs, lw, l_out, depth, prev, *, L, CS):
    B, T, _ = proj.shape
    NC = T // L
    const2 = lambda b, c: (0, 0)
    const3 = lambda b, c: (0, 0, 0)
    tspec = pl.BlockSpec((L, V7X_LANES), lambda b, c: (c, 0))
    in_specs = [
        pl.BlockSpec((None, L, D_IN_PAD), lambda b, c: (b, c, 0)),
        tspec, tspec, tspec, tspec,
        pl.BlockSpec((R_HEADS // 2, L, 2 * L), const3),
        pl.BlockSpec((L, R_W), const2),
        pl.BlockSpec((L, R_W), const2),
        pl.BlockSpec((L, L), const2),
        pl.BlockSpec((L, L), const2),
        pl.BlockSpec((L, L), const2),
        pl.BlockSpec((SEL_ROWS, V7X_LANES), const2),
        pl.BlockSpec((G_W, G_W), const2),
        pl.BlockSpec((G_QK, G_W), const2),
        pl.BlockSpec((V7X_LANES, V7X_LANES), const2),
        pl.BlockSpec((V7X_LANES, G_QK), const2),
        pl.BlockSpec((1, G_QK), const2),
        pl.BlockSpec((CONV_W, M_QK), const2),
        pl.BlockSpec((1, M_QK), const2),
        pl.BlockSpec((1, V7X_LANES), const2),
        pl.BlockSpec((1, D_MIX), const2),
    ]
    args = [proj, tabs["cosq"], tabs["sinq"], tabs["cosk"], tabs["sink"], tabs["rdec"], tabs["qdec"], tabs["kdec"],
            tabs["tri"], tabs["btri"], tabs["bones"], tabs["selr"], tabs["hmat"], tabs["emat"],
            tabs["pmask"], lw["wga"], lw["bga"], lw["convw"], lw["convb"], lw["bsm"], lw["gcat"]]
    n_in = len(args)
    in_specs += [pl.BlockSpec(memory_space=pl.ANY)] * 6
    args += list(prev)
    aliases = {n_in + i: 1 + i for i in range(6)}

    def st_spec(dims):
        return pl.BlockSpec((None, None) + dims, lambda b, c: (l_out, b) + (0,) * len(dims))

    out_shape = (jax.ShapeDtypeStruct((B, T, D_MIX), BF16),) + tuple(
        jax.ShapeDtypeStruct((depth, B) + d, F32) for d in _STATE_DIMS)
    out_specs = (pl.BlockSpec((None, L, D_MIX), lambda b, c: (b, c, 0)),) + tuple(st_spec(d) for d in _STATE_DIMS)
    scratch = [
        pltpu.VMEM((R_HEADS // 2, V7X_LANES, V7X_LANES), F32),
        pltpu.VMEM((G_HEADS // 2, V7X_LANES, V7X_LANES), F32),
        pltpu.VMEM((M_HEADS // 2, V7X_LANES, V7X_LANES), F32),
        pltpu.VMEM((V7X_SUBLANES, M_W), F32),
        pltpu.VMEM((V7X_SUBLANES, V7X_LANES), F32),
        pltpu.VMEM((CONV_AT + L, M_QK), F32),
        pltpu.VMEM((L, G_W), F32),
    ]
    kern = functools.partial(_mixer_prompt_kernel, L=L, CS=CS, NC=NC, chunk_decay=tabs["chunk_decay"])
    outs = pl.pallas_call(
        kern, out_shape=out_shape, grid=(B, NC), in_specs=in_specs, out_specs=out_specs,
        scratch_shapes=scratch, input_output_aliases=aliases,
        compiler_params=_cparams(("parallel", "arbitrary")), name="mixer_prompt",
    )(*args)
    return outs[0], tuple(outs[1:])


NS = 16
ROWS = 128


def _seq_bcast(x, t, T):
    n, w = x.shape
    x3 = x.reshape(n // T, T, w)
    return jnp.broadcast_to(x3[:, t:t + 1, :], (n // T, T, w)).reshape(n, w)


def _mixer_sample_kernel(proj_ref, cosq_ref, sinq_ref, cosk_ref, sink_ref, rdec_ref, qdec_ref, kdec_ref,
                         tri_ref, segones_ref, selr_ref, hmat_ref, emat_ref, mseg_ref, msegt_ref,
                         wga_ref, bga_ref, convw_ref, convb_ref, bsm_ref, gcat_ref,
                         sr_ref, sg_ref, c_ref, n_ref, m_ref, conv_ref, *rest, T, chunk_decay, slot, depth):
    mix_ref = rest[-8]
    st_outs = rest[-7:-1]
    conv_scr = rest[-1]
    if slot is not None:
        for e in range(depth):
            if e != slot:
                for dst in st_outs:
                    dst[e] = jnp.zeros(dst.shape[1:], F32)
        st_outs = [o.at[slot] for o in st_outs]
    sr_out, sg_out, c_out, n_out, m_out, conv_out = st_outs
    L = ROWS
    ns = L // T
    lane = lax.broadcasted_iota(jnp.int32, (L, V7X_LANES), 1)
    lo = lane < HEAD_W
    hi_m = lane >= HEAD_W
    first_half = (lane % R_DK) < (R_DK // 2)
    row = lax.broadcasted_iota(jnp.int32, (L, L), 0)
    col = lax.broadcasted_iota(jnp.int32, (L, L), 1)
    segcausal = (row // T == col // T) & (col <= row)
    hmat = hmat_ref[...]
    mseg = mseg_ref[...]
    msegt = msegt_ref[...]

    def rope(x, cos_ref, sin_ref):
        cos = cos_ref[...]
        sin = sin_ref[...]
        parts = []
        for t in range(R_W // V7X_LANES):
            xs = x[:, t * V7X_LANES:(t + 1) * V7X_LANES]
            rot = jnp.where(first_half, pltpu.roll(xs, V7X_LANES - R_DK // 2, 1), pltpu.roll(xs, R_DK // 2, 1))
            parts.append(xs * cos + rot * sin)
        return jnp.concatenate(parts, axis=1)

    def tile_lanes(x, n):
        return jnp.concatenate([x] * n, axis=1)

    def tile_rows(x, n):
        return jnp.concatenate([x] * n, axis=0)

    def pair_state_terms(qs, ks_f32, vs, st_ref, p, hd):
        r = st_ref[:, 2 * p:2 * p + 2].reshape(ns * 2 * hd, hd).astype(BF16)
        kt = tile_rows(ks_f32.T.astype(BF16), ns) * msegt
        inter, kv = [], []
        for hh, m in ((0, lo), (1, hi_m)):
            qh = jnp.where(m, qs, jnp.zeros_like(qs))
            inter.append(_dot(tile_lanes(qh, ns) * mseg, r))
            kv.append(_dot(kt, vs[:, hh * hd:(hh + 1) * hd]))
        return inter, kv

    q_r = rope(proj_ref[:, C_QR:C_QR + R_W], cosq_ref, sinq_ref)
    k_r = rope(proj_ref[:, C_KR:C_KR + R_W], cosk_ref, sink_ref)
    q_rb = q_r.astype(BF16)
    k_rb = k_r.astype(BF16)
    kd_r = k_r * kdec_ref[...]
    v_rb = proj_ref[:, C_VR:C_VR + R_W].astype(BF16)
    o_parts = []
    for p in range(R_HEADS // 2):
        sl = slice(p * V7X_LANES, (p + 1) * V7X_LANES)
        qs, ks, vs = q_rb[:, sl], k_rb[:, sl], v_rb[:, sl]
        sc = _dot_nt(qs, _stack_masked(ks, lo, hi_m)) * rdec_ref[p]
        inter, kv = pair_state_terms(qs, kd_r[:, sl], vs, sr_ref, p, R_DK)
        o = (_dot(sc.astype(BF16), _stack_masked(vs, lo, hi_m))
             + jnp.concatenate(inter, axis=1) * qdec_ref[:, sl])
        o_parts.append(o)
        for hh in range(2):
            h = 2 * p + hh
            sr_out[:, h] = sr_ref[:, h] * chunk_decay[h] + kv[hh].reshape(ns, 2, R_DK, R_DV)[:, hh]
    gate_r = _silu(proj_ref[:, C_GR:C_GR + R_W])
    mix_ref[:, 0:R_W] = _norm_gate(jnp.concatenate(o_parts, axis=1), hmat, gcat_ref[:, 0:R_W], gate_r)

    small = proj_ref[:, C_SM:C_SM + V7X_LANES]
    z = _dot(small.astype(BF16), wga_ref[...]) + bga_ref[...]
    log_a = _log_sigmoid(z) / G_NORMALIZER
    b = _dot3(tri_ref[...], log_a)
    b_tot = _seq_bcast(b, T - 1, T)
    q_g = proj_ref[:, C_QG:C_QG + G_QK] * (G_DK ** -0.5)
    k_g = proj_ref[:, C_KG:C_KG + G_QK]
    v_g = proj_ref[:, C_VG:C_VG + G_W]
    v_gb = v_g.astype(BF16)
    tok = lax.broadcasted_iota(jnp.int32, (L, G_QK), 0) % T
    emat = emat_ref[...]
    o_g = jnp.zeros((L, G_W), F32)
    for j in range(T):
        e = jnp.exp(jnp.where(tok >= j, b - _seq_bcast(b, j, T), -jnp.inf))
        tj = (e * q_g * _seq_bcast(k_g, j, T)).astype(BF16)
        o_g = o_g + _dot(tj, emat) * _seq_bcast(v_g, j, T)
    qt = (q_g * jnp.exp(b)).astype(BF16)
    kl = k_g * jnp.exp(b_tot - b)
    dtot = jnp.exp(b_tot)
    lane_h = lane // G_DK
    inter_parts = [None] * G_HEADS
    for h0, heads in ((0, (0, 1, 2, 3)), (2, (4, 5))):
        c0 = h0 * G_DK
        q_s = qt[:, c0:c0 + V7X_LANES]
        r = sg_ref[:, h0:h0 + 4].reshape(ns * V7X_LANES, G_DV)
        rb = r.astype(BF16)
        kt = tile_rows(kl[:, c0:c0 + V7X_LANES].T.astype(BF16), ns) * msegt
        dt = dtot[:, c0:c0 + V7X_LANES].T
        dcols = []
        for s in range(ns):
            dcols.append(jnp.broadcast_to(dt[:, s * T:s * T + 1], (V7X_LANES, G_DV)))
        dfull = jnp.concatenate(dcols, axis=0).reshape(ns, 4, G_DK, G_DV)
        for h in heads:
            qh = jnp.where(lane_h == h - h0, q_s, jnp.zeros_like(q_s))
            inter_parts[h] = _dot(tile_lanes(qh, ns) * mseg, rb)
            kv = _dot(kt, v_gb[:, h * G_DV:(h + 1) * G_DV])
            sg_out[:, h] = sg_ref[:, h] * dfull[:, h - h0] + kv.reshape(ns, 4, G_DK, G_DV)[:, h - h0]
    o_g = o_g + jnp.concatenate(inter_parts, axis=1)
    gate_g = _silu(proj_ref[:, C_RG:C_RG + G_W])
    mix_ref[:, R_W:R_W + G_W] = _norm_gate(o_g, hmat, gcat_ref[:, R_W:R_W + G_W], gate_g)

    u = proj_ref[:, C_QKM:C_QKM + M_QK]
    conv_scr[:, CONV_LO:CONV_AT, :] = conv_ref[...]
    conv_scr[:, CONV_AT:CONV_AT + T, :] = u.reshape(ns, T, M_QK)
    y = convb_ref[...]
    for j in range(CONV_W - 1):
        y = y + conv_scr[:, CONV_LO + j:CONV_LO + j + T, :].reshape(L, M_QK) * convw_ref[j:j + 1, :]
    y = y + u * convw_ref[CONV_W - 1:CONV_W, :]
    conv_out[...] = conv_scr[:, CONV_LO + T:CONV_AT + T, :]
    qk = _silu(y)
    q_m = qk[:, :M_W]
    k_m = qk[:, M_W:] * (M_DK ** -0.5)
    q_mb = q_m.astype(BF16)
    k_mb = k_m.astype(BF16)
    v_mb = proj_ref[:, C_VM:C_VM + M_W].astype(BF16)
    gates = small + bsm_ref[...]
    f_cum = _dot3(tri_ref[...], _log_sigmoid(gates))
    i_rows = _dot3_nt(selr_ref[...], gates)
    f_rows = _dot3_nt(selr_ref[...], f_cum)
    head_lane = (lane >= SM_I) & (lane < SM_I + M_HEADS)
    fc = jnp.where(head_lane, pltpu.roll(f_cum, V7X_LANES - (SM_F - SM_I), 1), 0.0)
    m_prev = m_ref[...]
    a = fc + m_prev
    mx = jnp.full((L, V7X_LANES), -jnp.inf, F32)
    dms = []
    for h in range(M_HEADS):
        dm = jnp.where(segcausal, (fc[:, SM_I + h:SM_I + h + 1] - f_rows[M_HEADS + h:M_HEADS + h + 1, :])
                       + i_rows[h:h + 1, :], -jnp.inf)
        dms.append(dm)
        mx = jnp.where(lane == SM_I + h, jnp.max(dm, axis=-1, keepdims=True), mx)
    m_tok = jnp.maximum(a, mx)
    w_inter = jnp.exp(a - m_tok)
    n_rows = n_ref[...]
    hs, lows = _split2(q_m * n_rows)
    qn_b = _dot(hs, hmat_ref[0:M_W, 0:M_W]) + _dot(lows, hmat_ref[0:M_W, 0:M_W])
    rs_all = jnp.zeros((L, V7X_LANES), F32)
    qn_all = jnp.zeros((L, V7X_LANES), F32)
    scs = []
    for p in range(M_HEADS // 2):
        sl = slice(p * V7X_LANES, (p + 1) * V7X_LANES)
        sc = _dot_nt(q_mb[:, sl], _stack_masked(k_mb[:, sl], lo, hi_m))
        mt_a = jnp.broadcast_to(m_tok[:, SM_I + 2 * p:SM_I + 2 * p + 1], (L, L))
        mt_b = jnp.broadcast_to(m_tok[:, SM_I + 2 * p + 1:SM_I + 2 * p + 2], (L, L))
        sc_a = sc[:, :L] * jnp.exp(dms[2 * p] - mt_a)
        sc_b = sc[:, L:] * jnp.exp(dms[2 * p + 1] - mt_b)
        rs_all = jnp.where(lane == SM_I + 2 * p, jnp.sum(sc_a, axis=-1, keepdims=True), rs_all)
        rs_all = jnp.where(lane == SM_I + 2 * p + 1, jnp.sum(sc_b, axis=-1, keepdims=True), rs_all)
        qn_all = jnp.where(lane == SM_I + 2 * p, qn_b[:, p * V7X_LANES:p * V7X_LANES + 1], qn_all)
        qn_all = jnp.where(lane == SM_I + 2 * p + 1, qn_b[:, p * V7X_LANES + HEAD_W:p * V7X_LANES + HEAD_W + 1], qn_all)
        scs.append(jnp.concatenate([sc_a, sc_b], axis=1).astype(BF16))
    den = rs_all + qn_all * w_inter
    inv = 1.0 / jnp.maximum(jnp.abs(den), jnp.exp(-m_tok))
    m_new = _seq_bcast(jnp.where(head_lane, m_tok, 0.0), T - 1, T)
    wk = jnp.exp(_seq_bcast(fc, T - 1, T) - fc + gates - m_new)
    scale = jnp.exp(_seq_bcast(a, T - 1, T) - m_new)
    h_parts = []
    kw_parts = []
    scale_parts = []
    for p in range(M_HEADS // 2):
        sl = slice(p * V7X_LANES, (p + 1) * V7X_LANES)
        ia, ib = SM_I + 2 * p, SM_I + 2 * p + 1
        kw = k_m[:, sl] * _lane_pick(wk, ia, ib, lo)
        kw_parts.append(kw)
        inter, kv = pair_state_terms(q_mb[:, sl], kw, v_mb[:, sl], c_ref, p, M_DK)
        num = (_dot(scs[p], _stack_masked(v_mb[:, sl], lo, hi_m))
               + jnp.concatenate(inter, axis=1) * _lane_pick(w_inter, ia, ib, lo))
        h_parts.append(num * _lane_pick(inv, ia, ib, lo))
        scale_parts.append(_lane_pick(scale, ia, ib, lo))
        for hh in range(2):
            h = 2 * p + hh
            sc_rows = jnp.broadcast_to(scale[:, SM_I + h:SM_I + h + 1], (L, M_DV)).reshape(ns, T, M_DV)
            sc_h = jnp.broadcast_to(sc_rows[:, 0:1, :], (ns, M_DK, M_DV))
            c_out[:, h] = c_ref[:, h] * sc_h + kv[hh].reshape(ns, 2, M_DK, M_DV)[:, hh]
    kw_all = jnp.concatenate(kw_parts, axis=1)
    n_out[...] = n_rows * jnp.concatenate(scale_parts, axis=1) + _dot3(segones_ref[...], kw_all)
    m_out[...] = m_new
    gate_m = _sigmoid(proj_ref[:, C_OM:C_OM + M_W])
    mix_ref[:, R_W + G_W:D_MIX] = _norm_gate(jnp.concatenate(h_parts, axis=1), hmat_ref[0:M_W, 0:M_W],
                                             gcat_ref[:, R_W + G_W:D_MIX], gate_m)


def _sample_tables(T, pos0):
    L = ROWS
    half = R_DK // 2
    inv = ROPE_BASE ** (-jnp.arange(half, dtype=F32) * 2.0 / R_DK)
    tok = np.arange(L) % T
    seq = np.arange(L) // T
    pos = pos0 + jnp.asarray(tok, F32)
    ang = pos[:, None] * inv[None, :]
    cos = jnp.tile(jnp.cos(ang), (1, V7X_LANES // half))
    sin_h = jnp.sin(ang)
    sin = jnp.tile(jnp.concatenate([-sin_h, sin_h], axis=1), (1, V7X_LANES // R_DK))
    qs = R_DK ** -0.5

    log_gamma = jnp.log(1.0 - 2.0 ** (-5.0 - jnp.arange(R_HEADS, dtype=F32)))
    tf = jnp.asarray(tok, F32)
    rel = tf[:, None] - tf[None, :]
    ok_np = (seq[:, None] == seq[None, :]) & (tok[None, :] <= tok[:, None])
    ok = jnp.asarray(ok_np)
    rdecay = jnp.where(ok[None], jnp.exp(log_gamma[:, None, None] * jnp.where(ok, rel, 0.0)[None]), 0.0)
    rdec = jnp.concatenate([rdecay[0::2], rdecay[1::2]], axis=2)
    qdec = jnp.repeat(jnp.exp(log_gamma[:, None] * (tf + 1.0)).T, R_DK, axis=1)
    kdec = jnp.repeat(jnp.exp(log_gamma[:, None] * (T - 1.0 - tf)).T, R_DK, axis=1)
    lg32 = np.log(1.0 - 2.0 ** (-5.0 - np.arange(R_HEADS, dtype=np.float64))).astype(np.float32)
    chunk_decay = tuple(float(np.exp(v * np.float32(T))) for v in lg32)

    selr = np.zeros((SEL_ROWS, V7X_LANES), np.float32)
    for h in range(M_HEADS):
        selr[h, SM_I + h] = 1.0
        selr[M_HEADS + h, SM_F + h] = 1.0
    hv = np.arange(G_W) // G_DV
    hc = np.arange(G_QK) // G_DK
    gmask = (hv[:, None] == hc[None, :]).astype(np.float32)
    hmat = (hv[:, None] == hv[None, :]).astype(np.float32)
    ns = L // T
    mseg = (seq[:, None] == (np.arange(ns * V7X_LANES) // V7X_LANES)[None, :]).astype(np.float32)
    return dict(
        cosq=cos * qs, sinq=sin * qs, cosk=cos, sink=sin, rdec=rdec, qdec=qdec, kdec=kdec, chunk_decay=chunk_decay,
        tri=jnp.asarray(ok_np, BF16), segones=jnp.asarray(seq[:, None] == seq[None, :], BF16),
        selr=jnp.asarray(selr, BF16), hmat=jnp.asarray(hmat, BF16),
        emat=jnp.asarray(gmask.T, BF16), mseg=jnp.asarray(mseg, BF16), msegt=jnp.asarray(mseg.T, BF16))


_SAMPLE_STATE_DIMS = ((R_HEADS, R_DK, R_DV), (G_HEADS, G_DK, G_DV), (M_HEADS, M_DK, M_DV), (M_W,),
                      (V7X_LANES,), (CONV_W - 1, M_QK))


def mixer_sample(proj, tabs, lw, state, l, depth, prev, *, T):
    M = proj.shape[0]
    B = M // T
    assert B % NS == 0 and NS * T == ROWS
    L = ROWS
    const2 = lambda b: (0, 0)
    const3 = lambda b: (0, 0, 0)
    dims = [(NS,) + d for d in _SAMPLE_STATE_DIMS]
    dims[3], dims[4] = (L, M_W), (L, V7X_LANES)
    full = [(B,) + d for d in _SAMPLE_STATE_DIMS]
    full[3], full[4] = (M, M_W), (M, V7X_LANES)

    def spec(d, lead):
        tail = (0,) * (len(d) - 1)
        if lead == "all":
            return pl.BlockSpec((depth,) + d, lambda b: (0, b) + tail)
        return pl.BlockSpec((None,) + d, lambda b: (lead, b) + tail)

    in_specs = [
        pl.BlockSpec((L, D_IN_PAD), lambda b: (b, 0)),
        pl.BlockSpec((L, V7X_LANES), const2), pl.BlockSpec((L, V7X_LANES), const2),
        pl.BlockSpec((L, V7X_LANES), const2), pl.BlockSpec((L, V7X_LANES), const2),
        pl.BlockSpec((R_HEADS // 2, L, 2 * L), const3),
        pl.BlockSpec((L, R_W), const2),
        pl.BlockSpec((L, R_W), const2),
        pl.BlockSpec((L, L), const2),
        pl.BlockSpec((L, L), const2),
        pl.BlockSpec((SEL_ROWS, V7X_LANES), const2),
        pl.BlockSpec((G_W, G_W), const2),
        pl.BlockSpec((G_QK, G_W), const2),
        pl.BlockSpec((L, NS * V7X_LANES), const2),
        pl.BlockSpec((NS * V7X_LANES, L), const2),
        pl.BlockSpec((V7X_LANES, G_QK), const2),
        pl.BlockSpec((1, G_QK), const2),
        pl.BlockSpec((CONV_W, M_QK), const2),
        pl.BlockSpec((1, M_QK), const2),
        pl.BlockSpec((1, V7X_LANES), const2),
        pl.BlockSpec((1, D_MIX), const2),
    ] + [spec(d, l) for d in dims]
    args = [proj, tabs["cosq"], tabs["sinq"], tabs["cosk"], tabs["sink"], tabs["rdec"], tabs["qdec"], tabs["kdec"],
            tabs["tri"], tabs["segones"], tabs["selr"], tabs["hmat"], tabs["emat"], tabs["mseg"], tabs["msegt"],
            lw["wga"], lw["bga"], lw["convw"], lw["convb"], lw["bsm"], lw["gcat"]] + list(state)
    aliases = {}
    if prev is not None:
        n_in = len(args)
        in_specs += [pl.BlockSpec(memory_space=pl.ANY)] * 6
        args += list(prev)
        aliases = {n_in + i: 1 + i for i in range(6)}
    out_shape = (jax.ShapeDtypeStruct((M, D_MIX), BF16),) + tuple(jax.ShapeDtypeStruct((depth,) + f, F32) for f in full)
    out_specs = (pl.BlockSpec((L, D_MIX), lambda b: (b, 0)),) + tuple(
        spec(d, "all" if prev is None else l) for d in dims)
    kern = functools.partial(_mixer_sample_kernel, T=T, chunk_decay=tabs["chunk_decay"],
                             slot=l if prev is None else None, depth=depth)
    outs = pl.pallas_call(
        kern, out_shape=out_shape, grid=(B // NS,), in_specs=in_specs, out_specs=out_specs,
        scratch_shapes=[pltpu.VMEM((NS, CONV_AT + T, M_QK), F32)],
        input_output_aliases=aliases,
        compiler_params=_cparams(("parallel",)), name="mixer_sample",
    )(*args)
    return outs[0], tuple(outs[1:])


def _prep_weights(w_in, w_out, w_xq, w_xk, w_xv, w_xo, w_gate, w_up, w_down):
    a0 = 2 * R_HEADS * R_DK + 2 * R_W + 2 * G_QK + 2 * G_W
    m0 = a0 + G_RANK
    g0 = m0 + M_QK + 2 * M_W
    cast = lambda w: w.astype(BF16)
    wb = cast(w_in)
    pad = jnp.zeros(w_in.shape[:2] + (D_IN_PAD - C_SM - G_RANK - 2 * M_HEADS,), BF16)
    w_pad = jnp.concatenate([wb[..., :a0], wb[..., m0:g0], wb[..., g0:g0 + 2 * M_HEADS], wb[..., a0:m0], pad], axis=2)
    return dict(w_in=w_pad, w_out=cast(w_out), w_xq=cast(w_xq), w_xk=cast(w_xk), w_xv=cast(w_xv),
                w_xo=cast(w_xo), w_gate=cast(w_gate), w_up=cast(w_up), w_down=cast(w_down))


def _prep_layer(l, g_mix, w_ga2, b_ga, conv_w, conv_b, b_i, b_f, g_ret, g_gla, g_mlstm, g_xattn, g_mem, g_ffn):
    wga = jnp.zeros((V7X_LANES, G_QK), F32).at[SM_AG:SM_AG + G_RANK].set(w_ga2[l]).astype(BF16)
    bsm = (jnp.zeros((1, V7X_LANES), F32).at[0, SM_I:SM_I + M_HEADS].set(b_i[l])
           .at[0, SM_F:SM_F + M_HEADS].set(b_f[l]))
    return dict(
        g_mix=g_mix[l], wga=wga, bga=b_ga[l].reshape(1, G_QK), convw=conv_w[l],
        convb=conv_b[l].reshape(1, M_QK), bsm=bsm,
        gcat=jnp.concatenate([g_ret[l], g_gla[l], g_mlstm[l]]).reshape(1, D_MIX),
        g_xattn=g_xattn[l], g_mem=g_mem[l], g_ffn=g_ffn[l])


def _layer(x, mix_fn, attend, W, lw, l, g_final, *, tm, final_norm):
    B, T, D = x.shape
    M = B * T
    x2 = x.reshape(M, D)
    proj = rms_matmul(x2, lw["g_mix"], W["w_in"], l, tm=tm)
    mix, new_state = mix_fn(proj)
    x2 = attend(mix, x2)
    x2 = swiglu_res(x2, lw["g_ffn"], W["w_gate"], W["w_up"], W["w_down"], l, g_final,
                    tm=tm, final_norm=final_norm)
    return x2.reshape(B, T, D), new_state


def kernel(x_prompt, x_sample, state_ret, state_gla, state_mlstm_C, state_mlstm_n, state_mlstm_m, state_mlstm_conv, cache_mem_k, cache_mem_v, mem_prompt, g_mix, w_in, w_ga2, b_ga, conv_w, conv_b, b_i, b_f, g_ret, g_gla, g_mlstm, w_out, g_xattn, g_mem, w_xq, w_xk, w_xv, w_xo, g_ffn, w_gate, w_up, w_down, g_final):
    B, T, D = x_prompt.shape
    Bs, Ts, _ = x_sample.shape
    depth = w_in.shape[0]
    assert T % CHUNK == 0 and Ts * NS == ROWS and Bs % NS == 0
    tabs_p = _prompt_tables(T, CHUNK, GLA_SUB)
    tabs_s = _sample_tables(Ts, float(PAST_LEN))
    tm_p = 512 if (B * T) % 512 == 0 else B * T
    tm_s = 512 if (Bs * Ts) % 512 == 0 else Bs * Ts
    tq_p = 1024 if T % 1024 == 0 else T
    nb_x = XATTN_SEQS if Bs % XATTN_SEQS == 0 else 1

    sample_state = (state_ret, state_gla, state_mlstm_C,
                    jnp.repeat(state_mlstm_n.reshape(depth, Bs, M_W), Ts, axis=1),
                    jnp.repeat(jnp.pad(state_mlstm_m, ((0, 0), (0, 0), (0, V7X_LANES - M_HEADS))), Ts, axis=1),
                    state_mlstm_conv)
    p_st = tuple(jnp.zeros((depth, B) + d, F32) for d in _STATE_DIMS)
    s_st = p_mem = None
    hp, hs = x_prompt, x_sample
    W = _prep_weights(w_in, w_out, w_xq, w_xk, w_xv, w_xo, w_gate, w_up, w_down)
    for l in range(depth):
        lw = _prep_layer(l, g_mix, w_ga2, b_ga, conv_w, conv_b, b_i, b_f, g_ret, g_gla, g_mlstm,
                         g_xattn, g_mem, g_ffn)
        last = l == depth - 1
        k5, v5, kb, vb = memory_kv(mem_prompt, lw["g_mem"], W["w_xk"], W["w_xv"], l, depth, p_mem)
        p_mem = (k5, v5)

        def mix_p(proj):
            mix, st = mixer_prompt(proj.reshape(B, T, D_IN_PAD), tabs_p, lw, l, depth, p_st, L=CHUNK, CS=GLA_SUB)
            return mix.reshape(B * T, D_MIX), st

        def attend_p(mix, x):
            y = post_mix(mix.reshape(B, T, D_MIX), x.reshape(B, T, D), kb, vb,
                         W["w_out"], lw["g_xattn"], W["w_xq"], W["w_xo"], l, tq=tq_p)
            return y.reshape(B * T, D)

        hp, p_st = _layer(hp, mix_p, attend_p, W, lw, l, g_final, tm=tm_p, final_norm=last)
        hs, s_st = _layer(hs, lambda proj: mixer_sample(proj, tabs_s, lw, sample_state, l, depth, s_st, T=Ts),
                          lambda mix, x: post_mix_cache(mix, x, cache_mem_k, cache_mem_v, W["w_out"],
                                                        lw["g_xattn"], W["w_xq"], W["w_xo"], l, nb=nb_x, T=Ts),
                          W, lw, l, g_final, tm=tm_s, final_norm=last)

    p_out = p_st[:4] + (p_st[4].reshape(depth, B, M_HEADS), p_st[5])
    s_out = s_st[:3] + (s_st[3][:, ::Ts].reshape(depth, Bs, M_HEADS, M_DK), s_st[4][:, ::Ts, :M_HEADS], s_st[5])
    return (hp, hs, *p_out, *p_mem, *s_out)
```
